```python
import math
import jax, jax.numpy as jnp
from jax import lax
import numpy as np

D_MODEL = 1024
BATCH = 2
SEQ = 16384
DEPTH = 2

N_A_LAYERS = DEPTH // 2
N_B_LAYERS = DEPTH - N_A_LAYERS
RMS_EPS = 1e-6

RET_HEADS = 4
RET_QK_DIM = D_MODEL // RET_HEADS
RET_V_DIM = 2 * D_MODEL // RET_HEADS
RET_CHUNK = 128
RET_IN = 2 * RET_HEADS * RET_QK_DIM + 2 * RET_HEADS * RET_V_DIM

FFN_HIDDEN = ((8 * D_MODEL + 3 * 256 - 1) // (3 * 256)) * 256

NSA_HEADS = 16
NSA_GROUPS = 4
NSA_REP = NSA_HEADS // NSA_GROUPS
NSA_HEAD_DIM = 64
CMP_LEN = 32
CMP_STRIDE = 16
CMP_HIDDEN = 256
SEL_LEN = 64
SEL_TOPK = 16
WINDOW = 512
Q_BLOCK = 128
NSA_IN = NSA_HEADS * NSA_HEAD_DIM + 3 * NSA_HEADS
KV_SHARED_OUT = 6 * NSA_GROUPS * NSA_HEAD_DIM

REL_BUCKETS = 32
REL_MAX_DIST = 128

kernel_name = "yoco_retnet_nsa_hybrid"


def rms_norm(x, g):
    xf = x.astype(jnp.float32)
    y = xf * lax.rsqrt(jnp.mean(xf * xf, axis=-1, keepdims=True) + RMS_EPS)
    return (y * g.astype(jnp.float32)).astype(x.dtype)


def swiglu_ffn(x, w_in, w_out):
    gate, up = jnp.split(x @ w_in, 2, axis=-1)
    return (jax.nn.silu(gate) * up) @ w_out


def rotate_pairs(x, cos, sin):
    x1 = x[..., 0::2]
    x2 = x[..., 1::2]
    return jnp.stack([x1 * cos - x2 * sin, x1 * sin + x2 * cos], axis=-1).reshape(x.shape)


def t5_bucket(dist):
    n = jnp.maximum(dist, 0)
    max_exact = REL_BUCKETS // 2
    nf = jnp.maximum(n, 1).astype(jnp.float32)
    large = max_exact + (jnp.log(nf / max_exact) / math.log(REL_MAX_DIST / max_exact)
                         * (REL_BUCKETS - max_exact)).astype(jnp.int32)
    large = jnp.minimum(large, REL_BUCKETS - 1)
    return jnp.where(n < max_exact, n, large)


def masked_softmax(logits, mask):
    logits = jnp.where(mask, logits, -jnp.inf)
    m = jnp.max(logits, axis=-1, keepdims=True)
    m = jnp.where(jnp.isfinite(m), m, 0.0)
    p = jnp.exp(logits - m)
    return p / jnp.maximum(jnp.sum(p, axis=-1, keepdims=True), jnp.finfo(jnp.float32).tiny)


def retention_mixer(x, w_in, w_out):
    B, S, _ = x.shape
    h, dk, dv, C = RET_HEADS, RET_QK_DIM, RET_V_DIM, RET_CHUNK
    proj = x @ w_in
    q, k, v, g = jnp.split(proj, [h * dk, 2 * h * dk, 2 * h * dk + h * dv], axis=-1)
    q = q.reshape(B, S, h, dk).astype(jnp.float32)
    k = k.reshape(B, S, h, dk).astype(jnp.float32)
    v = v.reshape(B, S, h, dv).astype(jnp.float32)
    pos = jnp.arange(S, dtype=jnp.float32)
    theta = 1.0 / (10000.0 ** jnp.linspace(0.0, 1.0, dk // 2, dtype=jnp.float32))
    ang = pos[:, None] * theta[None, :]
    cos = jnp.cos(ang)[:, None, :]
    sin = jnp.sin(ang)[:, None, :]
    q = rotate_pairs(q, cos, sin)
    k = rotate_pairs(k, cos, sin) * (dk ** -0.5)
    log_gamma = jnp.log(1.0 - 2.0 ** (-5.0 - jnp.arange(h, dtype=jnp.float32)))
    idx = jnp.arange(C, dtype=jnp.float32)
    rel = idx[:, None] - idx[None, :]
    intra_decay = jnp.where(rel >= 0, jnp.exp(jnp.maximum(rel, 0.0) * log_gamma[:, None, None]), 0.0)
    q_decay = jnp.exp((idx + 1.0)[None, :] * log_gamma[:, None])
    k_decay = jnp.exp((C - 1.0 - idx)[None, :] * log_gamma[:, None])
    chunk_decay = jnp.exp(C * log_gamma)
    nC = S // C

    def to_chunks(t):
        return t.reshape(B, nC, C, h, t.shape[-1]).transpose(1, 0, 3, 2, 4)

    def step(state, inp):
        qc, kc, vc = inp
        scores = jnp.einsum('bhid,bhjd->bhij', qc, kc) * intra_decay
        o = (jnp.einsum('bhij,bhje->bhie', scores, vc)
             + jnp.einsum('bhid,bhde->bhie', qc * q_decay[..., None], state))
        state = (state * chunk_decay[:, None, None]
                 + jnp.einsum('bhjd,bhje->bhde', kc * k_decay[..., None], vc))
        return state, o

    state0 = jnp.zeros((B, h, dk, dv), jnp.float32)
    _, o = lax.scan(step, state0, (to_chunks(q), to_chunks(k), to_chunks(v)))
    o = o.transpose(1, 0, 3, 2, 4).reshape(B, S, h, dv)
    o = o * lax.rsqrt(jnp.mean(o * o, axis=-1, keepdims=True) + RMS_EPS)
    o = (o.reshape(B, S, h * dv) * jax.nn.silu(g.astype(jnp.float32))).astype(x.dtype)
    return o @ w_out


def nsa_shared_kv(h, kv_norm, kv_w, cmp_pe_k, cmp_w1_k, cmp_w2_k, cmp_pe_v, cmp_w1_v, cmp_w2_v):
    B, S, _ = h.shape
    G, d = NSA_GROUPS, NSA_HEAD_DIM
    kv = rms_norm(h, kv_norm) @ kv_w
    k_c, v_c, k_s, v_s, k_w, v_w = [t.reshape(B, S, G, d) for t in jnp.split(kv, 6, axis=-1)]
    n_cmp = (S - CMP_LEN) // CMP_STRIDE + 1
    blk_idx = np.arange(n_cmp)[:, None] * CMP_STRIDE + np.arange(CMP_LEN)[None, :]

    def compress(t, pe, w1, w2):
        blocks = t[:, blk_idx] + pe[:, None, :]
        blocks = blocks.transpose(0, 1, 3, 2, 4).reshape(B, n_cmp, G, CMP_LEN * d)
        return jax.nn.silu(blocks @ w1) @ w2

    kc = compress(k_c, cmp_pe_k, cmp_w1_k, cmp_w2_k)
    vc = compress(v_c, cmp_pe_v, cmp_w1_v, cmp_w2_v)
    n_sel = S // SEL_LEN
    ks = k_s.reshape(B, n_sel, SEL_LEN, G, d).transpose(0, 3, 1, 2, 4)
    vs = v_s.reshape(B, n_sel, SEL_LEN, G, d).transpose(0, 3, 1, 2, 4)
    pad = ((0, 0), (WINDOW, 0), (0, 0), (0, 0))
    kw = jnp.pad(k_w, pad)
    vw = jnp.pad(v_w, pad)
    return (kc, vc, ks, vs, kw, vw)


def nsa_mixer(x, w_in, w_out, rel_bias, shared):
    B, S, _ = x.shape
    H, G, R, d = NSA_HEADS, NSA_GROUPS, NSA_REP, NSA_HEAD_DIM
    Q = Q_BLOCK
    kc, vc, ks, vs, kw, vw = shared
    proj = x @ w_in
    q = proj[..., :H * d].reshape(B, S, G, R, d)
    gate = jax.nn.sigmoid(proj[..., H * d:].astype(jnp.float32)).reshape(B, S, G, R, 3)
    n_cmp = kc.shape[1]
    n_sel = ks.shape[2]
    n_top = min(SEL_TOPK, n_sel)
    cmp_end = jnp.arange(n_cmp, dtype=jnp.int32) * CMP_STRIDE + CMP_LEN - 1
    ratio = SEL_LEN // CMP_STRIDE
    lead = CMP_LEN // CMP_STRIDE - 1
    span = ratio + lead
    sel_map = ratio * np.arange(n_sel)[:, None] - lead + np.arange(span)[None, :]
    sel_map = np.where((sel_map >= 0) & (sel_map < n_cmp), sel_map, n_cmp)
    bias_tab = rel_bias.astype(jnp.float32)
    bias_gr = bias_tab.reshape(REL_BUCKETS, G, R).transpose(1, 0, 2)
    scale = d ** -0.5
    gather_blocks = jax.vmap(jax.vmap(lambda kb, ib: kb[ib]))
    kc32 = kc.astype(jnp.float32)
    vc32 = vc.astype(jnp.float32)
    sel_ids = jnp.arange(n_sel, dtype=jnp.int32)
    blk_start = sel_ids * SEL_LEN
    win_off = jnp.arange(WINDOW + Q, dtype=jnp.int32)
    g_ids = jnp.arange(G)[None, :, None, None, None]

    def block(qb):
        start = qb * Q
        t = start + jnp.arange(Q, dtype=jnp.int32)
        qblk = lax.dynamic_slice_in_dim(q, start, Q, 1).astype(jnp.float32) * scale
        gg = lax.dynamic_slice_in_dim(gate, start, Q, 1)
        dist_c = t[:, None] - cmp_end[None, :]
        s = jnp.einsum('bqgrd,bngd->bgrqn', qblk, kc32)
        s = s + bias_tab[t5_bucket(dist_c)].reshape(Q, n_cmp, G, R).transpose(2, 3, 0, 1)
        p_c = masked_softmax(s, dist_c >= 0)
        o_c = jnp.einsum('bgrqn,bngd->bqgrd', p_c, vc32)
        imp_c = jnp.sum(p_c, axis=2)
        imp_c = jnp.concatenate([imp_c, jnp.zeros(imp_c.shape[:-1] + (1,), imp_c.dtype)], axis=-1)
        imp = jnp.sum(imp_c[..., sel_map], axis=-1)
        cur = t // SEL_LEN
        valid = blk_start[None, :] <= t[:, None]
        forced = ((sel_ids[None, :] == 0) | (sel_ids[None, :] == cur[:, None])
                  | (sel_ids[None, :] == cur[:, None] - 1))
        imp = jnp.where(forced, jnp.inf, jnp.where(valid, imp, -jnp.inf))
        _, sel_idx = lax.top_k(imp, n_top)
        k_sel = gather_blocks(ks, sel_idx).astype(jnp.float32)
        v_sel = gather_blocks(vs, sel_idx).astype(jnp.float32)
        pos = sel_idx[..., None] * SEL_LEN + jnp.arange(SEL_LEN, dtype=jnp.int32)
        dist_s = t[None, None, :, None, None] - pos
        s = jnp.einsum('bqgrd,bgqnld->bgrqnl', qblk, k_sel)
        s = s + jnp.moveaxis(bias_gr[g_ids, t5_bucket(dist_s)], -1, 2)
        s = s.reshape(B, G, R, Q, n_top * SEL_LEN)
        mask_s = (dist_s >= 0).reshape(B, G, 1, Q, n_top * SEL_LEN)
        p_s = masked_softmax(s, mask_s).reshape(B, G, R, Q, n_top, SEL_LEN)
        o_s = jnp.einsum('bgrqnl,bgqnld->bqgrd', p_s, v_sel)
        kwin = lax.dynamic_slice_in_dim(kw, start, WINDOW + Q, 1).astype(jnp.float32)
        vwin = lax.dynamic_slice_in_dim(vw, start, WINDOW + Q, 1).astype(jnp.float32)
        kpos = start - WINDOW + win_off
        dist_w = t[:, None] - kpos[None, :]
        s = jnp.einsum('bqgrd,bkgd->bgrqk', qblk, kwin)
        s = s + bias_tab[t5_bucket(dist_w)].reshape(Q, WINDOW + Q, G, R).transpose(2, 3, 0, 1)
        mask_w = (dist_w >= 0) & (dist_w < WINDOW) & (kpos[None, :] >= 0)
        p_w = masked_softmax(s, mask_w)
        o_w = jnp.einsum('bgrqk,bkgd->bqgrd', p_w, vwin)
        o = gg[..., 0:1] * o_c + gg[..., 1:2] * o_s + gg[..., 2:3] * o_w
        return o.reshape(B, Q, H * d)

    o = lax.map(block, jnp.arange(S // Q, dtype=jnp.int32))
    o = o.transpose(1, 0, 2, 3).reshape(B, S, H * d).astype(x.dtype)
    return o @ w_out


def setup_inputs(seed: int = 0) -> dict:
    key = jax.random.key(seed)
    ks = jax.random.split(key, 24)
    f32 = jnp.float32

    def dense(k, shape, fan_in):
        return jax.random.normal(k, shape, f32) * (fan_in ** -0.5)

    def gain(k, shape):
        return 1.0 + 0.1 * jax.random.normal(k, shape, f32)

    d = NSA_HEAD_DIM
    return {
        "x": jax.random.normal(ks[0], (BATCH, SEQ, D_MODEL), f32),
        "mix_norm_pre": gain(ks[1], (DEPTH, D_MODEL)),
        "mix_norm_post": gain(ks[2], (DEPTH, D_MODEL)),
        "ffn_norm_pre": gain(ks[3], (DEPTH, D_MODEL)),
        "ffn_norm_post": gain(ks[4], (DEPTH, D_MODEL)),
        "ffn_w_in": dense(ks[5], (DEPTH, D_MODEL, 2 * FFN_HIDDEN), D_MODEL),
        "ffn_w_out": dense(ks[6], (DEPTH, FFN_HIDDEN, D_MODEL), FFN_HIDDEN),
        "ret_w_in": dense(ks[7], (N_A_LAYERS, D_MODEL, RET_IN), D_MODEL),
        "ret_w_out": dense(ks[8], (N_A_LAYERS, RET_HEADS * RET_V_DIM, D_MODEL), RET_HEADS * RET_V_DIM),
        "kv_norm": gain(ks[9], (D_MODEL,)),
        "kv_w": dense(ks[10], (D_MODEL, KV_SHARED_OUT), D_MODEL),
        "cmp_pe_k": 0.5 * jax.random.normal(ks[11], (CMP_LEN, d), f32),
        "cmp_w1_k": dense(ks[12], (CMP_LEN * d, CMP_HIDDEN), CMP_LEN * d),
        "cmp_w2_k": dense(ks[13], (CMP_HIDDEN, d), CMP_HIDDEN),
        "cmp_pe_v": 0.5 * jax.random.normal(ks[14], (CMP_LEN, d), f32),
        "cmp_w1_v": dense(ks[15], (CMP_LEN * d, CMP_HIDDEN), CMP_LEN * d),
        "cmp_w2_v": dense(ks[16], (CMP_HIDDEN, d), CMP_HIDDEN),
        "nsa_w_in": dense(ks[17], (N_B_LAYERS, D_MODEL, NSA_IN), D_MODEL),
        "nsa_w_out": dense(ks[18], (N_B_LAYERS, NSA_HEADS * d, D_MODEL), NSA_HEADS * d),
        "rel_bias": 0.5 * jax.random.normal(ks[19], (REL_BUCKETS, NSA_HEADS), f32),
    }


def reference(x, mix_norm_pre, mix_norm_post, ffn_norm_pre, ffn_norm_post, ffn_w_in, ffn_w_out,
              ret_w_in, ret_w_out, kv_norm, kv_w, cmp_pe_k, cmp_w1_k, cmp_w2_k,
              cmp_pe_v, cmp_w1_v, cmp_w2_v, nsa_w_in, nsa_w_out, rel_bias):
    h = x
    shared = None
    for layer in range(DEPTH):
        if layer == N_A_LAYERS:
            shared = nsa_shared_kv(h, kv_norm, kv_w, cmp_pe_k, cmp_w1_k, cmp_w2_k,
                                   cmp_pe_v, cmp_w1_v, cmp_w2_v)
        normed = rms_norm(h, mix_norm_pre[layer])
        if layer < N_A_LAYERS:
            mixed = retention_mixer(normed, ret_w_in[layer], ret_w_out[layer])
        else:
            j = layer - N_A_LAYERS
            mixed = nsa_mixer(normed, nsa_w_in[j], nsa_w_out[j], rel_bias, shared)
        h = h + rms_norm(mixed, mix_norm_post[layer])
        normed = rms_norm(h, ffn_norm_pre[layer])
        h = h + rms_norm(swiglu_ffn(normed, ffn_w_in[layer], ffn_w_out[layer]), ffn_norm_post[layer])
    return h
```

```python
import functools
import math

import numpy as np
import jax
import jax.numpy as jnp
from jax import lax
from jax.experimental import pallas as pl
from jax.experimental.pallas import tpu as pltpu

F32 = jnp.float32
BF16 = jnp.bfloat16

D_MODEL = 1024
RMS_EPS = 1e-6

RET_HEADS = 4
RET_QK = 256
RET_V = 512
RET_CHUNK = 128

FFN_HIDDEN = 2816

NSA_HEADS = 16
NSA_GROUPS = 4
NSA_REP = 4
NSA_D = 64
CMP_LEN = 32
CMP_STRIDE = 16
CMP_HIDDEN = 256
SEL_LEN = 64
SEL_TOPK = 16
WINDOW = 512
REL_BUCKETS = 32
REL_MAX_DIST = 128

Q_TILE = 256
K_TILE = 256
SEL_PER_GROUP = 16
V_ROWS = 80
NEG_BIG = -(2.0 ** 100)
M_INIT = -(2.0 ** 120)

VMEM_LIMIT = 56 * 1024 * 1024


def _cparams(sem, vmem=None):
    return pltpu.CompilerParams(dimension_semantics=sem, vmem_limit_bytes=vmem)


def _rms(x, g):
    return x * lax.rsqrt(jnp.mean(x * x, axis=-1, keepdims=True) + RMS_EPS) * g


def _norm_matmul_body(x_ref, g_ref, w_ref, o_ref, xn_ref):
    @pl.when(pl.program_id(1) == 0)
    def _():
        xn_ref[...] = _rms(x_ref[...], g_ref[...]).astype(BF16)

    o_ref[...] = jnp.dot(xn_ref[...], w_ref[...], preferred_element_type=F32).astype(o_ref.dtype)


def norm_matmul(x, g, w, *, tm, tn, out_dtype=F32):
    t, d = x.shape
    n = w.shape[1]
    return pl.pallas_call(
        _norm_matmul_body,
        out_shape=jax.ShapeDtypeStruct((t, n), out_dtype),
        grid=(t // tm, n // tn),
        in_specs=[
            pl.BlockSpec((tm, d), lambda i, j: (i, 0)),
            pl.BlockSpec((1, d), lambda i, j: (0, 0)),
            pl.BlockSpec((d, tn), lambda i, j: (0, j)),
        ],
        out_specs=pl.BlockSpec((tm, tn), lambda i, j: (i, j)),
        scratch_shapes=[pltpu.VMEM((tm, d), BF16)],
        compiler_params=_cparams(("parallel", "arbitrary"), VMEM_LIMIT),
        name="norm_matmul",
    )(x, g.reshape(1, d), w)


def _proj_norm_res_body(y_ref, w_ref, g_ref, r_ref, o_ref):
    z = jnp.dot(y_ref[...].astype(BF16), w_ref[...], preferred_element_type=F32)
    o_ref[...] = r_ref[...] + _rms(z, g_ref[...])


def proj_norm_res(y, w, g, res, *, tm):
    t, k = y.shape
    d = w.shape[1]
    return pl.pallas_call(
        _proj_norm_res_body,
        out_shape=jax.ShapeDtypeStruct((t, d), F32),
        grid=(t // tm,),
        in_specs=[
            pl.BlockSpec((tm, k), lambda i: (i, 0)),
            pl.BlockSpec((k, d), lambda i: (0, 0)),
            pl.BlockSpec((1, d), lambda i: (0, 0)),
            pl.BlockSpec((tm, d), lambda i: (i, 0)),
        ],
        out_specs=pl.BlockSpec((tm, d), lambda i: (i, 0)),
        compiler_params=_cparams(("parallel",), VMEM_LIMIT),
        name="proj_norm_res",
    )(y, w, g.reshape(1, d), res)


def _projT_norm_res_body(yT_ref, w_ref, g_ref, r_ref, o_ref):
    z = lax.dot_general(yT_ref[0], w_ref[...], (((0,), (0,)), ((), ())),
                        preferred_element_type=F32)
    o_ref[...] = r_ref[...] + _rms(z, g_ref[...])


def projT_norm_res(yT, w, g, res, *, tm):
    b, k, s = yT.shape
    d = w.shape[1]
    nt = s // tm
    return pl.pallas_call(
        _projT_norm_res_body,
        out_shape=jax.ShapeDtypeStruct((b * s, d), F32),
        grid=(b, nt),
        in_specs=[
            pl.BlockSpec((1, k, tm), lambda bi, i: (bi, 0, i)),
            pl.BlockSpec((k, d), lambda bi, i: (0, 0)),
            pl.BlockSpec((1, d), lambda bi, i: (0, 0)),
            pl.BlockSpec((tm, d), lambda bi, i: (bi * nt + i, 0)),
        ],
        out_specs=pl.BlockSpec((tm, d), lambda bi, i: (bi * nt + i, 0)),
        compiler_params=_cparams(("parallel", "parallel"), VMEM_LIMIT),
        name="projT_norm_res",
    )(yT, w, g.reshape(1, d), res)


def _ffn_body(x_ref, gpre_ref, wg_ref, wu_ref, wo_ref, gpost_ref, o_ref, xn_ref, acc_ref):
    j = pl.program_id(1)

    @pl.when(j == 0)
    def _():
        xn_ref[...] = _rms(x_ref[...], gpre_ref[...]).astype(BF16)

    xn = xn_ref[...]
    gate = jnp.dot(xn, wg_ref[...], preferred_element_type=F32)
    up = jnp.dot(xn, wu_ref[...], preferred_element_type=F32)
    act = (gate * jax.nn.sigmoid(gate) * up).astype(BF16)
    part = jnp.dot(act, wo_ref[...], preferred_element_type=F32)

    @pl.when(j == 0)
    def _():
        acc_ref[...] = part

    @pl.when(j > 0)
    def _():
        acc_ref[...] += part

    @pl.when(j == pl.num_programs(1) - 1)
    def _():
        o_ref[...] = x_ref[...] + _rms(acc_ref[...], gpost_ref[...])


def ffn_block(x, g_pre, w_in, w_out, g_post, *, tm, th):
    t, d = x.shape
    hdim = w_out.shape[0]
    nh = hdim // th
    return pl.pallas_call(
        _ffn_body,
        out_shape=jax.ShapeDtypeStruct((t, d), F32),
        grid=(t // tm, nh),
        in_specs=[
            pl.BlockSpec((tm, d), lambda i, j: (i, 0)),
            pl.BlockSpec((1, d), lambda i, j: (0, 0)),
            pl.BlockSpec((d, th), lambda i, j: (0, j)),
            pl.BlockSpec((d, th), lambda i, j: (0, nh + j)),
            pl.BlockSpec((th, d), lambda i, j: (j, 0)),
            pl.BlockSpec((1, d), lambda i, j: (0, 0)),
        ],
        out_specs=pl.BlockSpec((tm, d), lambda i, j: (i, 0)),
        scratch_shapes=[pltpu.VMEM((tm, d), BF16), pltpu.VMEM((tm, d), F32)],
        compiler_params=_cparams(("parallel", "arbitrary"), VMEM_LIMIT),
        name="ffn_block",
    )(x, g_pre.reshape(1, d), w_in, w_in, w_out, g_post.reshape(1, d))


def _retention_body(q_ref, k_ref, v_ref, g_ref, cos_ref, sin_ref, dmat_ref, qdec_ref, kdec_ref,
                    cdec_ref, o_ref, state_ref):
    @pl.when(pl.program_id(1) == 0)
    def _():
        state_ref[...] = jnp.zeros_like(state_ref)

    cos = cos_ref[...]
    sin = sin_ref[...]
    half = RET_QK // 2

    def rotate(x_ref, h):
        x1 = x_ref[:, h * RET_QK:h * RET_QK + half]
        x2 = x_ref[:, h * RET_QK + half:(h + 1) * RET_QK]
        return jnp.concatenate([x1 * cos - x2 * sin, x1 * sin + x2 * cos], axis=1)

    for h in range(RET_HEADS):
        qr = rotate(q_ref, h)
        kr = rotate(k_ref, h) * (RET_QK ** -0.5)
        v = v_ref[:, h * RET_V:(h + 1) * RET_V].astype(BF16)
        scores = lax.dot_general(qr.astype(BF16), kr.astype(BF16), (((1,), (1,)), ((), ())),
                                 preferred_element_type=F32) * dmat_ref[h]
        state = state_ref[h]
        o = (jnp.dot(scores.astype(BF16), v, preferred_element_type=F32)
             + jnp.dot((qr * qdec_ref[h]).astype(BF16), state.astype(BF16),
                       preferred_element_type=F32))
        kd = (kr * kdec_ref[h]).astype(BF16)
        state_ref[h] = state * cdec_ref[h, 0:1, :] + lax.dot_general(
            kd, v, (((0,), (0,)), ((), ())), preferred_element_type=F32)
        o = o * lax.rsqrt(jnp.mean(o * o, axis=-1, keepdims=True) + RMS_EPS)
        gate = g_ref[:, h * RET_V:(h + 1) * RET_V]
        o_ref[:, h * RET_V:(h + 1) * RET_V] = (o * (gate * jax.nn.sigmoid(gate))).astype(o_ref.dtype)


def retention_core(proj, cos, sin, dmat, qdec, kdec, cdec, *, batch, seq):
    c = RET_CHUNK
    nc = seq // c
    hq = RET_HEADS * RET_QK
    hv = RET_HEADS * RET_V
    return pl.pallas_call(
        _retention_body,
        out_shape=jax.ShapeDtypeStruct((batch * seq, hv), BF16),
        grid=(batch, nc),
        in_specs=[
            pl.BlockSpec((c, hq), lambda b, t: (b * nc + t, 0)),
            pl.BlockSpec((c, hq), lambda b, t: (b * nc + t, 1)),
            pl.BlockSpec((c, hv), lambda b, t: (b * nc + t, 1)),
            pl.BlockSpec((c, hv), lambda b, t: (b * nc + t, 2)),
            pl.BlockSpec((c, RET_QK // 2), lambda b, t: (t, 0)),
            pl.BlockSpec((c, RET_QK // 2), lambda b, t: (t, 0)),
            pl.BlockSpec((RET_HEADS, c, c), lambda b, t: (0, 0, 0)),
            pl.BlockSpec((RET_HEADS, c, RET_QK), lambda b, t: (0, 0, 0)),
            pl.BlockSpec((RET_HEADS, c, RET_QK), lambda b, t: (0, 0, 0)),
            pl.BlockSpec((RET_HEADS, 8, RET_V), lambda b, t: (0, 0, 0)),
        ],
        out_specs=pl.BlockSpec((c, hv), lambda b, t: (b * nc + t, 0)),
        scratch_shapes=[pltpu.VMEM((RET_HEADS, RET_QK, RET_V), F32)],
        compiler_params=_cparams(("parallel", "arbitrary"), VMEM_LIMIT),
        name="retention_core",
    )(proj, proj, proj, proj, cos, sin, dmat, qdec, kdec, cdec)


def _retention_tables(seq):
    h, dk, c = RET_HEADS, RET_QK, RET_CHUNK
    pos = jnp.arange(seq, dtype=F32)
    theta = 1.0 / (10000.0 ** jnp.linspace(0.0, 1.0, dk // 2, dtype=F32))
    ang = pos[:, None] * theta[None, :]
    log_gamma = jnp.log(1.0 - 2.0 ** (-5.0 - jnp.arange(h, dtype=F32)))
    idx = jnp.arange(c, dtype=F32)
    rel = idx[:, None] - idx[None, :]
    dmat = jnp.where(rel >= 0, jnp.exp(jnp.maximum(rel, 0.0) * log_gamma[:, None, None]), 0.0)
    qdec = jnp.exp((idx + 1.0)[None, :] * log_gamma[:, None])
    kdec = jnp.exp((c - 1.0 - idx)[None, :] * log_gamma[:, None])
    cdec = jnp.exp(c * log_gamma)
    qdec = jnp.broadcast_to(qdec[:, :, None], (h, c, dk))
    kdec = jnp.broadcast_to(kdec[:, :, None], (h, c, dk))
    cdec = jnp.broadcast_to(cdec[:, None, None], (h, 8, RET_V))
    return jnp.cos(ang), jnp.sin(ang), dmat, qdec, kdec, cdec


def _deinterleave_qk_columns(w):
    nqk = 2 * RET_HEADS * RET_QK
    perm = []
    for h in range(2 * RET_HEADS):
        base = h * RET_QK
        perm += [base + 2 * i for i in range(RET_QK // 2)]
        perm += [base + 2 * i + 1 for i in range(RET_QK // 2)]
    perm = np.asarray(perm + list(range(nqk, w.shape[1])), dtype=np.int32)
    return w[:, perm]


def _kv_body(x_ref, g_ref, wk_ref, wvT_ref, kc_ref, vc_ref, ks_ref, kw_ref, vsT_ref, vwT_ref):
    xn = _rms(x_ref[...], g_ref[...]).astype(BF16)
    tm = xn.shape[0]
    kall = jnp.dot(xn, wk_ref[...], preferred_element_type=F32)
    gd = NSA_GROUPS * NSA_D
    kc_ref[...] = kall[:, 0:gd]
    vc_ref[...] = kall[:, gd:2 * gd]
    row = pl.program_id(1) * tm + lax.broadcasted_iota(jnp.int32, (tm, 128), 0)
    lane = lax.broadcasted_iota(jnp.int32, (tm, 128), 1)
    blk = (row // SEL_LEN) % SEL_PER_GROUP
    onehot = jnp.where(lane - NSA_D == blk, 1.0, 0.0).astype(F32)
    for g in range(NSA_GROUPS):
        ks = kall[:, 2 * gd + 128 * g:2 * gd + 128 * (g + 1)]
        ks_ref[0, g] = (ks + onehot).astype(BF16)
        kw = kall[:, 2 * gd + 512 + 128 * g:2 * gd + 512 + 128 * (g + 1)]
        kw_ref[0, g] = kw.astype(BF16)
    vT = lax.dot_general(wvT_ref[...], xn, (((1,), (1,)), ((), ())),
                         preferred_element_type=F32)
    extra = jnp.where(lax.broadcasted_iota(jnp.int32, (V_ROWS - NSA_D, K_TILE), 0) == 0, 1.0, 0.0)
    extra = extra.astype(BF16)
    for g in range(NSA_GROUPS):
        for c in range(tm // K_TILE):
            cols = slice(c * K_TILE, (c + 1) * K_TILE)
            vsT_ref[0, g, c, 0:NSA_D, :] = vT[NSA_D * g:NSA_D * (g + 1), cols].astype(BF16)
            vsT_ref[0, g, c, NSA_D:V_ROWS, :] = extra
            vwT_ref[0, g, c, 0:NSA_D, :] = vT[gd + NSA_D * g:gd + NSA_D * (g + 1), cols].astype(BF16)
            vwT_ref[0, g, c, NSA_D:V_ROWS, :] = extra


def kv_project(x, g, kv_w, *, batch, seq, tm):
    d = x.shape[1]
    gd = NSA_GROUPS * NSA_D
    k_c, v_c, k_s, v_s, k_w, v_w = [kv_w[:, i * gd:(i + 1) * gd] for i in range(6)]

    def pad_groups(w):
        w = w.reshape(d, NSA_GROUPS, NSA_D)
        return jnp.pad(w, ((0, 0), (0, 0), (0, 128 - NSA_D))).reshape(d, NSA_GROUPS * 128)

    wk = jnp.concatenate([k_c, v_c, pad_groups(k_s), pad_groups(k_w)], axis=1).astype(BF16)
    wvT = jnp.concatenate([v_s, v_w], axis=1).T.astype(BF16)
    nt = seq // tm
    G = NSA_GROUPS
    return pl.pallas_call(
        _kv_body,
        out_shape=(
            jax.ShapeDtypeStruct((batch * seq, gd), F32),
            jax.ShapeDtypeStruct((batch * seq, gd), F32),
            jax.ShapeDtypeStruct((batch, G, seq, 128), BF16),
            jax.ShapeDtypeStruct((batch, G, seq, 128), BF16),
            jax.ShapeDtypeStruct((batch, G, seq // K_TILE, V_ROWS, K_TILE), BF16),
            jax.ShapeDtypeStruct((batch, G, seq // K_TILE, V_ROWS, K_TILE), BF16),
        ),
        grid=(batch, nt),
        in_specs=[
            pl.BlockSpec((tm, d), lambda b, i: (b * nt + i, 0)),
            pl.BlockSpec((1, d), lambda b, i: (0, 0)),
            pl.BlockSpec(wk.shape, lambda b, i: (0, 0)),
            pl.BlockSpec(wvT.shape, lambda b, i: (0, 0)),
        ],
        out_specs=(
            pl.BlockSpec((tm, gd), lambda b, i: (b * nt + i, 0)),
            pl.BlockSpec((tm, gd), lambda b, i: (b * nt + i, 0)),
            pl.BlockSpec((1, G, tm, 128), lambda b, i: (b, 0, i, 0)),
            pl.BlockSpec((1, G, tm, 128), lambda b, i: (b, 0, i, 0)),
            pl.BlockSpec((1, G, tm // K_TILE, V_ROWS, K_TILE), lambda b, i: (b, 0, i, 0, 0)),
            pl.BlockSpec((1, G, tm // K_TILE, V_ROWS, K_TILE), lambda b, i: (b, 0, i, 0, 0)),
        ),
        compiler_params=_cparams(("parallel", "parallel"), VMEM_LIMIT),
        name="kv_project",
    )(x, g.reshape(1, d), wk, wvT)


def _compress_body(c_ref, pe_ref, w1_ref, w2_ref, w2T_ref, nat_ref, tr_ref, sh_ref):
    half = CMP_STRIDE * NSA_D
    nc = c_ref.shape[3]
    c = c_ref[0, 0, 0].astype(BF16)
    w1 = w1_ref[0]
    first = jnp.dot(c, w1[0:half], preferred_element_type=F32)
    second = jnp.dot(c, w1[half:2 * half], preferred_element_type=F32)
    pe_term = jnp.dot(pe_ref[0].astype(BF16), w1, preferred_element_type=F32)
    sh_ref[0:nc, :] = second
    sh_ref[nc:nc + 8, :] = jnp.zeros((8, CMP_HIDDEN), F32)
    pre = first + sh_ref[1:nc + 1, :] + pe_term[0:1, :]
    hid = (pre * jax.nn.sigmoid(pre)).astype(BF16)
    nat_ref[0, 0, 0] = jnp.dot(hid, w2_ref[0], preferred_element_type=F32).astype(BF16)
    tr_ref[0, 0, 0] = lax.dot_general(w2T_ref[0], hid, (((1,), (1,)), ((), ())),
                                      preferred_element_type=F32).astype(BF16)


def compress_blocks(kc_nat, vc_nat, pe_k, w1_k, w2_k, pe_v, w1_v, w2_v, *, batch, seq):
    G, d = NSA_GROUPS, NSA_D
    nc = seq // CMP_STRIDE

    def to_rows(t):
        t = t.reshape(batch, nc, CMP_STRIDE, G, d).transpose(0, 3, 1, 2, 4)
        return t.reshape(batch, G, nc, CMP_STRIDE * d)

    c_all = jnp.stack([to_rows(kc_nat), to_rows(vc_nat)])
    pe = jnp.stack([pe_k.reshape(1, -1), pe_v.reshape(1, -1)])
    pe = jnp.broadcast_to(pe, (2, 8, CMP_LEN * d))
    w1 = jnp.stack([w1_k, w1_v]).astype(BF16)
    w2 = jnp.stack([w2_k, w2_v]).astype(BF16)
    w2T = jnp.stack([w2_k.T, w2_v.T]).astype(BF16)
    return pl.pallas_call(
        _compress_body,
        out_shape=(
            jax.ShapeDtypeStruct((2, batch, G, nc, d), BF16),
            jax.ShapeDtypeStruct((2, batch, G, d, nc), BF16),
        ),
        grid=(2, batch, G),
        in_specs=[
            pl.BlockSpec((1, 1, 1, nc, CMP_STRIDE * d), lambda w, b, g: (w, b, g, 0, 0)),
            pl.BlockSpec((1, 8, CMP_LEN * d), lambda w, b, g: (w, 0, 0)),
            pl.BlockSpec((1, CMP_LEN * d, CMP_HIDDEN), lambda w, b, g: (w, 0, 0)),
            pl.BlockSpec((1, CMP_HIDDEN, d), lambda w, b, g: (w, 0, 0)),
            pl.BlockSpec((1, d, CMP_HIDDEN), lambda w, b, g: (w, 0, 0)),
        ],
        out_specs=(
            pl.BlockSpec((1, 1, 1, nc, d), lambda w, b, g: (w, b, g, 0, 0)),
            pl.BlockSpec((1, 1, 1, d, nc), lambda w, b, g: (w, b, g, 0, 0)),
        ),
        scratch_shapes=[pltpu.VMEM((nc + 8, CMP_HIDDEN), F32)],
        compiler_params=_cparams(("parallel", "parallel", "parallel"), VMEM_LIMIT),
        name="compress_blocks",
    )(c_all, pe, w1, w2, w2T)


GATE_ROWS = 16


def _qproj_body(x_ref, g_ref, wT_ref, qT_ref, gT_ref):
    xn = _rms(x_ref[...], g_ref[...]).astype(BF16)
    tm = xn.shape[0]
    pT = lax.dot_general(wT_ref[...], xn, (((1,), (1,)), ((), ())),
                         preferred_element_type=F32)
    hd = NSA_HEADS * NSA_D
    q = pT[0:hd] * (NSA_D ** -0.5)
    qT_ref[0] = q.reshape(NSA_HEADS, NSA_D, tm).astype(BF16)
    gates = jax.nn.sigmoid(pT[hd:hd + NSA_GROUPS * GATE_ROWS])
    gT_ref[0] = gates.reshape(NSA_GROUPS, GATE_ROWS, tm)


def q_project(x, g, w_in, *, batch, seq, tm):
    d = x.shape[1]
    hd = NSA_HEADS * NSA_D
    per_group = NSA_REP * 3
    wg = w_in[:, hd:].reshape(d, NSA_GROUPS, per_group)
    wg = jnp.pad(wg, ((0, 0), (0, 0), (0, GATE_ROWS - per_group))).reshape(d, NSA_GROUPS * GATE_ROWS)
    wT = jnp.concatenate([w_in[:, :hd], wg], axis=1).T.astype(BF16)
    nt = seq // tm
    return pl.pallas_call(
        _qproj_body,
        out_shape=(
            jax.ShapeDtypeStruct((batch, NSA_HEADS, NSA_D, seq), BF16),
            jax.ShapeDtypeStruct((batch, NSA_GROUPS, GATE_ROWS, seq), F32),
        ),
        grid=(batch, nt),
        in_specs=[
            pl.BlockSpec((tm, d), lambda b, i: (b * nt + i, 0)),
            pl.BlockSpec((1, d), lambda b, i: (0, 0)),
            pl.BlockSpec(wT.shape, lambda b, i: (0, 0)),
        ],
        out_specs=(
            pl.BlockSpec((1, NSA_HEADS, NSA_D, tm), lambda b, i: (b, 0, 0, i)),
            pl.BlockSpec((1, NSA_GROUPS, GATE_ROWS, tm), lambda b, i: (b, 0, 0, i)),
        ),
        compiler_params=_cparams(("parallel", "parallel"), VMEM_LIMIT),
        name="q_project",
    )(x, g.reshape(1, d), wT)


CMP_CHUNK = 256
CMP_PER_QTILE = Q_TILE // CMP_STRIDE
SEL_PER_QTILE = Q_TILE // SEL_LEN
LANES = NSA_REP * Q_TILE


def _nsa_body(qT_ref, gT_ref, kc_ref, vcT_ref, ks_ref, vsT_ref, kw_ref, vwT_ref,
              bc_ref, bs_ref, bw_ref, mmapT_ref, oT_ref,
              s_ref, imp_ref, msel_ref, qaug_ref, acc_ref, m_ref, out_ref):
    i = pl.program_id(2)
    ns = imp_ref.shape[0]
    qT = jnp.concatenate([qT_ref[0, r] for r in range(NSA_REP)], axis=1)
    tiny = jnp.finfo(F32).tiny

    def gate_row(j):
        return jnp.concatenate([gT_ref[0, 0, 3 * r + j:3 * r + j + 1, :] for r in range(NSA_REP)], axis=1)

    nchunks = i // (CMP_CHUNK // CMP_PER_QTILE) + 1
    visible = CMP_PER_QTILE * (i + 1)

    def chunk_rows(c):
        return pl.ds(pl.multiple_of(c * CMP_CHUNK, CMP_CHUNK), CMP_CHUNK)

    def cmp_scores(c, carry):
        s_ref[chunk_rows(c), :] = jnp.dot(kc_ref[0, 0, 0, chunk_rows(c), :], qT,
                                          preferred_element_type=F32)
        return carry

    lax.fori_loop(0, nchunks, cmp_scores, 0)

    @pl.when(i == 0)
    def _():
        s_ref[0:CMP_PER_QTILE, :] = s_ref[0:CMP_PER_QTILE, :] + bc_ref[0, CMP_PER_QTILE:2 * CMP_PER_QTILE, :]

    @pl.when(i > 0)
    def _():
        rows = pl.ds(pl.multiple_of(CMP_PER_QTILE * (i - 1), CMP_PER_QTILE), 2 * CMP_PER_QTILE)
        s_ref[rows, :] = s_ref[rows, :] + bc_ref[0]

    def cmp_max(c, m):
        rid = c * CMP_CHUNK + lax.broadcasted_iota(jnp.int32, (CMP_CHUNK, LANES), 0)
        s = jnp.where(rid < visible, s_ref[chunk_rows(c), :], -jnp.inf)
        s_ref[chunk_rows(c), :] = s
        return jnp.maximum(m, jnp.max(s, axis=0, keepdims=True))

    m_c = lax.fori_loop(0, nchunks, cmp_max, jnp.full((1, LANES), -jnp.inf, F32))
    m_c = jnp.where(jnp.isfinite(m_c), m_c, 0.0)

    def cmp_exp(c, l):
        p = jnp.exp(s_ref[chunk_rows(c), :] - m_c)
        s_ref[chunk_rows(c), :] = p
        return l + jnp.sum(p, axis=0, keepdims=True)

    l_c = lax.fori_loop(0, nchunks, cmp_exp, jnp.zeros((1, LANES), F32))
    inv_c = 1.0 / jnp.maximum(l_c, tiny)

    acc_ref[...] = jnp.zeros_like(acc_ref)
    imp_ref[...] = jnp.zeros_like(imp_ref)

    def cmp_pv(c, carry):
        p = s_ref[chunk_rows(c), :] * inv_c
        acc_ref[0:NSA_D, :] += jnp.dot(vcT_ref[0, 0, c], p.astype(BF16), preferred_element_type=F32)
        ic = p[:, 0:Q_TILE]
        for r in range(1, NSA_REP):
            ic = ic + p[:, r * Q_TILE:(r + 1) * Q_TILE]
        hi = ic.astype(BF16)
        rem = ic - hi.astype(F32)
        mid = rem.astype(BF16)
        lo = (rem - mid.astype(F32)).astype(BF16)
        mm = mmapT_ref[c]
        imp_ref[...] += (jnp.dot(mm, hi, preferred_element_type=F32)
                         + jnp.dot(mm, mid, preferred_element_type=F32)
                         + jnp.dot(mm, lo, preferred_element_type=F32))
        return carry

    lax.fori_loop(0, nchunks, cmp_pv, 0)
    out_ref[...] = gate_row(0) * acc_ref[0:NSA_D, :]

    jrow = lax.broadcasted_iota(jnp.int32, (ns, Q_TILE), 0)
    col = lax.broadcasted_iota(jnp.int32, (ns, Q_TILE), 1)
    cur = SEL_PER_QTILE * i + col // SEL_LEN
    valid = jrow <= cur
    forced = (jrow == 0) | (jrow == cur) | (jrow == cur - 1)
    jf = jrow.astype(F32)
    v0 = jnp.where(valid & jnp.logical_not(forced), imp_ref[...], -jnp.inf)
    sel0 = jnp.where(forced, 1.0, 0.0)

    def pick(_, carry):
        v, sel = carry
        best = jnp.max(v, axis=0, keepdims=True)
        first = jnp.min(jnp.where(v == best, jf, float(ns)), axis=0, keepdims=True)
        hit = jf == first
        return jnp.where(hit, -jnp.inf, v), jnp.where(hit, 1.0, sel)

    _, sel = lax.fori_loop(0, SEL_TOPK - 3, pick, (v0, sel0))
    mask_bias = jnp.where((sel > 0.0) & valid, 0.0, NEG_BIG).astype(BF16)
    msel_ref[...] = jnp.concatenate([mask_bias] * NSA_REP, axis=1)

    qaug_ref[0:NSA_D, :] = qT
    qaug_ref[NSA_D:, :] = jnp.zeros((qaug_ref.shape[0] - NSA_D, LANES), BF16)

    def reset():
        acc_ref[...] = jnp.zeros_like(acc_ref)
        m_ref[...] = jnp.full(m_ref.shape, M_INIT, F32)

    def tile(kt, k_ref, vT_ref, bias):
        rows = pl.ds(pl.multiple_of(kt * K_TILE, K_TILE), K_TILE)
        s = jnp.dot(k_ref[0, 0, rows, :], qaug_ref[...], preferred_element_type=F32)
        if bias is not None:
            s = s + bias
        m_old = m_ref[0:1, :]
        m_new = jnp.maximum(m_old, jnp.max(s, axis=0, keepdims=True))
        alpha = jnp.exp(m_old - m_new)
        p = jnp.exp(s - m_new).astype(BF16)
        acc_ref[...] = alpha * acc_ref[...] + jnp.dot(vT_ref[0, 0, kt], p, preferred_element_type=F32)
        m_ref[0:1, :] = m_new

    def finish(j):
        l = jnp.maximum(acc_ref[NSA_D:NSA_D + 1, :], tiny)
        out_ref[...] += (gate_row(j) * (1.0 / l)) * acc_ref[0:NSA_D, :]

    def load_block_mask(kt):
        grp = pl.ds(pl.multiple_of((kt // (SEL_PER_GROUP * SEL_LEN // K_TILE)) * SEL_PER_GROUP,
                                   SEL_PER_GROUP), SEL_PER_GROUP)
        qaug_ref[NSA_D:NSA_D + SEL_PER_GROUP, :] = msel_ref[grp, :]

    reset()

    def far(kt, carry):
        load_block_mask(kt)
        tile(kt, ks_ref, vsT_ref, None)
        return carry

    lax.fori_loop(0, jnp.maximum(i - 1, 0), far, 0)

    @pl.when(i >= 1)
    def _():
        load_block_mask(i - 1)
        tile(i - 1, ks_ref, vsT_ref, bs_ref[0, 0:K_TILE, :])

    load_block_mask(i)
    tile(i, ks_ref, vsT_ref, bs_ref[0, K_TILE:2 * K_TILE, :])
    finish(1)

    reset()

    @pl.when(i >= 2)
    def _():
        tile(i - 2, kw_ref, vwT_ref, bw_ref[0, 0:K_TILE, :])

    @pl.when(i >= 1)
    def _():
        tile(i - 1, kw_ref, vwT_ref, bw_ref[0, K_TILE:2 * K_TILE, :])

    tile(i, kw_ref, vwT_ref, bw_ref[0, 2 * K_TILE:3 * K_TILE, :])
    finish(2)

    for r in range(NSA_REP):
        oT_ref[0, r * NSA_D:(r + 1) * NSA_D, :] = out_ref[:, r * Q_TILE:(r + 1) * Q_TILE].astype(oT_ref.dtype)


def nsa_attention(qT, gT, cmp_nat, cmp_tr, ks, vsT, kw, vwT, bc, bs, bw, mmapT, *, batch, seq):
    G = NSA_GROUPS
    nc = seq // CMP_STRIDE
    ns = seq // SEL_LEN
    nq = seq // Q_TILE
    nkt = seq // K_TILE
    hd = NSA_HEADS * NSA_D
    vcT = cmp_tr[1].reshape(batch, G, NSA_D, nc // CMP_CHUNK, CMP_CHUNK).transpose(0, 1, 3, 2, 4)
    return pl.pallas_call(
        _nsa_body,
        out_shape=jax.ShapeDtypeStruct((batch, hd, seq), BF16),
        grid=(batch, G, nq),
        in_specs=[
            pl.BlockSpec((1, NSA_REP, NSA_D, Q_TILE), lambda b, g, i: (b, g, 0, i)),
            pl.BlockSpec((1, 1, GATE_ROWS, Q_TILE), lambda b, g, i: (b, g, 0, i)),
            pl.BlockSpec((1, 1, 1, nc, NSA_D), lambda b, g, i: (0, b, g, 0, 0)),
            pl.BlockSpec((1, 1, nc // CMP_CHUNK, NSA_D, CMP_CHUNK), lambda b, g, i: (b, g, 0, 0, 0)),
            pl.BlockSpec((1, 1, seq, 128), lambda b, g, i: (b, g, 0, 0)),
            pl.BlockSpec((1, 1, nkt, V_ROWS, K_TILE), lambda b, g, i: (b, g, 0, 0, 0)),
            pl.BlockSpec((1, 1, seq, 128), lambda b, g, i: (b, g, 0, 0)),
            pl.BlockSpec((1, 1, nkt, V_ROWS, K_TILE), lambda b, g, i: (b, g, 0, 0, 0)),
            pl.BlockSpec((1, 2 * CMP_PER_QTILE, LANES), lambda b, g, i: (g, 0, 0)),
            pl.BlockSpec((1, 2 * K_TILE, LANES), lambda b, g, i: (g, 0, 0)),
            pl.BlockSpec((1, 3 * K_TILE, LANES), lambda b, g, i: (g, 0, 0)),
            pl.BlockSpec((nc // CMP_CHUNK, ns, CMP_CHUNK), lambda b, g, i: (0, 0, 0)),
        ],
        out_specs=pl.BlockSpec((1, NSA_REP * NSA_D, Q_TILE), lambda b, g, i: (b, g, i)),
        scratch_shapes=[
            pltpu.VMEM((nc, LANES), F32),
            pltpu.VMEM((ns, Q_TILE), F32),
            pltpu.VMEM((ns, LANES), BF16),
            pltpu.VMEM((128, LANES), BF16),
            pltpu.VMEM((V_ROWS, LANES), F32),
            pltpu.VMEM((8, LANES), F32),
            pltpu.VMEM((NSA_D, LANES), F32),
        ],
        compiler_params=_cparams(("parallel", "parallel", "arbitrary"), VMEM_LIMIT),
        name="nsa_attention",
    )(qT, gT, cmp_nat, vcT, ks, vsT, kw, vwT, bc, bs, bw, mmapT)


def _t5_bucket_table(n_max):
    n = np.arange(n_max)
    max_exact = REL_BUCKETS // 2
    nf = np.maximum(n, 1).astype(np.float32)
    large = max_exact + (np.log(nf / np.float32(max_exact))
                         / np.float32(math.log(REL_MAX_DIST / max_exact))
                         * np.float32(REL_BUCKETS - max_exact)).astype(np.int32)
    large = np.minimum(large, REL_BUCKETS - 1)
    return np.where(n < max_exact, n, large).astype(np.int32)


def _bias_tables(rel_bias):
    n_max = WINDOW + Q_TILE
    buckets = _t5_bucket_table(n_max)
    assert (buckets[Q_TILE - CMP_LEN + 1:] == REL_BUCKETS - 1).all()
    tab = rel_bias.astype(F32)[buckets, :] - rel_bias.astype(F32)[REL_BUCKETS - 1][None, :]
    tab = tab.T
    c = np.arange(Q_TILE)[None, :]

    def slab(dist, ok, fill):
        vals = jnp.where(ok[None], tab[:, np.clip(dist, 0, n_max - 1)], fill)
        rows = dist.shape[0]
        vals = vals.reshape(NSA_GROUPS, NSA_REP, rows, Q_TILE).transpose(0, 2, 1, 3)
        return vals.reshape(NSA_GROUPS, rows, LANES)

    r = np.arange(2 * CMP_PER_QTILE)[:, None]
    dist_c = c + (Q_TILE - CMP_LEN + 1) - CMP_STRIDE * r
    bc = slab(dist_c, dist_c >= 0, -jnp.inf)
    r = np.arange(2 * K_TILE)[:, None]
    dist_s = c + K_TILE - r
    bs = slab(dist_s, dist_s >= 0, NEG_BIG)
    r = np.arange(3 * K_TILE)[:, None]
    dist_w = c + 2 * K_TILE - r
    bw = slab(dist_w, (dist_w >= 0) & (dist_w < WINDOW), NEG_BIG)
    return bc, bs, bw


def _selection_map(seq):
    nc = seq // CMP_STRIDE
    ns = seq // SEL_LEN
    n_cmp = (seq - CMP_LEN) // CMP_STRIDE + 1
    ratio = SEL_LEN // CMP_STRIDE
    lead = CMP_LEN // CMP_STRIDE - 1
    j = np.arange(ns)[:, None]
    n = np.arange(nc)[None, :]
    m = ((n >= ratio * j - lead) & (n < ratio * j + ratio) & (n < n_cmp)).astype(np.float32)
    m = m.reshape(ns, nc // CMP_CHUNK, CMP_CHUNK).transpose(1, 0, 2)
    return jnp.asarray(m, dtype=BF16)


def kernel(x, mix_norm_pre, mix_norm_post, ffn_norm_pre, ffn_norm_post, ffn_w_in, ffn_w_out,
           ret_w_in, ret_w_out, kv_norm, kv_w, cmp_pe_k, cmp_w1_k, cmp_w2_k,
           cmp_pe_v, cmp_w1_v, cmp_w2_v, nsa_w_in, nsa_w_out, rel_bias):
    batch, seq, d = x.shape
    n_ret = ret_w_in.shape[0]
    n_nsa = nsa_w_in.shape[0]
    assert seq % (SEL_PER_GROUP * SEL_LEN) == 0 and seq // SEL_LEN >= SEL_TOPK
    h = x.reshape(batch * seq, d)
    shared = None
    for layer in range(n_ret + n_nsa):
        if layer == n_ret:
            kc_nat, vc_nat, ks, kw, vsT, vwT = kv_project(h, kv_norm, kv_w, batch=batch, seq=seq, tm=1024)
            cmp_nat, cmp_tr = compress_blocks(kc_nat, vc_nat, cmp_pe_k, cmp_w1_k, cmp_w2_k,
                                              cmp_pe_v, cmp_w1_v, cmp_w2_v, batch=batch, seq=seq)
            shared = (cmp_nat, cmp_tr, ks, vsT, kw, vwT) + _bias_tables(rel_bias) + (_selection_map(seq),)
        if layer < n_ret:
            w_in = _deinterleave_qk_columns(ret_w_in[layer]).astype(BF16)
            proj = norm_matmul(h, mix_norm_pre[layer], w_in, tm=512, tn=1536)
            mixed = retention_core(proj, *_retention_tables(seq), batch=batch, seq=seq)
            h = proj_norm_res(mixed, ret_w_out[layer].astype(BF16), mix_norm_post[layer], h, tm=512)
        else:
            j = layer - n_ret
            qT, gT = q_project(h, mix_norm_pre[layer], nsa_w_in[j], batch=batch, seq=seq, tm=512)
            oT = nsa_attention(qT, gT, *shared, batch=batch, seq=seq)
            h = projT_norm_res(oT, nsa_w_out[j].astype(BF16), mix_norm_post[layer], h, tm=512)
        h = ffn_block(h, ffn_norm_pre[layer], ffn_w_in[layer].astype(BF16), ffn_w_out[layer].astype(BF16),
                      ffn_norm_post[layer], tm=512, th=FFN_HIDDEN // 2)
    return h.reshape(batch, seq, d)
```

```python
import functools
import math

import numpy as np
import jax
import jax.numpy as jnp
from jax import lax
from jax.experimental import pallas as pl
from jax.experimental.pallas import tpu as pltpu

F32 = jnp.float32
BF16 = jnp.bfloat16

D_MODEL = 1024
RMS_EPS = 1e-6

RET_HEADS = 4
RET_QK = 256
RET_V = 512
RET_CHUNK = 128

FFN_HIDDEN = 2816

NSA_HEADS = 16
NSA_GROUPS = 4
NSA_REP = 4
NSA_D = 64
CMP_LEN = 32
CMP_STRIDE = 16
CMP_HIDDEN = 256
SEL_LEN = 64
SEL_TOPK = 16
WINDOW = 512
REL_BUCKETS = 32
REL_MAX_DIST = 128

Q_TILE = 256
K_TILE = 256
SEL_PER_GROUP = 16
V_ROWS = 80
NEG_BIG = -(2.0 ** 100)
M_INIT = -(2.0 ** 120)

VMEM_LIMIT = 56 * 1024 * 1024


def _cparams(sem, vmem=None):
    return pltpu.CompilerParams(dimension_semantics=sem, vmem_limit_bytes=vmem)


def _rms(x, g):
    return x * lax.rsqrt(jnp.mean(x * x, axis=-1, keepdims=True) + RMS_EPS) * g


def _norm_matmul_body(x_ref, g_ref, w_ref, o_ref, xn_ref):
    @pl.when(pl.program_id(1) == 0)
    def _():
        xn_ref[...] = _rms(x_ref[...], g_ref[...]).astype(BF16)

    o_ref[...] = jnp.dot(xn_ref[...], w_ref[...], preferred_element_type=F32).astype(o_ref.dtype)


def norm_matmul(x, g, w, *, tm, tn, out_dtype=F32):
    t, d = x.shape
    n = w.shape[1]
    return pl.pallas_call(
        _norm_matmul_body,
        out_shape=jax.ShapeDtypeStruct((t, n), out_dtype),
        grid=(t // tm, n // tn),
        in_specs=[
            pl.BlockSpec((tm, d), lambda i, j: (i, 0)),
            pl.BlockSpec((1, d), lambda i, j: (0, 0)),
            pl.BlockSpec((d, tn), lambda i, j: (0, j)),
        ],
        out_specs=pl.BlockSpec((tm, tn), lambda i, j: (i, j)),
        scratch_shapes=[pltpu.VMEM((tm, d), BF16)],
        compiler_params=_cparams(("parallel", "arbitrary"), VMEM_LIMIT),
        name="norm_matmul",
    )(x, g.reshape(1, d), w)


def _proj_norm_res_body(y_ref, w_ref, g_ref, r_ref, o_ref):
    z = jnp.dot(y_ref[...].astype(BF16), w_ref[...], preferred_element_type=F32)
    o_ref[...] = r_ref[...] + _rms(z, g_ref[...])


def proj_norm_res(y, w, g, res, *, tm):
    t, k = y.shape
    d = w.shape[1]
    return pl.pallas_call(
        _proj_norm_res_body,
        out_shape=jax.ShapeDtypeStruct((t, d), F32),
        grid=(t // tm,),
        in_specs=[
            pl.BlockSpec((tm, k), lambda i: (i, 0)),
            pl.BlockSpec((k, d), lambda i: (0, 0)),
            pl.BlockSpec((1, d), lambda i: (0, 0)),
            pl.BlockSpec((tm, d), lambda i: (i, 0)),
        ],
        out_specs=pl.BlockSpec((tm, d), lambda i: (i, 0)),
        compiler_params=_cparams(("parallel",), VMEM_LIMIT),
        name="proj_norm_res",
    )(y, w, g.reshape(1, d), res)


def _projT_norm_res_body(yT_ref, w_ref, g_ref, r_ref, o_ref):
    z = lax.dot_general(yT_ref[0], w_ref[...], (((0,), (0,)), ((), ())),
                        preferred_element_type=F32)
    o_ref[...] = r_ref[...] + _rms(z, g_ref[...])


def projT_norm_res(yT, w, g, res, *, tm):
    b, k, s = yT.shape
    d = w.shape[1]
    nt = s // tm
    return pl.pallas_call(
        _projT_norm_res_body,
        out_shape=jax.ShapeDtypeStruct((b * s, d), F32),
        grid=(b, nt),
        in_specs=[
            pl.BlockSpec((1, k, tm), lambda bi, i: (bi, 0, i)),
            pl.BlockSpec((k, d), lambda bi, i: (0, 0)),
            pl.BlockSpec((1, d), lambda bi, i: (0, 0)),
            pl.BlockSpec((tm, d), lambda bi, i: (bi * nt + i, 0)),
        ],
        out_specs=pl.BlockSpec((tm, d), lambda bi, i: (bi * nt + i, 0)),
        compiler_params=_cparams(("parallel", "parallel"), VMEM_LIMIT),
        name="projT_norm_res",
    )(yT, w, g.reshape(1, d), res)


def _ffn_body(x_ref, gpre_ref, wg_ref, wu_ref, wo_ref, gpost_ref, o_ref, xn_ref, acc_ref):
    j = pl.program_id(1)

    @pl.when(j == 0)
    def _():
        xn_ref[...] = _rms(x_ref[...], gpre_ref[...]).astype(BF16)

    xn = xn_ref[...]
    gate = jnp.dot(xn, wg_ref[...], preferred_element_type=F32)
    up = jnp.dot(xn, wu_ref[...], preferred_element_type=F32)
    act = (gate * jax.nn.sigmoid(gate) * up).astype(BF16)
    part = jnp.dot(act, wo_ref[...], preferred_element_type=F32)

    @pl.when(j == 0)
    def _():
        acc_ref[...] = part

    @pl.when(j > 0)
    def _():
        acc_ref[...] += part

    @pl.when(j == pl.num_programs(1) - 1)
    def _():
        o_ref[...] = x_ref[...] + _rms(acc_ref[...], gpost_ref[...])


def ffn_block(x, g_pre, w_in, w_out, g_post, *, tm, th):
    t, d = x.shape
    hdim = w_out.shape[0]
    nh = hdim // th
    return pl.pallas_call(
        _ffn_body,
        out_shape=jax.ShapeDtypeStruct((t, d), F32),
        grid=(t // tm, nh),
        in_specs=[
            pl.BlockSpec((tm, d), lambda i, j: (i, 0)),
            pl.BlockSpec((1, d), lambda i, j: (0, 0)),
            pl.BlockSpec((d, th), lambda i, j: (0, j)),
            pl.BlockSpec((d, th), lambda i, j: (0, nh + j)),
            pl.BlockSpec((th, d), lambda i, j: (j, 0)),
            pl.BlockSpec((1, d), lambda i, j: (0, 0)),
        ],
        out_specs=pl.BlockSpec((tm, d), lambda i, j: (i, 0)),
        scratch_shapes=[pltpu.VMEM((tm, d), BF16), pltpu.VMEM((tm, d), F32)],
        compiler_params=_cparams(("parallel", "arbitrary"), VMEM_LIMIT),
        name="ffn_block",
    )(x, g_pre.reshape(1, d), w_in, w_in, w_out, g_post.reshape(1, d))


def _retention_body(q_ref, k_ref, v_ref, g_ref, cos_ref, sin_ref, dmat_ref, qdec_ref, kdec_ref,
                    cdec_ref, o_ref, state_ref):
    @pl.when(pl.program_id(1) == 0)
    def _():
        state_ref[...] = jnp.zeros_like(state_ref)

    cos = cos_ref[...]
    sin = sin_ref[...]
    half = RET_QK // 2

    def rotate(x_ref, h):
        x1 = x_ref[:, h * RET_QK:h * RET_QK + half]
        x2 = x_ref[:, h * RET_QK + half:(h + 1) * RET_QK]
        return jnp.concatenate([x1 * cos - x2 * sin, x1 * sin + x2 * cos], axis=1)

    for h in range(RET_HEADS):
        qr = rotate(q_ref, h)
        kr = rotate(k_ref, h) * (RET_QK ** -0.5)
        v = v_ref[:, h * RET_V:(h + 1) * RET_V].astype(BF16)
        scores = lax.dot_general(qr.astype(BF16), kr.astype(BF16), (((1,), (1,)), ((), ())),
                                 preferred_element_type=F32) * dmat_ref[h]
        state = state_ref[h]
        o = (jnp.dot(scores.astype(BF16), v, preferred_element_type=F32)
             + jnp.dot((qr * qdec_ref[h]).astype(BF16), state.astype(BF16),
                       preferred_element_type=F32))
        kd = (kr * kdec_ref[h]).astype(BF16)
        state_ref[h] = state * cdec_ref[h, 0:1, :] + lax.dot_general(
            kd, v, (((0,), (0,)), ((), ())), preferred_element_type=F32)
        o = o * lax.rsqrt(jnp.mean(o * o, axis=-1, keepdims=True) + RMS_EPS)
        gate = g_ref[:, h * RET_V:(h + 1) * RET_V]
        o_ref[:, h * RET_V:(h + 1) * RET_V] = (o * (gate * jax.nn.sigmoid(gate))).astype(o_ref.dtype)


def retention_core(proj, cos, sin, dmat, qdec, kdec, cdec, *, batch, seq):
    c = RET_CHUNK
    nc = seq // c
    hq = RET_HEADS * RET_QK
    hv = RET_HEADS * RET_V
    return pl.pallas_call(
        _retention_body,
        out_shape=jax.ShapeDtypeStruct((batch * seq, hv), BF16),
        grid=(batch, nc),
        in_specs=[
            pl.BlockSpec((c, hq), lambda b, t: (b * nc + t, 0)),
            pl.BlockSpec((c, hq), lambda b, t: (b * nc + t, 1)),
            pl.BlockSpec((c, hv), lambda b, t: (b * nc + t, 1)),
            pl.BlockSpec((c, hv), lambda b, t: (b * nc + t, 2)),
            pl.BlockSpec((c, RET_QK // 2), lambda b, t: (t, 0)),
            pl.BlockSpec((c, RET_QK // 2), lambda b, t: (t, 0)),
            pl.BlockSpec((RET_HEADS, c, c), lambda b, t: (0, 0, 0)),
            pl.BlockSpec((RET_HEADS, c, RET_QK), lambda b, t: (0, 0, 0)),
            pl.BlockSpec((RET_HEADS, c, RET_QK), lambda b, t: (0, 0, 0)),
            pl.BlockSpec((RET_HEADS, 8, RET_V), lambda b, t: (0, 0, 0)),
        ],
        out_specs=pl.BlockSpec((c, hv), lambda b, t: (b * nc + t, 0)),
        scratch_shapes=[pltpu.VMEM((RET_HEADS, RET_QK, RET_V), F32)],
        compiler_params=_cparams(("parallel", "arbitrary"), VMEM_LIMIT),
        name="retention_core",
    )(proj, proj, proj, proj, cos, sin, dmat, qdec, kdec, cdec)


def _retention_tables(seq):
    h, dk, c = RET_HEADS, RET_QK, RET_CHUNK
    pos = jnp.arange(seq, dtype=F32)
    theta = 1.0 / (10000.0 ** jnp.linspace(0.0, 1.0, dk // 2, dtype=F32))
    ang = pos[:, None] * theta[None, :]
    log_gamma = jnp.log(1.0 - 2.0 ** (-5.0 - jnp.arange(h, dtype=F32)))
    idx = jnp.arange(c, dtype=F32)
    rel = idx[:, None] - idx[None, :]
    dmat = jnp.where(rel >= 0, jnp.exp(jnp.maximum(rel, 0.0) * log_gamma[:, None, None]), 0.0)
    qdec = jnp.exp((idx + 1.0)[None, :] * log_gamma[:, None])
    kdec = jnp.exp((c - 1.0 - idx)[None, :] * log_gamma[:, None])
    cdec = jnp.exp(c * log_gamma)
    qdec = jnp.broadcast_to(qdec[:, :, None], (h, c, dk))
    kdec = jnp.broadcast_to(kdec[:, :, None], (h, c, dk))
    cdec = jnp.broadcast_to(cdec[:, None, None], (h, 8, RET_V))
    return jnp.cos(ang), jnp.sin(ang), dmat, qdec, kdec, cdec


def _deinterleave_qk_columns(w):
    nqk = 2 * RET_HEADS * RET_QK
    perm = []
    for h in range(2 * RET_HEADS):
        base = h * RET_QK
        perm += [base + 2 * i for i in range(RET_QK // 2)]
        perm += [base + 2 * i + 1 for i in range(RET_QK // 2)]
    perm = np.asarray(perm + list(range(nqk, w.shape[1])), dtype=np.int32)
    return w[:, perm]


def _kv_body(x_ref, g_ref, wk_ref, wvT_ref, kc_ref, vc_ref, ks_ref, kw_ref, vsT_ref, vwT_ref):
    xn = _rms(x_ref[...], g_ref[...]).astype(BF16)
    tm = xn.shape[0]
    kall = jnp.dot(xn, wk_ref[...], preferred_element_type=F32)
    gd = NSA_GROUPS * NSA_D
    kc_ref[...] = kall[:, 0:gd]
    vc_ref[...] = kall[:, gd:2 * gd]
    row = pl.program_id(1) * tm + lax.broadcasted_iota(jnp.int32, (tm, 128), 0)
    lane = lax.broadcasted_iota(jnp.int32, (tm, 128), 1)
    blk = (row // SEL_LEN) % SEL_PER_GROUP
    onehot = jnp.where(lane - NSA_D == blk, 1.0, 0.0).astype(F32)
    for g in range(NSA_GROUPS):
        ks = kall[:, 2 * gd + 128 * g:2 * gd + 128 * (g + 1)]
        ks_ref[0, g] = (ks + onehot).astype(BF16)
        kw = kall[:, 2 * gd + 512 + 128 * g:2 * gd + 512 + 128 * (g + 1)]
        kw_ref[0, g] = kw.astype(BF16)
    vT = lax.dot_general(wvT_ref[...], xn, (((1,), (1,)), ((), ())),
                         preferred_element_type=F32)
    extra = jnp.where(lax.broadcasted_iota(jnp.int32, (V_ROWS - NSA_D, K_TILE), 0) == 0, 1.0, 0.0)
    extra = extra.astype(BF16)
    for g in range(NSA_GROUPS):
        for c in range(tm // K_TILE):
            cols = slice(c * K_TILE, (c + 1) * K_TILE)
            vsT_ref[0, g, c, 0:NSA_D, :] = vT[NSA_D * g:NSA_D * (g + 1), cols].astype(BF16)
            vsT_ref[0, g, c, NSA_D:V_ROWS, :] = extra
            vwT_ref[0, g, c, 0:NSA_D, :] = vT[gd + NSA_D * g:gd + NSA_D * (g + 1), cols].astype(BF16)
            vwT_ref[0, g, c, NSA_D:V_ROWS, :] = extra


def kv_project(x, g, kv_w, *, batch, seq, tm):
    d = x.shape[1]
    gd = NSA_GROUPS * NSA_D
    k_c, v_c, k_s, v_s, k_w, v_w = [kv_w[:, i * gd:(i + 1) * gd] for i in range(6)]

    def pad_groups(w):
        w = w.reshape(d, NSA_GROUPS, NSA_D)
        return jnp.pad(w, ((0, 0), (0, 0), (0, 128 - NSA_D))).reshape(d, NSA_GROUPS * 128)

    wk = jnp.concatenate([k_c, v_c, pad_groups(k_s), pad_groups(k_w)], axis=1).astype(BF16)
    wvT = jnp.concatenate([v_s, v_w], axis=1).T.astype(BF16)
    nt = seq // tm
    G = NSA_GROUPS
    return pl.pallas_call(
        _kv_body,
        out_shape=(
            jax.ShapeDtypeStruct((batch * seq, gd), F32),
            jax.ShapeDtypeStruct((batch * seq, gd), F32),
            jax.ShapeDtypeStruct((batch, G, seq, 128), BF16),
            jax.ShapeDtypeStruct((batch, G, seq, 128), BF16),
            jax.ShapeDtypeStruct((batch, G, seq // K_TILE, V_ROWS, K_TILE), BF16),
            jax.ShapeDtypeStruct((batch, G, seq // K_TILE, V_ROWS, K_TILE), BF16),
        ),
        grid=(batch, nt),
        in_specs=[
            pl.BlockSpec((tm, d), lambda b, i: (b * nt + i, 0)),
            pl.BlockSpec((1, d), lambda b, i: (0, 0)),
            pl.BlockSpec(wk.shape, lambda b, i: (0, 0)),
            pl.BlockSpec(wvT.shape, lambda b, i: (0, 0)),
        ],
        out_specs=(
            pl.BlockSpec((tm, gd), lambda b, i: (b * nt + i, 0)),
            pl.BlockSpec((tm, gd), lambda b, i: (b * nt + i, 0)),
            pl.BlockSpec((1, G, tm, 128), lambda b, i: (b, 0, i, 0)),
            pl.BlockSpec((1, G, tm, 128), lambda b, i: (b, 0, i, 0)),
            pl.BlockSpec((1, G, tm // K_TILE, V_ROWS, K_TILE), lambda b, i: (b, 0, i, 0, 0)),
            pl.BlockSpec((1, G, tm // K_TILE, V_ROWS, K_TILE), lambda b, i: (b, 0, i, 0, 0)),
        ),
        compiler_params=_cparams(("parallel", "parallel"), VMEM_LIMIT),
        name="kv_project",
    )(x, g.reshape(1, d), wk, wvT)


def _compress_body(c_ref, pe_ref, w1_ref, w2_ref, w2T_ref, nat_ref, tr_ref, sh_ref):
    half = CMP_STRIDE * NSA_D
    nc = c_ref.shape[3]
    c = c_ref[0, 0, 0].astype(BF16)
    w1 = w1_ref[0]
    first = jnp.dot(c, w1[0:half], preferred_element_type=F32)
    second = jnp.dot(c, w1[half:2 * half], preferred_element_type=F32)
    pe_term = jnp.dot(pe_ref[0].astype(BF16), w1, preferred_element_type=F32)
    sh_ref[0:nc, :] = second
    sh_ref[nc:nc + 8, :] = jnp.zeros((8, CMP_HIDDEN), F32)
    pre = first + sh_ref[1:nc + 1, :] + pe_term[0:1, :]
    hid = (pre * jax.nn.sigmoid(pre)).astype(BF16)
    nat_ref[0, 0, 0] = jnp.dot(hid, w2_ref[0], preferred_element_type=F32).astype(BF16)
    tr_ref[0, 0, 0] = lax.dot_general(w2T_ref[0], hid, (((1,), (1,)), ((), ())),
                                      preferred_element_type=F32).astype(BF16)


def compress_blocks(kc_nat, vc_nat, pe_k, w1_k, w2_k, pe_v, w1_v, w2_v, *, batch, seq):
    G, d = NSA_GROUPS, NSA_D
    nc = seq // CMP_STRIDE

    def to_rows(t):
        t = t.reshape(batch, nc, CMP_STRIDE, G, d).transpose(0, 3, 1, 2, 4)
        return t.reshape(batch, G, nc, CMP_STRIDE * d)

    c_all = jnp.stack([to_rows(kc_nat), to_rows(vc_nat)])
    pe = jnp.stack([pe_k.reshape(1, -1), pe_v.reshape(1, -1)])
    pe = jnp.broadcast_to(pe, (2, 8, CMP_LEN * d))
    w1 = jnp.stack([w1_k, w1_v]).astype(BF16)
    w2 = jnp.stack([w2_k, w2_v]).astype(BF16)
    w2T = jnp.stack([w2_k.T, w2_v.T]).astype(BF16)
    return pl.pallas_call(
        _compress_body,
        out_shape=(
            jax.ShapeDtypeStruct((2, batch, G, nc, d), BF16),
            jax.ShapeDtypeStruct((2, batch, G, d, nc), BF16),
        ),
        grid=(2, batch, G),
        in_specs=[
            pl.BlockSpec((1, 1, 1, nc, CMP_STRIDE * d), lambda w, b, g: (w, b, g, 0, 0)),
            pl.BlockSpec((1, 8, CMP_LEN * d), lambda w, b, g: (w, 0, 0)),
            pl.BlockSpec((1, CMP_LEN * d, CMP_HIDDEN), lambda w, b, g: (w, 0, 0)),
            pl.BlockSpec((1, CMP_HIDDEN, d), lambda w, b, g: (w, 0, 0)),
            pl.BlockSpec((1, d, CMP_HIDDEN), lambda w, b, g: (w, 0, 0)),
        ],
        out_specs=(
            pl.BlockSpec((1, 1, 1, nc, d), lambda w, b, g: (w, b, g, 0, 0)),
            pl.BlockSpec((1, 1, 1, d, nc), lambda w, b, g: (w, b, g, 0, 0)),
        ),
        scratch_shapes=[pltpu.VMEM((nc + 8, CMP_HIDDEN), F32)],
        compiler_params=_cparams(("parallel", "parallel", "parallel"), VMEM_LIMIT),
        name="compress_blocks",
    )(c_all, pe, w1, w2, w2T)


GATE_ROWS = 16


def _qproj_body(x_ref, g_ref, wT_ref, qT_ref, gT_ref):
    xn = _rms(x_ref[...], g_ref[...]).astype(BF16)
    tm = xn.shape[0]
    pT = lax.dot_general(wT_ref[...], xn, (((1,), (1,)), ((), ())),
                         preferred_element_type=F32)
    hd = NSA_HEADS * NSA_D
    q = pT[0:hd] * (NSA_D ** -0.5)
    qT_ref[0] = q.reshape(NSA_HEADS, NSA_D, tm).astype(BF16)
    gates = jax.nn.sigmoid(pT[hd:hd + NSA_GROUPS * GATE_ROWS])
    gT_ref[0] = gates.reshape(NSA_GROUPS, GATE_ROWS, tm)


def q_project(x, g, w_in, *, batch, seq, tm):
    d = x.shape[1]
    hd = NSA_HEADS * NSA_D
    per_group = NSA_REP * 3
    wg = w_in[:, hd:].reshape(d, NSA_GROUPS, per_group)
    wg = jnp.pad(wg, ((0, 0), (0, 0), (0, GATE_ROWS - per_group))).reshape(d, NSA_GROUPS * GATE_ROWS)
    wT = jnp.concatenate([w_in[:, :hd], wg], axis=1).T.astype(BF16)
    nt = seq // tm
    return pl.pallas_call(
        _qproj_body,
        out_shape=(
            jax.ShapeDtypeStruct((batch, NSA_HEADS, NSA_D, seq), BF16),
            jax.ShapeDtypeStruct((batch, NSA_GROUPS, GATE_ROWS, seq), F32),
        ),
        grid=(batch, nt),
        in_specs=[
            pl.BlockSpec((tm, d), lambda b, i: (b * nt + i, 0)),
            pl.BlockSpec((1, d), lambda b, i: (0, 0)),
            pl.BlockSpec(wT.shape, lambda b, i: (0, 0)),
        ],
        out_specs=(
            pl.BlockSpec((1, NSA_HEADS, NSA_D, tm), lambda b, i: (b, 0, 0, i)),
            pl.BlockSpec((1, NSA_GROUPS, GATE_ROWS, tm), lambda b, i: (b, 0, 0, i)),
        ),
        compiler_params=_cparams(("parallel", "parallel"), VMEM_LIMIT),
        name="q_project",
    )(x, g.reshape(1, d), wT)


CMP_CHUNK = 256
CMP_PER_QTILE = Q_TILE // CMP_STRIDE
SEL_PER_QTILE = Q_TILE // SEL_LEN
LANES = NSA_REP * Q_TILE
TILES_PER_GROUP = SEL_PER_GROUP * SEL_LEN // K_TILE


def _nsa_body(qT_ref, gT_ref, kc_ref, vcT_ref, ks_ref, vsT_ref, kw_ref, vwT_ref,
              bc_ref, bs_ref, bw_ref, mmapT_ref, oT_ref,
              s_ref, imp_ref, msel_ref, qaug_ref, acc_ref, m_ref, out_ref):
    i = pl.program_id(2)
    ns = imp_ref.shape[0]
    qT = jnp.concatenate([qT_ref[0, r] for r in range(NSA_REP)], axis=1)
    tiny = jnp.finfo(F32).tiny

    def gate_row(j):
        return jnp.concatenate([gT_ref[0, 0, 3 * r + j:3 * r + j + 1, :] for r in range(NSA_REP)], axis=1)

    nchunks = i // (CMP_CHUNK // CMP_PER_QTILE) + 1
    visible = CMP_PER_QTILE * (i + 1)

    def chunk_rows(c):
        return pl.ds(pl.multiple_of(c * CMP_CHUNK, CMP_CHUNK), CMP_CHUNK)

    def cmp_scores(c, carry):
        s_ref[chunk_rows(c), :] = jnp.dot(kc_ref[0, 0, 0, chunk_rows(c), :], qT,
                                          preferred_element_type=F32)
        return carry

    lax.fori_loop(0, nchunks, cmp_scores, 0)

    @pl.when(i == 0)
    def _():
        s_ref[0:CMP_PER_QTILE, :] = s_ref[0:CMP_PER_QTILE, :] + bc_ref[0, CMP_PER_QTILE:2 * CMP_PER_QTILE, :]

    @pl.when(i > 0)
    def _():
        rows = pl.ds(pl.multiple_of(CMP_PER_QTILE * (i - 1), CMP_PER_QTILE), 2 * CMP_PER_QTILE)
        s_ref[rows, :] = s_ref[rows, :] + bc_ref[0]

    def cmp_max(c, m):
        rid = c * CMP_CHUNK + lax.broadcasted_iota(jnp.int32, (CMP_CHUNK, LANES), 0)
        s = jnp.where(rid < visible, s_ref[chunk_rows(c), :], -jnp.inf)
        s_ref[chunk_rows(c), :] = s
        return jnp.maximum(m, jnp.max(s, axis=0, keepdims=True))

    m_c = lax.fori_loop(0, nchunks, cmp_max, jnp.full((1, LANES), -jnp.inf, F32))
    m_c = jnp.where(jnp.isfinite(m_c), m_c, 0.0)

    def cmp_exp(c, l):
        p = jnp.exp(s_ref[chunk_rows(c), :] - m_c)
        s_ref[chunk_rows(c), :] = p
        return l + jnp.sum(p, axis=0, keepdims=True)

    l_c = lax.fori_loop(0, nchunks, cmp_exp, jnp.zeros((1, LANES), F32))
    inv_c = 1.0 / jnp.maximum(l_c, tiny)

    acc_ref[...] = jnp.zeros_like(acc_ref)
    imp_ref[...] = jnp.zeros_like(imp_ref)

    def cmp_pv(c, carry):
        p = s_ref[chunk_rows(c), :] * inv_c
        acc_ref[0:NSA_D, :] += jnp.dot(vcT_ref[0, 0, c], p.astype(BF16), preferred_element_type=F32)
        ic = p[:, 0:Q_TILE]
        for r in range(1, NSA_REP):
            ic = ic + p[:, r * Q_TILE:(r + 1) * Q_TILE]
        hi = ic.astype(BF16)
        rem = ic - hi.astype(F32)
        mid = rem.astype(BF16)
        lo = (rem - mid.astype(F32)).astype(BF16)
        mm = mmapT_ref[c]
        imp_ref[...] += (jnp.dot(mm, hi, preferred_element_type=F32)
                         + jnp.dot(mm, mid, preferred_element_type=F32)
                         + jnp.dot(mm, lo, preferred_element_type=F32))
        return carry

    lax.fori_loop(0, nchunks, cmp_pv, 0)
    out_ref[...] = gate_row(0) * acc_ref[0:NSA_D, :]

    jrow = lax.broadcasted_iota(jnp.int32, (ns, Q_TILE), 0)
    col = lax.broadcasted_iota(jnp.int32, (ns, Q_TILE), 1)
    cur = SEL_PER_QTILE * i + col // SEL_LEN
    valid = jrow <= cur
    forced = (jrow == 0) | (jrow == cur) | (jrow == cur - 1)
    jf = jrow.astype(F32)
    v0 = jnp.where(valid & jnp.logical_not(forced), imp_ref[...], -jnp.inf)
    sel0 = jnp.where(forced, 1.0, 0.0)

    def pick(_, carry):
        v, sel = carry
        best = jnp.max(v, axis=0, keepdims=True)
        first = jnp.min(jnp.where(v == best, jf, float(ns)), axis=0, keepdims=True)
        hit = jf == first
        return jnp.where(hit, -jnp.inf, v), jnp.where(hit, 1.0, sel)

    _, sel = lax.fori_loop(0, SEL_TOPK - 3, pick, (v0, sel0))
    mask_bias = jnp.where((sel > 0.0) & valid, 0.0, NEG_BIG).astype(BF16)
    msel_ref[...] = jnp.concatenate([mask_bias] * NSA_REP, axis=1)

    qaug_ref[0:NSA_D, :] = qT
    qaug_ref[NSA_D:, :] = jnp.zeros((qaug_ref.shape[0] - NSA_D, LANES), BF16)

    def reset():
        acc_ref[...] = jnp.zeros_like(acc_ref)
        m_ref[...] = jnp.full(m_ref.shape, M_INIT, F32)

    def sub_rows(j):
        return slice(j * K_TILE, (j + 1) * K_TILE)

    def scores_to_scratch(k_ref, tiles):
        for j, kt in enumerate(tiles):
            rows = pl.ds(pl.multiple_of(kt * K_TILE, K_TILE), K_TILE)
            s_ref[sub_rows(j), :] = jnp.dot(k_ref[0, 0, rows, :], qaug_ref[...], preferred_element_type=F32)

    def softmax_from_scratch(vT_ref, tiles):
        for j, kt in enumerate(tiles):
            s = s_ref[sub_rows(j), :]
            m_old = m_ref[0:1, :]
            m_new = jnp.maximum(m_old, jnp.max(s, axis=0, keepdims=True))
            alpha = jnp.exp(m_old - m_new)
            p = jnp.exp(s - m_new).astype(BF16)
            acc_ref[...] = alpha * acc_ref[...] + jnp.dot(vT_ref[0, 0, kt], p, preferred_element_type=F32)
            m_ref[0:1, :] = m_new

    def finish(j):
        l = jnp.maximum(acc_ref[NSA_D:NSA_D + 1, :], tiny)
        out_ref[...] += (gate_row(j) * (1.0 / l)) * acc_ref[0:NSA_D, :]

    last = i // TILES_PER_GROUP
    j_i = i % TILES_PER_GROUP

    def sel_scores(grp):
        slab = pl.ds(pl.multiple_of(grp * SEL_PER_GROUP, SEL_PER_GROUP), SEL_PER_GROUP)
        qaug_ref[NSA_D:NSA_D + SEL_PER_GROUP, :] = msel_ref[slab, :]
        scores_to_scratch(ks_ref, [grp * TILES_PER_GROUP + j for j in range(TILES_PER_GROUP)])

    def sel_softmax(grp):
        softmax_from_scratch(vsT_ref, [grp * TILES_PER_GROUP + j for j in range(TILES_PER_GROUP)])

    reset()

    def far(grp, carry):
        sel_scores(grp)
        sel_softmax(grp)
        return carry

    lax.fori_loop(0, jnp.maximum(last - 1, 0), far, 0)

    @pl.when(last >= 1)
    def _():
        sel_scores(last - 1)

        @pl.when(j_i == 0)
        def _():
            rows = sub_rows(TILES_PER_GROUP - 1)
            s_ref[rows, :] = s_ref[rows, :] + bs_ref[0, 0:K_TILE, :]

        sel_softmax(last - 1)

    sel_scores(last)

    @pl.when(j_i == 0)
    def _():
        s_ref[0:K_TILE, :] = s_ref[0:K_TILE, :] + bs_ref[0, K_TILE:2 * K_TILE, :]

    @pl.when(j_i > 0)
    def _():
        rows = pl.ds(pl.multiple_of((j_i - 1) * K_TILE, K_TILE), 2 * K_TILE)
        s_ref[rows, :] = s_ref[rows, :] + bs_ref[0]

    sel_softmax(last)
    finish(1)

    reset()
    win_tiles = [jnp.maximum(i - 2 + j, 0) for j in range(3)]
    scores_to_scratch(kw_ref, win_tiles)
    for j in range(3):
        before_start = jnp.where(i - 2 + j < 0, NEG_BIG, 0.0).astype(F32)
        s_ref[sub_rows(j), :] = s_ref[sub_rows(j), :] + (bw_ref[0, sub_rows(j), :] + before_start)
    softmax_from_scratch(vwT_ref, win_tiles)
    finish(2)

    for r in range(NSA_REP):
        oT_ref[0, r * NSA_D:(r + 1) * NSA_D, :] = out_ref[:, r * Q_TILE:(r + 1) * Q_TILE].astype(oT_ref.dtype)


def nsa_attention(qT, gT, cmp_nat, cmp_tr, ks, vsT, kw, vwT, bc, bs, bw, mmapT, *, batch, seq):
    G = NSA_GROUPS
    nc = seq // CMP_STRIDE
    ns = seq // SEL_LEN
    nq = seq // Q_TILE
    nkt = seq // K_TILE
    hd = NSA_HEADS * NSA_D
    vcT = cmp_tr[1].reshape(batch, G, NSA_D, nc // CMP_CHUNK, CMP_CHUNK).transpose(0, 1, 3, 2, 4)
    return pl.pallas_call(
        _nsa_body,
        out_shape=jax.ShapeDtypeStruct((batch, hd, seq), BF16),
        grid=(batch, G, nq),
        in_specs=[
            pl.BlockSpec((1, NSA_REP, NSA_D, Q_TILE), lambda b, g, i: (b, g, 0, i)),
            pl.BlockSpec((1, 1, GATE_ROWS, Q_TILE), lambda b, g, i: (b, g, 0, i)),
            pl.BlockSpec((1, 1, 1, nc, NSA_D), lambda b, g, i: (0, b, g, 0, 0)),
            pl.BlockSpec((1, 1, nc // CMP_CHUNK, NSA_D, CMP_CHUNK), lambda b, g, i: (b, g, 0, 0, 0)),
            pl.BlockSpec((1, 1, seq, 128), lambda b, g, i: (b, g, 0, 0)),
            pl.BlockSpec((1, 1, nkt, V_ROWS, K_TILE), lambda b, g, i: (b, g, 0, 0, 0)),
            pl.BlockSpec((1, 1, seq, 128), lambda b, g, i: (b, g, 0, 0)),
            pl.BlockSpec((1, 1, nkt, V_ROWS, K_TILE), lambda b, g, i: (b, g, 0, 0, 0)),
            pl.BlockSpec((1, 2 * CMP_PER_QTILE, LANES), lambda b, g, i: (g, 0, 0)),
            pl.BlockSpec((1, 2 * K_TILE, LANES), lambda b, g, i: (g, 0, 0)),
            pl.BlockSpec((1, 3 * K_TILE, LANES), lambda b, g, i: (g, 0, 0)),
            pl.BlockSpec((nc // CMP_CHUNK, ns, CMP_CHUNK), lambda b, g, i: (0, 0, 0)),
        ],
        out_specs=pl.BlockSpec((1, NSA_REP * NSA_D, Q_TILE), lambda b, g, i: (b, g, i)),
        scratch_shapes=[
            pltpu.VMEM((max(nc, TILES_PER_GROUP * K_TILE), LANES), F32),
            pltpu.VMEM((ns, Q_TILE), F32),
            pltpu.VMEM((ns, LANES), BF16),
            pltpu.VMEM((128, LANES), BF16),
            pltpu.VMEM((V_ROWS, LANES), F32),
            pltpu.VMEM((8, LANES), F32),
            pltpu.VMEM((NSA_D, LANES), F32),
        ],
        compiler_params=_cparams(("parallel", "parallel", "arbitrary"), VMEM_LIMIT),
        name="nsa_attention",
    )(qT, gT, cmp_nat, vcT, ks, vsT, kw, vwT, bc, bs, bw, mmapT)


def _t5_bucket_table(n_max):
    n = np.arange(n_max)
    max_exact = REL_BUCKETS // 2
    nf = np.maximum(n, 1).astype(np.float32)
    large = max_exact + (np.log(nf / np.float32(max_exact))
                         / np.float32(math.log(REL_MAX_DIST / max_exact))
                         * np.float32(REL_BUCKETS - max_exact)).astype(np.int32)
    large = np.minimum(large, REL_BUCKETS - 1)
    return np.where(n < max_exact, n, large).astype(np.int32)


def _bias_tables(rel_bias):
    n_max = WINDOW + Q_TILE
    buckets = _t5_bucket_table(n_max)
    assert (buckets[Q_TILE - CMP_LEN + 1:] == REL_BUCKETS - 1).all()
    tab = rel_bias.astype(F32)[buckets, :] - rel_bias.astype(F32)[REL_BUCKETS - 1][None, :]
    tab = tab.T
    def by_distance(dist, ok, fill):
        return jnp.where(ok[None], tab[:, np.clip(dist, 0, n_max - 1)], fill)

    def to_lanes(vals):
        rows = vals.shape[1]
        vals = vals.reshape(NSA_GROUPS, NSA_REP, rows, Q_TILE).transpose(0, 2, 1, 3)
        return vals.reshape(NSA_GROUPS, rows, LANES)

    def toeplitz(rows, offset, ok, fill):
        w = rows + Q_TILE
        k = np.arange(w)
        dist = np.where(k < Q_TILE, k, k - w) + offset
        line = by_distance(dist, ok(dist), fill)
        flat = jnp.tile(line, (1, rows))[:, :rows * (w - 1)]
        return to_lanes(flat.reshape(NSA_HEADS, rows, w - 1)[:, :, :Q_TILE])

    c = np.arange(Q_TILE)[None, :]
    r = np.arange(2 * CMP_PER_QTILE)[:, None]
    dist_c = c + (Q_TILE - CMP_LEN + 1) - CMP_STRIDE * r
    bc = to_lanes(by_distance(dist_c, dist_c >= 0, -jnp.inf))
    bs = toeplitz(2 * K_TILE, K_TILE, lambda dd: dd >= 0, NEG_BIG)
    bw = toeplitz(3 * K_TILE, 2 * K_TILE, lambda dd: (dd >= 0) & (dd < WINDOW), NEG_BIG)
    return bc, bs, bw


def _selection_map(seq):
    nc = seq // CMP_STRIDE
    ns = seq // SEL_LEN
    n_cmp = (seq - CMP_LEN) // CMP_STRIDE + 1
    ratio = SEL_LEN // CMP_STRIDE
    lead = CMP_LEN // CMP_STRIDE - 1
    j = np.arange(ns)[:, None]
    n = np.arange(nc)[None, :]
    m = ((n >= ratio * j - lead) & (n < ratio * j + ratio) & (n < n_cmp)).astype(np.float32)
    m = m.reshape(ns, nc // CMP_CHUNK, CMP_CHUNK).transpose(1, 0, 2)
    return jnp.asarray(m, dtype=BF16)


def kernel(x, mix_norm_pre, mix_norm_post, ffn_norm_pre, ffn_norm_post, ffn_w_in, ffn_w_out,
           ret_w_in, ret_w_out, kv_norm, kv_w, cmp_pe_k, cmp_w1_k, cmp_w2_k,
           cmp_pe_v, cmp_w1_v, cmp_w2_v, nsa_w_in, nsa_w_out, rel_bias):
    batch, seq, d = x.shape
    n_ret = ret_w_in.shape[0]
    n_nsa = nsa_w_in.shape[0]
    assert seq % (SEL_PER_GROUP * SEL_LEN) == 0 and seq // SEL_LEN >= SEL_TOPK
    h = x.reshape(batch * seq, d)
    shared = None
    for layer in range(n_ret + n_nsa):
        if layer == n_ret:
            kc_nat, vc_nat, ks, kw, vsT, vwT = kv_project(h, kv_norm, kv_w, batch=batch, seq=seq, tm=1024)
            cmp_nat, cmp_tr = compress_blocks(kc_nat, vc_nat, cmp_pe_k, cmp_w1_k, cmp_w2_k,
                                              cmp_pe_v, cmp_w1_v, cmp_w2_v, batch=batch, seq=seq)
            shared = (cmp_nat, cmp_tr, ks, vsT, kw, vwT) + _bias_tables(rel_bias) + (_selection_map(seq),)
        if layer < n_ret:
            w_in = _deinterleave_qk_columns(ret_w_in[layer]).astype(BF16)
            proj = norm_matmul(h, mix_norm_pre[layer], w_in, tm=512, tn=1536)
            mixed = retention_core(proj, *_retention_tables(seq), batch=batch, seq=seq)
            h = proj_norm_res(mixed, ret_w_out[layer].astype(BF16), mix_norm_post[layer], h, tm=512)
        else:
            j = layer - n_ret
            qT, gT = q_project(h, mix_norm_pre[layer], nsa_w_in[j], batch=batch, seq=seq, tm=512)
            oT = nsa_attention(qT, gT, *shared, batch=batch, seq=seq)
            h = projT_norm_res(oT, nsa_w_out[j].astype(BF16), mix_norm_post[layer], h, tm=512)
        h = ffn_block(h, ffn_norm_pre[layer], ffn_w_in[layer].astype(BF16), ffn_w_out[layer].astype(BF16),
                      ffn_norm_post[layer], tm=512, th=FFN_HIDDEN // 2)
    return h.reshape(batch, seq, d)
```

```python
import functools
import math

import numpy as np
import jax
import jax.numpy as jnp
from jax import lax
from jax.experimental import pallas as pl
from jax.experimental.pallas import tpu as pltpu

F32 = jnp.float32
BF16 = jnp.bfloat16

D_MODEL = 1024
RMS_EPS = 1e-6

RET_HEADS = 4
RET_QK = 256
RET_V = 512
RET_CHUNK = 128

FFN_HIDDEN = 2816

NSA_HEADS = 16
NSA_GROUPS = 4
NSA_REP = 4
NSA_D = 64
CMP_LEN = 32
CMP_STRIDE = 16
CMP_HIDDEN = 256
SEL_LEN = 64
SEL_TOPK = 16
WINDOW = 512
REL_BUCKETS = 32
REL_MAX_DIST = 128

Q_TILE = 256
K_TILE = 256
SEL_PER_GROUP = 16
V_ROWS = 80
LOG2_E = math.log2(math.e)
NEG_BIG = -(2.0 ** 100)
M_INIT = -(2.0 ** 120)

VMEM_LIMIT = 56 * 1024 * 1024


def _cparams(sem, vmem=None):
    return pltpu.CompilerParams(dimension_semantics=sem, vmem_limit_bytes=vmem)


def _rms(x, g):
    return x * lax.rsqrt(jnp.mean(x * x, axis=-1, keepdims=True) + RMS_EPS) * g


def _norm_matmul_body(x_ref, g_ref, w_ref, o_ref, xn_ref):
    @pl.when(pl.program_id(1) == 0)
    def _():
        xn_ref[...] = _rms(x_ref[...], g_ref[...]).astype(BF16)

    o_ref[...] = jnp.dot(xn_ref[...], w_ref[...], preferred_element_type=F32).astype(o_ref.dtype)


def norm_matmul(x, g, w, *, tm, tn, out_dtype=F32):
    t, d = x.shape
    n = w.shape[1]
    return pl.pallas_call(
        _norm_matmul_body,
        out_shape=jax.ShapeDtypeStruct((t, n), out_dtype),
        grid=(t // tm, n // tn),
        in_specs=[
            pl.BlockSpec((tm, d), lambda i, j: (i, 0)),
            pl.BlockSpec((1, d), lambda i, j: (0, 0)),
            pl.BlockSpec((d, tn), lambda i, j: (0, j)),
        ],
        out_specs=pl.BlockSpec((tm, tn), lambda i, j: (i, j)),
        scratch_shapes=[pltpu.VMEM((tm, d), BF16)],
        compiler_params=_cparams(("parallel", "arbitrary"), VMEM_LIMIT),
        name="norm_matmul",
    )(x, g.reshape(1, d), w)


def _proj_norm_res_body(y_ref, w_ref, g_ref, r_ref, o_ref):
    z = jnp.dot(y_ref[...].astype(BF16), w_ref[...], preferred_element_type=F32)
    o_ref[...] = r_ref[...] + _rms(z, g_ref[...])


def proj_norm_res(y, w, g, res, *, tm):
    t, k = y.shape
    d = w.shape[1]
    return pl.pallas_call(
        _proj_norm_res_body,
        out_shape=jax.ShapeDtypeStruct((t, d), F32),
        grid=(t // tm,),
        in_specs=[
            pl.BlockSpec((tm, k), lambda i: (i, 0)),
            pl.BlockSpec((k, d), lambda i: (0, 0)),
            pl.BlockSpec((1, d), lambda i: (0, 0)),
            pl.BlockSpec((tm, d), lambda i: (i, 0)),
        ],
        out_specs=pl.BlockSpec((tm, d), lambda i: (i, 0)),
        compiler_params=_cparams(("parallel",), VMEM_LIMIT),
        name="proj_norm_res",
    )(y, w, g.reshape(1, d), res)


def _projT_norm_res_body(yT_ref, w_ref, g_ref, r_ref, o_ref):
    z = lax.dot_general(yT_ref[0], w_ref[...], (((0,), (0,)), ((), ())),
                        preferred_element_type=F32)
    o_ref[...] = r_ref[...] + _rms(z, g_ref[...])


def projT_norm_res(yT, w, g, res, *, tm):
    b, k, s = yT.shape
    d = w.shape[1]
    nt = s // tm
    return pl.pallas_call(
        _projT_norm_res_body,
        out_shape=jax.ShapeDtypeStruct((b * s, d), F32),
        grid=(b, nt),
        in_specs=[
            pl.BlockSpec((1, k, tm), lambda bi, i: (bi, 0, i)),
            pl.BlockSpec((k, d), lambda bi, i: (0, 0)),
            pl.BlockSpec((1, d), lambda bi, i: (0, 0)),
            pl.BlockSpec((tm, d), lambda bi, i: (bi * nt + i, 0)),
        ],
        out_specs=pl.BlockSpec((tm, d), lambda bi, i: (bi * nt + i, 0)),
        compiler_params=_cparams(("parallel", "parallel"), VMEM_LIMIT),
        name="projT_norm_res",
    )(yT, w, g.reshape(1, d), res)


def _ffn_body(x_ref, gpre_ref, wg_ref, wu_ref, wo_ref, gpost_ref, o_ref, xn_ref, acc_ref):
    j = pl.program_id(1)

    @pl.when(j == 0)
    def _():
        xn_ref[...] = _rms(x_ref[...], gpre_ref[...]).astype(BF16)

    xn = xn_ref[...]
    gate = jnp.dot(xn, wg_ref[...], preferred_element_type=F32)
    up = jnp.dot(xn, wu_ref[...], preferred_element_type=F32)
    act = (gate * jax.nn.sigmoid(gate) * up).astype(BF16)
    part = jnp.dot(act, wo_ref[...], preferred_element_type=F32)

    @pl.when(j == 0)
    def _():
        acc_ref[...] = part

    @pl.when(j > 0)
    def _():
        acc_ref[...] += part

    @pl.when(j == pl.num_programs(1) - 1)
    def _():
        o_ref[...] = x_ref[...] + _rms(acc_ref[...], gpost_ref[...])


def ffn_block(x, g_pre, w_in, w_out, g_post, *, tm, th):
    t, d = x.shape
    hdim = w_out.shape[0]
    nh = hdim // th
    return pl.pallas_call(
        _ffn_body,
        out_shape=jax.ShapeDtypeStruct((t, d), F32),
        grid=(t // tm, nh),
        in_specs=[
            pl.BlockSpec((tm, d), lambda i, j: (i, 0)),
            pl.BlockSpec((1, d), lambda i, j: (0, 0)),
            pl.BlockSpec((d, th), lambda i, j: (0, j)),
            pl.BlockSpec((d, th), lambda i, j: (0, nh + j)),
            pl.BlockSpec((th, d), lambda i, j: (j, 0)),
            pl.BlockSpec((1, d), lambda i, j: (0, 0)),
        ],
        out_specs=pl.BlockSpec((tm, d), lambda i, j: (i, 0)),
        scratch_shapes=[pltpu.VMEM((tm, d), BF16), pltpu.VMEM((tm, d), F32)],
        compiler_params=_cparams(("parallel", "arbitrary"), VMEM_LIMIT),
        name="ffn_block",
    )(x, g_pre.reshape(1, d), w_in, w_in, w_out, g_post.reshape(1, d))


def _retention_body(q_ref, k_ref, v_ref, g_ref, cos_ref, sin_ref, dmat_ref, qdec_ref, kdec_ref,
                    cdec_ref, o_ref, state_ref):
    @pl.when(pl.program_id(1) == 0)
    def _():
        state_ref[...] = jnp.zeros_like(state_ref)

    cos = cos_ref[...]
    sin = sin_ref[...]
    half = RET_QK // 2

    def rotate(x_ref, h):
        x1 = x_ref[:, h * RET_QK:h * RET_QK + half].astype(F32)
        x2 = x_ref[:, h * RET_QK + half:(h + 1) * RET_QK].astype(F32)
        return jnp.concatenate([x1 * cos - x2 * sin, x1 * sin + x2 * cos], axis=1)

    for h in range(RET_HEADS):
        qr = rotate(q_ref, h)
        kr = rotate(k_ref, h) * (RET_QK ** -0.5)
        v = v_ref[:, h * RET_V:(h + 1) * RET_V].astype(BF16)
        scores = lax.dot_general(qr.astype(BF16), kr.astype(BF16), (((1,), (1,)), ((), ())),
                                 preferred_element_type=F32) * dmat_ref[h]
        state = state_ref[h]
        o = (jnp.dot(scores.astype(BF16), v, preferred_element_type=F32)
             + jnp.dot((qr * qdec_ref[h]).astype(BF16), state.astype(BF16),
                       preferred_element_type=F32))
        kd = (kr * kdec_ref[h]).astype(BF16)
        state_ref[h] = state * cdec_ref[h, 0:1, :] + lax.dot_general(
            kd, v, (((0,), (0,)), ((), ())), preferred_element_type=F32)
        o = o * lax.rsqrt(jnp.mean(o * o, axis=-1, keepdims=True) + RMS_EPS)
        gate = g_ref[:, h * RET_V:(h + 1) * RET_V].astype(F32)
        o_ref[:, h * RET_V:(h + 1) * RET_V] = (o * (gate * jax.nn.sigmoid(gate))).astype(o_ref.dtype)


def retention_core(proj, cos, sin, dmat, qdec, kdec, cdec, *, batch, seq):
    c = RET_CHUNK
    nc = seq // c
    hq = RET_HEADS * RET_QK
    hv = RET_HEADS * RET_V
    return pl.pallas_call(
        _retention_body,
        out_shape=jax.ShapeDtypeStruct((batch * seq, hv), BF16),
        grid=(batch, nc),
        in_specs=[
            pl.BlockSpec((c, hq), lambda b, t: (b * nc + t, 0)),
            pl.BlockSpec((c, hq), lambda b, t: (b * nc + t, 1)),
            pl.BlockSpec((c, hv), lambda b, t: (b * nc + t, 1)),
            pl.BlockSpec((c, hv), lambda b, t: (b * nc + t, 2)),
            pl.BlockSpec((c, RET_QK // 2), lambda b, t: (t, 0)),
            pl.BlockSpec((c, RET_QK // 2), lambda b, t: (t, 0)),
            pl.BlockSpec((RET_HEADS, c, c), lambda b, t: (0, 0, 0)),
            pl.BlockSpec((RET_HEADS, c, RET_QK), lambda b, t: (0, 0, 0)),
            pl.BlockSpec((RET_HEADS, c, RET_QK), lambda b, t: (0, 0, 0)),
            pl.BlockSpec((RET_HEADS, 8, RET_V), lambda b, t: (0, 0, 0)),
        ],
        out_specs=pl.BlockSpec((c, hv), lambda b, t: (b * nc + t, 0)),
        scratch_shapes=[pltpu.VMEM((RET_HEADS, RET_QK, RET_V), F32)],
        compiler_params=_cparams(("parallel", "arbitrary"), VMEM_LIMIT),
        name="retention_core",
    )(proj, proj, proj, proj, cos, sin, dmat, qdec, kdec, cdec)


def _retention_tables(seq):
    h, dk, c = RET_HEADS, RET_QK, RET_CHUNK
    pos = jnp.arange(seq, dtype=F32)
    theta = 1.0 / (10000.0 ** jnp.linspace(0.0, 1.0, dk // 2, dtype=F32))
    ang = pos[:, None] * theta[None, :]
    log_gamma = jnp.log(1.0 - 2.0 ** (-5.0 - jnp.arange(h, dtype=F32)))
    idx = jnp.arange(c, dtype=F32)
    rel = idx[:, None] - idx[None, :]
    dmat = jnp.where(rel >= 0, jnp.exp(jnp.maximum(rel, 0.0) * log_gamma[:, None, None]), 0.0)
    qdec = jnp.exp((idx + 1.0)[None, :] * log_gamma[:, None])
    kdec = jnp.exp((c - 1.0 - idx)[None, :] * log_gamma[:, None])
    cdec = jnp.exp(c * log_gamma)
    qdec = jnp.broadcast_to(qdec[:, :, None], (h, c, dk))
    kdec = jnp.broadcast_to(kdec[:, :, None], (h, c, dk))
    cdec = jnp.broadcast_to(cdec[:, None, None], (h, 8, RET_V))
    return jnp.cos(ang), jnp.sin(ang), dmat, qdec, kdec, cdec


def _deinterleave_qk_columns(w):
    nqk = 2 * RET_HEADS * RET_QK
    perm = []
    for h in range(2 * RET_HEADS):
        base = h * RET_QK
        perm += [base + 2 * i for i in range(RET_QK // 2)]
        perm += [base + 2 * i + 1 for i in range(RET_QK // 2)]
    perm = np.asarray(perm + list(range(nqk, w.shape[1])), dtype=np.int32)
    return w[:, perm]


def _kv_body(x_ref, g_ref, wk_ref, wvT_ref, kc_ref, vc_ref, ks_ref, kw_ref, vsT_ref, vwT_ref):
    xn = _rms(x_ref[...], g_ref[...]).astype(BF16)
    tm = xn.shape[0]
    kall = jnp.dot(xn, wk_ref[...], preferred_element_type=F32)
    gd = NSA_GROUPS * NSA_D
    kc_ref[...] = kall[:, 0:gd]
    vc_ref[...] = kall[:, gd:2 * gd]
    row = pl.program_id(1) * tm + lax.broadcasted_iota(jnp.int32, (tm, 128), 0)
    lane = lax.broadcasted_iota(jnp.int32, (tm, 128), 1)
    blk = (row // SEL_LEN) % SEL_PER_GROUP
    onehot = jnp.where(lane - NSA_D == blk, 1.0, 0.0).astype(F32)
    for g in range(NSA_GROUPS):
        ks = kall[:, 2 * gd + 128 * g:2 * gd + 128 * (g + 1)]
        ks_ref[0, g] = (ks + onehot).astype(BF16)
        kw = kall[:, 2 * gd + 512 + 128 * g:2 * gd + 512 + 128 * (g + 1)]
        kw_ref[0, g] = kw.astype(BF16)
    vT = lax.dot_general(wvT_ref[...], xn, (((1,), (1,)), ((), ())),
                         preferred_element_type=F32)
    extra = jnp.where(lax.broadcasted_iota(jnp.int32, (V_ROWS - NSA_D, K_TILE), 0) == 0, 1.0, 0.0)
    extra = extra.astype(BF16)
    for g in range(NSA_GROUPS):
        for c in range(tm // K_TILE):
            cols = slice(c * K_TILE, (c + 1) * K_TILE)
            vsT_ref[0, g, c, 0:NSA_D, :] = vT[NSA_D * g:NSA_D * (g + 1), cols].astype(BF16)
            vsT_ref[0, g, c, NSA_D:V_ROWS, :] = extra
            vwT_ref[0, g, c, 0:NSA_D, :] = vT[gd + NSA_D * g:gd + NSA_D * (g + 1), cols].astype(BF16)
            vwT_ref[0, g, c, NSA_D:V_ROWS, :] = extra


def kv_project(x, g, kv_w, *, batch, seq, tm):
    d = x.shape[1]
    gd = NSA_GROUPS * NSA_D
    k_c, v_c, k_s, v_s, k_w, v_w = [kv_w[:, i * gd:(i + 1) * gd] for i in range(6)]

    def pad_groups(w):
        w = w.reshape(d, NSA_GROUPS, NSA_D)
        return jnp.pad(w, ((0, 0), (0, 0), (0, 128 - NSA_D))).reshape(d, NSA_GROUPS * 128)

    wk = jnp.concatenate([k_c, v_c, pad_groups(k_s), pad_groups(k_w)], axis=1).astype(BF16)
    wvT = jnp.concatenate([v_s, v_w], axis=1).T.astype(BF16)
    nt = seq // tm
    G = NSA_GROUPS
    return pl.pallas_call(
        _kv_body,
        out_shape=(
            jax.ShapeDtypeStruct((batch * seq, gd), F32),
            jax.ShapeDtypeStruct((batch * seq, gd), F32),
            jax.ShapeDtypeStruct((batch, G, seq, 128), BF16),
            jax.ShapeDtypeStruct((batch, G, seq, 128), BF16),
            jax.ShapeDtypeStruct((batch, G, seq // K_TILE, V_ROWS, K_TILE), BF16),
            jax.ShapeDtypeStruct((batch, G, seq // K_TILE, V_ROWS, K_TILE), BF16),
        ),
        grid=(batch, nt),
        in_specs=[
            pl.BlockSpec((tm, d), lambda b, i: (b * nt + i, 0)),
            pl.BlockSpec((1, d), lambda b, i: (0, 0)),
            pl.BlockSpec(wk.shape, lambda b, i: (0, 0)),
            pl.BlockSpec(wvT.shape, lambda b, i: (0, 0)),
        ],
        out_specs=(
            pl.BlockSpec((tm, gd), lambda b, i: (b * nt + i, 0)),
            pl.BlockSpec((tm, gd), lambda b, i: (b * nt + i, 0)),
            pl.BlockSpec((1, G, tm, 128), lambda b, i: (b, 0, i, 0)),
            pl.BlockSpec((1, G, tm, 128), lambda b, i: (b, 0, i, 0)),
            pl.BlockSpec((1, G, tm // K_TILE, V_ROWS, K_TILE), lambda b, i: (b, 0, i, 0, 0)),
            pl.BlockSpec((1, G, tm // K_TILE, V_ROWS, K_TILE), lambda b, i: (b, 0, i, 0, 0)),
        ),
        compiler_params=_cparams(("parallel", "parallel"), VMEM_LIMIT),
        name="kv_project",
    )(x, g.reshape(1, d), wk, wvT)


def _compress_body(c_ref, pe_ref, w1_ref, w2_ref, w2T_ref, nat_ref, tr_ref, sh_ref):
    half = CMP_STRIDE * NSA_D
    nc = c_ref.shape[3]
    c = c_ref[0, 0, 0].astype(BF16)
    w1 = w1_ref[0]
    first = jnp.dot(c, w1[0:half], preferred_element_type=F32)
    second = jnp.dot(c, w1[half:2 * half], preferred_element_type=F32)
    pe_term = jnp.dot(pe_ref[0].astype(BF16), w1, preferred_element_type=F32)
    sh_ref[0:nc, :] = second
    sh_ref[nc:nc + 8, :] = jnp.zeros((8, CMP_HIDDEN), F32)
    pre = first + sh_ref[1:nc + 1, :] + pe_term[0:1, :]
    hid = (pre * jax.nn.sigmoid(pre)).astype(BF16)
    nat_ref[0, 0, 0] = jnp.dot(hid, w2_ref[0], preferred_element_type=F32).astype(BF16)
    tr_ref[0, 0, 0] = lax.dot_general(w2T_ref[0], hid, (((1,), (1,)), ((), ())),
                                      preferred_element_type=F32).astype(BF16)


def compress_blocks(kc_nat, vc_nat, pe_k, w1_k, w2_k, pe_v, w1_v, w2_v, *, batch, seq):
    G, d = NSA_GROUPS, NSA_D
    nc = seq // CMP_STRIDE

    def to_rows(t):
        t = t.reshape(batch, nc, CMP_STRIDE, G, d).transpose(0, 3, 1, 2, 4)
        return t.reshape(batch, G, nc, CMP_STRIDE * d)

    c_all = jnp.stack([to_rows(kc_nat), to_rows(vc_nat)])
    pe = jnp.stack([pe_k.reshape(1, -1), pe_v.reshape(1, -1)])
    pe = jnp.broadcast_to(pe, (2, 8, CMP_LEN * d))
    w1 = jnp.stack([w1_k, w1_v]).astype(BF16)
    w2 = jnp.stack([w2_k, w2_v]).astype(BF16)
    w2T = jnp.stack([w2_k.T, w2_v.T]).astype(BF16)
    return pl.pallas_call(
        _compress_body,
        out_shape=(
            jax.ShapeDtypeStruct((2, batch, G, nc, d), BF16),
            jax.ShapeDtypeStruct((2, batch, G, d, nc), BF16),
        ),
        grid=(2, batch, G),
        in_specs=[
            pl.BlockSpec((1, 1, 1, nc, CMP_STRIDE * d), lambda w, b, g: (w, b, g, 0, 0)),
            pl.BlockSpec((1, 8, CMP_LEN * d), lambda w, b, g: (w, 0, 0)),
            pl.BlockSpec((1, CMP_LEN * d, CMP_HIDDEN), lambda w, b, g: (w, 0, 0)),
            pl.BlockSpec((1, CMP_HIDDEN, d), lambda w, b, g: (w, 0, 0)),
            pl.BlockSpec((1, d, CMP_HIDDEN), lambda w, b, g: (w, 0, 0)),
        ],
        out_specs=(
            pl.BlockSpec((1, 1, 1, nc, d), lambda w, b, g: (w, b, g, 0, 0)),
            pl.BlockSpec((1, 1, 1, d, nc), lambda w, b, g: (w, b, g, 0, 0)),
        ),
        scratch_shapes=[pltpu.VMEM((nc + 8, CMP_HIDDEN), F32)],
        compiler_params=_cparams(("parallel", "parallel", "parallel"), VMEM_LIMIT),
        name="compress_blocks",
    )(c_all, pe, w1, w2, w2T)


GATE_ROWS = 16


def _qproj_body(x_ref, g_ref, wT_ref, qT_ref, gT_ref):
    xn = _rms(x_ref[...], g_ref[...]).astype(BF16)
    tm = xn.shape[0]
    pT = lax.dot_general(wT_ref[...], xn, (((1,), (1,)), ((), ())),
                         preferred_element_type=F32)
    hd = NSA_HEADS * NSA_D
    q = pT[0:hd] * (NSA_D ** -0.5 * LOG2_E)
    qT_ref[0] = q.reshape(NSA_HEADS, NSA_D, tm).astype(BF16)
    gates = jax.nn.sigmoid(pT[hd:hd + NSA_GROUPS * GATE_ROWS])
    gT_ref[0] = gates.reshape(NSA_GROUPS, GATE_ROWS, tm)


def q_project(x, g, w_in, *, batch, seq, tm):
    d = x.shape[1]
    hd = NSA_HEADS * NSA_D
    per_group = NSA_REP * 3
    wg = w_in[:, hd:].reshape(d, NSA_GROUPS, per_group)
    wg = jnp.pad(wg, ((0, 0), (0, 0), (0, GATE_ROWS - per_group))).reshape(d, NSA_GROUPS * GATE_ROWS)
    wT = jnp.concatenate([w_in[:, :hd], wg], axis=1).T.astype(BF16)
    nt = seq // tm
    return pl.pallas_call(
        _qproj_body,
        out_shape=(
            jax.ShapeDtypeStruct((batch, NSA_HEADS, NSA_D, seq), BF16),
            jax.ShapeDtypeStruct((batch, NSA_GROUPS, GATE_ROWS, seq), F32),
        ),
        grid=(batch, nt),
        in_specs=[
            pl.BlockSpec((tm, d), lambda b, i: (b * nt + i, 0)),
            pl.BlockSpec((1, d), lambda b, i: (0, 0)),
            pl.BlockSpec(wT.shape, lambda b, i: (0, 0)),
        ],
        out_specs=(
            pl.BlockSpec((1, NSA_HEADS, NSA_D, tm), lambda b, i: (b, 0, 0, i)),
            pl.BlockSpec((1, NSA_GROUPS, GATE_ROWS, tm), lambda b, i: (b, 0, 0, i)),
        ),
        compiler_params=_cparams(("parallel", "parallel"), VMEM_LIMIT),
        name="q_project",
    )(x, g.reshape(1, d), wT)


CMP_CHUNK = 256
CMP_PER_QTILE = Q_TILE // CMP_STRIDE
SEL_PER_QTILE = Q_TILE // SEL_LEN
LANES = NSA_REP * Q_TILE
TILES_PER_GROUP = SEL_PER_GROUP * SEL_LEN // K_TILE


def _nsa_body(qT_ref, gT_ref, kc_ref, vcT_ref, ks_ref, vsT_ref, kw_ref, vwT_ref,
              bc_ref, bs_ref, bw_ref, mmapT_ref, oT_ref,
              s_ref, imp_ref, msel_ref, qaug_ref, acc_ref, m_ref, out_ref):
    i = pl.program_id(2)
    ns = imp_ref.shape[0]
    qT = jnp.concatenate([qT_ref[0, r] for r in range(NSA_REP)], axis=1)
    tiny = jnp.finfo(F32).tiny

    def gate_row(j):
        return jnp.concatenate([gT_ref[0, 0, 3 * r + j:3 * r + j + 1, :] for r in range(NSA_REP)], axis=1)

    nchunks = i // (CMP_CHUNK // CMP_PER_QTILE) + 1
    visible = CMP_PER_QTILE * (i + 1)

    def chunk_rows(c):
        return pl.ds(pl.multiple_of(c * CMP_CHUNK, CMP_CHUNK), CMP_CHUNK)

    def cmp_scores(c, carry):
        s_ref[chunk_rows(c), :] = jnp.dot(kc_ref[0, 0, 0, chunk_rows(c), :], qT,
                                          preferred_element_type=F32)
        return carry

    lax.fori_loop(0, nchunks, cmp_scores, 0)

    @pl.when(i == 0)
    def _():
        s_ref[0:CMP_PER_QTILE, :] = s_ref[0:CMP_PER_QTILE, :] + bc_ref[0, CMP_PER_QTILE:2 * CMP_PER_QTILE, :]

    @pl.when(i > 0)
    def _():
        rows = pl.ds(pl.multiple_of(CMP_PER_QTILE * (i - 1), CMP_PER_QTILE), 2 * CMP_PER_QTILE)
        s_ref[rows, :] = s_ref[rows, :] + bc_ref[0]

    def cmp_max(c, m):
        rid = c * CMP_CHUNK + lax.broadcasted_iota(jnp.int32, (CMP_CHUNK, LANES), 0)
        s = jnp.where(rid < visible, s_ref[chunk_rows(c), :], -jnp.inf)
        s_ref[chunk_rows(c), :] = s
        return jnp.maximum(m, jnp.max(s, axis=0, keepdims=True))

    m_c = lax.fori_loop(0, nchunks, cmp_max, jnp.full((1, LANES), -jnp.inf, F32))
    m_c = jnp.where(jnp.isfinite(m_c), m_c, 0.0)

    def cmp_exp(c, l):
        p = jnp.exp2(s_ref[chunk_rows(c), :] - m_c)
        s_ref[chunk_rows(c), :] = p
        return l + jnp.sum(p, axis=0, keepdims=True)

    l_c = lax.fori_loop(0, nchunks, cmp_exp, jnp.zeros((1, LANES), F32))
    inv_c = 1.0 / jnp.maximum(l_c, tiny)

    acc_ref[...] = jnp.zeros_like(acc_ref)
    imp_ref[...] = jnp.zeros_like(imp_ref)

    def cmp_pv(c, carry):
        p = s_ref[chunk_rows(c), :] * inv_c
        acc_ref[0:NSA_D, :] += jnp.dot(vcT_ref[0, 0, c], p.astype(BF16), preferred_element_type=F32)
        ic = p[:, 0:Q_TILE]
        for r in range(1, NSA_REP):
            ic = ic + p[:, r * Q_TILE:(r + 1) * Q_TILE]
        hi = ic.astype(BF16)
        rem = ic - hi.astype(F32)
        mid = rem.astype(BF16)
        lo = (rem - mid.astype(F32)).astype(BF16)
        mm = mmapT_ref[c]
        imp_ref[...] += (jnp.dot(mm, hi, preferred_element_type=F32)
                         + jnp.dot(mm, mid, preferred_element_type=F32)
                         + jnp.dot(mm, lo, preferred_element_type=F32))
        return carry

    lax.fori_loop(0, nchunks, cmp_pv, 0)
    out_ref[...] = gate_row(0) * acc_ref[0:NSA_D, :]

    jrow = lax.broadcasted_iota(jnp.int32, (ns, Q_TILE), 0)
    col = lax.broadcasted_iota(jnp.int32, (ns, Q_TILE), 1)
    cur = SEL_PER_QTILE * i + col // SEL_LEN
    valid = jrow <= cur
    forced = (jrow == 0) | (jrow == cur) | (jrow == cur - 1)
    free = float(SEL_TOPK - 3)
    candidate = valid & jnp.logical_not(forced)
    v0 = jnp.where(candidate, imp_ref[...], -jnp.inf)

    def strip_max(_, carry):
        v, taken, theta, above = carry
        best = jnp.max(v, axis=0, keepdims=True)
        hit = v == best
        now = taken + jnp.sum(jnp.where(hit, 1.0, 0.0), axis=0, keepdims=True)
        crossed = (taken < free) & (now >= free)
        return (jnp.where(hit, -jnp.inf, v), now,
                jnp.where(crossed, best, theta), jnp.where(crossed, taken, above))

    zero_row = jnp.zeros((1, Q_TILE), F32)
    _, _, theta, above = lax.fori_loop(
        0, SEL_TOPK - 3, strip_max, (v0, zero_row, jnp.full((1, Q_TILE), jnp.inf, F32), zero_row))
    v0 = jnp.where(candidate, imp_ref[...], -jnp.inf)
    tied = v0 == theta
    lower = jnp.where(lax.broadcasted_iota(jnp.int32, (ns, ns), 1) < lax.broadcasted_iota(jnp.int32, (ns, ns), 0),
                      1.0, 0.0).astype(BF16)
    rank = jnp.dot(lower, jnp.where(tied, 1.0, 0.0).astype(BF16), preferred_element_type=F32)
    chosen = forced | (v0 > theta) | (tied & (rank < free - above))
    mask_bias = jnp.where(chosen & valid, 0.0, NEG_BIG).astype(BF16)
    msel_ref[...] = jnp.concatenate([mask_bias] * NSA_REP, axis=1)

    qaug_ref[0:NSA_D, :] = qT
    qaug_ref[NSA_D:, :] = jnp.zeros((qaug_ref.shape[0] - NSA_D, LANES), BF16)

    def reset():
        acc_ref[...] = jnp.zeros_like(acc_ref)
        m_ref[...] = jnp.full(m_ref.shape, M_INIT, F32)

    def tile_rows(j):
        return slice(j * K_TILE, (j + 1) * K_TILE)

    def scores_to_scratch(k_ref, tiles, slots):
        for kt, j in zip(tiles, slots):
            rows = pl.ds(pl.multiple_of(kt * K_TILE, K_TILE), K_TILE)
            s_ref[tile_rows(j), :] = jnp.dot(k_ref[0, 0, rows, :], qaug_ref[...], preferred_element_type=F32)

    def softmax_from_scratch(vT_ref, tiles, slots):
        rows = slice(slots[0] * K_TILE, (slots[-1] + 1) * K_TILE)
        m_old = m_ref[0:1, :]
        m_new = jnp.maximum(m_old, jnp.max(s_ref[rows, :], axis=0, keepdims=True))
        alpha = jnp.exp2(m_old - m_new)
        p = jnp.exp2(s_ref[rows, :] - m_new).astype(BF16)
        vT = jnp.concatenate([vT_ref[0, 0, kt] for kt in tiles], axis=1)
        acc_ref[...] = alpha * acc_ref[...] + jnp.dot(vT, p, preferred_element_type=F32)
        m_ref[0:1, :] = m_new

    def finish(j):
        l = jnp.maximum(acc_ref[NSA_D:NSA_D + 1, :], tiny)
        out_ref[...] += (gate_row(j) * (1.0 / l)) * acc_ref[0:NSA_D, :]

    last = i // TILES_PER_GROUP
    j_i = i % TILES_PER_GROUP

    half = TILES_PER_GROUP // 2

    def sel_tiles(grp, h):
        return [grp * TILES_PER_GROUP + h * half + j for j in range(half)]

    def sel_slots(h):
        return [h * half + j for j in range(half)]

    def sel_scores(grp, h):
        slab = pl.ds(pl.multiple_of(grp * SEL_PER_GROUP, SEL_PER_GROUP), SEL_PER_GROUP)
        qaug_ref[NSA_D:NSA_D + SEL_PER_GROUP, :] = msel_ref[slab, :]
        scores_to_scratch(ks_ref, sel_tiles(grp, h), sel_slots(h))

    def sel_softmax(grp, h):
        softmax_from_scratch(vsT_ref, sel_tiles(grp, h), sel_slots(h))

    def add_bias(slot, count, bias_tile):
        rows = slice(slot * K_TILE, (slot + count) * K_TILE)
        s_ref[rows, :] = s_ref[rows, :] + bs_ref[0, bias_tile * K_TILE:(bias_tile + count) * K_TILE, :]

    reset()
    sel_scores(0, 0)

    def far(grp, carry):
        sel_scores(grp, 1)
        sel_softmax(grp, 0)
        sel_scores(grp + 1, 0)
        sel_softmax(grp, 1)
        return carry

    lax.fori_loop(0, jnp.maximum(last - 1, 0), far, 0)

    @pl.when(last >= 1)
    def _():
        sel_scores(last - 1, 1)
        sel_softmax(last - 1, 0)

        @pl.when(j_i == 0)
        def _():
            add_bias(TILES_PER_GROUP - 1, 1, 0)

        sel_scores(last, 0)
        sel_softmax(last - 1, 1)

    @pl.when(j_i == 0)
    def _():
        add_bias(0, 1, 1)

    @pl.when(j_i == 1)
    def _():
        add_bias(0, 2, 0)

    @pl.when(j_i == 2)
    def _():
        add_bias(1, 1, 0)

    sel_scores(last, 1)
    sel_softmax(last, 0)

    @pl.when(j_i == 2)
    def _():
        add_bias(2, 1, 1)

    @pl.when(j_i == 3)
    def _():
        add_bias(2, 2, 0)

    win_tiles = [jnp.maximum(i - 2 + j, 0) for j in range(3)]
    win_slots = [0, 1, TILES_PER_GROUP]
    scores_to_scratch(kw_ref, win_tiles, win_slots)
    sel_softmax(last, 1)
    finish(1)

    reset()
    for j, slot in enumerate(win_slots):
        before_start = jnp.where(i - 2 + j < 0, NEG_BIG, 0.0).astype(F32)
        s_ref[tile_rows(slot), :] = s_ref[tile_rows(slot), :] + (bw_ref[0, tile_rows(j), :] + before_start)
    softmax_from_scratch(vwT_ref, win_tiles[0:2], win_slots[0:2])
    softmax_from_scratch(vwT_ref, win_tiles[2:3], win_slots[2:3])
    finish(2)

    for r in range(NSA_REP):
        oT_ref[0, r * NSA_D:(r + 1) * NSA_D, :] = out_ref[:, r * Q_TILE:(r + 1) * Q_TILE].astype(oT_ref.dtype)


def nsa_attention(qT, gT, cmp_nat, cmp_tr, ks, vsT, kw, vwT, bc, bs, bw, mmapT, *, batch, seq):
    G = NSA_GROUPS
    nc = seq // CMP_STRIDE
    ns = seq // SEL_LEN
    nq = seq // Q_TILE
    nkt = seq // K_TILE
    hd = NSA_HEADS * NSA_D
    vcT = cmp_tr[1].reshape(batch, G, NSA_D, nc // CMP_CHUNK, CMP_CHUNK).transpose(0, 1, 3, 2, 4)
    once = pl.Buffered(1)
    return pl.pallas_call(
        _nsa_body,
        out_shape=jax.ShapeDtypeStruct((batch, hd, seq), BF16),
        grid=(batch, G, nq),
        in_specs=[
            pl.BlockSpec((1, NSA_REP, NSA_D, Q_TILE), lambda b, g, i: (b, g, 0, i)),
            pl.BlockSpec((1, 1, GATE_ROWS, Q_TILE), lambda b, g, i: (b, g, 0, i)),
            pl.BlockSpec((1, 1, 1, nc, NSA_D), lambda b, g, i: (0, b, g, 0, 0)),
            pl.BlockSpec((1, 1, nc // CMP_CHUNK, NSA_D, CMP_CHUNK), lambda b, g, i: (b, g, 0, 0, 0)),
            pl.BlockSpec((1, 1, seq, 128), lambda b, g, i: (b, g, 0, 0), pipeline_mode=once),
            pl.BlockSpec((1, 1, nkt, V_ROWS, K_TILE), lambda b, g, i: (b, g, 0, 0, 0), pipeline_mode=once),
            pl.BlockSpec((1, 1, seq, 128), lambda b, g, i: (b, g, 0, 0), pipeline_mode=once),
            pl.BlockSpec((1, 1, nkt, V_ROWS, K_TILE), lambda b, g, i: (b, g, 0, 0, 0), pipeline_mode=once),
            pl.BlockSpec((1, 2 * CMP_PER_QTILE, LANES), lambda b, g, i: (g, 0, 0)),
            pl.BlockSpec((1, 2 * K_TILE, LANES), lambda b, g, i: (g, 0, 0), pipeline_mode=once),
            pl.BlockSpec((1, 3 * K_TILE, LANES), lambda b, g, i: (g, 0, 0), pipeline_mode=once),
            pl.BlockSpec((nc // CMP_CHUNK, ns, CMP_CHUNK), lambda b, g, i: (0, 0, 0)),
        ],
        out_specs=pl.BlockSpec((1, NSA_REP * NSA_D, Q_TILE), lambda b, g, i: (b, g, i)),
        scratch_shapes=[
            pltpu.VMEM((max(nc, (TILES_PER_GROUP + 1) * K_TILE), LANES), F32),
            pltpu.VMEM((ns, Q_TILE), F32),
            pltpu.VMEM((ns, LANES), BF16),
            pltpu.VMEM((128, LANES), BF16),
            pltpu.VMEM((V_ROWS, LANES), F32),
            pltpu.VMEM((8, LANES), F32),
            pltpu.VMEM((NSA_D, LANES), F32),
        ],
        compiler_params=_cparams(("parallel", "parallel", "arbitrary"), VMEM_LIMIT),
        name="nsa_attention",
    )(qT, gT, cmp_nat, vcT, ks, vsT, kw, vwT, bc, bs, bw, mmapT)


def _t5_bucket_table(n_max):
    n = np.arange(n_max)
    max_exact = REL_BUCKETS // 2
    nf = np.maximum(n, 1).astype(np.float32)
    large = max_exact + (np.log(nf / np.float32(max_exact))
                         / np.float32(math.log(REL_MAX_DIST / max_exact))
                         * np.float32(REL_BUCKETS - max_exact)).astype(np.int32)
    large = np.minimum(large, REL_BUCKETS - 1)
    return np.where(n < max_exact, n, large).astype(np.int32)


def _bias_tables(rel_bias):
    n_max = WINDOW + Q_TILE
    buckets = _t5_bucket_table(n_max)
    assert (buckets[Q_TILE - CMP_LEN + 1:] == REL_BUCKETS - 1).all()
    tab = rel_bias.astype(F32)[buckets, :] - rel_bias.astype(F32)[REL_BUCKETS - 1][None, :]
    tab = tab.T * LOG2_E
    def by_distance(dist, ok, fill):
        return jnp.where(ok[None], tab[:, np.clip(dist, 0, n_max - 1)], fill)

    def to_lanes(vals):
        rows = vals.shape[1]
        vals = vals.reshape(NSA_GROUPS, NSA_REP, rows, Q_TILE).transpose(0, 2, 1, 3)
        return vals.reshape(NSA_GROUPS, rows, LANES)

    def toeplitz(rows, offset, ok, fill):
        w = rows + Q_TILE
        k = np.arange(w)
        dist = np.where(k < Q_TILE, k, k - w) + offset
        line = by_distance(dist, ok(dist), fill)
        flat = jnp.tile(line, (1, rows))[:, :rows * (w - 1)]
        return to_lanes(flat.reshape(NSA_HEADS, rows, w - 1)[:, :, :Q_TILE])

    c = np.arange(Q_TILE)[None, :]
    r = np.arange(2 * CMP_PER_QTILE)[:, None]
    dist_c = c + (Q_TILE - CMP_LEN + 1) - CMP_STRIDE * r
    bc = to_lanes(by_distance(dist_c, dist_c >= 0, -jnp.inf))
    bs = toeplitz(2 * K_TILE, K_TILE, lambda dd: dd >= 0, NEG_BIG)
    bw = toeplitz(3 * K_TILE, 2 * K_TILE, lambda dd: (dd >= 0) & (dd < WINDOW), NEG_BIG)
    return bc, bs, bw


def _selection_map(seq):
    nc = seq // CMP_STRIDE
    ns = seq // SEL_LEN
    n_cmp = (seq - CMP_LEN) // CMP_STRIDE + 1
    ratio = SEL_LEN // CMP_STRIDE
    lead = CMP_LEN // CMP_STRIDE - 1
    j = np.arange(ns)[:, None]
    n = np.arange(nc)[None, :]
    m = ((n >= ratio * j - lead) & (n < ratio * j + ratio) & (n < n_cmp)).astype(np.float32)
    m = m.reshape(ns, nc // CMP_CHUNK, CMP_CHUNK).transpose(1, 0, 2)
    return jnp.asarray(m, dtype=BF16)


def kernel(x, mix_norm_pre, mix_norm_post, ffn_norm_pre, ffn_norm_post, ffn_w_in, ffn_w_out,
           ret_w_in, ret_w_out, kv_norm, kv_w, cmp_pe_k, cmp_w1_k, cmp_w2_k,
           cmp_pe_v, cmp_w1_v, cmp_w2_v, nsa_w_in, nsa_w_out, rel_bias):
    batch, seq, d = x.shape
    n_ret = ret_w_in.shape[0]
    n_nsa = nsa_w_in.shape[0]
    assert seq % (SEL_PER_GROUP * SEL_LEN) == 0 and seq // SEL_LEN >= SEL_TOPK
    h = x.reshape(batch * seq, d)
    shared = None
    for layer in range(n_ret + n_nsa):
        if layer == n_ret:
            kc_nat, vc_nat, ks, kw, vsT, vwT = kv_project(h, kv_norm, kv_w, batch=batch, seq=seq, tm=1024)
            cmp_nat, cmp_tr = compress_blocks(kc_nat, vc_nat, cmp_pe_k, cmp_w1_k, cmp_w2_k,
                                              cmp_pe_v, cmp_w1_v, cmp_w2_v, batch=batch, seq=seq)
            shared = (cmp_nat, cmp_tr, ks, vsT, kw, vwT) + _bias_tables(rel_bias) + (_selection_map(seq),)
        if layer < n_ret:
            w_in = _deinterleave_qk_columns(ret_w_in[layer]).astype(BF16)
            proj = norm_matmul(h, mix_norm_pre[layer], w_in, tm=1024, tn=2048, out_dtype=BF16)
            mixed = retention_core(proj, *_retention_tables(seq), batch=batch, seq=seq)
            h = proj_norm_res(mixed, ret_w_out[layer].astype(BF16), mix_norm_post[layer], h, tm=512)
        else:
            j = layer - n_ret
            qT, gT = q_project(h, mix_norm_pre[layer], nsa_w_in[j], batch=batch, seq=seq, tm=512)
            oT = nsa_attention(qT, gT, *shared, batch=batch, seq=seq)
            h = projT_norm_res(oT, nsa_w_out[j].astype(BF16), mix_norm_post[layer], h, tm=512)
        h = ffn_block(h, ffn_norm_pre[layer], ffn_w_in[layer].astype(BF16), ffn_w_out[layer].astype(BF16),
                      ffn_norm_post[layer], tm=512, th=FFN_HIDDEN // 2)
    return h.reshape(batch, seq, d)
```

```python
import functools
import math

import numpy as np
import jax
import jax.numpy as jnp
from jax import lax
from jax.experimental import pallas as pl
from jax.experimental.pallas import tpu as pltpu

F32 = jnp.float32
BF16 = jnp.bfloat16

D_MODEL = 1024
RMS_EPS = 1e-6

RET_HEADS = 4
RET_QK = 256
RET_V = 512
RET_CHUNK = 128

FFN_HIDDEN = 2816

NSA_HEADS = 16
NSA_GROUPS = 4
NSA_REP = 4
NSA_D = 64
CMP_LEN = 32
CMP_STRIDE = 16
CMP_HIDDEN = 256
SEL_LEN = 64
SEL_TOPK = 16
WINDOW = 512
REL_BUCKETS = 32
REL_MAX_DIST = 128

Q_TILE = 256
K_TILE = 256
SEL_PER_GROUP = 16
V_ROWS = 80
LOG2_E = math.log2(math.e)
NEG_BIG = -(2.0 ** 100)
M_INIT = -(2.0 ** 120)

VMEM_LIMIT = 56 * 1024 * 1024


def _cparams(sem, vmem=None, flags=None):
    return pltpu.CompilerParams(dimension_semantics=sem, vmem_limit_bytes=vmem, flags=flags)


def _rms(x, g):
    return x * lax.rsqrt(jnp.mean(x * x, axis=-1, keepdims=True) + RMS_EPS) * g


def _norm_matmul_body(x_ref, g_ref, w_ref, o_ref, xn_ref):
    @pl.when(pl.program_id(1) == 0)
    def _():
        xn_ref[...] = _rms(x_ref[...], g_ref[...]).astype(BF16)

    o_ref[...] = jnp.dot(xn_ref[...], w_ref[...], preferred_element_type=F32).astype(o_ref.dtype)


def norm_matmul(x, g, w, *, tm, tn, out_dtype=F32):
    t, d = x.shape
    n = w.shape[1]
    return pl.pallas_call(
        _norm_matmul_body,
        out_shape=jax.ShapeDtypeStruct((t, n), out_dtype),
        grid=(t // tm, n // tn),
        in_specs=[
            pl.BlockSpec((tm, d), lambda i, j: (i, 0)),
            pl.BlockSpec((1, d), lambda i, j: (0, 0)),
            pl.BlockSpec((d, tn), lambda i, j: (0, j)),
        ],
        out_specs=pl.BlockSpec((tm, tn), lambda i, j: (i, j)),
        scratch_shapes=[pltpu.VMEM((tm, d), BF16)],
        compiler_params=_cparams(("parallel", "arbitrary"), VMEM_LIMIT),
        name="norm_matmul",
    )(x, g.reshape(1, d), w)


def _proj_norm_res_body(y_ref, w_ref, g_ref, r_ref, o_ref):
    z = jnp.dot(y_ref[...].astype(BF16), w_ref[...], preferred_element_type=F32)
    o_ref[...] = r_ref[...] + _rms(z, g_ref[...])


def proj_norm_res(y, w, g, res, *, tm):
    t, k = y.shape
    d = w.shape[1]
    return pl.pallas_call(
        _proj_norm_res_body,
        out_shape=jax.ShapeDtypeStruct((t, d), F32),
        grid=(t // tm,),
        in_specs=[
            pl.BlockSpec((tm, k), lambda i: (i, 0)),
            pl.BlockSpec((k, d), lambda i: (0, 0)),
            pl.BlockSpec((1, d), lambda i: (0, 0)),
            pl.BlockSpec((tm, d), lambda i: (i, 0)),
        ],
        out_specs=pl.BlockSpec((tm, d), lambda i: (i, 0)),
        compiler_params=_cparams(("parallel",), VMEM_LIMIT),
        name="proj_norm_res",
    )(y, w, g.reshape(1, d), res)


def _projT_norm_res_body(yT_ref, w_ref, g_ref, r_ref, o_ref):
    z = lax.dot_general(yT_ref[0], w_ref[...], (((0,), (0,)), ((), ())),
                        preferred_element_type=F32)
    o_ref[...] = r_ref[...] + _rms(z, g_ref[...])


def projT_norm_res(yT, w, g, res, *, tm):
    b, k, s = yT.shape
    d = w.shape[1]
    nt = s // tm
    return pl.pallas_call(
        _projT_norm_res_body,
        out_shape=jax.ShapeDtypeStruct((b * s, d), F32),
        grid=(b, nt),
        in_specs=[
            pl.BlockSpec((1, k, tm), lambda bi, i: (bi, 0, i)),
            pl.BlockSpec((k, d), lambda bi, i: (0, 0)),
            pl.BlockSpec((1, d), lambda bi, i: (0, 0)),
            pl.BlockSpec((tm, d), lambda bi, i: (bi * nt + i, 0)),
        ],
        out_specs=pl.BlockSpec((tm, d), lambda bi, i: (bi * nt + i, 0)),
        compiler_params=_cparams(("parallel", "parallel"), VMEM_LIMIT),
        name="projT_norm_res",
    )(yT, w, g.reshape(1, d), res)


def _ffn_body(x_ref, gpre_ref, wg_ref, wu_ref, wo_ref, gpost_ref, o_ref, xn_ref, acc_ref):
    j = pl.program_id(1)

    @pl.when(j == 0)
    def _():
        xn_ref[...] = _rms(x_ref[...], gpre_ref[...]).astype(BF16)

    xn = xn_ref[...]
    gate = jnp.dot(xn, wg_ref[...], preferred_element_type=F32)
    up = jnp.dot(xn, wu_ref[...], preferred_element_type=F32)
    act = (gate * jax.nn.sigmoid(gate) * up).astype(BF16)
    part = jnp.dot(act, wo_ref[...], preferred_element_type=F32)

    @pl.when(j == 0)
    def _():
        acc_ref[...] = part

    @pl.when(j > 0)
    def _():
        acc_ref[...] += part

    @pl.when(j == pl.num_programs(1) - 1)
    def _():
        o_ref[...] = x_ref[...] + _rms(acc_ref[...], gpost_ref[...])


def ffn_block(x, g_pre, w_in, w_out, g_post, *, tm, th):
    t, d = x.shape
    hdim = w_out.shape[0]
    nh = hdim // th
    return pl.pallas_call(
        _ffn_body,
        out_shape=jax.ShapeDtypeStruct((t, d), F32),
        grid=(t // tm, nh),
        in_specs=[
            pl.BlockSpec((tm, d), lambda i, j: (i, 0)),
            pl.BlockSpec((1, d), lambda i, j: (0, 0)),
            pl.BlockSpec((d, th), lambda i, j: (0, j)),
            pl.BlockSpec((d, th), lambda i, j: (0, nh + j)),
            pl.BlockSpec((th, d), lambda i, j: (j, 0)),
            pl.BlockSpec((1, d), lambda i, j: (0, 0)),
        ],
        out_specs=pl.BlockSpec((tm, d), lambda i, j: (i, 0)),
        scratch_shapes=[pltpu.VMEM((tm, d), BF16), pltpu.VMEM((tm, d), F32)],
        compiler_params=_cparams(("parallel", "arbitrary"), VMEM_LIMIT),
        name="ffn_block",
    )(x, g_pre.reshape(1, d), w_in, w_in, w_out, g_post.reshape(1, d))


def _retention_body(q_ref, k_ref, v_ref, g_ref, cos_ref, sin_ref, dmat_ref, qdec_ref, kdec_ref,
                    cdec_ref, o_ref, state_ref):
    @pl.when(pl.program_id(1) == 0)
    def _():
        state_ref[...] = jnp.zeros_like(state_ref)

    cos = cos_ref[...]
    sin = sin_ref[...]
    half = RET_QK // 2

    def rotate(x_ref, h):
        x1 = x_ref[:, h * RET_QK:h * RET_QK + half].astype(F32)
        x2 = x_ref[:, h * RET_QK + half:(h + 1) * RET_QK].astype(F32)
        return jnp.concatenate([x1 * cos - x2 * sin, x1 * sin + x2 * cos], axis=1)

    for h in range(RET_HEADS):
        qr = rotate(q_ref, h)
        kr = rotate(k_ref, h) * (RET_QK ** -0.5)
        v = v_ref[:, h * RET_V:(h + 1) * RET_V].astype(BF16)
        scores = lax.dot_general(qr.astype(BF16), kr.astype(BF16), (((1,), (1,)), ((), ())),
                                 preferred_element_type=F32) * dmat_ref[h]
        state = state_ref[h]
        o = (jnp.dot(scores.astype(BF16), v, preferred_element_type=F32)
             + jnp.dot((qr * qdec_ref[h]).astype(BF16), state.astype(BF16),
                       preferred_element_type=F32))
        kd = (kr * kdec_ref[h]).astype(BF16)
        state_ref[h] = state * cdec_ref[h, 0:1, :] + lax.dot_general(
            kd, v, (((0,), (0,)), ((), ())), preferred_element_type=F32)
        o = o * lax.rsqrt(jnp.mean(o * o, axis=-1, keepdims=True) + RMS_EPS)
        gate = g_ref[:, h * RET_V:(h + 1) * RET_V].astype(F32)
        o_ref[:, h * RET_V:(h + 1) * RET_V] = (o * (gate * jax.nn.sigmoid(gate))).astype(o_ref.dtype)


def retention_core(proj, cos, sin, dmat, qdec, kdec, cdec, *, batch, seq):
    c = RET_CHUNK
    nc = seq // c
    hq = RET_HEADS * RET_QK
    hv = RET_HEADS * RET_V
    return pl.pallas_call(
        _retention_body,
        out_shape=jax.ShapeDtypeStruct((batch * seq, hv), BF16),
        grid=(batch, nc),
        in_specs=[
            pl.BlockSpec((c, hq), lambda b, t: (b * nc + t, 0)),
            pl.BlockSpec((c, hq), lambda b, t: (b * nc + t, 1)),
            pl.BlockSpec((c, hv), lambda b, t: (b * nc + t, 1)),
            pl.BlockSpec((c, hv), lambda b, t: (b * nc + t, 2)),
            pl.BlockSpec((c, RET_QK // 2), lambda b, t: (t, 0)),
            pl.BlockSpec((c, RET_QK // 2), lambda b, t: (t, 0)),
            pl.BlockSpec((RET_HEADS, c, c), lambda b, t: (0, 0, 0)),
            pl.BlockSpec((RET_HEADS, c, RET_QK), lambda b, t: (0, 0, 0)),
            pl.BlockSpec((RET_HEADS, c, RET_QK), lambda b, t: (0, 0, 0)),
            pl.BlockSpec((RET_HEADS, 8, RET_V), lambda b, t: (0, 0, 0)),
        ],
        out_specs=pl.BlockSpec((c, hv), lambda b, t: (b * nc + t, 0)),
        scratch_shapes=[pltpu.VMEM((RET_HEADS, RET_QK, RET_V), F32)],
        compiler_params=_cparams(("parallel", "arbitrary"), VMEM_LIMIT),
        name="retention_core",
    )(proj, proj, proj, proj, cos, sin, dmat, qdec, kdec, cdec)


def _retention_tables(seq):
    h, dk, c = RET_HEADS, RET_QK, RET_CHUNK
    pos = jnp.arange(seq, dtype=F32)
    theta = 1.0 / (10000.0 ** jnp.linspace(0.0, 1.0, dk // 2, dtype=F32))
    ang = pos[:, None] * theta[None, :]
    log_gamma = jnp.log(1.0 - 2.0 ** (-5.0 - jnp.arange(h, dtype=F32)))
    idx = jnp.arange(c, dtype=F32)
    rel = idx[:, None] - idx[None, :]
    dmat = jnp.where(rel >= 0, jnp.exp(jnp.maximum(rel, 0.0) * log_gamma[:, None, None]), 0.0)
    qdec = jnp.exp((idx + 1.0)[None, :] * log_gamma[:, None])
    kdec = jnp.exp((c - 1.0 - idx)[None, :] * log_gamma[:, None])
    cdec = jnp.exp(c * log_gamma)
    qdec = jnp.broadcast_to(qdec[:, :, None], (h, c, dk))
    kdec = jnp.broadcast_to(kdec[:, :, None], (h, c, dk))
    cdec = jnp.broadcast_to(cdec[:, None, None], (h, 8, RET_V))
    return jnp.cos(ang), jnp.sin(ang), dmat, qdec, kdec, cdec


def _deinterleave_qk_columns(w):
    nqk = 2 * RET_HEADS * RET_QK
    perm = []
    for h in range(2 * RET_HEADS):
        base = h * RET_QK
        perm += [base + 2 * i for i in range(RET_QK // 2)]
        perm += [base + 2 * i + 1 for i in range(RET_QK // 2)]
    perm = np.asarray(perm + list(range(nqk, w.shape[1])), dtype=np.int32)
    return w[:, perm]


def _kv_body(x_ref, g_ref, wk_ref, wvT_ref, kc_ref, vc_ref, ks_ref, kw_ref, vsT_ref, vwT_ref):
    xn = _rms(x_ref[...], g_ref[...]).astype(BF16)
    tm = xn.shape[0]
    kall = jnp.dot(xn, wk_ref[...], preferred_element_type=F32)
    gd = NSA_GROUPS * NSA_D
    kc_ref[...] = kall[:, 0:gd]
    vc_ref[...] = kall[:, gd:2 * gd]
    row = pl.program_id(1) * tm + lax.broadcasted_iota(jnp.int32, (tm, 128), 0)
    lane = lax.broadcasted_iota(jnp.int32, (tm, 128), 1)
    blk = (row // SEL_LEN) % SEL_PER_GROUP
    onehot = jnp.where(lane - NSA_D == blk, 1.0, 0.0).astype(F32)
    for g in range(NSA_GROUPS):
        ks = kall[:, 2 * gd + 128 * g:2 * gd + 128 * (g + 1)]
        ks_ref[0, g] = (ks + onehot).astype(BF16)
        kw = kall[:, 2 * gd + 512 + 128 * g:2 * gd + 512 + 128 * (g + 1)]
        kw_ref[0, g] = kw.astype(BF16)
    vT = lax.dot_general(wvT_ref[...], xn, (((1,), (1,)), ((), ())),
                         preferred_element_type=F32)
    extra = jnp.where(lax.broadcasted_iota(jnp.int32, (V_ROWS - NSA_D, K_TILE), 0) == 0, 1.0, 0.0)
    extra = extra.astype(BF16)
    for g in range(NSA_GROUPS):
        for c in range(tm // K_TILE):
            cols = slice(c * K_TILE, (c + 1) * K_TILE)
            vsT_ref[0, g, c, 0:NSA_D, :] = vT[NSA_D * g:NSA_D * (g + 1), cols].astype(BF16)
            vsT_ref[0, g, c, NSA_D:V_ROWS, :] = extra
            vwT_ref[0, g, c, 0:NSA_D, :] = vT[gd + NSA_D * g:gd + NSA_D * (g + 1), cols].astype(BF16)
            vwT_ref[0, g, c, NSA_D:V_ROWS, :] = extra


def kv_project(x, g, kv_w, *, batch, seq, tm):
    d = x.shape[1]
    gd = NSA_GROUPS * NSA_D
    k_c, v_c, k_s, v_s, k_w, v_w = [kv_w[:, i * gd:(i + 1) * gd] for i in range(6)]

    def pad_groups(w):
        w = w.reshape(d, NSA_GROUPS, NSA_D)
        return jnp.pad(w, ((0, 0), (0, 0), (0, 128 - NSA_D))).reshape(d, NSA_GROUPS * 128)

    wk = jnp.concatenate([k_c, v_c, pad_groups(k_s), pad_groups(k_w)], axis=1).astype(BF16)
    wvT = jnp.concatenate([v_s, v_w], axis=1).T.astype(BF16)
    nt = seq // tm
    G = NSA_GROUPS
    return pl.pallas_call(
        _kv_body,
        out_shape=(
            jax.ShapeDtypeStruct((batch * seq, gd), F32),
            jax.ShapeDtypeStruct((batch * seq, gd), F32),
            jax.ShapeDtypeStruct((batch, G, seq, 128), BF16),
            jax.ShapeDtypeStruct((batch, G, seq, 128), BF16),
            jax.ShapeDtypeStruct((batch, G, seq // K_TILE, V_ROWS, K_TILE), BF16),
            jax.ShapeDtypeStruct((batch, G, seq // K_TILE, V_ROWS, K_TILE), BF16),
        ),
        grid=(batch, nt),
        in_specs=[
            pl.BlockSpec((tm, d), lambda b, i: (b * nt + i, 0)),
            pl.BlockSpec((1, d), lambda b, i: (0, 0)),
            pl.BlockSpec(wk.shape, lambda b, i: (0, 0)),
            pl.BlockSpec(wvT.shape, lambda b, i: (0, 0)),
        ],
        out_specs=(
            pl.BlockSpec((tm, gd), lambda b, i: (b * nt + i, 0)),
            pl.BlockSpec((tm, gd), lambda b, i: (b * nt + i, 0)),
            pl.BlockSpec((1, G, tm, 128), lambda b, i: (b, 0, i, 0)),
            pl.BlockSpec((1, G, tm, 128), lambda b, i: (b, 0, i, 0)),
            pl.BlockSpec((1, G, tm // K_TILE, V_ROWS, K_TILE), lambda b, i: (b, 0, i, 0, 0)),
            pl.BlockSpec((1, G, tm // K_TILE, V_ROWS, K_TILE), lambda b, i: (b, 0, i, 0, 0)),
        ),
        compiler_params=_cparams(("parallel", "parallel"), VMEM_LIMIT),
        name="kv_project",
    )(x, g.reshape(1, d), wk, wvT)


def _compress_body(c_ref, pe_ref, w1_ref, w2_ref, w2T_ref, nat_ref, tr_ref, sh_ref):
    half = CMP_STRIDE * NSA_D
    nc = c_ref.shape[3]
    c = c_ref[0, 0, 0].astype(BF16)
    w1 = w1_ref[0]
    first = jnp.dot(c, w1[0:half], preferred_element_type=F32)
    second = jnp.dot(c, w1[half:2 * half], preferred_element_type=F32)
    pe_term = jnp.dot(pe_ref[0].astype(BF16), w1, preferred_element_type=F32)
    sh_ref[0:nc, :] = second
    sh_ref[nc:nc + 8, :] = jnp.zeros((8, CMP_HIDDEN), F32)
    pre = first + sh_ref[1:nc + 1, :] + pe_term[0:1, :]
    hid = (pre * jax.nn.sigmoid(pre)).astype(BF16)
    nat_ref[0, 0, 0] = jnp.dot(hid, w2_ref[0], preferred_element_type=F32).astype(BF16)
    tr_ref[0, 0, 0] = lax.dot_general(w2T_ref[0], hid, (((1,), (1,)), ((), ())),
                                      preferred_element_type=F32).astype(BF16)


def compress_blocks(kc_nat, vc_nat, pe_k, w1_k, w2_k, pe_v, w1_v, w2_v, *, batch, seq):
    G, d = NSA_GROUPS, NSA_D
    nc = seq // CMP_STRIDE

    def to_rows(t):
        t = t.reshape(batch, nc, CMP_STRIDE, G, d).transpose(0, 3, 1, 2, 4)
        return t.reshape(batch, G, nc, CMP_STRIDE * d)

    c_all = jnp.stack([to_rows(kc_nat), to_rows(vc_nat)])
    pe = jnp.stack([pe_k.reshape(1, -1), pe_v.reshape(1, -1)])
    pe = jnp.broadcast_to(pe, (2, 8, CMP_LEN * d))
    w1 = jnp.stack([w1_k, w1_v]).astype(BF16)
    w2 = jnp.stack([w2_k, w2_v]).astype(BF16)
    w2T = jnp.stack([w2_k.T, w2_v.T]).astype(BF16)
    return pl.pallas_call(
        _compress_body,
        out_shape=(
            jax.ShapeDtypeStruct((2, batch, G, nc, d), BF16),
            jax.ShapeDtypeStruct((2, batch, G, d, nc), BF16),
        ),
        grid=(2, batch, G),
        in_specs=[
            pl.BlockSpec((1, 1, 1, nc, CMP_STRIDE * d), lambda w, b, g: (w, b, g, 0, 0)),
            pl.BlockSpec((1, 8, CMP_LEN * d), lambda w, b, g: (w, 0, 0)),
            pl.BlockSpec((1, CMP_LEN * d, CMP_HIDDEN), lambda w, b, g: (w, 0, 0)),
            pl.BlockSpec((1, CMP_HIDDEN, d), lambda w, b, g: (w, 0, 0)),
            pl.BlockSpec((1, d, CMP_HIDDEN), lambda w, b, g: (w, 0, 0)),
        ],
        out_specs=(
            pl.BlockSpec((1, 1, 1, nc, d), lambda w, b, g: (w, b, g, 0, 0)),
            pl.BlockSpec((1, 1, 1, d, nc), lambda w, b, g: (w, b, g, 0, 0)),
        ),
        scratch_shapes=[pltpu.VMEM((nc + 8, CMP_HIDDEN), F32)],
        compiler_params=_cparams(("parallel", "parallel", "parallel"), VMEM_LIMIT),
        name="compress_blocks",
    )(c_all, pe, w1, w2, w2T)


GATE_ROWS = 16


def _qproj_body(x_ref, g_ref, wT_ref, qT_ref, gT_ref):
    xn = _rms(x_ref[...], g_ref[...]).astype(BF16)
    tm = xn.shape[0]
    pT = lax.dot_general(wT_ref[...], xn, (((1,), (1,)), ((), ())),
                         preferred_element_type=F32)
    hd = NSA_HEADS * NSA_D
    q = pT[0:hd] * (NSA_D ** -0.5 * LOG2_E)
    qT_ref[0] = q.reshape(NSA_HEADS, NSA_D, tm).astype(BF16)
    gates = jax.nn.sigmoid(pT[hd:hd + NSA_GROUPS * GATE_ROWS])
    gT_ref[0] = gates.reshape(NSA_GROUPS, GATE_ROWS, tm)


def q_project(x, g, w_in, *, batch, seq, tm):
    d = x.shape[1]
    hd = NSA_HEADS * NSA_D
    per_group = NSA_REP * 3
    wg = w_in[:, hd:].reshape(d, NSA_GROUPS, per_group)
    wg = jnp.pad(wg, ((0, 0), (0, 0), (0, GATE_ROWS - per_group))).reshape(d, NSA_GROUPS * GATE_ROWS)
    wT = jnp.concatenate([w_in[:, :hd], wg], axis=1).T.astype(BF16)
    nt = seq // tm
    return pl.pallas_call(
        _qproj_body,
        out_shape=(
            jax.ShapeDtypeStruct((batch, NSA_HEADS, NSA_D, seq), BF16),
            jax.ShapeDtypeStruct((batch, NSA_GROUPS, GATE_ROWS, seq), F32),
        ),
        grid=(batch, nt),
        in_specs=[
            pl.BlockSpec((tm, d), lambda b, i: (b * nt + i, 0)),
            pl.BlockSpec((1, d), lambda b, i: (0, 0)),
            pl.BlockSpec(wT.shape, lambda b, i: (0, 0)),
        ],
        out_specs=(
            pl.BlockSpec((1, NSA_HEADS, NSA_D, tm), lambda b, i: (b, 0, 0, i)),
            pl.BlockSpec((1, NSA_GROUPS, GATE_ROWS, tm), lambda b, i: (b, 0, 0, i)),
        ),
        compiler_params=_cparams(("parallel", "parallel"), VMEM_LIMIT),
        name="q_project",
    )(x, g.reshape(1, d), wT)


CMP_CHUNK = 256
CMP_PER_QTILE = Q_TILE // CMP_STRIDE
SEL_PER_QTILE = Q_TILE // SEL_LEN
LANES = NSA_REP * Q_TILE
TILES_PER_GROUP = SEL_PER_GROUP * SEL_LEN // K_TILE
BIAS_LINE = 1024


def _nsa_body(qT_ref, gT_ref, kc_ref, vcT_ref, ks_ref, vsT_ref, kw_ref, vwT_ref,
              lines_ref, mmapT_ref, oT_ref,
              s_ref, imp_ref, msel_ref, qaug_ref, acc_ref, m_ref, out_ref, bc_ref, bs_ref, bw_ref):
    i = pl.program_id(2)
    ns = imp_ref.shape[0]
    qT = jnp.concatenate([qT_ref[0, r] for r in range(NSA_REP)], axis=1)
    tiny = jnp.finfo(F32).tiny

    @pl.when(i == 0)
    def _():
        def expand(kind, r, rows, stride):
            line = lines_ref[0, kind, r:r + 1, :]
            shifted = pltpu.roll(jnp.broadcast_to(line, (rows, line.shape[1])), 0, 1,
                                 stride=stride, stride_axis=0)
            return shifted[:, 0:Q_TILE]

        for r in range(NSA_REP):
            cols = slice(r * Q_TILE, (r + 1) * Q_TILE)
            bc_ref[:, cols] = expand(0, r, bc_ref.shape[0], CMP_STRIDE)
            bs_ref[:, cols] = expand(1, r, bs_ref.shape[0], 1)
            bw_ref[:, cols] = expand(2, r, bw_ref.shape[0], 1)

    def gate_row(j):
        return jnp.concatenate([gT_ref[0, 0, 3 * r + j:3 * r + j + 1, :] for r in range(NSA_REP)], axis=1)

    nchunks = i // (CMP_CHUNK // CMP_PER_QTILE) + 1
    visible = CMP_PER_QTILE * (i + 1)

    def chunk_rows(c):
        return pl.ds(pl.multiple_of(c * CMP_CHUNK, CMP_CHUNK), CMP_CHUNK)

    window_start = CMP_PER_QTILE * (i - 1)

    def cmp_scores(c, m):
        s = jnp.dot(kc_ref[0, 0, 0, chunk_rows(c), :], qT, preferred_element_type=F32)
        s_ref[chunk_rows(c), :] = s
        rid = c * CMP_CHUNK + lax.broadcasted_iota(jnp.int32, (CMP_CHUNK, LANES), 0)
        return jnp.maximum(m, jnp.max(jnp.where(rid < window_start, s, -jnp.inf), axis=0, keepdims=True))

    m_c = lax.fori_loop(0, nchunks, cmp_scores, jnp.full((1, LANES), -jnp.inf, F32))
    m_ref[0:1, :] = m_c

    @pl.when(i == 0)
    def _():
        s = s_ref[0:CMP_PER_QTILE, :] + bc_ref[CMP_PER_QTILE:2 * CMP_PER_QTILE, :]
        s_ref[0:CMP_PER_QTILE, :] = s
        m_ref[0:1, :] = jnp.maximum(m_ref[0:1, :], jnp.max(s, axis=0, keepdims=True))

    @pl.when(i > 0)
    def _():
        rows = pl.ds(pl.multiple_of(window_start, CMP_PER_QTILE), 2 * CMP_PER_QTILE)
        s = s_ref[rows, :] + bc_ref[...]
        s_ref[rows, :] = s
        m_ref[0:1, :] = jnp.maximum(m_ref[0:1, :], jnp.max(s, axis=0, keepdims=True))

    m_c = m_ref[0:1, :]
    m_c = jnp.where(jnp.isfinite(m_c), m_c, 0.0)

    def cmp_exp(c, l):
        rid = c * CMP_CHUNK + lax.broadcasted_iota(jnp.int32, (CMP_CHUNK, LANES), 0)
        p = jnp.exp2(jnp.where(rid < visible, s_ref[chunk_rows(c), :], -jnp.inf) - m_c)
        s_ref[chunk_rows(c), :] = p
        return l + jnp.sum(p, axis=0, keepdims=True)

    l_c = lax.fori_loop(0, nchunks, cmp_exp, jnp.zeros((1, LANES), F32))
    inv_c = 1.0 / jnp.maximum(l_c, tiny)

    acc_ref[...] = jnp.zeros_like(acc_ref)
    imp_ref[...] = jnp.zeros_like(imp_ref)

    def cmp_pv(c, carry):
        p = s_ref[chunk_rows(c), :] * inv_c
        acc_ref[0:NSA_D, :] += jnp.dot(vcT_ref[0, 0, c], p.astype(BF16), preferred_element_type=F32)
        ic = p[:, 0:Q_TILE]
        for r in range(1, NSA_REP):
            ic = ic + p[:, r * Q_TILE:(r + 1) * Q_TILE]
        hi = ic.astype(BF16)
        rem = ic - hi.astype(F32)
        mid = rem.astype(BF16)
        lo = (rem - mid.astype(F32)).astype(BF16)
        mm = mmapT_ref[c]
        imp_ref[...] += (jnp.dot(mm, hi, preferred_element_type=F32)
                         + jnp.dot(mm, mid, preferred_element_type=F32)
                         + jnp.dot(mm, lo, preferred_element_type=F32))
        return carry

    lax.fori_loop(0, nchunks, cmp_pv, 0)
    out_ref[...] = gate_row(0) * acc_ref[0:NSA_D, :]

    jrow = lax.broadcasted_iota(jnp.int32, (ns, Q_TILE), 0)
    col = lax.broadcasted_iota(jnp.int32, (ns, Q_TILE), 1)
    cur = SEL_PER_QTILE * i + col // SEL_LEN
    valid = jrow <= cur
    forced = (jrow == 0) | (jrow == cur) | (jrow == cur - 1)
    free = float(SEL_TOPK - 3)
    candidate = valid & jnp.logical_not(forced)
    v0 = jnp.where(candidate, imp_ref[...], -jnp.inf)

    def strip_max(_, carry):
        v, taken, theta, above = carry
        best = jnp.max(v, axis=0, keepdims=True)
        hit = v == best
        now = taken + jnp.sum(jnp.where(hit, 1.0, 0.0), axis=0, keepdims=True)
        crossed = (taken < free) & (now >= free)
        return (jnp.where(hit, -jnp.inf, v), now,
                jnp.where(crossed, best, theta), jnp.where(crossed, taken, above))

    zero_row = jnp.zeros((1, Q_TILE), F32)
    _, _, theta, above = lax.fori_loop(
        0, SEL_TOPK - 3, strip_max, (v0, zero_row, jnp.full((1, Q_TILE), jnp.inf, F32), zero_row))
    v0 = jnp.where(candidate, imp_ref[...], -jnp.inf)
    tied = v0 == theta
    lower = jnp.where(lax.broadcasted_iota(jnp.int32, (ns, ns), 1) < lax.broadcasted_iota(jnp.int32, (ns, ns), 0),
                      1.0, 0.0).astype(BF16)
    rank = jnp.dot(lower, jnp.where(tied, 1.0, 0.0).astype(BF16), preferred_element_type=F32)
    chosen = forced | (v0 > theta) | (tied & (rank < free - above))
    mask_bias = jnp.where(chosen & valid, 0.0, NEG_BIG).astype(BF16)
    msel_ref[...] = jnp.concatenate([mask_bias] * NSA_REP, axis=1)

    qaug_ref[0:NSA_D, :] = qT
    qaug_ref[NSA_D:, :] = jnp.zeros((qaug_ref.shape[0] - NSA_D, LANES), BF16)

    def reset():
        acc_ref[...] = jnp.zeros_like(acc_ref)
        m_ref[...] = jnp.full(m_ref.shape, M_INIT, F32)

    def tile_rows(j):
        return slice(j * K_TILE, (j + 1) * K_TILE)

    def scores_to_scratch(k_ref, first_tile, count, slot):
        rows = pl.ds(pl.multiple_of(first_tile * K_TILE, K_TILE), count * K_TILE)
        s_ref[slot * K_TILE:(slot + count) * K_TILE, :] = jnp.dot(
            k_ref[0, 0, rows, :], qaug_ref[...], preferred_element_type=F32)

    def softmax_from_scratch(vT_ref, tiles, slots):
        rows = slice(slots[0] * K_TILE, (slots[-1] + 1) * K_TILE)
        m_old = m_ref[0:1, :]
        m_new = jnp.maximum(m_old, jnp.max(s_ref[rows, :], axis=0, keepdims=True))
        alpha = jnp.exp2(m_old - m_new)
        p = jnp.exp2(s_ref[rows, :] - m_new).astype(BF16)
        vT = jnp.concatenate([vT_ref[0, 0, kt] for kt in tiles], axis=1)
        acc_ref[...] = alpha * acc_ref[...] + jnp.dot(vT, p, preferred_element_type=F32)
        m_ref[0:1, :] = m_new

    def finish(j):
        l = jnp.maximum(acc_ref[NSA_D:NSA_D + 1, :], tiny)
        out_ref[...] += (gate_row(j) * (1.0 / l)) * acc_ref[0:NSA_D, :]

    last = i // TILES_PER_GROUP
    j_i = i % TILES_PER_GROUP

    half = TILES_PER_GROUP // 2

    def sel_tiles(grp, h):
        return [grp * TILES_PER_GROUP + h * half + j for j in range(half)]

    def sel_slots(h):
        return [h * half + j for j in range(half)]

    def sel_scores(grp, h):
        slab = pl.ds(pl.multiple_of(grp * SEL_PER_GROUP, SEL_PER_GROUP), SEL_PER_GROUP)
        qaug_ref[NSA_D:NSA_D + SEL_PER_GROUP, :] = msel_ref[slab, :]
        scores_to_scratch(ks_ref, sel_tiles(grp, h)[0], half, h * half)

    def sel_softmax(grp, h):
        softmax_from_scratch(vsT_ref, sel_tiles(grp, h), sel_slots(h))

    def add_bias(slot, count, bias_tile):
        rows = slice(slot * K_TILE, (slot + count) * K_TILE)
        s_ref[rows, :] = s_ref[rows, :] + bs_ref[bias_tile * K_TILE:(bias_tile + count) * K_TILE, :]

    reset()
    sel_scores(0, 0)

    def far(grp, carry):
        sel_scores(grp, 1)
        sel_softmax(grp, 0)
        sel_scores(grp + 1, 0)
        sel_softmax(grp, 1)
        return carry

    lax.fori_loop(0, jnp.maximum(last - 1, 0), far, 0)

    @pl.when(last >= 1)
    def _():
        sel_scores(last - 1, 1)
        sel_softmax(last - 1, 0)

        @pl.when(j_i == 0)
        def _():
            add_bias(TILES_PER_GROUP - 1, 1, 0)

        sel_scores(last, 0)
        sel_softmax(last - 1, 1)

    @pl.when(j_i == 0)
    def _():
        add_bias(0, 1, 1)

    @pl.when(j_i == 1)
    def _():
        add_bias(0, 2, 0)

    @pl.when(j_i == 2)
    def _():
        add_bias(1, 1, 0)

    sel_scores(last, 1)
    sel_softmax(last, 0)

    @pl.when(j_i == 2)
    def _():
        add_bias(2, 1, 1)

    @pl.when(j_i == 3)
    def _():
        add_bias(2, 2, 0)

    win_tiles = [jnp.maximum(i - 2 + j, 0) for j in range(3)]
    win_slots = [0, 1, TILES_PER_GROUP]
    for kt, slot in zip(win_tiles, win_slots):
        scores_to_scratch(kw_ref, kt, 1, slot)
    sel_softmax(last, 1)
    finish(1)

    reset()
    for j, slot in enumerate(win_slots):
        before_start = jnp.where(i - 2 + j < 0, NEG_BIG, 0.0).astype(F32)
        s_ref[tile_rows(slot), :] = s_ref[tile_rows(slot), :] + (bw_ref[tile_rows(j), :] + before_start)
    softmax_from_scratch(vwT_ref, win_tiles[0:2], win_slots[0:2])
    softmax_from_scratch(vwT_ref, win_tiles[2:3], win_slots[2:3])
    finish(2)

    for r in range(NSA_REP):
        oT_ref[0, r * NSA_D:(r + 1) * NSA_D, :] = out_ref[:, r * Q_TILE:(r + 1) * Q_TILE].astype(oT_ref.dtype)


def nsa_attention(qT, gT, cmp_nat, cmp_tr, ks, vsT, kw, vwT, lines, mmapT, *, batch, seq):
    G = NSA_GROUPS
    nc = seq // CMP_STRIDE
    ns = seq // SEL_LEN
    nq = seq // Q_TILE
    nkt = seq // K_TILE
    hd = NSA_HEADS * NSA_D
    vcT = cmp_tr[1].reshape(batch, G, NSA_D, nc // CMP_CHUNK, CMP_CHUNK).transpose(0, 1, 3, 2, 4)
    once = pl.Buffered(1)
    return pl.pallas_call(
        _nsa_body,
        out_shape=jax.ShapeDtypeStruct((batch, hd, seq), BF16),
        grid=(batch, G, nq),
        in_specs=[
            pl.BlockSpec((1, NSA_REP, NSA_D, Q_TILE), lambda b, g, i: (b, g, 0, i)),
            pl.BlockSpec((1, 1, GATE_ROWS, Q_TILE), lambda b, g, i: (b, g, 0, i)),
            pl.BlockSpec((1, 1, 1, nc, NSA_D), lambda b, g, i: (0, b, g, 0, 0)),
            pl.BlockSpec((1, 1, nc // CMP_CHUNK, NSA_D, CMP_CHUNK), lambda b, g, i: (b, g, 0, 0, 0)),
            pl.BlockSpec((1, 1, seq, 128), lambda b, g, i: (b, g, 0, 0), pipeline_mode=once),
            pl.BlockSpec((1, 1, nkt, V_ROWS, K_TILE), lambda b, g, i: (b, g, 0, 0, 0), pipeline_mode=once),
            pl.BlockSpec((1, 1, seq, 128), lambda b, g, i: (b, g, 0, 0), pipeline_mode=once),
            pl.BlockSpec((1, 1, nkt, V_ROWS, K_TILE), lambda b, g, i: (b, g, 0, 0, 0), pipeline_mode=once),
            pl.BlockSpec((1, 3, NSA_REP, BIAS_LINE), lambda b, g, i: (g, 0, 0, 0)),
            pl.BlockSpec((nc // CMP_CHUNK, ns, CMP_CHUNK), lambda b, g, i: (0, 0, 0)),
        ],
        out_specs=pl.BlockSpec((1, NSA_REP * NSA_D, Q_TILE), lambda b, g, i: (b, g, i)),
        scratch_shapes=[
            pltpu.VMEM((max(nc, (TILES_PER_GROUP + 1) * K_TILE), LANES), F32),
            pltpu.VMEM((ns, Q_TILE), F32),
            pltpu.VMEM((ns, LANES), BF16),
            pltpu.VMEM((128, LANES), BF16),
            pltpu.VMEM((V_ROWS, LANES), F32),
            pltpu.VMEM((8, LANES), F32),
            pltpu.VMEM((NSA_D, LANES), F32),
            pltpu.VMEM((2 * CMP_PER_QTILE, LANES), F32),
            pltpu.VMEM((2 * K_TILE, LANES), F32),
            pltpu.VMEM((3 * K_TILE, LANES), F32),
        ],
        compiler_params=_cparams(("parallel", "parallel", "arbitrary"), VMEM_LIMIT),
        name="nsa_attention",
    )(qT, gT, cmp_nat, vcT, ks, vsT, kw, vwT, lines, mmapT)


def _t5_bucket_table(n_max):
    n = np.arange(n_max)
    max_exact = REL_BUCKETS // 2
    nf = np.maximum(n, 1).astype(np.float32)
    large = max_exact + (np.log(nf / np.float32(max_exact))
                         / np.float32(math.log(REL_MAX_DIST / max_exact))
                         * np.float32(REL_BUCKETS - max_exact)).astype(np.int32)
    large = np.minimum(large, REL_BUCKETS - 1)
    return np.where(n < max_exact, n, large).astype(np.int32)


def _bias_tables(rel_bias):
    n_max = WINDOW + Q_TILE
    buckets = _t5_bucket_table(n_max)
    assert (buckets[Q_TILE - CMP_LEN + 1:] == REL_BUCKETS - 1).all()
    tab = rel_bias.astype(F32)[buckets, :] - rel_bias.astype(F32)[REL_BUCKETS - 1][None, :]
    tab = tab.T * LOG2_E
    def line(rows, stride, offset, ok, fill):
        assert Q_TILE + stride * (rows - 1) <= BIAS_LINE
        k = np.arange(BIAS_LINE)
        dist = np.where(k < Q_TILE, k, k - BIAS_LINE) + offset
        vals = jnp.where(ok(dist)[None], tab[:, np.clip(dist, 0, n_max - 1)], fill)
        return vals.reshape(NSA_GROUPS, NSA_REP, BIAS_LINE)

    return jnp.stack([
        line(2 * CMP_PER_QTILE, CMP_STRIDE, Q_TILE - CMP_LEN + 1, lambda dd: dd >= 0, -jnp.inf),
        line(2 * K_TILE, 1, K_TILE, lambda dd: dd >= 0, NEG_BIG),
        line(3 * K_TILE, 1, 2 * K_TILE, lambda dd: (dd >= 0) & (dd < WINDOW), NEG_BIG),
    ], axis=1)


def _selection_map(seq):
    nc = seq // CMP_STRIDE
    ns = seq // SEL_LEN
    n_cmp = (seq - CMP_LEN) // CMP_STRIDE + 1
    ratio = SEL_LEN // CMP_STRIDE
    lead = CMP_LEN // CMP_STRIDE - 1
    j = np.arange(ns)[:, None]
    n = np.arange(nc)[None, :]
    m = ((n >= ratio * j - lead) & (n < ratio * j + ratio) & (n < n_cmp)).astype(np.float32)
    m = m.reshape(ns, nc // CMP_CHUNK, CMP_CHUNK).transpose(1, 0, 2)
    return jnp.asarray(m, dtype=BF16)


def kernel(x, mix_norm_pre, mix_norm_post, ffn_norm_pre, ffn_norm_post, ffn_w_in, ffn_w_out,
           ret_w_in, ret_w_out, kv_norm, kv_w, cmp_pe_k, cmp_w1_k, cmp_w2_k,
           cmp_pe_v, cmp_w1_v, cmp_w2_v, nsa_w_in, nsa_w_out, rel_bias):
    batch, seq, d = x.shape
    n_ret = ret_w_in.shape[0]
    n_nsa = nsa_w_in.shape[0]
    assert seq % (SEL_PER_GROUP * SEL_LEN) == 0 and seq // SEL_LEN >= SEL_TOPK
    h = x.reshape(batch * seq, d)
    shared = None
    for layer in range(n_ret + n_nsa):
        if layer == n_ret:
            kc_nat, vc_nat, ks, kw, vsT, vwT = kv_project(h, kv_norm, kv_w, batch=batch, seq=seq, tm=1024)
            cmp_nat, cmp_tr = compress_blocks(kc_nat, vc_nat, cmp_pe_k, cmp_w1_k, cmp_w2_k,
                                              cmp_pe_v, cmp_w1_v, cmp_w2_v, batch=batch, seq=seq)
            shared = (cmp_nat, cmp_tr, ks, vsT, kw, vwT, _bias_tables(rel_bias), _selection_map(seq))
        if layer < n_ret:
            w_in = _deinterleave_qk_columns(ret_w_in[layer]).astype(BF16)
            proj = norm_matmul(h, mix_norm_pre[layer], w_in, tm=1024, tn=2048, out_dtype=BF16)
            mixed = retention_core(proj, *_retention_tables(seq), batch=batch, seq=seq)
            h = proj_norm_res(mixed, ret_w_out[layer].astype(BF16), mix_norm_post[layer], h, tm=512)
        else:
            j = layer - n_ret
            qT, gT = q_project(h, mix_norm_pre[layer], nsa_w_in[j], batch=batch, seq=seq, tm=512)
            oT = nsa_attention(qT, gT, *shared, batch=batch, seq=seq)
            h = projT_norm_res(oT, nsa_w_out[j].astype(BF16), mix_norm_post[layer], h, tm=512)
        h = ffn_block(h, ffn_norm_pre[layer], ffn_w_in[layer].astype(BF16), ffn_w_out[layer].astype(BF16),
                      ffn_norm_post[layer], tm=512, th=FFN_HIDDEN // 2)
    return h.reshape(batch, seq, d)
```

```python
import functools
import math

import numpy as np
import jax
import jax.numpy as jnp
from jax import lax
from jax.experimental import pallas as pl
from jax.experimental.pallas import tpu as pltpu

F32 = jnp.float32
BF16 = jnp.bfloat16

D_MODEL = 1024
RMS_EPS = 1e-6

RET_HEADS = 4
RET_QK = 256
RET_V = 512
RET_CHUNK = 128

FFN_HIDDEN = 2816

NSA_HEADS = 16
NSA_GROUPS = 4
NSA_REP = 4
NSA_D = 64
CMP_LEN = 32
CMP_STRIDE = 16
CMP_HIDDEN = 256
SEL_LEN = 64
SEL_TOPK = 16
WINDOW = 512
REL_BUCKETS = 32
REL_MAX_DIST = 128

Q_TILE = 256
K_TILE = 256
SEL_PER_GROUP = 16
V_ROWS = 80
LOG2_E = math.log2(math.e)
NEG_BIG = -(2.0 ** 100)
M_INIT = -(2.0 ** 120)

VMEM_LIMIT = 56 * 1024 * 1024


def _cparams(sem, vmem=None, flags=None):
    return pltpu.CompilerParams(dimension_semantics=sem, vmem_limit_bytes=vmem, flags=flags)


def _rms(x, g):
    return x * lax.rsqrt(jnp.mean(x * x, axis=-1, keepdims=True) + RMS_EPS) * g


def _norm_matmul_body(x_ref, g_ref, w_ref, o_ref, xn_ref):
    @pl.when(pl.program_id(1) == 0)
    def _():
        xn_ref[...] = _rms(x_ref[...], g_ref[...]).astype(BF16)

    o_ref[...] = jnp.dot(xn_ref[...], w_ref[...], preferred_element_type=F32).astype(o_ref.dtype)


def norm_matmul(x, g, w, *, tm, tn, out_dtype=F32):
    t, d = x.shape
    n = w.shape[1]
    return pl.pallas_call(
        _norm_matmul_body,
        out_shape=jax.ShapeDtypeStruct((t, n), out_dtype),
        grid=(t // tm, n // tn),
        in_specs=[
            pl.BlockSpec((tm, d), lambda i, j: (i, 0)),
            pl.BlockSpec((1, d), lambda i, j: (0, 0)),
            pl.BlockSpec((d, tn), lambda i, j: (0, j)),
        ],
        out_specs=pl.BlockSpec((tm, tn), lambda i, j: (i, j)),
        scratch_shapes=[pltpu.VMEM((tm, d), BF16)],
        compiler_params=_cparams(("parallel", "arbitrary"), VMEM_LIMIT),
        name="norm_matmul",
    )(x, g.reshape(1, d), w)


def _proj_norm_res_body(y_ref, w_ref, g_ref, r_ref, o_ref):
    z = jnp.dot(y_ref[...].astype(BF16), w_ref[...], preferred_element_type=F32)
    o_ref[...] = r_ref[...] + _rms(z, g_ref[...])


def proj_norm_res(y, w, g, res, *, tm):
    t, k = y.shape
    d = w.shape[1]
    return pl.pallas_call(
        _proj_norm_res_body,
        out_shape=jax.ShapeDtypeStruct((t, d), F32),
        grid=(t // tm,),
        in_specs=[
            pl.BlockSpec((tm, k), lambda i: (i, 0)),
            pl.BlockSpec((k, d), lambda i: (0, 0)),
            pl.BlockSpec((1, d), lambda i: (0, 0)),
            pl.BlockSpec((tm, d), lambda i: (i, 0)),
        ],
        out_specs=pl.BlockSpec((tm, d), lambda i: (i, 0)),
        compiler_params=_cparams(("parallel",), VMEM_LIMIT),
        name="proj_norm_res",
    )(y, w, g.reshape(1, d), res)


def _projT_norm_res_body(yT_ref, w_ref, g_ref, r_ref, o_ref):
    z = lax.dot_general(yT_ref[0], w_ref[...], (((0,), (0,)), ((), ())),
                        preferred_element_type=F32)
    o_ref[...] = r_ref[...] + _rms(z, g_ref[...])


def projT_norm_res(yT, w, g, res, *, tm):
    b, k, s = yT.shape
    d = w.shape[1]
    nt = s // tm
    return pl.pallas_call(
        _projT_norm_res_body,
        out_shape=jax.ShapeDtypeStruct((b * s, d), F32),
        grid=(b, nt),
        in_specs=[
            pl.BlockSpec((1, k, tm), lambda bi, i: (bi, 0, i)),
            pl.BlockSpec((k, d), lambda bi, i: (0, 0)),
            pl.BlockSpec((1, d), lambda bi, i: (0, 0)),
            pl.BlockSpec((tm, d), lambda bi, i: (bi * nt + i, 0)),
        ],
        out_specs=pl.BlockSpec((tm, d), lambda bi, i: (bi * nt + i, 0)),
        compiler_params=_cparams(("parallel", "parallel"), VMEM_LIMIT),
        name="projT_norm_res",
    )(yT, w, g.reshape(1, d), res)


def _ffn_body(x_ref, gpre_ref, win_ref, wo_ref, gpost_ref, o_ref, *, chunks):
    x = x_ref[...]
    xn = _rms(x, gpre_ref[...]).astype(BF16)
    hdim = wo_ref.shape[0]
    th = hdim // chunks
    y = None
    for c in range(chunks):
        gate = jnp.dot(xn, win_ref[:, c * th:(c + 1) * th], preferred_element_type=F32)
        up = jnp.dot(xn, win_ref[:, hdim + c * th:hdim + (c + 1) * th], preferred_element_type=F32)
        act = (gate * jax.nn.sigmoid(gate) * up).astype(BF16)
        part = jnp.dot(act, wo_ref[c * th:(c + 1) * th, :], preferred_element_type=F32)
        y = part if y is None else y + part
    o_ref[...] = x + _rms(y, gpost_ref[...])


def ffn_block(x, g_pre, w_in, w_out, g_post, *, tm, chunks):
    t, d = x.shape
    hdim = w_out.shape[0]
    assert (hdim // chunks) % 128 == 0 and hdim % chunks == 0
    once = pl.Buffered(1)
    return pl.pallas_call(
        functools.partial(_ffn_body, chunks=chunks),
        out_shape=jax.ShapeDtypeStruct((t, d), F32),
        grid=(t // tm,),
        in_specs=[
            pl.BlockSpec((tm, d), lambda i: (i, 0)),
            pl.BlockSpec((1, d), lambda i: (0, 0)),
            pl.BlockSpec((d, 2 * hdim), lambda i: (0, 0), pipeline_mode=once),
            pl.BlockSpec((hdim, d), lambda i: (0, 0), pipeline_mode=once),
            pl.BlockSpec((1, d), lambda i: (0, 0)),
        ],
        out_specs=pl.BlockSpec((tm, d), lambda i: (i, 0)),
        compiler_params=_cparams(("parallel",), VMEM_LIMIT),
        name="ffn_block",
    )(x, g_pre.reshape(1, d), w_in, w_out, g_post.reshape(1, d))


def _retention_body(q_ref, k_ref, v_ref, g_ref, cos_ref, sin_ref, dmat_ref, qdec_ref, kdec_ref,
                    cdec_ref, o_ref, state_ref):
    @pl.when(pl.program_id(1) == 0)
    def _():
        state_ref[...] = jnp.zeros_like(state_ref)

    cos = cos_ref[...]
    sin = sin_ref[...]
    half = RET_QK // 2

    def rotate(x_ref, h):
        x1 = x_ref[:, h * RET_QK:h * RET_QK + half].astype(F32)
        x2 = x_ref[:, h * RET_QK + half:(h + 1) * RET_QK].astype(F32)
        return jnp.concatenate([x1 * cos - x2 * sin, x1 * sin + x2 * cos], axis=1)

    for h in range(RET_HEADS):
        qr = rotate(q_ref, h)
        kr = rotate(k_ref, h) * (RET_QK ** -0.5)
        v = v_ref[:, h * RET_V:(h + 1) * RET_V].astype(BF16)
        scores = lax.dot_general(qr.astype(BF16), kr.astype(BF16), (((1,), (1,)), ((), ())),
                                 preferred_element_type=F32) * dmat_ref[h]
        state = state_ref[h]
        o = (jnp.dot(scores.astype(BF16), v, preferred_element_type=F32)
             + jnp.dot((qr * qdec_ref[h]).astype(BF16), state.astype(BF16),
                       preferred_element_type=F32))
        kd = (kr * kdec_ref[h]).astype(BF16)
        state_ref[h] = state * cdec_ref[h, 0:1, :] + lax.dot_general(
            kd, v, (((0,), (0,)), ((), ())), preferred_element_type=F32)
        o = o * lax.rsqrt(jnp.mean(o * o, axis=-1, keepdims=True) + RMS_EPS)
        gate = g_ref[:, h * RET_V:(h + 1) * RET_V].astype(F32)
        o_ref[:, h * RET_V:(h + 1) * RET_V] = (o * (gate * jax.nn.sigmoid(gate))).astype(o_ref.dtype)


def retention_core(proj, cos, sin, dmat, qdec, kdec, cdec, *, batch, seq):
    c = RET_CHUNK
    nc = seq // c
    hq = RET_HEADS * RET_QK
    hv = RET_HEADS * RET_V
    return pl.pallas_call(
        _retention_body,
        out_shape=jax.ShapeDtypeStruct((batch * seq, hv), BF16),
        grid=(batch, nc),
        in_specs=[
            pl.BlockSpec((c, hq), lambda b, t: (b * nc + t, 0)),
            pl.BlockSpec((c, hq), lambda b, t: (b * nc + t, 1)),
            pl.BlockSpec((c, hv), lambda b, t: (b * nc + t, 1)),
            pl.BlockSpec((c, hv), lambda b, t: (b * nc + t, 2)),
            pl.BlockSpec((c, RET_QK // 2), lambda b, t: (t, 0)),
            pl.BlockSpec((c, RET_QK // 2), lambda b, t: (t, 0)),
            pl.BlockSpec((RET_HEADS, c, c), lambda b, t: (0, 0, 0)),
            pl.BlockSpec((RET_HEADS, c, RET_QK), lambda b, t: (0, 0, 0)),
            pl.BlockSpec((RET_HEADS, c, RET_QK), lambda b, t: (0, 0, 0)),
            pl.BlockSpec((RET_HEADS, 8, RET_V), lambda b, t: (0, 0, 0)),
        ],
        out_specs=pl.BlockSpec((c, hv), lambda b, t: (b * nc + t, 0)),
        scratch_shapes=[pltpu.VMEM((RET_HEADS, RET_QK, RET_V), F32)],
        compiler_params=_cparams(("parallel", "arbitrary"), VMEM_LIMIT),
        name="retention_core",
    )(proj, proj, proj, proj, cos, sin, dmat, qdec, kdec, cdec)


def _retention_tables(seq):
    h, dk, c = RET_HEADS, RET_QK, RET_CHUNK
    pos = jnp.arange(seq, dtype=F32)
    theta = 1.0 / (10000.0 ** jnp.linspace(0.0, 1.0, dk // 2, dtype=F32))
    ang = pos[:, None] * theta[None, :]
    log_gamma = jnp.log(1.0 - 2.0 ** (-5.0 - jnp.arange(h, dtype=F32)))
    idx = jnp.arange(c, dtype=F32)
    rel = idx[:, None] - idx[None, :]
    dmat = jnp.where(rel >= 0, jnp.exp(jnp.maximum(rel, 0.0) * log_gamma[:, None, None]), 0.0)
    qdec = jnp.exp((idx + 1.0)[None, :] * log_gamma[:, None])
    kdec = jnp.exp((c - 1.0 - idx)[None, :] * log_gamma[:, None])
    cdec = jnp.exp(c * log_gamma)
    qdec = jnp.broadcast_to(qdec[:, :, None], (h, c, dk))
    kdec = jnp.broadcast_to(kdec[:, :, None], (h, c, dk))
    cdec = jnp.broadcast_to(cdec[:, None, None], (h, 8, RET_V))
    return jnp.cos(ang), jnp.sin(ang), dmat, qdec, kdec, cdec


def _deinterleave_qk_columns(w):
    nqk = 2 * RET_HEADS * RET_QK
    perm = []
    for h in range(2 * RET_HEADS):
        base = h * RET_QK
        perm += [base + 2 * i for i in range(RET_QK // 2)]
        perm += [base + 2 * i + 1 for i in range(RET_QK // 2)]
    perm = np.asarray(perm + list(range(nqk, w.shape[1])), dtype=np.int32)
    return w[:, perm]


def _kv_body(x_ref, g_ref, wk_ref, wvT_ref, kc_ref, vc_ref, ks_ref, kw_ref, vsT_ref, vwT_ref):
    xn = _rms(x_ref[...], g_ref[...]).astype(BF16)
    tm = xn.shape[0]
    kall = jnp.dot(xn, wk_ref[...], preferred_element_type=F32)
    gd = NSA_GROUPS * NSA_D
    kc_ref[...] = kall[:, 0:gd]
    vc_ref[...] = kall[:, gd:2 * gd]
    row = pl.program_id(1) * tm + lax.broadcasted_iota(jnp.int32, (tm, 128), 0)
    lane = lax.broadcasted_iota(jnp.int32, (tm, 128), 1)
    blk = (row // SEL_LEN) % SEL_PER_GROUP
    onehot = jnp.where(lane - NSA_D == blk, 1.0, 0.0).astype(F32)
    for g in range(NSA_GROUPS):
        ks = kall[:, 2 * gd + 128 * g:2 * gd + 128 * (g + 1)]
        ks_ref[0, g] = (ks + onehot).astype(BF16)
        kw = kall[:, 2 * gd + 512 + 128 * g:2 * gd + 512 + 128 * (g + 1)]
        kw_ref[0, g] = kw.astype(BF16)
    vT = lax.dot_general(wvT_ref[...], xn, (((1,), (1,)), ((), ())),
                         preferred_element_type=F32)
    extra = jnp.where(lax.broadcasted_iota(jnp.int32, (V_ROWS - NSA_D, K_TILE), 0) == 0, 1.0, 0.0)
    extra = extra.astype(BF16)
    for g in range(NSA_GROUPS):
        for c in range(tm // K_TILE):
            cols = slice(c * K_TILE, (c + 1) * K_TILE)
            vsT_ref[0, g, c, 0:NSA_D, :] = vT[NSA_D * g:NSA_D * (g + 1), cols].astype(BF16)
            vsT_ref[0, g, c, NSA_D:V_ROWS, :] = extra
            vwT_ref[0, g, c, 0:NSA_D, :] = vT[gd + NSA_D * g:gd + NSA_D * (g + 1), cols].astype(BF16)
            vwT_ref[0, g, c, NSA_D:V_ROWS, :] = extra


def kv_project(x, g, kv_w, *, batch, seq, tm):
    d = x.shape[1]
    gd = NSA_GROUPS * NSA_D
    k_c, v_c, k_s, v_s, k_w, v_w = [kv_w[:, i * gd:(i + 1) * gd] for i in range(6)]

    def pad_groups(w):
        w = w.reshape(d, NSA_GROUPS, NSA_D)
        return jnp.pad(w, ((0, 0), (0, 0), (0, 128 - NSA_D))).reshape(d, NSA_GROUPS * 128)

    wk = jnp.concatenate([k_c, v_c, pad_groups(k_s), pad_groups(k_w)], axis=1).astype(BF16)
    wvT = jnp.concatenate([v_s, v_w], axis=1).T.astype(BF16)
    nt = seq // tm
    G = NSA_GROUPS
    return pl.pallas_call(
        _kv_body,
        out_shape=(
            jax.ShapeDtypeStruct((batch * seq, gd), F32),
            jax.ShapeDtypeStruct((batch * seq, gd), F32),
            jax.ShapeDtypeStruct((batch, G, seq, 128), BF16),
            jax.ShapeDtypeStruct((batch, G, seq, 128), BF16),
            jax.ShapeDtypeStruct((batch, G, seq // K_TILE, V_ROWS, K_TILE), BF16),
            jax.ShapeDtypeStruct((batch, G, seq // K_TILE, V_ROWS, K_TILE), BF16),
        ),
        grid=(batch, nt),
        in_specs=[
            pl.BlockSpec((tm, d), lambda b, i: (b * nt + i, 0)),
            pl.BlockSpec((1, d), lambda b, i: (0, 0)),
            pl.BlockSpec(wk.shape, lambda b, i: (0, 0)),
            pl.BlockSpec(wvT.shape, lambda b, i: (0, 0)),
        ],
        out_specs=(
            pl.BlockSpec((tm, gd), lambda b, i: (b * nt + i, 0)),
            pl.BlockSpec((tm, gd), lambda b, i: (b * nt + i, 0)),
            pl.BlockSpec((1, G, tm, 128), lambda b, i: (b, 0, i, 0)),
            pl.BlockSpec((1, G, tm, 128), lambda b, i: (b, 0, i, 0)),
            pl.BlockSpec((1, G, tm // K_TILE, V_ROWS, K_TILE), lambda b, i: (b, 0, i, 0, 0)),
            pl.BlockSpec((1, G, tm // K_TILE, V_ROWS, K_TILE), lambda b, i: (b, 0, i, 0, 0)),
        ),
        compiler_params=_cparams(("parallel", "parallel"), VMEM_LIMIT),
        name="kv_project",
    )(x, g.reshape(1, d), wk, wvT)


def _compress_body(c_ref, pe_ref, w1_ref, w2_ref, w2T_ref, nat_ref, tr_ref, sh_ref):
    half = CMP_STRIDE * NSA_D
    nc = c_ref.shape[3]
    c = c_ref[0, 0, 0].astype(BF16)
    w1 = w1_ref[0]
    first = jnp.dot(c, w1[0:half], preferred_element_type=F32)
    second = jnp.dot(c, w1[half:2 * half], preferred_element_type=F32)
    pe_term = jnp.dot(pe_ref[0].astype(BF16), w1, preferred_element_type=F32)
    sh_ref[0:nc, :] = second
    sh_ref[nc:nc + 8, :] = jnp.zeros((8, CMP_HIDDEN), F32)
    pre = first + sh_ref[1:nc + 1, :] + pe_term[0:1, :]
    hid = (pre * jax.nn.sigmoid(pre)).astype(BF16)
    nat_ref[0, 0, 0] = jnp.dot(hid, w2_ref[0], preferred_element_type=F32).astype(BF16)
    tr_ref[0, 0, 0] = lax.dot_general(w2T_ref[0], hid, (((1,), (1,)), ((), ())),
                                      preferred_element_type=F32).astype(BF16)


def compress_blocks(kc_nat, vc_nat, pe_k, w1_k, w2_k, pe_v, w1_v, w2_v, *, batch, seq):
    G, d = NSA_GROUPS, NSA_D
    nc = seq // CMP_STRIDE

    def to_rows(t):
        t = t.reshape(batch, nc, CMP_STRIDE, G, d).transpose(0, 3, 1, 2, 4)
        return t.reshape(batch, G, nc, CMP_STRIDE * d)

    c_all = jnp.stack([to_rows(kc_nat), to_rows(vc_nat)])
    pe = jnp.stack([pe_k.reshape(1, -1), pe_v.reshape(1, -1)])
    pe = jnp.broadcast_to(pe, (2, 8, CMP_LEN * d))
    w1 = jnp.stack([w1_k, w1_v]).astype(BF16)
    w2 = jnp.stack([w2_k, w2_v]).astype(BF16)
    w2T = jnp.stack([w2_k.T, w2_v.T]).astype(BF16)
    return pl.pallas_call(
        _compress_body,
        out_shape=(
            jax.ShapeDtypeStruct((2, batch, G, nc, d), BF16),
            jax.ShapeDtypeStruct((2, batch, G, d, nc), BF16),
        ),
        grid=(2, batch, G),
        in_specs=[
            pl.BlockSpec((1, 1, 1, nc, CMP_STRIDE * d), lambda w, b, g: (w, b, g, 0, 0)),
            pl.BlockSpec((1, 8, CMP_LEN * d), lambda w, b, g: (w, 0, 0)),
            pl.BlockSpec((1, CMP_LEN * d, CMP_HIDDEN), lambda w, b, g: (w, 0, 0)),
            pl.BlockSpec((1, CMP_HIDDEN, d), lambda w, b, g: (w, 0, 0)),
            pl.BlockSpec((1, d, CMP_HIDDEN), lambda w, b, g: (w, 0, 0)),
        ],
        out_specs=(
            pl.BlockSpec((1, 1, 1, nc, d), lambda w, b, g: (w, b, g, 0, 0)),
            pl.BlockSpec((1, 1, 1, d, nc), lambda w, b, g: (w, b, g, 0, 0)),
        ),
        scratch_shapes=[pltpu.VMEM((nc + 8, CMP_HIDDEN), F32)],
        compiler_params=_cparams(("parallel", "parallel", "parallel"), VMEM_LIMIT),
        name="compress_blocks",
    )(c_all, pe, w1, w2, w2T)


GATE_ROWS = 16


def _qproj_body(x_ref, g_ref, wT_ref, qT_ref, gT_ref):
    xn = _rms(x_ref[...], g_ref[...]).astype(BF16)
    tm = xn.shape[0]
    pT = lax.dot_general(wT_ref[...], xn, (((1,), (1,)), ((), ())),
                         preferred_element_type=F32)
    hd = NSA_HEADS * NSA_D
    q = pT[0:hd] * (NSA_D ** -0.5 * LOG2_E)
    qT_ref[0] = q.reshape(NSA_HEADS, NSA_D, tm).astype(BF16)
    gates = jax.nn.sigmoid(pT[hd:hd + NSA_GROUPS * GATE_ROWS])
    gT_ref[0] = gates.reshape(NSA_GROUPS, GATE_ROWS, tm)


def q_project(x, g, w_in, *, batch, seq, tm):
    d = x.shape[1]
    hd = NSA_HEADS * NSA_D
    per_group = NSA_REP * 3
    wg = w_in[:, hd:].reshape(d, NSA_GROUPS, per_group)
    wg = jnp.pad(wg, ((0, 0), (0, 0), (0, GATE_ROWS - per_group))).reshape(d, NSA_GROUPS * GATE_ROWS)
    wT = jnp.concatenate([w_in[:, :hd], wg], axis=1).T.astype(BF16)
    nt = seq // tm
    return pl.pallas_call(
        _qproj_body,
        out_shape=(
            jax.ShapeDtypeStruct((batch, NSA_HEADS, NSA_D, seq), BF16),
            jax.ShapeDtypeStruct((batch, NSA_GROUPS, GATE_ROWS, seq), F32),
        ),
        grid=(batch, nt),
        in_specs=[
            pl.BlockSpec((tm, d), lambda b, i: (b * nt + i, 0)),
            pl.BlockSpec((1, d), lambda b, i: (0, 0)),
            pl.BlockSpec(wT.shape, lambda b, i: (0, 0)),
        ],
        out_specs=(
            pl.BlockSpec((1, NSA_HEADS, NSA_D, tm), lambda b, i: (b, 0, 0, i)),
            pl.BlockSpec((1, NSA_GROUPS, GATE_ROWS, tm), lambda b, i: (b, 0, 0, i)),
        ),
        compiler_params=_cparams(("parallel", "parallel"), VMEM_LIMIT),
        name="q_project",
    )(x, g.reshape(1, d), wT)


CMP_CHUNK = 256
CMP_PER_QTILE = Q_TILE // CMP_STRIDE
SEL_PER_QTILE = Q_TILE // SEL_LEN
LANES = NSA_REP * Q_TILE
TILES_PER_GROUP = SEL_PER_GROUP * SEL_LEN // K_TILE
BIAS_LINE = 1024
WIN_TILES = (WINDOW + Q_TILE) // K_TILE


def _nsa_body(qT_ref, gT_ref, kc_ref, vcT_ref, ks_ref, vsT_ref, kw_ref, vwT_ref,
              lines_ref, mmapT_ref, oT_ref,
              s_ref, imp_ref, msel_ref, qaug_ref, acc_ref, m_ref, out_ref, bc_ref, bs_ref, bw_ref):
    i = pl.program_id(2)
    ns = imp_ref.shape[0]
    qT = jnp.concatenate([qT_ref[0, r] for r in range(NSA_REP)], axis=1)
    tiny = jnp.finfo(F32).tiny

    @pl.when(i == 0)
    def _():
        def expand(kind, r, rows, stride):
            line = lines_ref[0, kind, r:r + 1, :]
            shifted = pltpu.roll(jnp.broadcast_to(line, (rows, line.shape[1])), 0, 1,
                                 stride=stride, stride_axis=0)
            return shifted[:, 0:Q_TILE]

        for r in range(NSA_REP):
            cols = slice(r * Q_TILE, (r + 1) * Q_TILE)
            bc_ref[:, cols] = expand(0, r, bc_ref.shape[0], CMP_STRIDE)
            bs_ref[:, cols] = expand(1, r, bs_ref.shape[0], 1)
            bw_ref[:, cols] = expand(2, r, bw_ref.shape[0], 1)

    def gate_row(j):
        return jnp.concatenate([gT_ref[0, 0, 3 * r + j:3 * r + j + 1, :] for r in range(NSA_REP)], axis=1)

    nchunks = i // (CMP_CHUNK // CMP_PER_QTILE) + 1
    visible = CMP_PER_QTILE * (i + 1)

    def chunk_rows(c):
        return pl.ds(pl.multiple_of(c * CMP_CHUNK, CMP_CHUNK), CMP_CHUNK)

    window_start = CMP_PER_QTILE * (i - 1)

    def cmp_scores(c, m):
        s = jnp.dot(kc_ref[0, 0, 0, chunk_rows(c), :], qT, preferred_element_type=F32)
        s_ref[chunk_rows(c), :] = s
        rid = c * CMP_CHUNK + lax.broadcasted_iota(jnp.int32, (CMP_CHUNK, LANES), 0)
        return jnp.maximum(m, jnp.max(jnp.where(rid < window_start, s, -jnp.inf), axis=0, keepdims=True))

    m_c = lax.fori_loop(0, nchunks, cmp_scores, jnp.full((1, LANES), -jnp.inf, F32))
    m_ref[0:1, :] = m_c

    @pl.when(i == 0)
    def _():
        s = s_ref[0:CMP_PER_QTILE, :] + bc_ref[CMP_PER_QTILE:2 * CMP_PER_QTILE, :]
        s_ref[0:CMP_PER_QTILE, :] = s
        m_ref[0:1, :] = jnp.maximum(m_ref[0:1, :], jnp.max(s, axis=0, keepdims=True))

    @pl.when(i > 0)
    def _():
        rows = pl.ds(pl.multiple_of(window_start, CMP_PER_QTILE), 2 * CMP_PER_QTILE)
        s = s_ref[rows, :] + bc_ref[...]
        s_ref[rows, :] = s
        m_ref[0:1, :] = jnp.maximum(m_ref[0:1, :], jnp.max(s, axis=0, keepdims=True))

    m_c = m_ref[0:1, :]
    m_c = jnp.where(jnp.isfinite(m_c), m_c, 0.0)

    def cmp_exp(c, l):
        rid = c * CMP_CHUNK + lax.broadcasted_iota(jnp.int32, (CMP_CHUNK, LANES), 0)
        p = jnp.exp2(jnp.where(rid < visible, s_ref[chunk_rows(c), :], -jnp.inf) - m_c)
        s_ref[chunk_rows(c), :] = p
        return l + jnp.sum(p, axis=0, keepdims=True)

    l_c = lax.fori_loop(0, nchunks, cmp_exp, jnp.zeros((1, LANES), F32))
    inv_c = 1.0 / jnp.maximum(l_c, tiny)

    acc_ref[...] = jnp.zeros_like(acc_ref)
    imp_ref[...] = jnp.zeros_like(imp_ref)

    def cmp_pv(c, carry):
        p = s_ref[chunk_rows(c), :] * inv_c
        acc_ref[0:NSA_D, :] += jnp.dot(vcT_ref[0, 0, c], p.astype(BF16), preferred_element_type=F32)
        ic = p[:, 0:Q_TILE]
        for r in range(1, NSA_REP):
            ic = ic + p[:, r * Q_TILE:(r + 1) * Q_TILE]
        hi = ic.astype(BF16)
        rem = ic - hi.astype(F32)
        mid = rem.astype(BF16)
        lo = (rem - mid.astype(F32)).astype(BF16)
        mm = mmapT_ref[c]
        imp_ref[...] += (jnp.dot(mm, hi, preferred_element_type=F32)
                         + jnp.dot(mm, mid, preferred_element_type=F32)
                         + jnp.dot(mm, lo, preferred_element_type=F32))
        return carry

    lax.fori_loop(0, nchunks, cmp_pv, 0)
    out_ref[...] = gate_row(0) * acc_ref[0:NSA_D, :]

    jrow = lax.broadcasted_iota(jnp.int32, (ns, Q_TILE), 0)
    col = lax.broadcasted_iota(jnp.int32, (ns, Q_TILE), 1)
    cur = SEL_PER_QTILE * i + col // SEL_LEN
    valid = jrow <= cur
    forced = (jrow == 0) | (jrow == cur) | (jrow == cur - 1)
    free = float(SEL_TOPK - 3)
    candidate = valid & jnp.logical_not(forced)
    v0 = jnp.where(candidate, imp_ref[...], -jnp.inf)

    def strip_max(_, carry):
        v, taken, theta, above = carry
        best = jnp.max(v, axis=0, keepdims=True)
        hit = v == best
        now = taken + jnp.sum(jnp.where(hit, 1.0, 0.0), axis=0, keepdims=True)
        crossed = (taken < free) & (now >= free)
        return (jnp.where(hit, -jnp.inf, v), now,
                jnp.where(crossed, best, theta), jnp.where(crossed, taken, above))

    zero_row = jnp.zeros((1, Q_TILE), F32)
    _, _, theta, above = lax.fori_loop(
        0, SEL_TOPK - 3, strip_max, (v0, zero_row, jnp.full((1, Q_TILE), jnp.inf, F32), zero_row))
    v0 = jnp.where(candidate, imp_ref[...], -jnp.inf)
    tied = v0 == theta
    lower = jnp.where(lax.broadcasted_iota(jnp.int32, (ns, ns), 1) < lax.broadcasted_iota(jnp.int32, (ns, ns), 0),
                      1.0, 0.0).astype(BF16)
    rank = jnp.dot(lower, jnp.where(tied, 1.0, 0.0).astype(BF16), preferred_element_type=F32)
    chosen = forced | (v0 > theta) | (tied & (rank < free - above))
    mask_bias = jnp.where(chosen & valid, 0.0, NEG_BIG).astype(BF16)
    msel_ref[...] = jnp.concatenate([mask_bias] * NSA_REP, axis=1)

    qaug_ref[0:NSA_D, :] = qT
    qaug_ref[NSA_D:, :] = jnp.zeros((qaug_ref.shape[0] - NSA_D, LANES), BF16)

    def reset():
        acc_ref[...] = jnp.zeros_like(acc_ref)
        m_ref[...] = jnp.full(m_ref.shape, M_INIT, F32)

    def tile_rows(j):
        return slice(j * K_TILE, (j + 1) * K_TILE)

    def scores_to_scratch(k_ref, first_tile, count, slot):
        rows = pl.ds(pl.multiple_of(first_tile * K_TILE, K_TILE), count * K_TILE)
        s_ref[slot * K_TILE:(slot + count) * K_TILE, :] = jnp.dot(
            k_ref[0, 0, rows, :], qaug_ref[...], preferred_element_type=F32)

    def softmax_from_scratch(vT_ref, tiles, slots):
        rows = slice(slots[0] * K_TILE, (slots[-1] + 1) * K_TILE)
        m_old = m_ref[0:1, :]
        m_new = jnp.maximum(m_old, jnp.max(s_ref[rows, :], axis=0, keepdims=True))
        alpha = jnp.exp2(m_old - m_new)
        p = jnp.exp2(s_ref[rows, :] - m_new).astype(BF16)
        vT = jnp.concatenate([vT_ref[0, 0, kt] for kt in tiles], axis=1)
        acc_ref[...] = alpha * acc_ref[...] + jnp.dot(vT, p, preferred_element_type=F32)
        m_ref[0:1, :] = m_new

    def finish(j):
        l = jnp.maximum(acc_ref[NSA_D:NSA_D + 1, :], tiny)
        out_ref[...] += (gate_row(j) * (1.0 / l)) * acc_ref[0:NSA_D, :]

    last = i // TILES_PER_GROUP
    j_i = i % TILES_PER_GROUP

    half = TILES_PER_GROUP // 2

    def sel_tiles(grp, h):
        return [grp * TILES_PER_GROUP + h * half + j for j in range(half)]

    def sel_slots(h):
        return [h * half + j for j in range(half)]

    def sel_scores(grp, h):
        slab = pl.ds(pl.multiple_of(grp * SEL_PER_GROUP, SEL_PER_GROUP), SEL_PER_GROUP)
        qaug_ref[NSA_D:NSA_D + SEL_PER_GROUP, :] = msel_ref[slab, :]
        scores_to_scratch(ks_ref, sel_tiles(grp, h)[0], half, h * half)

    def sel_softmax(grp, h):
        softmax_from_scratch(vsT_ref, sel_tiles(grp, h), sel_slots(h))

    def add_bias(slot, count, bias_tile):
        rows = slice(slot * K_TILE, (slot + count) * K_TILE)
        s_ref[rows, :] = s_ref[rows, :] + bs_ref[bias_tile * K_TILE:(bias_tile + count) * K_TILE, :]

    reset()
    sel_scores(0, 0)

    def far(grp):
        sel_scores(grp, 1)
        sel_softmax(grp, 0)
        sel_scores(grp + 1, 0)
        sel_softmax(grp, 1)

    def far_pair(pair, carry):
        far(2 * pair)
        far(2 * pair + 1)
        return carry

    n_far = jnp.maximum(last - 1, 0)
    lax.fori_loop(0, n_far // 2, far_pair, 0)

    @pl.when(n_far % 2 == 1)
    def _():
        far(n_far - 1)

    @pl.when(last >= 1)
    def _():
        sel_scores(last - 1, 1)
        sel_softmax(last - 1, 0)

        @pl.when(j_i == 0)
        def _():
            add_bias(TILES_PER_GROUP - 1, 1, 0)

        sel_scores(last, 0)
        sel_softmax(last - 1, 1)

    @pl.when(j_i == 0)
    def _():
        add_bias(0, 1, 1)

    @pl.when(j_i == 1)
    def _():
        add_bias(0, 2, 0)

    @pl.when(j_i == 2)
    def _():
        add_bias(1, 1, 0)

    win_tiles = [jnp.maximum(i - 2 + j, 0) for j in range(WIN_TILES)]
    win_slots = [TILES_PER_GROUP + j for j in range(WIN_TILES)]

    def win_scores():
        for kt, slot in zip(win_tiles, win_slots):
            scores_to_scratch(kw_ref, kt, 1, slot)

    @pl.when(j_i >= half)
    def _():
        sel_scores(last, 1)
        sel_softmax(last, 0)

        @pl.when(j_i == 2)
        def _():
            add_bias(2, 1, 1)

        @pl.when(j_i == 3)
        def _():
            add_bias(2, 2, 0)

        win_scores()
        sel_softmax(last, 1)

    @pl.when(j_i < half)
    def _():
        win_scores()
        sel_softmax(last, 0)

    finish(1)

    reset()
    for j, slot in enumerate(win_slots):
        before_start = jnp.where(i - 2 + j < 0, NEG_BIG, 0.0).astype(F32)
        s_ref[tile_rows(slot), :] = s_ref[tile_rows(slot), :] + (bw_ref[tile_rows(j), :] + before_start)
    softmax_from_scratch(vwT_ref, win_tiles, win_slots)
    finish(2)

    for r in range(NSA_REP):
        oT_ref[0, r * NSA_D:(r + 1) * NSA_D, :] = out_ref[:, r * Q_TILE:(r + 1) * Q_TILE].astype(oT_ref.dtype)


def nsa_attention(qT, gT, cmp_nat, cmp_tr, ks, vsT, kw, vwT, lines, mmapT, *, batch, seq):
    G = NSA_GROUPS
    nc = seq // CMP_STRIDE
    ns = seq // SEL_LEN
    nq = seq // Q_TILE
    nkt = seq // K_TILE
    hd = NSA_HEADS * NSA_D
    vcT = cmp_tr[1].reshape(batch, G, NSA_D, nc // CMP_CHUNK, CMP_CHUNK).transpose(0, 1, 3, 2, 4)
    once = pl.Buffered(1)
    return pl.pallas_call(
        _nsa_body,
        out_shape=jax.ShapeDtypeStruct((batch, hd, seq), BF16),
        grid=(batch, G, nq),
        in_specs=[
            pl.BlockSpec((1, NSA_REP, NSA_D, Q_TILE), lambda b, g, i: (b, g, 0, i)),
            pl.BlockSpec((1, 1, GATE_ROWS, Q_TILE), lambda b, g, i: (b, g, 0, i)),
            pl.BlockSpec((1, 1, 1, nc, NSA_D), lambda b, g, i: (0, b, g, 0, 0)),
            pl.BlockSpec((1, 1, nc // CMP_CHUNK, NSA_D, CMP_CHUNK), lambda b, g, i: (b, g, 0, 0, 0)),
            pl.BlockSpec((1, 1, seq, 128), lambda b, g, i: (b, g, 0, 0), pipeline_mode=once),
            pl.BlockSpec((1, 1, nkt, V_ROWS, K_TILE), lambda b, g, i: (b, g, 0, 0, 0), pipeline_mode=once),
            pl.BlockSpec((1, 1, seq, 128), lambda b, g, i: (b, g, 0, 0), pipeline_mode=once),
            pl.BlockSpec((1, 1, nkt, V_ROWS, K_TILE), lambda b, g, i: (b, g, 0, 0, 0), pipeline_mode=once),
            pl.BlockSpec((1, 3, NSA_REP, BIAS_LINE), lambda b, g, i: (g, 0, 0, 0)),
            pl.BlockSpec((nc // CMP_CHUNK, ns, CMP_CHUNK), lambda b, g, i: (0, 0, 0)),
        ],
        out_specs=pl.BlockSpec((1, NSA_REP * NSA_D, Q_TILE), lambda b, g, i: (b, g, i)),
        scratch_shapes=[
            pltpu.VMEM((max(nc, (TILES_PER_GROUP + WIN_TILES) * K_TILE), LANES), F32),
            pltpu.VMEM((ns, Q_TILE), F32),
            pltpu.VMEM((ns, LANES), BF16),
            pltpu.VMEM((128, LANES), BF16),
            pltpu.VMEM((V_ROWS, LANES), F32),
            pltpu.VMEM((8, LANES), F32),
            pltpu.VMEM((NSA_D, LANES), F32),
            pltpu.VMEM((2 * CMP_PER_QTILE, LANES), F32),
            pltpu.VMEM((2 * K_TILE, LANES), F32),
            pltpu.VMEM((3 * K_TILE, LANES), F32),
        ],
        compiler_params=_cparams(("parallel", "parallel", "arbitrary"), VMEM_LIMIT),
        name="nsa_attention",
    )(qT, gT, cmp_nat, vcT, ks, vsT, kw, vwT, lines, mmapT)


def _t5_bucket_table(n_max):
    n = np.arange(n_max)
    max_exact = REL_BUCKETS // 2
    nf = np.maximum(n, 1).astype(np.float32)
    large = max_exact + (np.log(nf / np.float32(max_exact))
                         / np.float32(math.log(REL_MAX_DIST / max_exact))
                         * np.float32(REL_BUCKETS - max_exact)).astype(np.int32)
    large = np.minimum(large, REL_BUCKETS - 1)
    return np.where(n < max_exact, n, large).astype(np.int32)


def _bias_tables(rel_bias):
    n_max = WINDOW + Q_TILE
    buckets = _t5_bucket_table(n_max)
    assert (buckets[Q_TILE - CMP_LEN + 1:] == REL_BUCKETS - 1).all()
    tab = rel_bias.astype(F32)[buckets, :] - rel_bias.astype(F32)[REL_BUCKETS - 1][None, :]
    tab = tab.T * LOG2_E
    def line(rows, stride, offset, ok, fill):
        assert Q_TILE + stride * (rows - 1) <= BIAS_LINE
        k = np.arange(BIAS_LINE)
        dist = np.where(k < Q_TILE, k, k - BIAS_LINE) + offset
        vals = jnp.where(ok(dist)[None], tab[:, np.clip(dist, 0, n_max - 1)], fill)
        return vals.reshape(NSA_GROUPS, NSA_REP, BIAS_LINE)

    return jnp.stack([
        line(2 * CMP_PER_QTILE, CMP_STRIDE, Q_TILE - CMP_LEN + 1, lambda dd: dd >= 0, -jnp.inf),
        line(2 * K_TILE, 1, K_TILE, lambda dd: dd >= 0, NEG_BIG),
        line(3 * K_TILE, 1, 2 * K_TILE, lambda dd: (dd >= 0) & (dd < WINDOW), NEG_BIG),
    ], axis=1)


def _selection_map(seq):
    nc = seq // CMP_STRIDE
    ns = seq // SEL_LEN
    n_cmp = (seq - CMP_LEN) // CMP_STRIDE + 1
    ratio = SEL_LEN // CMP_STRIDE
    lead = CMP_LEN // CMP_STRIDE - 1
    j = np.arange(ns)[:, None]
    n = np.arange(nc)[None, :]
    m = ((n >= ratio * j - lead) & (n < ratio * j + ratio) & (n < n_cmp)).astype(np.float32)
    m = m.reshape(ns, nc // CMP_CHUNK, CMP_CHUNK).transpose(1, 0, 2)
    return jnp.asarray(m, dtype=BF16)


def kernel(x, mix_norm_pre, mix_norm_post, ffn_norm_pre, ffn_norm_post, ffn_w_in, ffn_w_out,
           ret_w_in, ret_w_out, kv_norm, kv_w, cmp_pe_k, cmp_w1_k, cmp_w2_k,
           cmp_pe_v, cmp_w1_v, cmp_w2_v, nsa_w_in, nsa_w_out, rel_bias):
    batch, seq, d = x.shape
    n_ret = ret_w_in.shape[0]
    n_nsa = nsa_w_in.shape[0]
    assert seq % (SEL_PER_GROUP * SEL_LEN) == 0 and seq // SEL_LEN >= SEL_TOPK
    h = x.reshape(batch * seq, d)
    shared = None
    for layer in range(n_ret + n_nsa):
        if layer == n_ret:
            kc_nat, vc_nat, ks, kw, vsT, vwT = kv_project(h, kv_norm, kv_w, batch=batch, seq=seq, tm=1024)
            cmp_nat, cmp_tr = compress_blocks(kc_nat, vc_nat, cmp_pe_k, cmp_w1_k, cmp_w2_k,
                                              cmp_pe_v, cmp_w1_v, cmp_w2_v, batch=batch, seq=seq)
            shared = (cmp_nat, cmp_tr, ks, vsT, kw, vwT, _bias_tables(rel_bias), _selection_map(seq))
        if layer < n_ret:
            w_in = _deinterleave_qk_columns(ret_w_in[layer]).astype(BF16)
            proj = norm_matmul(h, mix_norm_pre[layer], w_in, tm=1024, tn=2048, out_dtype=BF16)
            mixed = retention_core(proj, *_retention_tables(seq), batch=batch, seq=seq)
            h = proj_norm_res(mixed, ret_w_out[layer].astype(BF16), mix_norm_post[layer], h, tm=512)
        else:
            j = layer - n_ret
            qT, gT = q_project(h, mix_norm_pre[layer], nsa_w_in[j], batch=batch, seq=seq, tm=512)
            oT = nsa_attention(qT, gT, *shared, batch=batch, seq=seq)
            h = projT_norm_res(oT, nsa_w_out[j].astype(BF16), mix_norm_post[layer], h, tm=512)
        h = ffn_block(h, ffn_norm_pre[layer], ffn_w_in[layer].astype(BF16), ffn_w_out[layer].astype(BF16),
                      ffn_norm_post[layer], tm=512, chunks=2)
    return h.reshape(batch, seq, d)
```

```python
import functools
import math

import numpy as np
import jax
import jax.numpy as jnp
from jax import lax
from jax.experimental import pallas as pl
from jax.experimental.pallas import tpu as pltpu

F32 = jnp.float32
BF16 = jnp.bfloat16

D_MODEL = 1024
RMS_EPS = 1e-6

RET_HEADS = 4
RET_QK = 256
RET_V = 512
RET_CHUNK = 128

FFN_HIDDEN = 2816

NSA_HEADS = 16
NSA_GROUPS = 4
NSA_REP = 4
NSA_D = 64
CMP_LEN = 32
CMP_STRIDE = 16
CMP_HIDDEN = 256
SEL_LEN = 64
SEL_TOPK = 16
WINDOW = 512
REL_BUCKETS = 32
REL_MAX_DIST = 128

Q_TILE = 256
K_TILE = 256
SEL_PER_GROUP = 16
V_ROWS = 80
LOG2_E = math.log2(math.e)
NEG_BIG = -(2.0 ** 100)
M_INIT = -(2.0 ** 120)

VMEM_LIMIT = 56 * 1024 * 1024


def _cparams(sem, vmem=None, flags=None):
    return pltpu.CompilerParams(dimension_semantics=sem, vmem_limit_bytes=vmem, flags=flags)


def _rms(x, g):
    return x * lax.rsqrt(jnp.mean(x * x, axis=-1, keepdims=True) + RMS_EPS) * g


def _norm_matmul_body(x_ref, g_ref, w_ref, o_ref, xn_ref):
    @pl.when(pl.program_id(1) == 0)
    def _():
        xn_ref[...] = _rms(x_ref[...], g_ref[...]).astype(BF16)

    o_ref[...] = jnp.dot(xn_ref[...], w_ref[...], preferred_element_type=F32).astype(o_ref.dtype)


def norm_matmul(x, g, w, *, tm, tn, out_dtype=F32):
    t, d = x.shape
    n = w.shape[1]
    return pl.pallas_call(
        _norm_matmul_body,
        out_shape=jax.ShapeDtypeStruct((t, n), out_dtype),
        grid=(t // tm, n // tn),
        in_specs=[
            pl.BlockSpec((tm, d), lambda i, j: (i, 0)),
            pl.BlockSpec((1, d), lambda i, j: (0, 0)),
            pl.BlockSpec((d, tn), lambda i, j: (0, j)),
        ],
        out_specs=pl.BlockSpec((tm, tn), lambda i, j: (i, j)),
        scratch_shapes=[pltpu.VMEM((tm, d), BF16)],
        compiler_params=_cparams(("parallel", "arbitrary"), VMEM_LIMIT),
        name="norm_matmul",
    )(x, g.reshape(1, d), w)


def _proj_norm_res_body(y_ref, w_ref, g_ref, r_ref, o_ref):
    z = jnp.dot(y_ref[...].astype(BF16), w_ref[...], preferred_element_type=F32)
    o_ref[...] = r_ref[...] + _rms(z, g_ref[...])


def proj_norm_res(y, w, g, res, *, tm):
    t, k = y.shape
    d = w.shape[1]
    return pl.pallas_call(
        _proj_norm_res_body,
        out_shape=jax.ShapeDtypeStruct((t, d), F32),
        grid=(t // tm,),
        in_specs=[
            pl.BlockSpec((tm, k), lambda i: (i, 0)),
            pl.BlockSpec((k, d), lambda i: (0, 0)),
            pl.BlockSpec((1, d), lambda i: (0, 0)),
            pl.BlockSpec((tm, d), lambda i: (i, 0)),
        ],
        out_specs=pl.BlockSpec((tm, d), lambda i: (i, 0)),
        compiler_params=_cparams(("parallel",), VMEM_LIMIT),
        name="proj_norm_res",
    )(y, w, g.reshape(1, d), res)


def _projT_norm_res_body(yT_ref, w_ref, g_ref, r_ref, o_ref):
    z = lax.dot_general(yT_ref[0], w_ref[...], (((0,), (0,)), ((), ())),
                        preferred_element_type=F32)
    o_ref[...] = r_ref[...] + _rms(z, g_ref[...])


def projT_norm_res(yT, w, g, res, *, tm):
    b, k, s = yT.shape
    d = w.shape[1]
    nt = s // tm
    return pl.pallas_call(
        _projT_norm_res_body,
        out_shape=jax.ShapeDtypeStruct((b * s, d), F32),
        grid=(b, nt),
        in_specs=[
            pl.BlockSpec((1, k, tm), lambda bi, i: (bi, 0, i)),
            pl.BlockSpec((k, d), lambda bi, i: (0, 0)),
            pl.BlockSpec((1, d), lambda bi, i: (0, 0)),
            pl.BlockSpec((tm, d), lambda bi, i: (bi * nt + i, 0)),
        ],
        out_specs=pl.BlockSpec((tm, d), lambda bi, i: (bi * nt + i, 0)),
        compiler_params=_cparams(("parallel", "parallel"), VMEM_LIMIT),
        name="projT_norm_res",
    )(yT, w, g.reshape(1, d), res)


def _ffn_body(x_ref, gpre_ref, win_ref, wo_ref, gpost_ref, o_ref, *, chunks):
    x = x_ref[...]
    xn = _rms(x, gpre_ref[...]).astype(BF16)
    hdim = wo_ref.shape[0]
    th = hdim // chunks
    y = None
    for c in range(chunks):
        gate = jnp.dot(xn, win_ref[:, c * th:(c + 1) * th], preferred_element_type=F32)
        up = jnp.dot(xn, win_ref[:, hdim + c * th:hdim + (c + 1) * th], preferred_element_type=F32)
        act = (gate * jax.nn.sigmoid(gate) * up).astype(BF16)
        part = jnp.dot(act, wo_ref[c * th:(c + 1) * th, :], preferred_element_type=F32)
        y = part if y is None else y + part
    o_ref[...] = x + _rms(y, gpost_ref[...])


def ffn_block(x, g_pre, w_in, w_out, g_post, *, tm, chunks):
    t, d = x.shape
    hdim = w_out.shape[0]
    assert (hdim // chunks) % 128 == 0 and hdim % chunks == 0
    once = pl.Buffered(1)
    return pl.pallas_call(
        functools.partial(_ffn_body, chunks=chunks),
        out_shape=jax.ShapeDtypeStruct((t, d), F32),
        grid=(t // tm,),
        in_specs=[
            pl.BlockSpec((tm, d), lambda i: (i, 0)),
            pl.BlockSpec((1, d), lambda i: (0, 0)),
            pl.BlockSpec((d, 2 * hdim), lambda i: (0, 0), pipeline_mode=once),
            pl.BlockSpec((hdim, d), lambda i: (0, 0), pipeline_mode=once),
            pl.BlockSpec((1, d), lambda i: (0, 0)),
        ],
        out_specs=pl.BlockSpec((tm, d), lambda i: (i, 0)),
        compiler_params=_cparams(("parallel",), VMEM_LIMIT),
        name="ffn_block",
    )(x, g_pre.reshape(1, d), w_in, w_out, g_post.reshape(1, d))


def _retention_body(q_ref, k_ref, v_ref, g_ref, cos_ref, sin_ref, dmat_ref, qdec_ref, kdec_ref,
                    cdec_ref, o_ref, state_ref):
    @pl.when(pl.program_id(1) == 0)
    def _():
        state_ref[...] = jnp.zeros_like(state_ref)

    cos = cos_ref[...]
    sin = sin_ref[...]
    half = RET_QK // 2

    def rotate(x_ref, h):
        x1 = x_ref[:, h * RET_QK:h * RET_QK + half].astype(F32)
        x2 = x_ref[:, h * RET_QK + half:(h + 1) * RET_QK].astype(F32)
        return jnp.concatenate([x1 * cos - x2 * sin, x1 * sin + x2 * cos], axis=1)

    for h in range(RET_HEADS):
        qr = rotate(q_ref, h)
        kr = rotate(k_ref, h) * (RET_QK ** -0.5)
        v = v_ref[:, h * RET_V:(h + 1) * RET_V].astype(BF16)
        scores = lax.dot_general(qr.astype(BF16), kr.astype(BF16), (((1,), (1,)), ((), ())),
                                 preferred_element_type=F32) * dmat_ref[h]
        state = state_ref[h]
        o = (jnp.dot(scores.astype(BF16), v, preferred_element_type=F32)
             + jnp.dot((qr * qdec_ref[h]).astype(BF16), state.astype(BF16),
                       preferred_element_type=F32))
        kd = (kr * kdec_ref[h]).astype(BF16)
        state_ref[h] = state * cdec_ref[h, 0:1, :] + lax.dot_general(
            kd, v, (((0,), (0,)), ((), ())), preferred_element_type=F32)
        o = o * lax.rsqrt(jnp.mean(o * o, axis=-1, keepdims=True) + RMS_EPS)
        gate = g_ref[:, h * RET_V:(h + 1) * RET_V].astype(F32)
        o_ref[:, h * RET_V:(h + 1) * RET_V] = (o * (gate * jax.nn.sigmoid(gate))).astype(o_ref.dtype)


def retention_core(proj, cos, sin, dmat, qdec, kdec, cdec, *, batch, seq):
    c = RET_CHUNK
    nc = seq // c
    hq = RET_HEADS * RET_QK
    hv = RET_HEADS * RET_V
    return pl.pallas_call(
        _retention_body,
        out_shape=jax.ShapeDtypeStruct((batch * seq, hv), BF16),
        grid=(batch, nc),
        in_specs=[
            pl.BlockSpec((c, hq), lambda b, t: (b * nc + t, 0)),
            pl.BlockSpec((c, hq), lambda b, t: (b * nc + t, 1)),
            pl.BlockSpec((c, hv), lambda b, t: (b * nc + t, 1)),
            pl.BlockSpec((c, hv), lambda b, t: (b * nc + t, 2)),
            pl.BlockSpec((c, RET_QK // 2), lambda b, t: (t, 0)),
            pl.BlockSpec((c, RET_QK // 2), lambda b, t: (t, 0)),
            pl.BlockSpec((RET_HEADS, c, c), lambda b, t: (0, 0, 0)),
            pl.BlockSpec((RET_HEADS, c, RET_QK), lambda b, t: (0, 0, 0)),
            pl.BlockSpec((RET_HEADS, c, RET_QK), lambda b, t: (0, 0, 0)),
            pl.BlockSpec((RET_HEADS, 8, RET_V), lambda b, t: (0, 0, 0)),
        ],
        out_specs=pl.BlockSpec((c, hv), lambda b, t: (b * nc + t, 0)),
        scratch_shapes=[pltpu.VMEM((RET_HEADS, RET_QK, RET_V), F32)],
        compiler_params=_cparams(("parallel", "arbitrary"), VMEM_LIMIT),
        name="retention_core",
    )(proj, proj, proj, proj, cos, sin, dmat, qdec, kdec, cdec)


def _retention_tables(seq):
    h, dk, c = RET_HEADS, RET_QK, RET_CHUNK
    pos = jnp.arange(seq, dtype=F32)
    theta = 1.0 / (10000.0 ** jnp.linspace(0.0, 1.0, dk // 2, dtype=F32))
    ang = pos[:, None] * theta[None, :]
    log_gamma = jnp.log(1.0 - 2.0 ** (-5.0 - jnp.arange(h, dtype=F32)))
    idx = jnp.arange(c, dtype=F32)
    rel = idx[:, None] - idx[None, :]
    dmat = jnp.where(rel >= 0, jnp.exp(jnp.maximum(rel, 0.0) * log_gamma[:, None, None]), 0.0)
    qdec = jnp.exp((idx + 1.0)[None, :] * log_gamma[:, None])
    kdec = jnp.exp((c - 1.0 - idx)[None, :] * log_gamma[:, None])
    cdec = jnp.exp(c * log_gamma)
    qdec = jnp.broadcast_to(qdec[:, :, None], (h, c, dk))
    kdec = jnp.broadcast_to(kdec[:, :, None], (h, c, dk))
    cdec = jnp.broadcast_to(cdec[:, None, None], (h, 8, RET_V))
    return jnp.cos(ang), jnp.sin(ang), dmat, qdec, kdec, cdec


def _deinterleave_qk_columns(w):
    nqk = 2 * RET_HEADS * RET_QK
    perm = []
    for h in range(2 * RET_HEADS):
        base = h * RET_QK
        perm += [base + 2 * i for i in range(RET_QK // 2)]
        perm += [base + 2 * i + 1 for i in range(RET_QK // 2)]
    perm = np.asarray(perm + list(range(nqk, w.shape[1])), dtype=np.int32)
    return w[:, perm]


def _kv_body(x_ref, g_ref, wk_ref, wvT_ref, kc_ref, vc_ref, ks_ref, kw_ref, vsT_ref, vwT_ref):
    xn = _rms(x_ref[...], g_ref[...]).astype(BF16)
    tm = xn.shape[0]
    kall = jnp.dot(xn, wk_ref[...], preferred_element_type=F32)
    gd = NSA_GROUPS * NSA_D
    kc_ref[...] = kall[:, 0:gd]
    vc_ref[...] = kall[:, gd:2 * gd]
    row = pl.program_id(1) * tm + lax.broadcasted_iota(jnp.int32, (tm, 128), 0)
    lane = lax.broadcasted_iota(jnp.int32, (tm, 128), 1)
    blk = (row // SEL_LEN) % SEL_PER_GROUP
    onehot = jnp.where(lane - NSA_D == blk, 1.0, 0.0).astype(F32)
    for g in range(NSA_GROUPS):
        ks = kall[:, 2 * gd + 128 * g:2 * gd + 128 * (g + 1)]
        ks_ref[0, g] = (ks + onehot).astype(BF16)
        kw = kall[:, 2 * gd + 512 + 128 * g:2 * gd + 512 + 128 * (g + 1)]
        kw_ref[0, g] = kw.astype(BF16)
    vT = lax.dot_general(wvT_ref[...], xn, (((1,), (1,)), ((), ())),
                         preferred_element_type=F32)
    extra = jnp.where(lax.broadcasted_iota(jnp.int32, (V_ROWS - NSA_D, K_TILE), 0) == 0, 1.0, 0.0)
    extra = extra.astype(BF16)
    for g in range(NSA_GROUPS):
        for c in range(tm // K_TILE):
            cols = slice(c * K_TILE, (c + 1) * K_TILE)
            vsT_ref[0, g, c, 0:NSA_D, :] = vT[NSA_D * g:NSA_D * (g + 1), cols].astype(BF16)
            vsT_ref[0, g, c, NSA_D:V_ROWS, :] = extra
            vwT_ref[0, g, c, 0:NSA_D, :] = vT[gd + NSA_D * g:gd + NSA_D * (g + 1), cols].astype(BF16)
            vwT_ref[0, g, c, NSA_D:V_ROWS, :] = extra


def kv_project(x, g, kv_w, *, batch, seq, tm):
    d = x.shape[1]
    gd = NSA_GROUPS * NSA_D
    k_c, v_c, k_s, v_s, k_w, v_w = [kv_w[:, i * gd:(i + 1) * gd] for i in range(6)]

    def pad_groups(w):
        w = w.reshape(d, NSA_GROUPS, NSA_D)
        return jnp.pad(w, ((0, 0), (0, 0), (0, 128 - NSA_D))).reshape(d, NSA_GROUPS * 128)

    wk = jnp.concatenate([k_c, v_c, pad_groups(k_s), pad_groups(k_w)], axis=1).astype(BF16)
    wvT = jnp.concatenate([v_s, v_w], axis=1).T.astype(BF16)
    nt = seq // tm
    G = NSA_GROUPS
    return pl.pallas_call(
        _kv_body,
        out_shape=(
            jax.ShapeDtypeStruct((batch * seq, gd), F32),
            jax.ShapeDtypeStruct((batch * seq, gd), F32),
            jax.ShapeDtypeStruct((batch, G, seq, 128), BF16),
            jax.ShapeDtypeStruct((batch, G, seq, 128), BF16),
            jax.ShapeDtypeStruct((batch, G, seq // K_TILE, V_ROWS, K_TILE), BF16),
            jax.ShapeDtypeStruct((batch, G, seq // K_TILE, V_ROWS, K_TILE), BF16),
        ),
        grid=(batch, nt),
        in_specs=[
            pl.BlockSpec((tm, d), lambda b, i: (b * nt + i, 0)),
            pl.BlockSpec((1, d), lambda b, i: (0, 0)),
            pl.BlockSpec(wk.shape, lambda b, i: (0, 0)),
            pl.BlockSpec(wvT.shape, lambda b, i: (0, 0)),
        ],
        out_specs=(
            pl.BlockSpec((tm, gd), lambda b, i: (b * nt + i, 0)),
            pl.BlockSpec((tm, gd), lambda b, i: (b * nt + i, 0)),
            pl.BlockSpec((1, G, tm, 128), lambda b, i: (b, 0, i, 0)),
            pl.BlockSpec((1, G, tm, 128), lambda b, i: (b, 0, i, 0)),
            pl.BlockSpec((1, G, tm // K_TILE, V_ROWS, K_TILE), lambda b, i: (b, 0, i, 0, 0)),
            pl.BlockSpec((1, G, tm // K_TILE, V_ROWS, K_TILE), lambda b, i: (b, 0, i, 0, 0)),
        ),
        compiler_params=_cparams(("parallel", "parallel"), VMEM_LIMIT),
        name="kv_project",
    )(x, g.reshape(1, d), wk, wvT)


def _compress_body(c_ref, pe_ref, w1_ref, w2_ref, w2T_ref, nat_ref, tr_ref, sh_ref):
    half = CMP_STRIDE * NSA_D
    nc = c_ref.shape[3]
    c = c_ref[0, 0, 0].astype(BF16)
    w1 = w1_ref[0]
    first = jnp.dot(c, w1[0:half], preferred_element_type=F32)
    second = jnp.dot(c, w1[half:2 * half], preferred_element_type=F32)
    pe_term = jnp.dot(pe_ref[0].astype(BF16), w1, preferred_element_type=F32)
    sh_ref[0:nc, :] = second
    sh_ref[nc:nc + 8, :] = jnp.zeros((8, CMP_HIDDEN), F32)
    pre = first + sh_ref[1:nc + 1, :] + pe_term[0:1, :]
    hid = (pre * jax.nn.sigmoid(pre)).astype(BF16)
    nat_ref[0, 0, 0] = jnp.dot(hid, w2_ref[0], preferred_element_type=F32).astype(BF16)
    tr_ref[0, 0, 0] = lax.dot_general(w2T_ref[0], hid, (((1,), (1,)), ((), ())),
                                      preferred_element_type=F32).astype(BF16)


def compress_blocks(kc_nat, vc_nat, pe_k, w1_k, w2_k, pe_v, w1_v, w2_v, *, batch, seq):
    G, d = NSA_GROUPS, NSA_D
    nc = seq // CMP_STRIDE

    def to_rows(t):
        t = t.reshape(batch, nc, CMP_STRIDE, G, d).transpose(0, 3, 1, 2, 4)
        return t.reshape(batch, G, nc, CMP_STRIDE * d)

    c_all = jnp.stack([to_rows(kc_nat), to_rows(vc_nat)])
    pe = jnp.stack([pe_k.reshape(1, -1), pe_v.reshape(1, -1)])
    pe = jnp.broadcast_to(pe, (2, 8, CMP_LEN * d))
    w1 = jnp.stack([w1_k, w1_v]).astype(BF16)
    w2 = jnp.stack([w2_k, w2_v]).astype(BF16)
    w2T = jnp.stack([w2_k.T, w2_v.T]).astype(BF16)
    return pl.pallas_call(
        _compress_body,
        out_shape=(
            jax.ShapeDtypeStruct((2, batch, G, nc, d), BF16),
            jax.ShapeDtypeStruct((2, batch, G, d, nc), BF16),
        ),
        grid=(2, batch, G),
        in_specs=[
            pl.BlockSpec((1, 1, 1, nc, CMP_STRIDE * d), lambda w, b, g: (w, b, g, 0, 0)),
            pl.BlockSpec((1, 8, CMP_LEN * d), lambda w, b, g: (w, 0, 0)),
            pl.BlockSpec((1, CMP_LEN * d, CMP_HIDDEN), lambda w, b, g: (w, 0, 0)),
            pl.BlockSpec((1, CMP_HIDDEN, d), lambda w, b, g: (w, 0, 0)),
            pl.BlockSpec((1, d, CMP_HIDDEN), lambda w, b, g: (w, 0, 0)),
        ],
        out_specs=(
            pl.BlockSpec((1, 1, 1, nc, d), lambda w, b, g: (w, b, g, 0, 0)),
            pl.BlockSpec((1, 1, 1, d, nc), lambda w, b, g: (w, b, g, 0, 0)),
        ),
        scratch_shapes=[pltpu.VMEM((nc + 8, CMP_HIDDEN), F32)],
        compiler_params=_cparams(("parallel", "parallel", "parallel"), VMEM_LIMIT),
        name="compress_blocks",
    )(c_all, pe, w1, w2, w2T)


GATE_ROWS = 16


def _qproj_body(x_ref, g_ref, wT_ref, qT_ref, gT_ref):
    xn = _rms(x_ref[...], g_ref[...]).astype(BF16)
    tm = xn.shape[0]
    pT = lax.dot_general(wT_ref[...], xn, (((1,), (1,)), ((), ())),
                         preferred_element_type=F32)
    hd = NSA_HEADS * NSA_D
    q = pT[0:hd] * (NSA_D ** -0.5 * LOG2_E)
    qT_ref[0] = q.reshape(NSA_HEADS, NSA_D, tm).astype(BF16)
    gates = jax.nn.sigmoid(pT[hd:hd + NSA_GROUPS * GATE_ROWS])
    gT_ref[0] = gates.reshape(NSA_GROUPS, GATE_ROWS, tm)


def q_project(x, g, w_in, *, batch, seq, tm):
    d = x.shape[1]
    hd = NSA_HEADS * NSA_D
    per_group = NSA_REP * 3
    wg = w_in[:, hd:].reshape(d, NSA_GROUPS, per_group)
    wg = jnp.pad(wg, ((0, 0), (0, 0), (0, GATE_ROWS - per_group))).reshape(d, NSA_GROUPS * GATE_ROWS)
    wT = jnp.concatenate([w_in[:, :hd], wg], axis=1).T.astype(BF16)
    nt = seq // tm
    return pl.pallas_call(
        _qproj_body,
        out_shape=(
            jax.ShapeDtypeStruct((batch, NSA_HEADS, NSA_D, seq), BF16),
            jax.ShapeDtypeStruct((batch, NSA_GROUPS, GATE_ROWS, seq), F32),
        ),
        grid=(batch, nt),
        in_specs=[
            pl.BlockSpec((tm, d), lambda b, i: (b * nt + i, 0)),
            pl.BlockSpec((1, d), lambda b, i: (0, 0)),
            pl.BlockSpec(wT.shape, lambda b, i: (0, 0)),
        ],
        out_specs=(
            pl.BlockSpec((1, NSA_HEADS, NSA_D, tm), lambda b, i: (b, 0, 0, i)),
            pl.BlockSpec((1, NSA_GROUPS, GATE_ROWS, tm), lambda b, i: (b, 0, 0, i)),
        ),
        compiler_params=_cparams(("parallel", "parallel"), VMEM_LIMIT),
        name="q_project",
    )(x, g.reshape(1, d), wT)


CMP_CHUNK = 256
CMP_PER_QTILE = Q_TILE // CMP_STRIDE
SEL_PER_QTILE = Q_TILE // SEL_LEN
LANES = NSA_REP * Q_TILE
TILES_PER_GROUP = SEL_PER_GROUP * SEL_LEN // K_TILE
BIAS_LINE = 1024
WIN_TILES = (WINDOW + Q_TILE) // K_TILE


def _nsa_body(qT_ref, gT_ref, kc_ref, vcT_ref, ks_ref, vsT_ref, kw_ref, vwT_ref,
              lines_ref, mmapT_ref, oT_ref,
              s_ref, imp_ref, msel_ref, qaug_ref, acc_ref, m_ref, out_ref, bc_ref, bs_ref, bw_ref):
    i = pl.program_id(2)
    ns = imp_ref.shape[0]
    qT = jnp.concatenate([qT_ref[0, r] for r in range(NSA_REP)], axis=1)
    tiny = jnp.finfo(F32).tiny

    @pl.when(i == 0)
    def _():
        def expand(kind, r, rows, stride):
            line = lines_ref[0, kind, r:r + 1, :]
            shifted = pltpu.roll(jnp.broadcast_to(line, (rows, line.shape[1])), 0, 1,
                                 stride=stride, stride_axis=0)
            return shifted[:, 0:Q_TILE]

        for r in range(NSA_REP):
            cols = slice(r * Q_TILE, (r + 1) * Q_TILE)
            bc_ref[:, cols] = expand(0, r, bc_ref.shape[0], CMP_STRIDE)
            bs_ref[:, cols] = expand(1, r, bs_ref.shape[0], 1)
            bw_ref[:, cols] = expand(2, r, bw_ref.shape[0], 1)

    def gate_row(j):
        return jnp.concatenate([gT_ref[0, 0, 3 * r + j:3 * r + j + 1, :] for r in range(NSA_REP)], axis=1)

    nchunks = i // (CMP_CHUNK // CMP_PER_QTILE) + 1
    visible = CMP_PER_QTILE * (i + 1)

    def chunk_rows(c):
        return pl.ds(pl.multiple_of(c * CMP_CHUNK, CMP_CHUNK), CMP_CHUNK)

    window_start = CMP_PER_QTILE * (i - 1)

    def cmp_scores(c, m):
        s = jnp.dot(kc_ref[0, 0, 0, chunk_rows(c), :], qT, preferred_element_type=F32)
        s_ref[chunk_rows(c), :] = s
        rid = c * CMP_CHUNK + lax.broadcasted_iota(jnp.int32, (CMP_CHUNK, LANES), 0)
        return jnp.maximum(m, jnp.max(jnp.where(rid < window_start, s, -jnp.inf), axis=0, keepdims=True))

    m_c = lax.fori_loop(0, nchunks, cmp_scores, jnp.full((1, LANES), -jnp.inf, F32))
    m_ref[0:1, :] = m_c

    @pl.when(i == 0)
    def _():
        s = s_ref[0:CMP_PER_QTILE, :] + bc_ref[CMP_PER_QTILE:2 * CMP_PER_QTILE, :]
        s_ref[0:CMP_PER_QTILE, :] = s
        m_ref[0:1, :] = jnp.maximum(m_ref[0:1, :], jnp.max(s, axis=0, keepdims=True))

    @pl.when(i > 0)
    def _():
        rows = pl.ds(pl.multiple_of(window_start, CMP_PER_QTILE), 2 * CMP_PER_QTILE)
        s = s_ref[rows, :] + bc_ref[...]
        s_ref[rows, :] = s
        m_ref[0:1, :] = jnp.maximum(m_ref[0:1, :], jnp.max(s, axis=0, keepdims=True))

    m_c = m_ref[0:1, :]
    m_c = jnp.where(jnp.isfinite(m_c), m_c, 0.0)

    def cmp_exp(c, l):
        rid = c * CMP_CHUNK + lax.broadcasted_iota(jnp.int32, (CMP_CHUNK, LANES), 0)
        p = jnp.exp2(jnp.where(rid < visible, s_ref[chunk_rows(c), :], -jnp.inf) - m_c)
        s_ref[chunk_rows(c), :] = p
        return l + jnp.sum(p, axis=0, keepdims=True)

    l_c = lax.fori_loop(0, nchunks, cmp_exp, jnp.zeros((1, LANES), F32))
    inv_c = 1.0 / jnp.maximum(l_c, tiny)

    acc_ref[...] = jnp.zeros_like(acc_ref)
    imp_ref[...] = jnp.zeros_like(imp_ref)

    def cmp_pv(c, carry):
        p = s_ref[chunk_rows(c), :] * inv_c
        acc_ref[0:NSA_D, :] += jnp.dot(vcT_ref[0, 0, c], p.astype(BF16), preferred_element_type=F32)
        ic = p[:, 0:Q_TILE]
        for r in range(1, NSA_REP):
            ic = ic + p[:, r * Q_TILE:(r + 1) * Q_TILE]
        hi = ic.astype(BF16)
        rem = ic - hi.astype(F32)
        mid = rem.astype(BF16)
        lo = (rem - mid.astype(F32)).astype(BF16)
        mm = mmapT_ref[c]
        imp_ref[...] += (jnp.dot(mm, hi, preferred_element_type=F32)
                         + jnp.dot(mm, mid, preferred_element_type=F32)
                         + jnp.dot(mm, lo, preferred_element_type=F32))
        return carry

    lax.fori_loop(0, nchunks, cmp_pv, 0)
    out_ref[...] = gate_row(0) * acc_ref[0:NSA_D, :]

    qaug_ref[0:NSA_D, :] = qT
    qaug_ref[NSA_D:, :] = jnp.zeros((qaug_ref.shape[0] - NSA_D, LANES), BF16)

    def reset():
        acc_ref[...] = jnp.zeros_like(acc_ref)
        m_ref[...] = jnp.full(m_ref.shape, M_INIT, F32)

    def tile_rows(j):
        return slice(j * K_TILE, (j + 1) * K_TILE)

    def scores_to_scratch(k_ref, first_tile, count, slot):
        rows = pl.ds(pl.multiple_of(first_tile * K_TILE, K_TILE), count * K_TILE)
        s_ref[slot * K_TILE:(slot + count) * K_TILE, :] = jnp.dot(
            k_ref[0, 0, rows, :], qaug_ref[...], preferred_element_type=F32)

    def softmax_from_scratch(vT_ref, tiles, slots):
        rows = slice(slots[0] * K_TILE, (slots[-1] + 1) * K_TILE)
        m_old = m_ref[0:1, :]
        m_new = jnp.maximum(m_old, jnp.max(s_ref[rows, :], axis=0, keepdims=True))
        alpha = jnp.exp2(m_old - m_new)
        p = jnp.exp2(s_ref[rows, :] - m_new).astype(BF16)
        vT = jnp.concatenate([vT_ref[0, 0, kt] for kt in tiles], axis=1)
        acc_ref[...] = alpha * acc_ref[...] + jnp.dot(vT, p, preferred_element_type=F32)
        m_ref[0:1, :] = m_new

    def finish(j):
        l = jnp.maximum(acc_ref[NSA_D:NSA_D + 1, :], tiny)
        out_ref[...] += (gate_row(j) * (1.0 / l)) * acc_ref[0:NSA_D, :]

    win_slots = [TILES_PER_GROUP + j for j in range(WIN_TILES)]
    win_tiles = [jnp.maximum(i - (WIN_TILES - 1) + j, 0) for j in range(WIN_TILES)]
    reset()
    for j, (kt, slot) in enumerate(zip(win_tiles, win_slots)):
        scores_to_scratch(kw_ref, kt, 1, slot)
        before_start = jnp.where(i - (WIN_TILES - 1) + j < 0, NEG_BIG, 0.0).astype(F32)
        s_ref[tile_rows(slot), :] = s_ref[tile_rows(slot), :] + (bw_ref[tile_rows(j), :] + before_start)
    softmax_from_scratch(vwT_ref, win_tiles, win_slots)
    finish(2)

    jrow = lax.broadcasted_iota(jnp.int32, (ns, Q_TILE), 0)
    col = lax.broadcasted_iota(jnp.int32, (ns, Q_TILE), 1)
    cur = SEL_PER_QTILE * i + col // SEL_LEN
    valid = jrow <= cur
    forced = (jrow == 0) | (jrow == cur) | (jrow == cur - 1)
    free = float(SEL_TOPK - 3)
    candidate = valid & jnp.logical_not(forced)
    v0 = jnp.where(candidate, imp_ref[...], -jnp.inf)

    def strip_max(carry):
        v, taken, theta, above = carry
        best = jnp.max(v, axis=0, keepdims=True)
        hit = v == best
        now = taken + jnp.sum(jnp.where(hit, 1.0, 0.0), axis=0, keepdims=True)
        crossed = (taken < free) & (now >= free)
        return (jnp.where(hit, -jnp.inf, v), now,
                jnp.where(crossed, best, theta), jnp.where(crossed, taken, above))

    zero_row = jnp.zeros((1, Q_TILE), F32)
    carry = (v0, zero_row, jnp.full((1, Q_TILE), jnp.inf, F32), zero_row)
    for _ in range(SEL_TOPK - 3):
        carry = strip_max(carry)
    _, _, theta, above = carry
    v0 = jnp.where(candidate, imp_ref[...], -jnp.inf)
    tied = v0 == theta
    lower = jnp.where(lax.broadcasted_iota(jnp.int32, (ns, ns), 1) < lax.broadcasted_iota(jnp.int32, (ns, ns), 0),
                      1.0, 0.0).astype(BF16)
    rank = jnp.dot(lower, jnp.where(tied, 1.0, 0.0).astype(BF16), preferred_element_type=F32)
    chosen = forced | (v0 > theta) | (tied & (rank < free - above))
    mask_bias = jnp.where(chosen & valid, 0.0, NEG_BIG).astype(BF16)
    msel_ref[...] = jnp.concatenate([mask_bias] * NSA_REP, axis=1)

    last = i // TILES_PER_GROUP
    j_i = i % TILES_PER_GROUP

    half = TILES_PER_GROUP // 2

    def sel_tiles(grp, h):
        return [grp * TILES_PER_GROUP + h * half + j for j in range(half)]

    def sel_slots(h):
        return [h * half + j for j in range(half)]

    def sel_scores(grp, h):
        slab = pl.ds(pl.multiple_of(grp * SEL_PER_GROUP, SEL_PER_GROUP), SEL_PER_GROUP)
        qaug_ref[NSA_D:NSA_D + SEL_PER_GROUP, :] = msel_ref[slab, :]
        scores_to_scratch(ks_ref, sel_tiles(grp, h)[0], half, h * half)

    def sel_softmax(grp, h):
        softmax_from_scratch(vsT_ref, sel_tiles(grp, h), sel_slots(h))

    def add_bias(slot, count, bias_tile):
        rows = slice(slot * K_TILE, (slot + count) * K_TILE)
        s_ref[rows, :] = s_ref[rows, :] + bs_ref[bias_tile * K_TILE:(bias_tile + count) * K_TILE, :]

    reset()
    sel_scores(0, 0)

    def far(grp):
        sel_scores(grp, 1)
        sel_softmax(grp, 0)
        sel_scores(grp + 1, 0)
        sel_softmax(grp, 1)

    def far_pair(pair, carry):
        far(2 * pair)
        far(2 * pair + 1)
        return carry

    n_far = jnp.maximum(last - 1, 0)
    lax.fori_loop(0, n_far // 2, far_pair, 0)

    @pl.when(n_far % 2 == 1)
    def _():
        far(n_far - 1)

    @pl.when(last >= 1)
    def _():
        sel_scores(last - 1, 1)
        sel_softmax(last - 1, 0)

        @pl.when(j_i == 0)
        def _():
            add_bias(TILES_PER_GROUP - 1, 1, 0)

        sel_scores(last, 0)
        sel_softmax(last - 1, 1)

    @pl.when(j_i == 0)
    def _():
        add_bias(0, 1, 1)

    @pl.when(j_i == 1)
    def _():
        add_bias(0, 2, 0)

    @pl.when(j_i == 2)
    def _():
        add_bias(1, 1, 0)

    @pl.when(j_i >= half)
    def _():
        sel_scores(last, 1)
        sel_softmax(last, 0)

        @pl.when(j_i == 2)
        def _():
            add_bias(2, 1, 1)

        @pl.when(j_i == 3)
        def _():
            add_bias(2, 2, 0)

        sel_softmax(last, 1)

    @pl.when(j_i < half)
    def _():
        sel_softmax(last, 0)

    finish(1)

    for r in range(NSA_REP):
        oT_ref[0, r * NSA_D:(r + 1) * NSA_D, :] = out_ref[:, r * Q_TILE:(r + 1) * Q_TILE].astype(oT_ref.dtype)


def nsa_attention(qT, gT, cmp_nat, cmp_tr, ks, vsT, kw, vwT, lines, mmapT, *, batch, seq):
    G = NSA_GROUPS
    nc = seq // CMP_STRIDE
    ns = seq // SEL_LEN
    nq = seq // Q_TILE
    nkt = seq // K_TILE
    hd = NSA_HEADS * NSA_D
    vcT = cmp_tr[1].reshape(batch, G, NSA_D, nc // CMP_CHUNK, CMP_CHUNK).transpose(0, 1, 3, 2, 4)
    once = pl.Buffered(1)
    return pl.pallas_call(
        _nsa_body,
        out_shape=jax.ShapeDtypeStruct((batch, hd, seq), BF16),
        grid=(batch, G, nq),
        in_specs=[
            pl.BlockSpec((1, NSA_REP, NSA_D, Q_TILE), lambda b, g, i: (b, g, 0, i)),
            pl.BlockSpec((1, 1, GATE_ROWS, Q_TILE), lambda b, g, i: (b, g, 0, i)),
            pl.BlockSpec((1, 1, 1, nc, NSA_D), lambda b, g, i: (0, b, g, 0, 0)),
            pl.BlockSpec((1, 1, nc // CMP_CHUNK, NSA_D, CMP_CHUNK), lambda b, g, i: (b, g, 0, 0, 0)),
            pl.BlockSpec((1, 1, seq, 128), lambda b, g, i: (b, g, 0, 0), pipeline_mode=once),
            pl.BlockSpec((1, 1, nkt, V_ROWS, K_TILE), lambda b, g, i: (b, g, 0, 0, 0), pipeline_mode=once),
            pl.BlockSpec((1, 1, seq, 128), lambda b, g, i: (b, g, 0, 0), pipeline_mode=once),
            pl.BlockSpec((1, 1, nkt, V_ROWS, K_TILE), lambda b, g, i: (b, g, 0, 0, 0), pipeline_mode=once),
            pl.BlockSpec((1, 3, NSA_REP, BIAS_LINE), lambda b, g, i: (g, 0, 0, 0)),
            pl.BlockSpec((nc // CMP_CHUNK, ns, CMP_CHUNK), lambda b, g, i: (0, 0, 0)),
        ],
        out_specs=pl.BlockSpec((1, NSA_REP * NSA_D, Q_TILE), lambda b, g, i: (b, g, i)),
        scratch_shapes=[
            pltpu.VMEM((max(nc, (TILES_PER_GROUP + WIN_TILES) * K_TILE), LANES), F32),
            pltpu.VMEM((ns, Q_TILE), F32),
            pltpu.VMEM((ns, LANES), BF16),
            pltpu.VMEM((128, LANES), BF16),
            pltpu.VMEM((V_ROWS, LANES), F32),
            pltpu.VMEM((8, LANES), F32),
            pltpu.VMEM((NSA_D, LANES), F32),
            pltpu.VMEM((2 * CMP_PER_QTILE, LANES), F32),
            pltpu.VMEM((2 * K_TILE, LANES), F32),
            pltpu.VMEM((3 * K_TILE, LANES), F32),
        ],
        compiler_params=_cparams(("parallel", "parallel", "arbitrary"), VMEM_LIMIT),
        name="nsa_attention",
    )(qT, gT, cmp_nat, vcT, ks, vsT, kw, vwT, lines, mmapT)


def _t5_bucket_table(n_max):
    n = np.arange(n_max)
    max_exact = REL_BUCKETS // 2
    nf = np.maximum(n, 1).astype(np.float32)
    large = max_exact + (np.log(nf / np.float32(max_exact))
                         / np.float32(math.log(REL_MAX_DIST / max_exact))
                         * np.float32(REL_BUCKETS - max_exact)).astype(np.int32)
    large = np.minimum(large, REL_BUCKETS - 1)
    return np.where(n < max_exact, n, large).astype(np.int32)


def _bias_tables(rel_bias):
    n_max = WINDOW + Q_TILE
    buckets = _t5_bucket_table(n_max)
    assert (buckets[Q_TILE - CMP_LEN + 1:] == REL_BUCKETS - 1).all()
    tab = rel_bias.astype(F32)[buckets, :] - rel_bias.astype(F32)[REL_BUCKETS - 1][None, :]
    tab = tab.T * LOG2_E
    def line(rows, stride, offset, ok, fill):
        assert Q_TILE + stride * (rows - 1) <= BIAS_LINE
        k = np.arange(BIAS_LINE)
        dist = np.where(k < Q_TILE, k, k - BIAS_LINE) + offset
        vals = jnp.where(ok(dist)[None], tab[:, np.clip(dist, 0, n_max - 1)], fill)
        return vals.reshape(NSA_GROUPS, NSA_REP, BIAS_LINE)

    return jnp.stack([
        line(2 * CMP_PER_QTILE, CMP_STRIDE, Q_TILE - CMP_LEN + 1, lambda dd: dd >= 0, -jnp.inf),
        line(2 * K_TILE, 1, K_TILE, lambda dd: dd >= 0, NEG_BIG),
        line(3 * K_TILE, 1, 2 * K_TILE, lambda dd: (dd >= 0) & (dd < WINDOW), NEG_BIG),
    ], axis=1)


def _selection_map(seq):
    nc = seq // CMP_STRIDE
    ns = seq // SEL_LEN
    n_cmp = (seq - CMP_LEN) // CMP_STRIDE + 1
    ratio = SEL_LEN // CMP_STRIDE
    lead = CMP_LEN // CMP_STRIDE - 1
    j = np.arange(ns)[:, None]
    n = np.arange(nc)[None, :]
    m = ((n >= ratio * j - lead) & (n < ratio * j + ratio) & (n < n_cmp)).astype(np.float32)
    m = m.reshape(ns, nc // CMP_CHUNK, CMP_CHUNK).transpose(1, 0, 2)
    return jnp.asarray(m, dtype=BF16)


def kernel(x, mix_norm_pre, mix_norm_post, ffn_norm_pre, ffn_norm_post, ffn_w_in, ffn_w_out,
           ret_w_in, ret_w_out, kv_norm, kv_w, cmp_pe_k, cmp_w1_k, cmp_w2_k,
           cmp_pe_v, cmp_w1_v, cmp_w2_v, nsa_w_in, nsa_w_out, rel_bias):
    batch, seq, d = x.shape
    n_ret = ret_w_in.shape[0]
    n_nsa = nsa_w_in.shape[0]
    assert seq % (SEL_PER_GROUP * SEL_LEN) == 0 and seq // SEL_LEN >= SEL_TOPK
    h = x.reshape(batch * seq, d)
    shared = None
    for layer in range(n_ret + n_nsa):
        if layer == n_ret:
            kc_nat, vc_nat, ks, kw, vsT, vwT = kv_project(h, kv_norm, kv_w, batch=batch, seq=seq, tm=1024)
            cmp_nat, cmp_tr = compress_blocks(kc_nat, vc_nat, cmp_pe_k, cmp_w1_k, cmp_w2_k,
                                              cmp_pe_v, cmp_w1_v, cmp_w2_v, batch=batch, seq=seq)
            shared = (cmp_nat, cmp_tr, ks, vsT, kw, vwT, _bias_tables(rel_bias), _selection_map(seq))
        if layer < n_ret:
            w_in = _deinterleave_qk_columns(ret_w_in[layer]).astype(BF16)
            proj = norm_matmul(h, mix_norm_pre[layer], w_in, tm=1024, tn=2048, out_dtype=BF16)
            mixed = retention_core(proj, *_retention_tables(seq), batch=batch, seq=seq)
            h = proj_norm_res(mixed, ret_w_out[layer].astype(BF16), mix_norm_post[layer], h, tm=512)
        else:
            j = layer - n_ret
            qT, gT = q_project(h, mix_norm_pre[layer], nsa_w_in[j], batch=batch, seq=seq, tm=512)
            oT = nsa_attention(qT, gT, *shared, batch=batch, seq=seq)
            h = projT_norm_res(oT, nsa_w_out[j].astype(BF16), mix_norm_post[layer], h, tm=512)
        h = ffn_block(h, ffn_norm_pre[layer], ffn_w_in[layer].astype(BF16), ffn_w_out[layer].astype(BF16),
                      ffn_norm_post[layer], tm=512, chunks=2)
    return h.reshape(batch, seq, d)
```

```python
import functools
import math

import numpy as np
import jax
import jax.numpy as jnp
from jax import lax
from jax.experimental import pallas as pl
from jax.experimental.pallas import tpu as pltpu

F32 = jnp.float32
BF16 = jnp.bfloat16

D_MODEL = 1024
RMS_EPS = 1e-6

RET_HEADS = 4
RET_QK = 256
RET_V = 512
RET_CHUNK = 128

FFN_HIDDEN = 2816

NSA_HEADS = 16
NSA_GROUPS = 4
NSA_REP = 4
NSA_D = 64
CMP_LEN = 32
CMP_STRIDE = 16
CMP_HIDDEN = 256
SEL_LEN = 64
SEL_TOPK = 16
WINDOW = 512
REL_BUCKETS = 32
REL_MAX_DIST = 128

Q_TILE = 256
K_TILE = 256
SEL_PER_GROUP = 16
V_ROWS = 80
LOG2_E = math.log2(math.e)
NEG_BIG = -(2.0 ** 100)
M_INIT = -(2.0 ** 120)

VMEM_LIMIT = 56 * 1024 * 1024


def _cparams(sem, vmem=None, flags=None):
    return pltpu.CompilerParams(dimension_semantics=sem, vmem_limit_bytes=vmem, flags=flags)


def _rms(x, g):
    return x * lax.rsqrt(jnp.mean(x * x, axis=-1, keepdims=True) + RMS_EPS) * g


def _norm_matmul_body(x_ref, g_ref, w_ref, o_ref, xn_ref):
    @pl.when(pl.program_id(1) == 0)
    def _():
        xn_ref[...] = _rms(x_ref[...], g_ref[...]).astype(BF16)

    o_ref[...] = jnp.dot(xn_ref[...], w_ref[...], preferred_element_type=F32).astype(o_ref.dtype)


def norm_matmul(x, g, w, *, tm, tn, out_dtype=F32):
    t, d = x.shape
    n = w.shape[1]
    return pl.pallas_call(
        _norm_matmul_body,
        out_shape=jax.ShapeDtypeStruct((t, n), out_dtype),
        grid=(t // tm, n // tn),
        in_specs=[
            pl.BlockSpec((tm, d), lambda i, j: (i, 0)),
            pl.BlockSpec((1, d), lambda i, j: (0, 0)),
            pl.BlockSpec((d, tn), lambda i, j: (0, j)),
        ],
        out_specs=pl.BlockSpec((tm, tn), lambda i, j: (i, j)),
        scratch_shapes=[pltpu.VMEM((tm, d), BF16)],
        compiler_params=_cparams(("parallel", "arbitrary"), VMEM_LIMIT),
        name="norm_matmul",
    )(x, g.reshape(1, d), w)


def _proj_norm_res_body(y_ref, w_ref, g_ref, r_ref, o_ref):
    z = jnp.dot(y_ref[...].astype(BF16), w_ref[...], preferred_element_type=F32)
    o_ref[...] = r_ref[...] + _rms(z, g_ref[...])


def proj_norm_res(y, w, g, res, *, tm):
    t, k = y.shape
    d = w.shape[1]
    return pl.pallas_call(
        _proj_norm_res_body,
        out_shape=jax.ShapeDtypeStruct((t, d), F32),
        grid=(t // tm,),
        in_specs=[
            pl.BlockSpec((tm, k), lambda i: (i, 0)),
            pl.BlockSpec((k, d), lambda i: (0, 0)),
            pl.BlockSpec((1, d), lambda i: (0, 0)),
            pl.BlockSpec((tm, d), lambda i: (i, 0)),
        ],
        out_specs=pl.BlockSpec((tm, d), lambda i: (i, 0)),
        compiler_params=_cparams(("parallel",), VMEM_LIMIT),
        name="proj_norm_res",
    )(y, w, g.reshape(1, d), res)


def _projT_norm_res_body(yT_ref, w_ref, g_ref, r_ref, o_ref):
    z = lax.dot_general(yT_ref[0], w_ref[...], (((0,), (0,)), ((), ())),
                        preferred_element_type=F32)
    o_ref[...] = r_ref[...] + _rms(z, g_ref[...])


def projT_norm_res(yT, w, g, res, *, tm):
    b, k, s = yT.shape
    d = w.shape[1]
    nt = s // tm
    return pl.pallas_call(
        _projT_norm_res_body,
        out_shape=jax.ShapeDtypeStruct((b * s, d), F32),
        grid=(b, nt),
        in_specs=[
            pl.BlockSpec((1, k, tm), lambda bi, i: (bi, 0, i)),
            pl.BlockSpec((k, d), lambda bi, i: (0, 0)),
            pl.BlockSpec((1, d), lambda bi, i: (0, 0)),
            pl.BlockSpec((tm, d), lambda bi, i: (bi * nt + i, 0)),
        ],
        out_specs=pl.BlockSpec((tm, d), lambda bi, i: (bi * nt + i, 0)),
        compiler_params=_cparams(("parallel", "parallel"), VMEM_LIMIT),
        name="projT_norm_res",
    )(yT, w, g.reshape(1, d), res)


def _ffn_body(x_ref, gpre_ref, win_ref, wo_ref, gpost_ref, o_ref, *, chunks):
    x = x_ref[...]
    xn = _rms(x, gpre_ref[...]).astype(BF16)
    hdim = wo_ref.shape[0]
    th = hdim // chunks
    y = None
    for c in range(chunks):
        gate = jnp.dot(xn, win_ref[:, c * th:(c + 1) * th], preferred_element_type=F32)
        up = jnp.dot(xn, win_ref[:, hdim + c * th:hdim + (c + 1) * th], preferred_element_type=F32)
        act = (gate * jax.nn.sigmoid(gate) * up).astype(BF16)
        part = jnp.dot(act, wo_ref[c * th:(c + 1) * th, :], preferred_element_type=F32)
        y = part if y is None else y + part
    o_ref[...] = x + _rms(y, gpost_ref[...])


def ffn_block(x, g_pre, w_in, w_out, g_post, *, tm, chunks):
    t, d = x.shape
    hdim = w_out.shape[0]
    assert (hdim // chunks) % 128 == 0 and hdim % chunks == 0
    once = pl.Buffered(1)
    return pl.pallas_call(
        functools.partial(_ffn_body, chunks=chunks),
        out_shape=jax.ShapeDtypeStruct((t, d), F32),
        grid=(t // tm,),
        in_specs=[
            pl.BlockSpec((tm, d), lambda i: (i, 0)),
            pl.BlockSpec((1, d), lambda i: (0, 0)),
            pl.BlockSpec((d, 2 * hdim), lambda i: (0, 0), pipeline_mode=once),
            pl.BlockSpec((hdim, d), lambda i: (0, 0), pipeline_mode=once),
            pl.BlockSpec((1, d), lambda i: (0, 0)),
        ],
        out_specs=pl.BlockSpec((tm, d), lambda i: (i, 0)),
        compiler_params=_cparams(("parallel",), VMEM_LIMIT),
        name="ffn_block",
    )(x, g_pre.reshape(1, d), w_in, w_out, g_post.reshape(1, d))


def _retention_body(q_ref, k_ref, v_ref, g_ref, cos_ref, sin_ref, dmat_ref, qdec_ref, kdec_ref,
                    cdec_ref, o_ref, state_ref):
    @pl.when(pl.program_id(1) == 0)
    def _():
        state_ref[...] = jnp.zeros_like(state_ref)

    cos = cos_ref[...]
    sin = sin_ref[...]
    half = RET_QK // 2

    def rotate(x_ref, h):
        x1 = x_ref[:, h * RET_QK:h * RET_QK + half].astype(F32)
        x2 = x_ref[:, h * RET_QK + half:(h + 1) * RET_QK].astype(F32)
        return jnp.concatenate([x1 * cos - x2 * sin, x1 * sin + x2 * cos], axis=1)

    for h in range(RET_HEADS):
        qr = rotate(q_ref, h)
        kr = rotate(k_ref, h) * (RET_QK ** -0.5)
        v = v_ref[:, h * RET_V:(h + 1) * RET_V].astype(BF16)
        scores = lax.dot_general(qr.astype(BF16), kr.astype(BF16), (((1,), (1,)), ((), ())),
                                 preferred_element_type=F32) * dmat_ref[h]
        state = state_ref[h]
        o = (jnp.dot(scores.astype(BF16), v, preferred_element_type=F32)
             + jnp.dot((qr * qdec_ref[h]).astype(BF16), state.astype(BF16),
                       preferred_element_type=F32))
        kd = (kr * kdec_ref[h]).astype(BF16)
        state_ref[h] = state * cdec_ref[h, 0:1, :] + lax.dot_general(
            kd, v, (((0,), (0,)), ((), ())), preferred_element_type=F32)
        o = o * lax.rsqrt(jnp.mean(o * o, axis=-1, keepdims=True) + RMS_EPS)
        gate = g_ref[:, h * RET_V:(h + 1) * RET_V].astype(F32)
        o_ref[:, h * RET_V:(h + 1) * RET_V] = (o * (gate * jax.nn.sigmoid(gate))).astype(o_ref.dtype)


def retention_core(proj, cos, sin, dmat, qdec, kdec, cdec, *, batch, seq):
    c = RET_CHUNK
    nc = seq // c
    hq = RET_HEADS * RET_QK
    hv = RET_HEADS * RET_V
    return pl.pallas_call(
        _retention_body,
        out_shape=jax.ShapeDtypeStruct((batch * seq, hv), BF16),
        grid=(batch, nc),
        in_specs=[
            pl.BlockSpec((c, hq), lambda b, t: (b * nc + t, 0)),
            pl.BlockSpec((c, hq), lambda b, t: (b * nc + t, 1)),
            pl.BlockSpec((c, hv), lambda b, t: (b * nc + t, 1)),
            pl.BlockSpec((c, hv), lambda b, t: (b * nc + t, 2)),
            pl.BlockSpec((c, RET_QK // 2), lambda b, t: (t, 0)),
            pl.BlockSpec((c, RET_QK // 2), lambda b, t: (t, 0)),
            pl.BlockSpec((RET_HEADS, c, c), lambda b, t: (0, 0, 0)),
            pl.BlockSpec((RET_HEADS, c, RET_QK), lambda b, t: (0, 0, 0)),
            pl.BlockSpec((RET_HEADS, c, RET_QK), lambda b, t: (0, 0, 0)),
            pl.BlockSpec((RET_HEADS, 8, RET_V), lambda b, t: (0, 0, 0)),
        ],
        out_specs=pl.BlockSpec((c, hv), lambda b, t: (b * nc + t, 0)),
        scratch_shapes=[pltpu.VMEM((RET_HEADS, RET_QK, RET_V), F32)],
        compiler_params=_cparams(("parallel", "arbitrary"), VMEM_LIMIT),
        name="retention_core",
    )(proj, proj, proj, proj, cos, sin, dmat, qdec, kdec, cdec)


def _retention_tables(seq):
    h, dk, c = RET_HEADS, RET_QK, RET_CHUNK
    pos = jnp.arange(seq, dtype=F32)
    theta = 1.0 / (10000.0 ** jnp.linspace(0.0, 1.0, dk // 2, dtype=F32))
    ang = pos[:, None] * theta[None, :]
    log_gamma = jnp.log(1.0 - 2.0 ** (-5.0 - jnp.arange(h, dtype=F32)))
    idx = jnp.arange(c, dtype=F32)
    rel = idx[:, None] - idx[None, :]
    dmat = jnp.where(rel >= 0, jnp.exp(jnp.maximum(rel, 0.0) * log_gamma[:, None, None]), 0.0)
    qdec = jnp.exp((idx + 1.0)[None, :] * log_gamma[:, None])
    kdec = jnp.exp((c - 1.0 - idx)[None, :] * log_gamma[:, None])
    cdec = jnp.exp(c * log_gamma)
    qdec = jnp.broadcast_to(qdec[:, :, None], (h, c, dk))
    kdec = jnp.broadcast_to(kdec[:, :, None], (h, c, dk))
    cdec = jnp.broadcast_to(cdec[:, None, None], (h, 8, RET_V))
    return jnp.cos(ang), jnp.sin(ang), dmat, qdec, kdec, cdec


def _deinterleave_qk_columns(w):
    nqk = 2 * RET_HEADS * RET_QK
    perm = []
    for h in range(2 * RET_HEADS):
        base = h * RET_QK
        perm += [base + 2 * i for i in range(RET_QK // 2)]
        perm += [base + 2 * i + 1 for i in range(RET_QK // 2)]
    perm = np.asarray(perm + list(range(nqk, w.shape[1])), dtype=np.int32)
    return w[:, perm]


def _kv_body(x_ref, g_ref, wk_ref, wvT_ref, kc_ref, vc_ref, ks_ref, kw_ref, vsT_ref, vwT_ref):
    xn = _rms(x_ref[...], g_ref[...]).astype(BF16)
    tm = xn.shape[0]
    kall = jnp.dot(xn, wk_ref[...], preferred_element_type=F32)
    gd = NSA_GROUPS * NSA_D
    kc_ref[...] = kall[:, 0:gd]
    vc_ref[...] = kall[:, gd:2 * gd]
    row = pl.program_id(1) * tm + lax.broadcasted_iota(jnp.int32, (tm, 128), 0)
    lane = lax.broadcasted_iota(jnp.int32, (tm, 128), 1)
    blk = (row // SEL_LEN) % SEL_PER_GROUP
    onehot = jnp.where(lane - NSA_D == blk, 1.0, 0.0).astype(F32)
    for g in range(NSA_GROUPS):
        ks = kall[:, 2 * gd + 128 * g:2 * gd + 128 * (g + 1)]
        ks_ref[0, g] = (ks + onehot).astype(BF16)
        kw = kall[:, 2 * gd + 512 + 128 * g:2 * gd + 512 + 128 * (g + 1)]
        kw_ref[0, g] = kw.astype(BF16)
    vT = lax.dot_general(wvT_ref[...], xn, (((1,), (1,)), ((), ())),
                         preferred_element_type=F32)
    extra = jnp.where(lax.broadcasted_iota(jnp.int32, (V_ROWS - NSA_D, K_TILE), 0) == 0, 1.0, 0.0)
    extra = extra.astype(BF16)
    for g in range(NSA_GROUPS):
        for c in range(tm // K_TILE):
            cols = slice(c * K_TILE, (c + 1) * K_TILE)
            vsT_ref[0, g, c, 0:NSA_D, :] = vT[NSA_D * g:NSA_D * (g + 1), cols].astype(BF16)
            vsT_ref[0, g, c, NSA_D:V_ROWS, :] = extra
            vwT_ref[0, g, c, 0:NSA_D, :] = vT[gd + NSA_D * g:gd + NSA_D * (g + 1), cols].astype(BF16)
            vwT_ref[0, g, c, NSA_D:V_ROWS, :] = extra


def kv_project(x, g, kv_w, *, batch, seq, tm):
    d = x.shape[1]
    gd = NSA_GROUPS * NSA_D
    k_c, v_c, k_s, v_s, k_w, v_w = [kv_w[:, i * gd:(i + 1) * gd] for i in range(6)]

    def pad_groups(w):
        w = w.reshape(d, NSA_GROUPS, NSA_D)
        return jnp.pad(w, ((0, 0), (0, 0), (0, 128 - NSA_D))).reshape(d, NSA_GROUPS * 128)

    wk = jnp.concatenate([k_c, v_c, pad_groups(k_s), pad_groups(k_w)], axis=1).astype(BF16)
    wvT = jnp.concatenate([v_s, v_w], axis=1).T.astype(BF16)
    nt = seq // tm
    G = NSA_GROUPS
    return pl.pallas_call(
        _kv_body,
        out_shape=(
            jax.ShapeDtypeStruct((batch * seq, gd), F32),
            jax.ShapeDtypeStruct((batch * seq, gd), F32),
            jax.ShapeDtypeStruct((batch, G, seq, 128), BF16),
            jax.ShapeDtypeStruct((batch, G, seq, 128), BF16),
            jax.ShapeDtypeStruct((batch, G, seq // K_TILE, V_ROWS, K_TILE), BF16),
            jax.ShapeDtypeStruct((batch, G, seq // K_TILE, V_ROWS, K_TILE), BF16),
        ),
        grid=(batch, nt),
        in_specs=[
            pl.BlockSpec((tm, d), lambda b, i: (b * nt + i, 0)),
            pl.BlockSpec((1, d), lambda b, i: (0, 0)),
            pl.BlockSpec(wk.shape, lambda b, i: (0, 0)),
            pl.BlockSpec(wvT.shape, lambda b, i: (0, 0)),
        ],
        out_specs=(
            pl.BlockSpec((tm, gd), lambda b, i: (b * nt + i, 0)),
            pl.BlockSpec((tm, gd), lambda b, i: (b * nt + i, 0)),
            pl.BlockSpec((1, G, tm, 128), lambda b, i: (b, 0, i, 0)),
            pl.BlockSpec((1, G, tm, 128), lambda b, i: (b, 0, i, 0)),
            pl.BlockSpec((1, G, tm // K_TILE, V_ROWS, K_TILE), lambda b, i: (b, 0, i, 0, 0)),
            pl.BlockSpec((1, G, tm // K_TILE, V_ROWS, K_TILE), lambda b, i: (b, 0, i, 0, 0)),
        ),
        compiler_params=_cparams(("parallel", "parallel"), VMEM_LIMIT),
        name="kv_project",
    )(x, g.reshape(1, d), wk, wvT)


def _compress_body(c_ref, pe_ref, w1_ref, w2_ref, w2T_ref, nat_ref, tr_ref, sh_ref):
    half = CMP_STRIDE * NSA_D
    nc = c_ref.shape[3]
    c = c_ref[0, 0, 0].astype(BF16)
    w1 = w1_ref[0]
    first = jnp.dot(c, w1[0:half], preferred_element_type=F32)
    second = jnp.dot(c, w1[half:2 * half], preferred_element_type=F32)
    pe_term = jnp.dot(pe_ref[0].astype(BF16), w1, preferred_element_type=F32)
    sh_ref[0:nc, :] = second
    sh_ref[nc:nc + 8, :] = jnp.zeros((8, CMP_HIDDEN), F32)
    pre = first + sh_ref[1:nc + 1, :] + pe_term[0:1, :]
    hid = (pre * jax.nn.sigmoid(pre)).astype(BF16)
    nat_ref[0, 0, 0] = jnp.dot(hid, w2_ref[0], preferred_element_type=F32).astype(BF16)
    tr_ref[0, 0, 0] = lax.dot_general(w2T_ref[0], hid, (((1,), (1,)), ((), ())),
                                      preferred_element_type=F32).astype(BF16)


def compress_blocks(kc_nat, vc_nat, pe_k, w1_k, w2_k, pe_v, w1_v, w2_v, *, batch, seq):
    G, d = NSA_GROUPS, NSA_D
    nc = seq // CMP_STRIDE

    def to_rows(t):
        t = t.reshape(batch, nc, CMP_STRIDE, G, d).transpose(0, 3, 1, 2, 4)
        return t.reshape(batch, G, nc, CMP_STRIDE * d)

    c_all = jnp.stack([to_rows(kc_nat), to_rows(vc_nat)])
    pe = jnp.stack([pe_k.reshape(1, -1), pe_v.reshape(1, -1)])
    pe = jnp.broadcast_to(pe, (2, 8, CMP_LEN * d))
    w1 = jnp.stack([w1_k, w1_v]).astype(BF16)
    w2 = jnp.stack([w2_k, w2_v]).astype(BF16)
    w2T = jnp.stack([w2_k.T, w2_v.T]).astype(BF16)
    return pl.pallas_call(
        _compress_body,
        out_shape=(
            jax.ShapeDtypeStruct((2, batch, G, nc, d), BF16),
            jax.ShapeDtypeStruct((2, batch, G, d, nc), BF16),
        ),
        grid=(2, batch, G),
        in_specs=[
            pl.BlockSpec((1, 1, 1, nc, CMP_STRIDE * d), lambda w, b, g: (w, b, g, 0, 0)),
            pl.BlockSpec((1, 8, CMP_LEN * d), lambda w, b, g: (w, 0, 0)),
            pl.BlockSpec((1, CMP_LEN * d, CMP_HIDDEN), lambda w, b, g: (w, 0, 0)),
            pl.BlockSpec((1, CMP_HIDDEN, d), lambda w, b, g: (w, 0, 0)),
            pl.BlockSpec((1, d, CMP_HIDDEN), lambda w, b, g: (w, 0, 0)),
        ],
        out_specs=(
            pl.BlockSpec((1, 1, 1, nc, d), lambda w, b, g: (w, b, g, 0, 0)),
            pl.BlockSpec((1, 1, 1, d, nc), lambda w, b, g: (w, b, g, 0, 0)),
        ),
        scratch_shapes=[pltpu.VMEM((nc + 8, CMP_HIDDEN), F32)],
        compiler_params=_cparams(("parallel", "parallel", "parallel"), VMEM_LIMIT),
        name="compress_blocks",
    )(c_all, pe, w1, w2, w2T)


GATE_ROWS = 16


def _qproj_body(x_ref, g_ref, wT_ref, qT_ref, gT_ref):
    xn = _rms(x_ref[...], g_ref[...]).astype(BF16)
    tm = xn.shape[0]
    pT = lax.dot_general(wT_ref[...], xn, (((1,), (1,)), ((), ())),
                         preferred_element_type=F32)
    hd = NSA_HEADS * NSA_D
    q = pT[0:hd] * (NSA_D ** -0.5 * LOG2_E)
    qT_ref[0] = q.reshape(NSA_HEADS, NSA_D, tm).astype(BF16)
    gates = jax.nn.sigmoid(pT[hd:hd + NSA_GROUPS * GATE_ROWS])
    gT_ref[0] = gates.reshape(NSA_GROUPS, GATE_ROWS, tm)


def q_project(x, g, w_in, *, batch, seq, tm):
    d = x.shape[1]
    hd = NSA_HEADS * NSA_D
    per_group = NSA_REP * 3
    wg = w_in[:, hd:].reshape(d, NSA_GROUPS, per_group)
    wg = jnp.pad(wg, ((0, 0), (0, 0), (0, GATE_ROWS - per_group))).reshape(d, NSA_GROUPS * GATE_ROWS)
    wT = jnp.concatenate([w_in[:, :hd], wg], axis=1).T.astype(BF16)
    nt = seq // tm
    return pl.pallas_call(
        _qproj_body,
        out_shape=(
            jax.ShapeDtypeStruct((batch, NSA_HEADS, NSA_D, seq), BF16),
            jax.ShapeDtypeStruct((batch, NSA_GROUPS, GATE_ROWS, seq), F32),
        ),
        grid=(batch, nt),
        in_specs=[
            pl.BlockSpec((tm, d), lambda b, i: (b * nt + i, 0)),
            pl.BlockSpec((1, d), lambda b, i: (0, 0)),
            pl.BlockSpec(wT.shape, lambda b, i: (0, 0)),
        ],
        out_specs=(
            pl.BlockSpec((1, NSA_HEADS, NSA_D, tm), lambda b, i: (b, 0, 0, i)),
            pl.BlockSpec((1, NSA_GROUPS, GATE_ROWS, tm), lambda b, i: (b, 0, 0, i)),
        ),
        compiler_params=_cparams(("parallel", "parallel"), VMEM_LIMIT),
        name="q_project",
    )(x, g.reshape(1, d), wT)


CMP_CHUNK = 256
CMP_PER_QTILE = Q_TILE // CMP_STRIDE
SEL_PER_QTILE = Q_TILE // SEL_LEN
LANES = NSA_REP * Q_TILE
TILES_PER_GROUP = SEL_PER_GROUP * SEL_LEN // K_TILE
BIAS_LINE = 1024
WIN_TILES = (WINDOW + Q_TILE) // K_TILE
FAR_UNROLL = 2


def _nsa_body(qT_ref, gT_ref, kc_ref, cmpL_ref, ks_ref, vsT_ref, kw_ref, vwT_ref,
              lines_ref, oT_ref,
              s_ref, imp_ref, msel_ref, qaug_ref, acc_ref, m_ref, out_ref, bc_ref, bs_ref, bw_ref, cacc_ref):
    i = pl.program_id(2)
    ns = imp_ref.shape[0]
    qT = jnp.concatenate([qT_ref[0, r] for r in range(NSA_REP)], axis=1)
    tiny = jnp.finfo(F32).tiny

    @pl.when(i == 0)
    def _():
        def expand(kind, r, rows, stride):
            line = lines_ref[0, kind, r:r + 1, :]
            shifted = pltpu.roll(jnp.broadcast_to(line, (rows, line.shape[1])), 0, 1,
                                 stride=stride, stride_axis=0)
            return shifted[:, 0:Q_TILE]

        for r in range(NSA_REP):
            cols = slice(r * Q_TILE, (r + 1) * Q_TILE)
            bc_ref[:, cols] = expand(0, r, bc_ref.shape[0], CMP_STRIDE)
            bs_ref[:, cols] = expand(1, r, bs_ref.shape[0], 1)
            bw_ref[:, cols] = expand(2, r, bw_ref.shape[0], 1)

    def gate_row(j):
        return jnp.concatenate([gT_ref[0, 0, 3 * r + j:3 * r + j + 1, :] for r in range(NSA_REP)], axis=1)

    nchunks = i // (CMP_CHUNK // CMP_PER_QTILE) + 1
    visible = CMP_PER_QTILE * (i + 1)

    def chunk_rows(c, count=1):
        return pl.ds(pl.multiple_of(c * CMP_CHUNK, CMP_CHUNK), count * CMP_CHUNK)

    def for_chunks(body):
        def pair(pr, carry):
            body(2 * pr, 2)
            return carry

        lax.fori_loop(0, nchunks // 2, pair, 0)

        @pl.when(nchunks % 2 == 1)
        def _():
            body(nchunks - 1, 1)

    def cmp_scores(c, count):
        s_ref[chunk_rows(c, count), :] = jnp.dot(kc_ref[0, 0, 0, chunk_rows(c, count), :], qT,
                                                 preferred_element_type=F32)

    for_chunks(cmp_scores)

    @pl.when(i == 0)
    def _():
        s_ref[0:CMP_PER_QTILE, :] = s_ref[0:CMP_PER_QTILE, :] + bc_ref[CMP_PER_QTILE:2 * CMP_PER_QTILE, :]

    @pl.when(i > 0)
    def _():
        rows = pl.ds(pl.multiple_of(CMP_PER_QTILE * (i - 1), CMP_PER_QTILE), 2 * CMP_PER_QTILE)
        s_ref[rows, :] = s_ref[rows, :] + bc_ref[...]

    cacc_ref[...] = jnp.zeros_like(cacc_ref)
    m_ref[...] = jnp.full(m_ref.shape, M_INIT, F32)

    def cmp_step(c, count):
        rid = c * CMP_CHUNK + lax.broadcasted_iota(jnp.int32, (count * CMP_CHUNK, LANES), 0)
        s = jnp.where(rid < visible, s_ref[chunk_rows(c, count), :], NEG_BIG)
        m_old = m_ref[0:1, :]
        m_new = jnp.maximum(m_old, jnp.max(s, axis=0, keepdims=True))
        alpha = jnp.exp2(m_old - m_new)
        p = jnp.exp2(s - m_new).astype(BF16)
        left = jnp.concatenate([cmpL_ref[0, 0, c + j] for j in range(count)], axis=1)
        cacc_ref[...] = alpha * cacc_ref[...] + jnp.dot(left, p, preferred_element_type=F32)
        m_ref[0:1, :] = m_new

    for_chunks(cmp_step)
    sees_block = m_ref[0:1, :] > 0.5 * NEG_BIG
    inv_c = jnp.where(sees_block, 1.0 / jnp.maximum(cacc_ref[NSA_D:NSA_D + 1, :], tiny), 0.0)
    out_ref[...] = (gate_row(0) * inv_c) * cacc_ref[0:NSA_D, :]
    weights = cacc_ref[V_ROWS:V_ROWS + ns, :] * inv_c
    imp = weights[:, 0:Q_TILE]
    for r in range(1, NSA_REP):
        imp = imp + weights[:, r * Q_TILE:(r + 1) * Q_TILE]
    imp_ref[...] = imp

    qaug_ref[0:NSA_D, :] = qT
    qaug_ref[NSA_D:, :] = jnp.zeros((qaug_ref.shape[0] - NSA_D, LANES), BF16)

    def reset():
        acc_ref[...] = jnp.zeros_like(acc_ref)
        m_ref[...] = jnp.full(m_ref.shape, M_INIT, F32)

    def tile_rows(j):
        return slice(j * K_TILE, (j + 1) * K_TILE)

    def scores_to_scratch(k_ref, first_tile, count, slot):
        rows = pl.ds(pl.multiple_of(first_tile * K_TILE, K_TILE), count * K_TILE)
        s_ref[slot * K_TILE:(slot + count) * K_TILE, :] = jnp.dot(
            k_ref[0, 0, rows, :], qaug_ref[...], preferred_element_type=F32)

    def softmax_from_scratch(vT_ref, tiles, slots):
        rows = slice(slots[0] * K_TILE, (slots[-1] + 1) * K_TILE)
        m_old = m_ref[0:1, :]
        m_new = jnp.maximum(m_old, jnp.max(s_ref[rows, :], axis=0, keepdims=True))
        alpha = jnp.exp2(m_old - m_new)
        p = jnp.exp2(s_ref[rows, :] - m_new).astype(BF16)
        vT = jnp.concatenate([vT_ref[0, 0, kt] for kt in tiles], axis=1)
        acc_ref[...] = alpha * acc_ref[...] + jnp.dot(vT, p, preferred_element_type=F32)
        m_ref[0:1, :] = m_new

    def finish(j):
        l = jnp.maximum(acc_ref[NSA_D:NSA_D + 1, :], tiny)
        out_ref[...] += (gate_row(j) * (1.0 / l)) * acc_ref[0:NSA_D, :]

    win_slots = [TILES_PER_GROUP + j for j in range(WIN_TILES)]
    win_tiles = [jnp.maximum(i - (WIN_TILES - 1) + j, 0) for j in range(WIN_TILES)]
    reset()
    for j, (kt, slot) in enumerate(zip(win_tiles, win_slots)):
        scores_to_scratch(kw_ref, kt, 1, slot)
        before_start = jnp.where(i - (WIN_TILES - 1) + j < 0, NEG_BIG, 0.0).astype(F32)
        s_ref[tile_rows(slot), :] = s_ref[tile_rows(slot), :] + (bw_ref[tile_rows(j), :] + before_start)
    softmax_from_scratch(vwT_ref, win_tiles, win_slots)
    finish(2)

    jrow = lax.broadcasted_iota(jnp.int32, (ns, Q_TILE), 0)
    col = lax.broadcasted_iota(jnp.int32, (ns, Q_TILE), 1)
    cur = SEL_PER_QTILE * i + col // SEL_LEN
    valid = jrow <= cur
    forced = (jrow == 0) | (jrow == cur) | (jrow == cur - 1)
    free = float(SEL_TOPK - 3)
    candidate = valid & jnp.logical_not(forced)
    v0 = jnp.where(candidate, imp_ref[...], -jnp.inf)

    def strip_max(carry):
        v, taken, theta, above = carry
        best = jnp.max(v, axis=0, keepdims=True)
        hit = v == best
        now = taken + jnp.sum(jnp.where(hit, 1.0, 0.0), axis=0, keepdims=True)
        crossed = (taken < free) & (now >= free)
        return (jnp.where(hit, -jnp.inf, v), now,
                jnp.where(crossed, best, theta), jnp.where(crossed, taken, above))

    zero_row = jnp.zeros((1, Q_TILE), F32)
    carry = (v0, zero_row, jnp.full((1, Q_TILE), jnp.inf, F32), zero_row)
    for _ in range(SEL_TOPK - 3):
        carry = strip_max(carry)
    _, _, theta, above = carry
    v0 = jnp.where(candidate, imp_ref[...], -jnp.inf)
    tied = v0 == theta
    lower = jnp.where(lax.broadcasted_iota(jnp.int32, (ns, ns), 1) < lax.broadcasted_iota(jnp.int32, (ns, ns), 0),
                      1.0, 0.0).astype(BF16)
    rank = jnp.dot(lower, jnp.where(tied, 1.0, 0.0).astype(BF16), preferred_element_type=F32)
    chosen = forced | (v0 > theta) | (tied & (rank < free - above))
    mask_bias = jnp.where(chosen & valid, 0.0, NEG_BIG).astype(BF16)
    msel_ref[...] = jnp.concatenate([mask_bias] * NSA_REP, axis=1)

    last = i // TILES_PER_GROUP
    j_i = i % TILES_PER_GROUP

    half = TILES_PER_GROUP // 2

    def sel_tiles(grp, h):
        return [grp * TILES_PER_GROUP + h * half + j for j in range(half)]

    def sel_slots(h):
        return [h * half + j for j in range(half)]

    def sel_scores(grp, h):
        slab = pl.ds(pl.multiple_of(grp * SEL_PER_GROUP, SEL_PER_GROUP), SEL_PER_GROUP)
        qaug_ref[NSA_D:NSA_D + SEL_PER_GROUP, :] = msel_ref[slab, :]
        scores_to_scratch(ks_ref, sel_tiles(grp, h)[0], half, h * half)

    def sel_softmax(grp, h):
        softmax_from_scratch(vsT_ref, sel_tiles(grp, h), sel_slots(h))

    def add_bias(slot, count, bias_tile):
        rows = slice(slot * K_TILE, (slot + count) * K_TILE)
        s_ref[rows, :] = s_ref[rows, :] + bs_ref[bias_tile * K_TILE:(bias_tile + count) * K_TILE, :]

    reset()
    sel_scores(0, 0)

    def far(grp):
        sel_scores(grp, 1)
        sel_softmax(grp, 0)
        sel_scores(grp + 1, 0)
        sel_softmax(grp, 1)

    def far_many(trip, carry):
        for u in range(FAR_UNROLL):
            far(FAR_UNROLL * trip + u)
        return carry

    prev_is_near = (last >= 1) & (j_i == 0)
    n_far = jnp.where(prev_is_near, last - 1, last)
    lax.fori_loop(0, n_far // FAR_UNROLL, far_many, 0)

    def far_rest(grp, carry):
        far(grp)
        return carry

    lax.fori_loop((n_far // FAR_UNROLL) * FAR_UNROLL, n_far, far_rest, 0)

    @pl.when(prev_is_near)
    def _():
        sel_scores(last - 1, 1)
        sel_softmax(last - 1, 0)
        add_bias(TILES_PER_GROUP - 1, 1, 0)
        sel_scores(last, 0)
        sel_softmax(last - 1, 1)

    @pl.when(j_i == 0)
    def _():
        add_bias(0, 1, 1)

    @pl.when(j_i == 1)
    def _():
        add_bias(0, 2, 0)

    @pl.when(j_i == 2)
    def _():
        add_bias(1, 1, 0)

    @pl.when(j_i >= half)
    def _():
        sel_scores(last, 1)
        sel_softmax(last, 0)

        @pl.when(j_i == 2)
        def _():
            add_bias(2, 1, 1)

        @pl.when(j_i == 3)
        def _():
            add_bias(2, 2, 0)

        sel_softmax(last, 1)

    @pl.when(j_i < half)
    def _():
        sel_softmax(last, 0)

    finish(1)

    for r in range(NSA_REP):
        oT_ref[0, r * NSA_D:(r + 1) * NSA_D, :] = out_ref[:, r * Q_TILE:(r + 1) * Q_TILE].astype(oT_ref.dtype)


def nsa_attention(qT, gT, cmp_nat, cmp_tr, ks, vsT, kw, vwT, lines, mmapT, *, batch, seq):
    G = NSA_GROUPS
    nc = seq // CMP_STRIDE
    ns = seq // SEL_LEN
    nq = seq // Q_TILE
    nkt = seq // K_TILE
    hd = NSA_HEADS * NSA_D
    nch = nc // CMP_CHUNK
    vcT = cmp_tr[1].reshape(batch, G, NSA_D, nch, CMP_CHUNK).transpose(0, 1, 3, 2, 4)
    ones = jnp.zeros((V_ROWS - NSA_D, CMP_CHUNK), BF16).at[0].set(1.0)
    cmpL = jnp.concatenate([
        vcT,
        jnp.broadcast_to(ones, (batch, G, nch, V_ROWS - NSA_D, CMP_CHUNK)),
        jnp.broadcast_to(mmapT, (batch, G, nch, ns, CMP_CHUNK)),
    ], axis=3)
    once = pl.Buffered(1)
    return pl.pallas_call(
        _nsa_body,
        out_shape=jax.ShapeDtypeStruct((batch, hd, seq), BF16),
        grid=(batch, G, nq),
        in_specs=[
            pl.BlockSpec((1, NSA_REP, NSA_D, Q_TILE), lambda b, g, i: (b, g, 0, i)),
            pl.BlockSpec((1, 1, GATE_ROWS, Q_TILE), lambda b, g, i: (b, g, 0, i)),
            pl.BlockSpec((1, 1, 1, nc, NSA_D), lambda b, g, i: (0, b, g, 0, 0)),
            pl.BlockSpec((1, 1, nch, V_ROWS + ns, CMP_CHUNK), lambda b, g, i: (b, g, 0, 0, 0)),
            pl.BlockSpec((1, 1, seq, 128), lambda b, g, i: (b, g, 0, 0), pipeline_mode=once),
            pl.BlockSpec((1, 1, nkt, V_ROWS, K_TILE), lambda b, g, i: (b, g, 0, 0, 0), pipeline_mode=once),
            pl.BlockSpec((1, 1, seq, 128), lambda b, g, i: (b, g, 0, 0), pipeline_mode=once),
            pl.BlockSpec((1, 1, nkt, V_ROWS, K_TILE), lambda b, g, i: (b, g, 0, 0, 0), pipeline_mode=once),
            pl.BlockSpec((1, 3, NSA_REP, BIAS_LINE), lambda b, g, i: (g, 0, 0, 0)),
        ],
        out_specs=pl.BlockSpec((1, NSA_REP * NSA_D, Q_TILE), lambda b, g, i: (b, g, i)),
        scratch_shapes=[
            pltpu.VMEM((max(nc, (TILES_PER_GROUP + WIN_TILES) * K_TILE), LANES), F32),
            pltpu.VMEM((ns, Q_TILE), F32),
            pltpu.VMEM((ns, LANES), BF16),
            pltpu.VMEM((128, LANES), BF16),
            pltpu.VMEM((V_ROWS, LANES), F32),
            pltpu.VMEM((8, LANES), F32),
            pltpu.VMEM((NSA_D, LANES), F32),
            pltpu.VMEM((2 * CMP_PER_QTILE, LANES), F32),
            pltpu.VMEM((2 * K_TILE, LANES), F32),
            pltpu.VMEM((3 * K_TILE, LANES), F32),
            pltpu.VMEM((V_ROWS + ns, LANES), F32),
        ],
        compiler_params=_cparams(("parallel", "parallel", "arbitrary"), VMEM_LIMIT),
        name="nsa_attention",
    )(qT, gT, cmp_nat, cmpL, ks, vsT, kw, vwT, lines)


def _t5_bucket_table(n_max):
    n = np.arange(n_max)
    max_exact = REL_BUCKETS // 2
    nf = np.maximum(n, 1).astype(np.float32)
    large = max_exact + (np.log(nf / np.float32(max_exact))
                         / np.float32(math.log(REL_MAX_DIST / max_exact))
                         * np.float32(REL_BUCKETS - max_exact)).astype(np.int32)
    large = np.minimum(large, REL_BUCKETS - 1)
    return np.where(n < max_exact, n, large).astype(np.int32)


def _bias_tables(rel_bias):
    n_max = WINDOW + Q_TILE
    buckets = _t5_bucket_table(n_max)
    assert (buckets[Q_TILE - CMP_LEN + 1:] == REL_BUCKETS - 1).all()
    tab = rel_bias.astype(F32)[buckets, :] - rel_bias.astype(F32)[REL_BUCKETS - 1][None, :]
    tab = tab.T * LOG2_E
    def line(rows, stride, offset, ok, fill):
        assert Q_TILE + stride * (rows - 1) <= BIAS_LINE
        k = np.arange(BIAS_LINE)
        dist = np.where(k < Q_TILE, k, k - BIAS_LINE) + offset
        vals = jnp.where(ok(dist)[None], tab[:, np.clip(dist, 0, n_max - 1)], fill)
        return vals.reshape(NSA_GROUPS, NSA_REP, BIAS_LINE)

    return jnp.stack([
        line(2 * CMP_PER_QTILE, CMP_STRIDE, Q_TILE - CMP_LEN + 1, lambda dd: dd >= 0, NEG_BIG),
        line(2 * K_TILE, 1, K_TILE, lambda dd: dd >= 0, NEG_BIG),
        line(3 * K_TILE, 1, 2 * K_TILE, lambda dd: (dd >= 0) & (dd < WINDOW), NEG_BIG),
    ], axis=1)


def _selection_map(seq):
    nc = seq // CMP_STRIDE
    ns = seq // SEL_LEN
    n_cmp = (seq - CMP_LEN) // CMP_STRIDE + 1
    ratio = SEL_LEN // CMP_STRIDE
    lead = CMP_LEN // CMP_STRIDE - 1
    j = np.arange(ns)[:, None]
    n = np.arange(nc)[None, :]
    m = ((n >= ratio * j - lead) & (n < ratio * j + ratio) & (n < n_cmp)).astype(np.float32)
    m = m.reshape(ns, nc // CMP_CHUNK, CMP_CHUNK).transpose(1, 0, 2)
    return jnp.asarray(m, dtype=BF16)


def kernel(x, mix_norm_pre, mix_norm_post, ffn_norm_pre, ffn_norm_post, ffn_w_in, ffn_w_out,
           ret_w_in, ret_w_out, kv_norm, kv_w, cmp_pe_k, cmp_w1_k, cmp_w2_k,
           cmp_pe_v, cmp_w1_v, cmp_w2_v, nsa_w_in, nsa_w_out, rel_bias):
    batch, seq, d = x.shape
    n_ret = ret_w_in.shape[0]
    n_nsa = nsa_w_in.shape[0]
    assert seq % (SEL_PER_GROUP * SEL_LEN) == 0 and seq // SEL_LEN >= SEL_TOPK
    h = x.reshape(batch * seq, d)
    shared = None
    for layer in range(n_ret + n_nsa):
        if layer == n_ret:
            kc_nat, vc_nat, ks, kw, vsT, vwT = kv_project(h, kv_norm, kv_w, batch=batch, seq=seq, tm=1024)
            cmp_nat, cmp_tr = compress_blocks(kc_nat, vc_nat, cmp_pe_k, cmp_w1_k, cmp_w2_k,
                                              cmp_pe_v, cmp_w1_v, cmp_w2_v, batch=batch, seq=seq)
            shared = (cmp_nat, cmp_tr, ks, vsT, kw, vwT, _bias_tables(rel_bias), _selection_map(seq))
        if layer < n_ret:
            w_in = _deinterleave_qk_columns(ret_w_in[layer]).astype(BF16)
            proj = norm_matmul(h, mix_norm_pre[layer], w_in, tm=1024, tn=2048, out_dtype=BF16)
            mixed = retention_core(proj, *_retention_tables(seq), batch=batch, seq=seq)
            h = proj_norm_res(mixed, ret_w_out[layer].astype(BF16), mix_norm_post[layer], h, tm=512)
        else:
            j = layer - n_ret
            qT, gT = q_project(h, mix_norm_pre[layer], nsa_w_in[j], batch=batch, seq=seq, tm=512)
            oT = nsa_attention(qT, gT, *shared, batch=batch, seq=seq)
            h = projT_norm_res(oT, nsa_w_out[j].astype(BF16), mix_norm_post[layer], h, tm=512)
        h = ffn_block(h, ffn_norm_pre[layer], ffn_w_in[layer].astype(BF16), ffn_w_out[layer].astype(BF16),
                      ffn_norm_post[layer], tm=512, chunks=2)
    return h.reshape(batch, seq, d)
```

```python
import functools
import math

import numpy as np
import jax
import jax.numpy as jnp
from jax import lax
from jax.experimental import pallas as pl
from jax.experimental.pallas import tpu as pltpu

F32 = jnp.float32
BF16 = jnp.bfloat16

D_MODEL = 1024
RMS_EPS = 1e-6

RET_HEADS = 4
RET_QK = 256
RET_V = 512
RET_CHUNK = 128

FFN_HIDDEN = 2816

NSA_HEADS = 16
NSA_GROUPS = 4
NSA_REP = 4
NSA_D = 64
CMP_LEN = 32
CMP_STRIDE = 16
CMP_HIDDEN = 256
SEL_LEN = 64
SEL_TOPK = 16
WINDOW = 512
REL_BUCKETS = 32
REL_MAX_DIST = 128

Q_TILE = 256
K_TILE = 256
SEL_PER_GROUP = 16
V_ROWS = 80
LOG2_E = math.log2(math.e)
NEG_BIG = -(2.0 ** 100)
M_INIT = -(2.0 ** 120)

VMEM_LIMIT = 56 * 1024 * 1024


def _cparams(sem, vmem=None, flags=None):
    return pltpu.CompilerParams(dimension_semantics=sem, vmem_limit_bytes=vmem, flags=flags)


def _rms(x, g):
    return x * lax.rsqrt(jnp.mean(x * x, axis=-1, keepdims=True) + RMS_EPS) * g


def _norm_matmul_body(x_ref, g_ref, w_ref, o_ref, xn_ref):
    @pl.when(pl.program_id(1) == 0)
    def _():
        xn_ref[...] = _rms(x_ref[...], g_ref[...]).astype(BF16)

    o_ref[...] = jnp.dot(xn_ref[...], w_ref[...], preferred_element_type=F32).astype(o_ref.dtype)


def norm_matmul(x, g, w, *, tm, tn, out_dtype=F32):
    t, d = x.shape
    n = w.shape[1]
    return pl.pallas_call(
        _norm_matmul_body,
        out_shape=jax.ShapeDtypeStruct((t, n), out_dtype),
        grid=(t // tm, n // tn),
        in_specs=[
            pl.BlockSpec((tm, d), lambda i, j: (i, 0)),
            pl.BlockSpec((1, d), lambda i, j: (0, 0)),
            pl.BlockSpec((d, tn), lambda i, j: (0, j)),
        ],
        out_specs=pl.BlockSpec((tm, tn), lambda i, j: (i, j)),
        scratch_shapes=[pltpu.VMEM((tm, d), BF16)],
        compiler_params=_cparams(("parallel", "arbitrary"), VMEM_LIMIT),
        name="norm_matmul",
    )(x, g.reshape(1, d), w)


def _mix_out_ffn_body(y_ref, wp_ref, gmix_ref, res_ref, gpre_ref, win_ref, wo_ref, gpost_ref, o_ref,
                      *, chunks, feature_major):
    if feature_major:
        z = lax.dot_general(y_ref[0], wp_ref[...], (((0,), (0,)), ((), ())), preferred_element_type=F32)
    else:
        z = jnp.dot(y_ref[...], wp_ref[...], preferred_element_type=F32)
    x = res_ref[...] + _rms(z, gmix_ref[...])
    xn = _rms(x, gpre_ref[...]).astype(BF16)
    hdim = wo_ref.shape[0]
    th = hdim // chunks
    y = None
    for c in range(chunks):
        gate = jnp.dot(xn, win_ref[:, c * th:(c + 1) * th], preferred_element_type=F32)
        up = jnp.dot(xn, win_ref[:, hdim + c * th:hdim + (c + 1) * th], preferred_element_type=F32)
        act = (gate * jax.nn.sigmoid(gate) * up).astype(BF16)
        part = jnp.dot(act, wo_ref[c * th:(c + 1) * th, :], preferred_element_type=F32)
        y = part if y is None else y + part
    o_ref[...] = x + _rms(y, gpost_ref[...])


def mix_out_ffn(y, w_proj, g_mix, res, g_pre, w_in, w_out, g_post, *, batch, seq, tm, chunks, feature_major):
    k, d = w_proj.shape
    hdim = w_out.shape[0]
    assert (hdim // chunks) % 128 == 0 and hdim % chunks == 0
    nt = seq // tm
    once = pl.Buffered(1)
    row = lambda b, i: (b * nt + i, 0)
    const = lambda b, i: (0, 0)
    if feature_major:
        y_spec = pl.BlockSpec((1, k, tm), lambda b, i: (b, 0, i))
    else:
        y_spec = pl.BlockSpec((tm, k), row)
    return pl.pallas_call(
        functools.partial(_mix_out_ffn_body, chunks=chunks, feature_major=feature_major),
        out_shape=jax.ShapeDtypeStruct((batch * seq, d), F32),
        grid=(batch, nt),
        in_specs=[
            y_spec,
            pl.BlockSpec((k, d), const, pipeline_mode=once),
            pl.BlockSpec((1, d), const),
            pl.BlockSpec((tm, d), row),
            pl.BlockSpec((1, d), const),
            pl.BlockSpec((d, 2 * hdim), const, pipeline_mode=once),
            pl.BlockSpec((hdim, d), const, pipeline_mode=once),
            pl.BlockSpec((1, d), const),
        ],
        out_specs=pl.BlockSpec((tm, d), row),
        compiler_params=_cparams(("parallel", "parallel"), VMEM_LIMIT),
        name="mix_out_ffn",
    )(y, w_proj, g_mix.reshape(1, d), res, g_pre.reshape(1, d), w_in, w_out, g_post.reshape(1, d))


def _retention_body(q_ref, k_ref, v_ref, g_ref, cos_ref, sin_ref, dmat_ref, qdec_ref, kdec_ref,
                    cdec_ref, o_ref, state_ref):
    @pl.when(pl.program_id(1) == 0)
    def _():
        state_ref[...] = jnp.zeros_like(state_ref)

    cos = cos_ref[...]
    sin = sin_ref[...]
    half = RET_QK // 2

    def rotate(x_ref, h):
        x1 = x_ref[:, h * RET_QK:h * RET_QK + half].astype(F32)
        x2 = x_ref[:, h * RET_QK + half:(h + 1) * RET_QK].astype(F32)
        return jnp.concatenate([x1 * cos - x2 * sin, x1 * sin + x2 * cos], axis=1)

    for h in range(RET_HEADS):
        qr = rotate(q_ref, h)
        kr = rotate(k_ref, h) * (RET_QK ** -0.5)
        v = v_ref[:, h * RET_V:(h + 1) * RET_V].astype(BF16)
        scores = lax.dot_general(qr.astype(BF16), kr.astype(BF16), (((1,), (1,)), ((), ())),
                                 preferred_element_type=F32) * dmat_ref[h]
        state = state_ref[h]
        o = (jnp.dot(scores.astype(BF16), v, preferred_element_type=F32)
             + jnp.dot((qr * qdec_ref[h]).astype(BF16), state.astype(BF16),
                       preferred_element_type=F32))
        kd = (kr * kdec_ref[h]).astype(BF16)
        state_ref[h] = state * cdec_ref[h, 0:1, :] + lax.dot_general(
            kd, v, (((0,), (0,)), ((), ())), preferred_element_type=F32)
        o = o * lax.rsqrt(jnp.mean(o * o, axis=-1, keepdims=True) + RMS_EPS)
        gate = g_ref[:, h * RET_V:(h + 1) * RET_V].astype(F32)
        o_ref[:, h * RET_V:(h + 1) * RET_V] = (o * (gate * jax.nn.sigmoid(gate))).astype(o_ref.dtype)


def retention_core(proj, cos, sin, dmat, qdec, kdec, cdec, *, batch, seq):
    c = RET_CHUNK
    nc = seq // c
    hq = RET_HEADS * RET_QK
    hv = RET_HEADS * RET_V
    return pl.pallas_call(
        _retention_body,
        out_shape=jax.ShapeDtypeStruct((batch * seq, hv), BF16),
        grid=(batch, nc),
        in_specs=[
            pl.BlockSpec((c, hq), lambda b, t: (b * nc + t, 0)),
            pl.BlockSpec((c, hq), lambda b, t: (b * nc + t, 1)),
            pl.BlockSpec((c, hv), lambda b, t: (b * nc + t, 1)),
            pl.BlockSpec((c, hv), lambda b, t: (b * nc + t, 2)),
            pl.BlockSpec((c, RET_QK // 2), lambda b, t: (t, 0)),
            pl.BlockSpec((c, RET_QK // 2), lambda b, t: (t, 0)),
            pl.BlockSpec((RET_HEADS, c, c), lambda b, t: (0, 0, 0)),
            pl.BlockSpec((RET_HEADS, c, RET_QK), lambda b, t: (0, 0, 0)),
            pl.BlockSpec((RET_HEADS, c, RET_QK), lambda b, t: (0, 0, 0)),
            pl.BlockSpec((RET_HEADS, 8, RET_V), lambda b, t: (0, 0, 0)),
        ],
        out_specs=pl.BlockSpec((c, hv), lambda b, t: (b * nc + t, 0)),
        scratch_shapes=[pltpu.VMEM((RET_HEADS, RET_QK, RET_V), F32)],
        compiler_params=_cparams(("parallel", "arbitrary"), VMEM_LIMIT),
        name="retention_core",
    )(proj, proj, proj, proj, cos, sin, dmat, qdec, kdec, cdec)


def _retention_tables(seq):
    h, dk, c = RET_HEADS, RET_QK, RET_CHUNK
    pos = jnp.arange(seq, dtype=F32)
    theta = 1.0 / (10000.0 ** jnp.linspace(0.0, 1.0, dk // 2, dtype=F32))
    ang = pos[:, None] * theta[None, :]
    log_gamma = jnp.log(1.0 - 2.0 ** (-5.0 - jnp.arange(h, dtype=F32)))
    idx = jnp.arange(c, dtype=F32)
    rel = idx[:, None] - idx[None, :]
    dmat = jnp.where(rel >= 0, jnp.exp(jnp.maximum(rel, 0.0) * log_gamma[:, None, None]), 0.0)
    qdec = jnp.exp((idx + 1.0)[None, :] * log_gamma[:, None])
    kdec = jnp.exp((c - 1.0 - idx)[None, :] * log_gamma[:, None])
    cdec = jnp.exp(c * log_gamma)
    qdec = jnp.broadcast_to(qdec[:, :, None], (h, c, dk))
    kdec = jnp.broadcast_to(kdec[:, :, None], (h, c, dk))
    cdec = jnp.broadcast_to(cdec[:, None, None], (h, 8, RET_V))
    return jnp.cos(ang), jnp.sin(ang), dmat, qdec, kdec, cdec


def _deinterleave_qk_columns(w):
    nqk = 2 * RET_HEADS * RET_QK
    perm = []
    for h in range(2 * RET_HEADS):
        base = h * RET_QK
        perm += [base + 2 * i for i in range(RET_QK // 2)]
        perm += [base + 2 * i + 1 for i in range(RET_QK // 2)]
    perm = np.asarray(perm + list(range(nqk, w.shape[1])), dtype=np.int32)
    return w[:, perm]


def _kv_body(x_ref, g_ref, wk_ref, wvT_ref, kc_ref, vc_ref, ks_ref, kw_ref, vsT_ref, vwT_ref):
    xn = _rms(x_ref[...], g_ref[...]).astype(BF16)
    tm = xn.shape[0]
    kall = jnp.dot(xn, wk_ref[...], preferred_element_type=F32)
    gd = NSA_GROUPS * NSA_D
    kc_ref[...] = kall[:, 0:gd]
    vc_ref[...] = kall[:, gd:2 * gd]
    row = pl.program_id(1) * tm + lax.broadcasted_iota(jnp.int32, (tm, 128), 0)
    lane = lax.broadcasted_iota(jnp.int32, (tm, 128), 1)
    blk = (row // SEL_LEN) % SEL_PER_GROUP
    onehot = jnp.where(lane - NSA_D == blk, 1.0, 0.0).astype(F32)
    for g in range(NSA_GROUPS):
        ks = kall[:, 2 * gd + 128 * g:2 * gd + 128 * (g + 1)]
        ks_ref[0, g] = (ks + onehot).astype(BF16)
        kw = kall[:, 2 * gd + 512 + 128 * g:2 * gd + 512 + 128 * (g + 1)]
        kw_ref[0, g] = kw.astype(BF16)
    vT = lax.dot_general(wvT_ref[...], xn, (((1,), (1,)), ((), ())),
                         preferred_element_type=F32)
    extra = jnp.where(lax.broadcasted_iota(jnp.int32, (V_ROWS - NSA_D, K_TILE), 0) == 0, 1.0, 0.0)
    extra = extra.astype(BF16)
    for g in range(NSA_GROUPS):
        for c in range(tm // K_TILE):
            cols = slice(c * K_TILE, (c + 1) * K_TILE)
            vsT_ref[0, g, c, 0:NSA_D, :] = vT[NSA_D * g:NSA_D * (g + 1), cols].astype(BF16)
            vsT_ref[0, g, c, NSA_D:V_ROWS, :] = extra
            vwT_ref[0, g, c, 0:NSA_D, :] = vT[gd + NSA_D * g:gd + NSA_D * (g + 1), cols].astype(BF16)
            vwT_ref[0, g, c, NSA_D:V_ROWS, :] = extra


def kv_project(x, g, kv_w, *, batch, seq, tm):
    d = x.shape[1]
    gd = NSA_GROUPS * NSA_D
    k_c, v_c, k_s, v_s, k_w, v_w = [kv_w[:, i * gd:(i + 1) * gd] for i in range(6)]

    def pad_groups(w):
        w = w.reshape(d, NSA_GROUPS, NSA_D)
        return jnp.pad(w, ((0, 0), (0, 0), (0, 128 - NSA_D))).reshape(d, NSA_GROUPS * 128)

    wk = jnp.concatenate([k_c, v_c, pad_groups(k_s), pad_groups(k_w)], axis=1).astype(BF16)
    wvT = jnp.concatenate([v_s, v_w], axis=1).T.astype(BF16)
    nt = seq // tm
    G = NSA_GROUPS
    return pl.pallas_call(
        _kv_body,
        out_shape=(
            jax.ShapeDtypeStruct((batch * seq, gd), F32),
            jax.ShapeDtypeStruct((batch * seq, gd), F32),
            jax.ShapeDtypeStruct((batch, G, seq, 128), BF16),
            jax.ShapeDtypeStruct((batch, G, seq, 128), BF16),
            jax.ShapeDtypeStruct((batch, G, seq // K_TILE, V_ROWS, K_TILE), BF16),
            jax.ShapeDtypeStruct((batch, G, seq // K_TILE, V_ROWS, K_TILE), BF16),
        ),
        grid=(batch, nt),
        in_specs=[
            pl.BlockSpec((tm, d), lambda b, i: (b * nt + i, 0)),
            pl.BlockSpec((1, d), lambda b, i: (0, 0)),
            pl.BlockSpec(wk.shape, lambda b, i: (0, 0)),
            pl.BlockSpec(wvT.shape, lambda b, i: (0, 0)),
        ],
        out_specs=(
            pl.BlockSpec((tm, gd), lambda b, i: (b * nt + i, 0)),
            pl.BlockSpec((tm, gd), lambda b, i: (b * nt + i, 0)),
            pl.BlockSpec((1, G, tm, 128), lambda b, i: (b, 0, i, 0)),
            pl.BlockSpec((1, G, tm, 128), lambda b, i: (b, 0, i, 0)),
            pl.BlockSpec((1, G, tm // K_TILE, V_ROWS, K_TILE), lambda b, i: (b, 0, i, 0, 0)),
            pl.BlockSpec((1, G, tm // K_TILE, V_ROWS, K_TILE), lambda b, i: (b, 0, i, 0, 0)),
        ),
        compiler_params=_cparams(("parallel", "parallel"), VMEM_LIMIT),
        name="kv_project",
    )(x, g.reshape(1, d), wk, wvT)


def _compress_body(c_ref, pe_ref, w1_ref, w2_ref, w2T_ref, nat_ref, tr_ref, sh_ref):
    half = CMP_STRIDE * NSA_D
    nc = c_ref.shape[3]
    c = c_ref[0, 0, 0].astype(BF16)
    w1 = w1_ref[0]
    first = jnp.dot(c, w1[0:half], preferred_element_type=F32)
    second = jnp.dot(c, w1[half:2 * half], preferred_element_type=F32)
    pe_term = jnp.dot(pe_ref[0].astype(BF16), w1, preferred_element_type=F32)
    sh_ref[0:nc, :] = second
    sh_ref[nc:nc + 8, :] = jnp.zeros((8, CMP_HIDDEN), F32)
    pre = first + sh_ref[1:nc + 1, :] + pe_term[0:1, :]
    hid = (pre * jax.nn.sigmoid(pre)).astype(BF16)
    nat_ref[0, 0, 0] = jnp.dot(hid, w2_ref[0], preferred_element_type=F32).astype(BF16)
    tr_ref[0, 0, 0] = lax.dot_general(w2T_ref[0], hid, (((1,), (1,)), ((), ())),
                                      preferred_element_type=F32).astype(BF16)


def compress_blocks(kc_nat, vc_nat, pe_k, w1_k, w2_k, pe_v, w1_v, w2_v, *, batch, seq):
    G, d = NSA_GROUPS, NSA_D
    nc = seq // CMP_STRIDE

    def to_rows(t):
        t = t.reshape(batch, nc, CMP_STRIDE, G, d).transpose(0, 3, 1, 2, 4)
        return t.reshape(batch, G, nc, CMP_STRIDE * d)

    c_all = jnp.stack([to_rows(kc_nat), to_rows(vc_nat)])
    pe = jnp.stack([pe_k.reshape(1, -1), pe_v.reshape(1, -1)])
    pe = jnp.broadcast_to(pe, (2, 8, CMP_LEN * d))
    w1 = jnp.stack([w1_k, w1_v]).astype(BF16)
    w2 = jnp.stack([w2_k, w2_v]).astype(BF16)
    w2T = jnp.stack([w2_k.T, w2_v.T]).astype(BF16)
    return pl.pallas_call(
        _compress_body,
        out_shape=(
            jax.ShapeDtypeStruct((2, batch, G, nc, d), BF16),
            jax.ShapeDtypeStruct((2, batch, G, d, nc), BF16),
        ),
        grid=(2, batch, G),
        in_specs=[
            pl.BlockSpec((1, 1, 1, nc, CMP_STRIDE * d), lambda w, b, g: (w, b, g, 0, 0)),
            pl.BlockSpec((1, 8, CMP_LEN * d), lambda w, b, g: (w, 0, 0)),
            pl.BlockSpec((1, CMP_LEN * d, CMP_HIDDEN), lambda w, b, g: (w, 0, 0)),
            pl.BlockSpec((1, CMP_HIDDEN, d), lambda w, b, g: (w, 0, 0)),
            pl.BlockSpec((1, d, CMP_HIDDEN), lambda w, b, g: (w, 0, 0)),
        ],
        out_specs=(
            pl.BlockSpec((1, 1, 1, nc, d), lambda w, b, g: (w, b, g, 0, 0)),
            pl.BlockSpec((1, 1, 1, d, nc), lambda w, b, g: (w, b, g, 0, 0)),
        ),
        scratch_shapes=[pltpu.VMEM((nc + 8, CMP_HIDDEN), F32)],
        compiler_params=_cparams(("parallel", "parallel", "parallel"), VMEM_LIMIT),
        name="compress_blocks",
    )(c_all, pe, w1, w2, w2T)


GATE_ROWS = 16


def _qproj_body(x_ref, g_ref, wT_ref, qT_ref, gT_ref):
    xn = _rms(x_ref[...], g_ref[...]).astype(BF16)
    tm = xn.shape[0]
    pT = lax.dot_general(wT_ref[...], xn, (((1,), (1,)), ((), ())),
                         preferred_element_type=F32)
    hd = NSA_HEADS * NSA_D
    q = pT[0:hd] * (NSA_D ** -0.5 * LOG2_E)
    qT_ref[0] = q.reshape(NSA_HEADS, NSA_D, tm).astype(BF16)
    gates = jax.nn.sigmoid(pT[hd:hd + NSA_GROUPS * GATE_ROWS])
    gT_ref[0] = gates.reshape(NSA_GROUPS, GATE_ROWS, tm)


def q_project(x, g, w_in, *, batch, seq, tm):
    d = x.shape[1]
    hd = NSA_HEADS * NSA_D
    per_group = NSA_REP * 3
    wg = w_in[:, hd:].reshape(d, NSA_GROUPS, per_group)
    wg = jnp.pad(wg, ((0, 0), (0, 0), (0, GATE_ROWS - per_group))).reshape(d, NSA_GROUPS * GATE_ROWS)
    wT = jnp.concatenate([w_in[:, :hd], wg], axis=1).T.astype(BF16)
    nt = seq // tm
    return pl.pallas_call(
        _qproj_body,
        out_shape=(
            jax.ShapeDtypeStruct((batch, NSA_HEADS, NSA_D, seq), BF16),
            jax.ShapeDtypeStruct((batch, NSA_GROUPS, GATE_ROWS, seq), F32),
        ),
        grid=(batch, nt),
        in_specs=[
            pl.BlockSpec((tm, d), lambda b, i: (b * nt + i, 0)),
            pl.BlockSpec((1, d), lambda b, i: (0, 0)),
            pl.BlockSpec(wT.shape, lambda b, i: (0, 0)),
        ],
        out_specs=(
            pl.BlockSpec((1, NSA_HEADS, NSA_D, tm), lambda b, i: (b, 0, 0, i)),
            pl.BlockSpec((1, NSA_GROUPS, GATE_ROWS, tm), lambda b, i: (b, 0, 0, i)),
        ),
        compiler_params=_cparams(("parallel", "parallel"), VMEM_LIMIT),
        name="q_project",
    )(x, g.reshape(1, d), wT)


CMP_CHUNK = 256
CMP_PER_QTILE = Q_TILE // CMP_STRIDE
SEL_PER_QTILE = Q_TILE // SEL_LEN
LANES = NSA_REP * Q_TILE
TILES_PER_GROUP = SEL_PER_GROUP * SEL_LEN // K_TILE
BIAS_LINE = 1024
WIN_TILES = (WINDOW + Q_TILE) // K_TILE
FAR_UNROLL = 2


def _nsa_body(qT_ref, gT_ref, kc_ref, cmpL_ref, ks_ref, vsT_ref, kw_ref, vwT_ref,
              lines_ref, oT_ref,
              s_ref, imp_ref, msel_ref, qaug_ref, acc_ref, m_ref, out_ref, bc_ref, bs_ref, bw_ref, cacc_ref):
    i = pl.program_id(2)
    ns = imp_ref.shape[0]
    qT = jnp.concatenate([qT_ref[0, r] for r in range(NSA_REP)], axis=1)
    tiny = jnp.finfo(F32).tiny

    @pl.when(i == 0)
    def _():
        def expand(kind, r, rows, stride):
            line = lines_ref[0, kind, r:r + 1, :]
            shifted = pltpu.roll(jnp.broadcast_to(line, (rows, line.shape[1])), 0, 1,
                                 stride=stride, stride_axis=0)
            return shifted[:, 0:Q_TILE]

        for r in range(NSA_REP):
            cols = slice(r * Q_TILE, (r + 1) * Q_TILE)
            bc_ref[:, cols] = expand(0, r, bc_ref.shape[0], CMP_STRIDE)
            bs_ref[:, cols] = expand(1, r, bs_ref.shape[0], 1)
            bw_ref[:, cols] = expand(2, r, bw_ref.shape[0], 1)

    def gate_row(j):
        return jnp.concatenate([gT_ref[0, 0, 3 * r + j:3 * r + j + 1, :] for r in range(NSA_REP)], axis=1)

    nchunks = i // (CMP_CHUNK // CMP_PER_QTILE) + 1
    visible = CMP_PER_QTILE * (i + 1)

    def chunk_rows(c, count=1):
        return pl.ds(pl.multiple_of(c * CMP_CHUNK, CMP_CHUNK), count * CMP_CHUNK)

    def for_chunks(body):
        def pair(pr, carry):
            body(2 * pr, 2)
            return carry

        lax.fori_loop(0, nchunks // 2, pair, 0)

        @pl.when(nchunks % 2 == 1)
        def _():
            body(nchunks - 1, 1)

    def cmp_scores(c, count):
        s_ref[chunk_rows(c, count), :] = jnp.dot(kc_ref[0, 0, 0, chunk_rows(c, count), :], qT,
                                                 preferred_element_type=F32)

    for_chunks(cmp_scores)

    @pl.when(i == 0)
    def _():
        s_ref[0:CMP_PER_QTILE, :] = s_ref[0:CMP_PER_QTILE, :] + bc_ref[CMP_PER_QTILE:2 * CMP_PER_QTILE, :]

    @pl.when(i > 0)
    def _():
        rows = pl.ds(pl.multiple_of(CMP_PER_QTILE * (i - 1), CMP_PER_QTILE), 2 * CMP_PER_QTILE)
        s_ref[rows, :] = s_ref[rows, :] + bc_ref[...]

    cacc_ref[...] = jnp.zeros_like(cacc_ref)
    m_ref[...] = jnp.full(m_ref.shape, M_INIT, F32)

    def cmp_step(c, count):
        rid = c * CMP_CHUNK + lax.broadcasted_iota(jnp.int32, (count * CMP_CHUNK, LANES), 0)
        s = jnp.where(rid < visible, s_ref[chunk_rows(c, count), :], NEG_BIG)
        m_old = m_ref[0:1, :]
        m_new = jnp.maximum(m_old, jnp.max(s, axis=0, keepdims=True))
        alpha = jnp.exp2(m_old - m_new)
        p = jnp.exp2(s - m_new).astype(BF16)
        left = jnp.concatenate([cmpL_ref[0, 0, c + j] for j in range(count)], axis=1)
        cacc_ref[...] = alpha * cacc_ref[...] + jnp.dot(left, p, preferred_element_type=F32)
        m_ref[0:1, :] = m_new

    for_chunks(cmp_step)
    sees_block = m_ref[0:1, :] > 0.5 * NEG_BIG
    inv_c = jnp.where(sees_block, 1.0 / jnp.maximum(cacc_ref[NSA_D:NSA_D + 1, :], tiny), 0.0)
    out_ref[...] = (gate_row(0) * inv_c) * cacc_ref[0:NSA_D, :]
    weights = cacc_ref[V_ROWS:V_ROWS + ns, :] * inv_c
    imp = weights[:, 0:Q_TILE]
    for r in range(1, NSA_REP):
        imp = imp + weights[:, r * Q_TILE:(r + 1) * Q_TILE]
    imp_ref[...] = imp

    qaug_ref[0:NSA_D, :] = qT
    qaug_ref[NSA_D:, :] = jnp.zeros((qaug_ref.shape[0] - NSA_D, LANES), BF16)

    def reset():
        acc_ref[...] = jnp.zeros_like(acc_ref)
        m_ref[...] = jnp.full(m_ref.shape, M_INIT, F32)

    def tile_rows(j):
        return slice(j * K_TILE, (j + 1) * K_TILE)

    def scores_to_scratch(k_ref, first_tile, count, slot):
        rows = pl.ds(pl.multiple_of(first_tile * K_TILE, K_TILE), count * K_TILE)
        s_ref[slot * K_TILE:(slot + count) * K_TILE, :] = jnp.dot(
            k_ref[0, 0, rows, :], qaug_ref[...], preferred_element_type=F32)

    def softmax_from_scratch(vT_ref, tiles, slots):
        rows = slice(slots[0] * K_TILE, (slots[-1] + 1) * K_TILE)
        m_old = m_ref[0:1, :]
        m_new = jnp.maximum(m_old, jnp.max(s_ref[rows, :], axis=0, keepdims=True))
        alpha = jnp.exp2(m_old - m_new)
        p = jnp.exp2(s_ref[rows, :] - m_new).astype(BF16)
        vT = jnp.concatenate([vT_ref[0, 0, kt] for kt in tiles], axis=1)
        acc_ref[...] = alpha * acc_ref[...] + jnp.dot(vT, p, preferred_element_type=F32)
        m_ref[0:1, :] = m_new

    def finish(j):
        l = jnp.maximum(acc_ref[NSA_D:NSA_D + 1, :], tiny)
        out_ref[...] += (gate_row(j) * (1.0 / l)) * acc_ref[0:NSA_D, :]

    win_slots = [TILES_PER_GROUP + j for j in range(WIN_TILES)]
    win_tiles = [jnp.maximum(i - (WIN_TILES - 1) + j, 0) for j in range(WIN_TILES)]
    reset()
    for j, (kt, slot) in enumerate(zip(win_tiles, win_slots)):
        scores_to_scratch(kw_ref, kt, 1, slot)
        before_start = jnp.where(i - (WIN_TILES - 1) + j < 0, NEG_BIG, 0.0).astype(F32)
        s_ref[tile_rows(slot), :] = s_ref[tile_rows(slot), :] + (bw_ref[tile_rows(j), :] + before_start)
    softmax_from_scratch(vwT_ref, win_tiles, win_slots)
    finish(2)

    jrow = lax.broadcasted_iota(jnp.int32, (ns, Q_TILE), 0)
    col = lax.broadcasted_iota(jnp.int32, (ns, Q_TILE), 1)
    cur = SEL_PER_QTILE * i + col // SEL_LEN
    valid = jrow <= cur
    forced = (jrow == 0) | (jrow == cur) | (jrow == cur - 1)
    free = float(SEL_TOPK - 3)
    candidate = valid & jnp.logical_not(forced)
    v0 = jnp.where(candidate, imp_ref[...], -jnp.inf)

    def strip_max(carry):
        v, taken, theta, above = carry
        best = jnp.max(v, axis=0, keepdims=True)
        hit = v == best
        now = taken + jnp.sum(jnp.where(hit, 1.0, 0.0), axis=0, keepdims=True)
        crossed = (taken < free) & (now >= free)
        return (jnp.where(hit, -jnp.inf, v), now,
                jnp.where(crossed, best, theta), jnp.where(crossed, taken, above))

    zero_row = jnp.zeros((1, Q_TILE), F32)
    carry = (v0, zero_row, jnp.full((1, Q_TILE), jnp.inf, F32), zero_row)
    for _ in range(SEL_TOPK - 3):
        carry = strip_max(carry)
    _, _, theta, above = carry
    v0 = jnp.where(candidate, imp_ref[...], -jnp.inf)
    tied = v0 == theta
    lower = jnp.where(lax.broadcasted_iota(jnp.int32, (ns, ns), 1) < lax.broadcasted_iota(jnp.int32, (ns, ns), 0),
                      1.0, 0.0).astype(BF16)
    rank = jnp.dot(lower, jnp.where(tied, 1.0, 0.0).astype(BF16), preferred_element_type=F32)
    chosen = forced | (v0 > theta) | (tied & (rank < free - above))
    mask_bias = jnp.where(chosen & valid, 0.0, NEG_BIG).astype(BF16)
    msel_ref[...] = jnp.concatenate([mask_bias] * NSA_REP, axis=1)

    last = i // TILES_PER_GROUP
    j_i = i % TILES_PER_GROUP

    half = TILES_PER_GROUP // 2

    def sel_tiles(grp, h):
        return [grp * TILES_PER_GROUP + h * half + j for j in range(half)]

    def sel_slots(h):
        return [h * half + j for j in range(half)]

    def sel_scores(grp, h):
        slab = pl.ds(pl.multiple_of(grp * SEL_PER_GROUP, SEL_PER_GROUP), SEL_PER_GROUP)
        qaug_ref[NSA_D:NSA_D + SEL_PER_GROUP, :] = msel_ref[slab, :]
        scores_to_scratch(ks_ref, sel_tiles(grp, h)[0], half, h * half)

    def sel_softmax(grp, h):
        softmax_from_scratch(vsT_ref, sel_tiles(grp, h), sel_slots(h))

    def add_bias(slot, count, bias_tile):
        rows = slice(slot * K_TILE, (slot + count) * K_TILE)
        s_ref[rows, :] = s_ref[rows, :] + bs_ref[bias_tile * K_TILE:(bias_tile + count) * K_TILE, :]

    reset()
    sel_scores(0, 0)

    def far(grp):
        sel_scores(grp, 1)
        sel_softmax(grp, 0)
        sel_scores(grp + 1, 0)
        sel_softmax(grp, 1)

    def far_many(trip, carry):
        for u in range(FAR_UNROLL):
            far(FAR_UNROLL * trip + u)
        return carry

    prev_is_near = (last >= 1) & (j_i == 0)
    n_far = jnp.where(prev_is_near, last - 1, last)
    lax.fori_loop(0, n_far // FAR_UNROLL, far_many, 0)

    def far_rest(grp, carry):
        far(grp)
        return carry

    lax.fori_loop((n_far // FAR_UNROLL) * FAR_UNROLL, n_far, far_rest, 0)

    @pl.when(prev_is_near)
    def _():
        sel_scores(last - 1, 1)
        sel_softmax(last - 1, 0)
        add_bias(TILES_PER_GROUP - 1, 1, 0)
        sel_scores(last, 0)
        sel_softmax(last - 1, 1)

    @pl.when(j_i == 0)
    def _():
        add_bias(0, 1, 1)

    @pl.when(j_i == 1)
    def _():
        add_bias(0, 2, 0)

    @pl.when(j_i == 2)
    def _():
        add_bias(1, 1, 0)

    @pl.when(j_i >= half)
    def _():
        sel_scores(last, 1)
        sel_softmax(last, 0)

        @pl.when(j_i == 2)
        def _():
            add_bias(2, 1, 1)

        @pl.when(j_i == 3)
        def _():
            add_bias(2, 2, 0)

        sel_softmax(last, 1)

    @pl.when(j_i < half)
    def _():
        sel_softmax(last, 0)

    finish(1)

    for r in range(NSA_REP):
        oT_ref[0, r * NSA_D:(r + 1) * NSA_D, :] = out_ref[:, r * Q_TILE:(r + 1) * Q_TILE].astype(oT_ref.dtype)


def nsa_attention(qT, gT, cmp_nat, cmp_tr, ks, vsT, kw, vwT, lines, mmapT, *, batch, seq):
    G = NSA_GROUPS
    nc = seq // CMP_STRIDE
    ns = seq // SEL_LEN
    nq = seq // Q_TILE
    nkt = seq // K_TILE
    hd = NSA_HEADS * NSA_D
    nch = nc // CMP_CHUNK
    vcT = cmp_tr[1].reshape(batch, G, NSA_D, nch, CMP_CHUNK).transpose(0, 1, 3, 2, 4)
    ones = jnp.zeros((V_ROWS - NSA_D, CMP_CHUNK), BF16).at[0].set(1.0)
    cmpL = jnp.concatenate([
        vcT,
        jnp.broadcast_to(ones, (batch, G, nch, V_ROWS - NSA_D, CMP_CHUNK)),
        jnp.broadcast_to(mmapT, (batch, G, nch, ns, CMP_CHUNK)),
    ], axis=3)
    once = pl.Buffered(1)
    return pl.pallas_call(
        _nsa_body,
        out_shape=jax.ShapeDtypeStruct((batch, hd, seq), BF16),
        grid=(batch, G, nq),
        in_specs=[
            pl.BlockSpec((1, NSA_REP, NSA_D, Q_TILE), lambda b, g, i: (b, g, 0, i)),
            pl.BlockSpec((1, 1, GATE_ROWS, Q_TILE), lambda b, g, i: (b, g, 0, i)),
            pl.BlockSpec((1, 1, 1, nc, NSA_D), lambda b, g, i: (0, b, g, 0, 0)),
            pl.BlockSpec((1, 1, nch, V_ROWS + ns, CMP_CHUNK), lambda b, g, i: (b, g, 0, 0, 0)),
            pl.BlockSpec((1, 1, seq, 128), lambda b, g, i: (b, g, 0, 0), pipeline_mode=once),
            pl.BlockSpec((1, 1, nkt, V_ROWS, K_TILE), lambda b, g, i: (b, g, 0, 0, 0), pipeline_mode=once),
            pl.BlockSpec((1, 1, seq, 128), lambda b, g, i: (b, g, 0, 0), pipeline_mode=once),
            pl.BlockSpec((1, 1, nkt, V_ROWS, K_TILE), lambda b, g, i: (b, g, 0, 0, 0), pipeline_mode=once),
            pl.BlockSpec((1, 3, NSA_REP, BIAS_LINE), lambda b, g, i: (g, 0, 0, 0)),
        ],
        out_specs=pl.BlockSpec((1, NSA_REP * NSA_D, Q_TILE), lambda b, g, i: (b, g, i)),
        scratch_shapes=[
            pltpu.VMEM((max(nc, (TILES_PER_GROUP + WIN_TILES) * K_TILE), LANES), F32),
            pltpu.VMEM((ns, Q_TILE), F32),
            pltpu.VMEM((ns, LANES), BF16),
            pltpu.VMEM((128, LANES), BF16),
            pltpu.VMEM((V_ROWS, LANES), F32),
            pltpu.VMEM((8, LANES), F32),
            pltpu.VMEM((NSA_D, LANES), F32),
            pltpu.VMEM((2 * CMP_PER_QTILE, LANES), F32),
            pltpu.VMEM((2 * K_TILE, LANES), F32),
            pltpu.VMEM((3 * K_TILE, LANES), F32),
            pltpu.VMEM((V_ROWS + ns, LANES), F32),
        ],
        compiler_params=_cparams(("parallel", "parallel", "arbitrary"), VMEM_LIMIT),
        name="nsa_attention",
    )(qT, gT, cmp_nat, cmpL, ks, vsT, kw, vwT, lines)


def _t5_bucket_table(n_max):
    n = np.arange(n_max)
    max_exact = REL_BUCKETS // 2
    nf = np.maximum(n, 1).astype(np.float32)
    large = max_exact + (np.log(nf / np.float32(max_exact))
                         / np.float32(math.log(REL_MAX_DIST / max_exact))
                         * np.float32(REL_BUCKETS - max_exact)).astype(np.int32)
    large = np.minimum(large, REL_BUCKETS - 1)
    return np.where(n < max_exact, n, large).astype(np.int32)


def _bias_tables(rel_bias):
    n_max = WINDOW + Q_TILE
    buckets = _t5_bucket_table(n_max)
    assert (buckets[Q_TILE - CMP_LEN + 1:] == REL_BUCKETS - 1).all()
    tab = rel_bias.astype(F32)[buckets, :] - rel_bias.astype(F32)[REL_BUCKETS - 1][None, :]
    tab = tab.T * LOG2_E
    def line(rows, stride, offset, ok, fill):
        assert Q_TILE + stride * (rows - 1) <= BIAS_LINE
        k = np.arange(BIAS_LINE)
        dist = np.where(k < Q_TILE, k, k - BIAS_LINE) + offset
        vals = jnp.where(ok(dist)[None], tab[:, np.clip(dist, 0, n_max - 1)], fill)
        return vals.reshape(NSA_GROUPS, NSA_REP, BIAS_LINE)

    return jnp.stack([
        line(2 * CMP_PER_QTILE, CMP_STRIDE, Q_TILE - CMP_LEN + 1, lambda dd: dd >= 0, NEG_BIG),
        line(2 * K_TILE, 1, K_TILE, lambda dd: dd >= 0, NEG_BIG),
        line(3 * K_TILE, 1, 2 * K_TILE, lambda dd: (dd >= 0) & (dd < WINDOW), NEG_BIG),
    ], axis=1)


def _selection_map(seq):
    nc = seq // CMP_STRIDE
    ns = seq // SEL_LEN
    n_cmp = (seq - CMP_LEN) // CMP_STRIDE + 1
    ratio = SEL_LEN // CMP_STRIDE
    lead = CMP_LEN // CMP_STRIDE - 1
    j = np.arange(ns)[:, None]
    n = np.arange(nc)[None, :]
    m = ((n >= ratio * j - lead) & (n < ratio * j + ratio) & (n < n_cmp)).astype(np.float32)
    m = m.reshape(ns, nc // CMP_CHUNK, CMP_CHUNK).transpose(1, 0, 2)
    return jnp.asarray(m, dtype=BF16)


def kernel(x, mix_norm_pre, mix_norm_post, ffn_norm_pre, ffn_norm_post, ffn_w_in, ffn_w_out,
           ret_w_in, ret_w_out, kv_norm, kv_w, cmp_pe_k, cmp_w1_k, cmp_w2_k,
           cmp_pe_v, cmp_w1_v, cmp_w2_v, nsa_w_in, nsa_w_out, rel_bias):
    batch, seq, d = x.shape
    n_ret = ret_w_in.shape[0]
    n_nsa = nsa_w_in.shape[0]
    assert seq % (SEL_PER_GROUP * SEL_LEN) == 0 and seq // SEL_LEN >= SEL_TOPK
    h = x.reshape(batch * seq, d)
    shared = None
    for layer in range(n_ret + n_nsa):
        if layer == n_ret:
            kc_nat, vc_nat, ks, kw, vsT, vwT = kv_project(h, kv_norm, kv_w, batch=batch, seq=seq, tm=1024)
            cmp_nat, cmp_tr = compress_blocks(kc_nat, vc_nat, cmp_pe_k, cmp_w1_k, cmp_w2_k,
                                              cmp_pe_v, cmp_w1_v, cmp_w2_v, batch=batch, seq=seq)
            shared = (cmp_nat, cmp_tr, ks, vsT, kw, vwT, _bias_tables(rel_bias), _selection_map(seq))
        if layer < n_ret:
            w_in = _deinterleave_qk_columns(ret_w_in[layer]).astype(BF16)
            proj = norm_matmul(h, mix_norm_pre[layer], w_in, tm=1024, tn=2048, out_dtype=BF16)
            mixed = retention_core(proj, *_retention_tables(seq), batch=batch, seq=seq)
            w_o, feature_major = ret_w_out[layer], False
        else:
            j = layer - n_ret
            qT, gT = q_project(h, mix_norm_pre[layer], nsa_w_in[j], batch=batch, seq=seq, tm=512)
            mixed = nsa_attention(qT, gT, *shared, batch=batch, seq=seq)
            w_o, feature_major = nsa_w_out[j], True
        h = mix_out_ffn(mixed, w_o.astype(BF16), mix_norm_post[layer], h, ffn_norm_pre[layer],
                        ffn_w_in[layer].astype(BF16), ffn_w_out[layer].astype(BF16), ffn_norm_post[layer],
                        batch=batch, seq=seq, tm=512, chunks=2, feature_major=feature_major)
    return h.reshape(batch, seq, d)
```

```python
import functools
import math

import numpy as np
import jax
import jax.numpy as jnp
from jax import lax
from jax.experimental import pallas as pl
from jax.experimental.pallas import tpu as pltpu

F32 = jnp.float32
BF16 = jnp.bfloat16

D_MODEL = 1024
RMS_EPS = 1e-6

RET_HEADS = 4
RET_QK = 256
RET_V = 512
RET_CHUNK = 128

FFN_HIDDEN = 2816

NSA_HEADS = 16
NSA_GROUPS = 4
NSA_REP = 4
NSA_D = 64
CMP_LEN = 32
CMP_STRIDE = 16
CMP_HIDDEN = 256
SEL_LEN = 64
SEL_TOPK = 16
WINDOW = 512
REL_BUCKETS = 32
REL_MAX_DIST = 128

Q_TILE = 256
K_TILE = 256
SEL_PER_GROUP = 16
V_ROWS = 80
LOG2_E = math.log2(math.e)
NEG_BIG = -(2.0 ** 100)
M_INIT = -(2.0 ** 120)

LANE_TILE = 128
K_LANES = LANE_TILE
VMEM_V7X = 64 * 1024 * 1024
VMEM_LIMIT = VMEM_V7X - 8 * 1024 * 1024

TM_PROJ, TN_PROJ = 1024, 2048
TM_KV = 1024
TM_Q = 512
TM_FFN, FFN_CHUNKS = 512, 2


def _cparams(sem, vmem=None, flags=None):
    return pltpu.CompilerParams(dimension_semantics=sem, vmem_limit_bytes=vmem, flags=flags)


def _rms(x, g):
    return x * lax.rsqrt(jnp.mean(x * x, axis=-1, keepdims=True) + RMS_EPS) * g


def _norm_matmul_body(x_ref, g_ref, w_ref, o_ref, xn_ref):
    @pl.when(pl.program_id(1) == 0)
    def _():
        xn_ref[...] = _rms(x_ref[...], g_ref[...]).astype(BF16)

    o_ref[...] = jnp.dot(xn_ref[...], w_ref[...], preferred_element_type=F32).astype(o_ref.dtype)


def norm_matmul(x, g, w, *, tm, tn, out_dtype=F32):
    t, d = x.shape
    n = w.shape[1]
    return pl.pallas_call(
        _norm_matmul_body,
        out_shape=jax.ShapeDtypeStruct((t, n), out_dtype),
        grid=(t // tm, n // tn),
        in_specs=[
            pl.BlockSpec((tm, d), lambda i, j: (i, 0)),
            pl.BlockSpec((1, d), lambda i, j: (0, 0)),
            pl.BlockSpec((d, tn), lambda i, j: (0, j)),
        ],
        out_specs=pl.BlockSpec((tm, tn), lambda i, j: (i, j)),
        scratch_shapes=[pltpu.VMEM((tm, d), BF16)],
        compiler_params=_cparams(("parallel", "arbitrary"), VMEM_LIMIT),
        name="norm_matmul",
    )(x, g.reshape(1, d), w)


def _mix_out_ffn_body(y_ref, wp_ref, gmix_ref, res_ref, gpre_ref, win_ref, wo_ref, gpost_ref, o_ref,
                      *, chunks, feature_major):
    if feature_major:
        z = lax.dot_general(y_ref[0], wp_ref[...], (((0,), (0,)), ((), ())), preferred_element_type=F32)
    else:
        z = jnp.dot(y_ref[...], wp_ref[...], preferred_element_type=F32)
    x = res_ref[...] + _rms(z, gmix_ref[...])
    xn = _rms(x, gpre_ref[...]).astype(BF16)
    hdim = wo_ref.shape[0]
    th = hdim // chunks
    y = None
    for c in range(chunks):
        gate = jnp.dot(xn, win_ref[:, c * th:(c + 1) * th], preferred_element_type=F32)
        up = jnp.dot(xn, win_ref[:, hdim + c * th:hdim + (c + 1) * th], preferred_element_type=F32)
        act = (gate * jax.nn.sigmoid(gate) * up).astype(BF16)
        part = jnp.dot(act, wo_ref[c * th:(c + 1) * th, :], preferred_element_type=F32)
        y = part if y is None else y + part
    o_ref[...] = x + _rms(y, gpost_ref[...])


def mix_out_ffn(y, w_proj, g_mix, res, g_pre, w_in, w_out, g_post, *, batch, seq, tm, chunks, feature_major):
    k, d = w_proj.shape
    hdim = w_out.shape[0]
    assert (hdim // chunks) % LANE_TILE == 0 and hdim % chunks == 0
    nt = seq // tm
    once = pl.Buffered(1)
    row = lambda b, i: (b * nt + i, 0)
    const = lambda b, i: (0, 0)
    if feature_major:
        y_spec = pl.BlockSpec((1, k, tm), lambda b, i: (b, 0, i))
    else:
        y_spec = pl.BlockSpec((tm, k), row)
    return pl.pallas_call(
        functools.partial(_mix_out_ffn_body, chunks=chunks, feature_major=feature_major),
        out_shape=jax.ShapeDtypeStruct((batch * seq, d), F32),
        grid=(batch, nt),
        in_specs=[
            y_spec,
            pl.BlockSpec((k, d), const, pipeline_mode=once),
            pl.BlockSpec((1, d), const),
            pl.BlockSpec((tm, d), row),
            pl.BlockSpec((1, d), const),
            pl.BlockSpec((d, 2 * hdim), const, pipeline_mode=once),
            pl.BlockSpec((hdim, d), const, pipeline_mode=once),
            pl.BlockSpec((1, d), const),
        ],
        out_specs=pl.BlockSpec((tm, d), row),
        compiler_params=_cparams(("parallel", "parallel"), VMEM_LIMIT),
        name="mix_out_ffn",
    )(y, w_proj, g_mix.reshape(1, d), res, g_pre.reshape(1, d), w_in, w_out, g_post.reshape(1, d))


def _retention_body(q_ref, k_ref, v_ref, g_ref, cos_ref, sin_ref, dmat_ref, qdec_ref, kdec_ref,
                    cdec_ref, o_ref, state_ref):
    @pl.when(pl.program_id(1) == 0)
    def _():
        state_ref[...] = jnp.zeros_like(state_ref)

    cos = cos_ref[...]
    sin = sin_ref[...]
    half = RET_QK // 2

    def rotate(x_ref, h):
        x1 = x_ref[:, h * RET_QK:h * RET_QK + half].astype(F32)
        x2 = x_ref[:, h * RET_QK + half:(h + 1) * RET_QK].astype(F32)
        return jnp.concatenate([x1 * cos - x2 * sin, x1 * sin + x2 * cos], axis=1)

    for h in range(RET_HEADS):
        qr = rotate(q_ref, h)
        kr = rotate(k_ref, h) * (RET_QK ** -0.5)
        v = v_ref[:, h * RET_V:(h + 1) * RET_V].astype(BF16)
        scores = lax.dot_general(qr.astype(BF16), kr.astype(BF16), (((1,), (1,)), ((), ())),
                                 preferred_element_type=F32) * dmat_ref[h]
        state = state_ref[h]
        o = (jnp.dot(scores.astype(BF16), v, preferred_element_type=F32)
             + jnp.dot((qr * qdec_ref[h]).astype(BF16), state.astype(BF16),
                       preferred_element_type=F32))
        kd = (kr * kdec_ref[h]).astype(BF16)
        state_ref[h] = state * cdec_ref[h, 0:1, :] + lax.dot_general(
            kd, v, (((0,), (0,)), ((), ())), preferred_element_type=F32)
        o = o * lax.rsqrt(jnp.mean(o * o, axis=-1, keepdims=True) + RMS_EPS)
        gate = g_ref[:, h * RET_V:(h + 1) * RET_V].astype(F32)
        o_ref[:, h * RET_V:(h + 1) * RET_V] = (o * (gate * jax.nn.sigmoid(gate))).astype(o_ref.dtype)


def retention_core(proj, cos, sin, dmat, qdec, kdec, cdec, *, batch, seq):
    c = RET_CHUNK
    nc = seq // c
    hq = RET_HEADS * RET_QK
    hv = RET_HEADS * RET_V
    return pl.pallas_call(
        _retention_body,
        out_shape=jax.ShapeDtypeStruct((batch * seq, hv), BF16),
        grid=(batch, nc),
        in_specs=[
            pl.BlockSpec((c, hq), lambda b, t: (b * nc + t, 0)),
            pl.BlockSpec((c, hq), lambda b, t: (b * nc + t, 1)),
            pl.BlockSpec((c, hv), lambda b, t: (b * nc + t, 1)),
            pl.BlockSpec((c, hv), lambda b, t: (b * nc + t, 2)),
            pl.BlockSpec((c, RET_QK // 2), lambda b, t: (t, 0)),
            pl.BlockSpec((c, RET_QK // 2), lambda b, t: (t, 0)),
            pl.BlockSpec((RET_HEADS, c, c), lambda b, t: (0, 0, 0)),
            pl.BlockSpec((RET_HEADS, c, RET_QK), lambda b, t: (0, 0, 0)),
            pl.BlockSpec((RET_HEADS, c, RET_QK), lambda b, t: (0, 0, 0)),
            pl.BlockSpec((RET_HEADS, 8, RET_V), lambda b, t: (0, 0, 0)),
        ],
        out_specs=pl.BlockSpec((c, hv), lambda b, t: (b * nc + t, 0)),
        scratch_shapes=[pltpu.VMEM((RET_HEADS, RET_QK, RET_V), F32)],
        compiler_params=_cparams(("parallel", "arbitrary"), VMEM_LIMIT),
        name="retention_core",
    )(proj, proj, proj, proj, cos, sin, dmat, qdec, kdec, cdec)


def _retention_tables(seq):
    h, dk, c = RET_HEADS, RET_QK, RET_CHUNK
    pos = jnp.arange(seq, dtype=F32)
    theta = 1.0 / (10000.0 ** jnp.linspace(0.0, 1.0, dk // 2, dtype=F32))
    ang = pos[:, None] * theta[None, :]
    log_gamma = jnp.log(1.0 - 2.0 ** (-5.0 - jnp.arange(h, dtype=F32)))
    idx = jnp.arange(c, dtype=F32)
    rel = idx[:, None] - idx[None, :]
    dmat = jnp.where(rel >= 0, jnp.exp(jnp.maximum(rel, 0.0) * log_gamma[:, None, None]), 0.0)
    qdec = jnp.exp((idx + 1.0)[None, :] * log_gamma[:, None])
    kdec = jnp.exp((c - 1.0 - idx)[None, :] * log_gamma[:, None])
    cdec = jnp.exp(c * log_gamma)
    qdec = jnp.broadcast_to(qdec[:, :, None], (h, c, dk))
    kdec = jnp.broadcast_to(kdec[:, :, None], (h, c, dk))
    cdec = jnp.broadcast_to(cdec[:, None, None], (h, 8, RET_V))
    return jnp.cos(ang), jnp.sin(ang), dmat, qdec, kdec, cdec


def _deinterleave_qk_columns(w):
    nqk = 2 * RET_HEADS * RET_QK
    perm = []
    for h in range(2 * RET_HEADS):
        base = h * RET_QK
        perm += [base + 2 * i for i in range(RET_QK // 2)]
        perm += [base + 2 * i + 1 for i in range(RET_QK // 2)]
    perm = np.asarray(perm + list(range(nqk, w.shape[1])), dtype=np.int32)
    return w[:, perm]


def _kv_body(x_ref, g_ref, wk_ref, wvT_ref, kc_ref, vc_ref, ks_ref, kw_ref, vsT_ref, vwT_ref):
    xn = _rms(x_ref[...], g_ref[...]).astype(BF16)
    tm = xn.shape[0]
    kall = jnp.dot(xn, wk_ref[...], preferred_element_type=F32)
    gd = NSA_GROUPS * NSA_D
    kc_ref[...] = kall[:, 0:gd]
    vc_ref[...] = kall[:, gd:2 * gd]
    row = pl.program_id(1) * tm + lax.broadcasted_iota(jnp.int32, (tm, K_LANES), 0)
    lane = lax.broadcasted_iota(jnp.int32, (tm, K_LANES), 1)
    blk = (row // SEL_LEN) % SEL_PER_GROUP
    onehot = jnp.where(lane - NSA_D == blk, 1.0, 0.0).astype(F32)
    for g in range(NSA_GROUPS):
        ks = kall[:, 2 * gd + K_LANES * g:2 * gd + K_LANES * (g + 1)]
        ks_ref[0, g] = (ks + onehot).astype(BF16)
        kw_base = 2 * gd + NSA_GROUPS * K_LANES
        kw = kall[:, kw_base + K_LANES * g:kw_base + K_LANES * (g + 1)]
        kw_ref[0, g] = kw.astype(BF16)
    vT = lax.dot_general(wvT_ref[...], xn, (((1,), (1,)), ((), ())),
                         preferred_element_type=F32)
    extra = jnp.where(lax.broadcasted_iota(jnp.int32, (V_ROWS - NSA_D, K_TILE), 0) == 0, 1.0, 0.0)
    extra = extra.astype(BF16)
    for g in range(NSA_GROUPS):
        for c in range(tm // K_TILE):
            cols = slice(c * K_TILE, (c + 1) * K_TILE)
            vsT_ref[0, g, c, 0:NSA_D, :] = vT[NSA_D * g:NSA_D * (g + 1), cols].astype(BF16)
            vsT_ref[0, g, c, NSA_D:V_ROWS, :] = extra
            vwT_ref[0, g, c, 0:NSA_D, :] = vT[gd + NSA_D * g:gd + NSA_D * (g + 1), cols].astype(BF16)
            vwT_ref[0, g, c, NSA_D:V_ROWS, :] = extra


def kv_project(x, g, kv_w, *, batch, seq, tm):
    d = x.shape[1]
    gd = NSA_GROUPS * NSA_D
    k_c, v_c, k_s, v_s, k_w, v_w = [kv_w[:, i * gd:(i + 1) * gd] for i in range(6)]

    def pad_groups(w):
        w = w.reshape(d, NSA_GROUPS, NSA_D)
        return jnp.pad(w, ((0, 0), (0, 0), (0, K_LANES - NSA_D))).reshape(d, NSA_GROUPS * K_LANES)

    wk = jnp.concatenate([k_c, v_c, pad_groups(k_s), pad_groups(k_w)], axis=1).astype(BF16)
    wvT = jnp.concatenate([v_s, v_w], axis=1).T.astype(BF16)
    nt = seq // tm
    G = NSA_GROUPS
    return pl.pallas_call(
        _kv_body,
        out_shape=(
            jax.ShapeDtypeStruct((batch * seq, gd), F32),
            jax.ShapeDtypeStruct((batch * seq, gd), F32),
            jax.ShapeDtypeStruct((batch, G, seq, K_LANES), BF16),
            jax.ShapeDtypeStruct((batch, G, seq, K_LANES), BF16),
            jax.ShapeDtypeStruct((batch, G, seq // K_TILE, V_ROWS, K_TILE), BF16),
            jax.ShapeDtypeStruct((batch, G, seq // K_TILE, V_ROWS, K_TILE), BF16),
        ),
        grid=(batch, nt),
        in_specs=[
            pl.BlockSpec((tm, d), lambda b, i: (b * nt + i, 0)),
            pl.BlockSpec((1, d), lambda b, i: (0, 0)),
            pl.BlockSpec(wk.shape, lambda b, i: (0, 0)),
            pl.BlockSpec(wvT.shape, lambda b, i: (0, 0)),
        ],
        out_specs=(
            pl.BlockSpec((tm, gd), lambda b, i: (b * nt + i, 0)),
            pl.BlockSpec((tm, gd), lambda b, i: (b * nt + i, 0)),
            pl.BlockSpec((1, G, tm, K_LANES), lambda b, i: (b, 0, i, 0)),
            pl.BlockSpec((1, G, tm, K_LANES), lambda b, i: (b, 0, i, 0)),
            pl.BlockSpec((1, G, tm // K_TILE, V_ROWS, K_TILE), lambda b, i: (b, 0, i, 0, 0)),
            pl.BlockSpec((1, G, tm // K_TILE, V_ROWS, K_TILE), lambda b, i: (b, 0, i, 0, 0)),
        ),
        compiler_params=_cparams(("parallel", "parallel"), VMEM_LIMIT),
        name="kv_project",
    )(x, g.reshape(1, d), wk, wvT)


def _compress_body(c_ref, pe_ref, w1_ref, w2_ref, w2T_ref, nat_ref, tr_ref, sh_ref):
    half = CMP_STRIDE * NSA_D
    nc = c_ref.shape[3]
    c = c_ref[0, 0, 0].astype(BF16)
    w1 = w1_ref[0]
    first = jnp.dot(c, w1[0:half], preferred_element_type=F32)
    second = jnp.dot(c, w1[half:2 * half], preferred_element_type=F32)
    pe_term = jnp.dot(pe_ref[0].astype(BF16), w1, preferred_element_type=F32)
    sh_ref[0:nc, :] = second
    sh_ref[nc:nc + 8, :] = jnp.zeros((8, CMP_HIDDEN), F32)
    pre = first + sh_ref[1:nc + 1, :] + pe_term[0:1, :]
    hid = (pre * jax.nn.sigmoid(pre)).astype(BF16)
    nat_ref[0, 0, 0] = jnp.dot(hid, w2_ref[0], preferred_element_type=F32).astype(BF16)
    tr_ref[0, 0, 0] = lax.dot_general(w2T_ref[0], hid, (((1,), (1,)), ((), ())),
                                      preferred_element_type=F32).astype(BF16)


def compress_blocks(kc_nat, vc_nat, pe_k, w1_k, w2_k, pe_v, w1_v, w2_v, *, batch, seq):
    G, d = NSA_GROUPS, NSA_D
    nc = seq // CMP_STRIDE

    def to_rows(t):
        t = t.reshape(batch, nc, CMP_STRIDE, G, d).transpose(0, 3, 1, 2, 4)
        return t.reshape(batch, G, nc, CMP_STRIDE * d)

    c_all = jnp.stack([to_rows(kc_nat), to_rows(vc_nat)])
    pe = jnp.stack([pe_k.reshape(1, -1), pe_v.reshape(1, -1)])
    pe = jnp.broadcast_to(pe, (2, 8, CMP_LEN * d))
    w1 = jnp.stack([w1_k, w1_v]).astype(BF16)
    w2 = jnp.stack([w2_k, w2_v]).astype(BF16)
    w2T = jnp.stack([w2_k.T, w2_v.T]).astype(BF16)
    return pl.pallas_call(
        _compress_body,
        out_shape=(
            jax.ShapeDtypeStruct((2, batch, G, nc, d), BF16),
            jax.ShapeDtypeStruct((2, batch, G, d, nc), BF16),
        ),
        grid=(2, batch, G),
        in_specs=[
            pl.BlockSpec((1, 1, 1, nc, CMP_STRIDE * d), lambda w, b, g: (w, b, g, 0, 0)),
            pl.BlockSpec((1, 8, CMP_LEN * d), lambda w, b, g: (w, 0, 0)),
            pl.BlockSpec((1, CMP_LEN * d, CMP_HIDDEN), lambda w, b, g: (w, 0, 0)),
            pl.BlockSpec((1, CMP_HIDDEN, d), lambda w, b, g: (w, 0, 0)),
            pl.BlockSpec((1, d, CMP_HIDDEN), lambda w, b, g: (w, 0, 0)),
        ],
        out_specs=(
            pl.BlockSpec((1, 1, 1, nc, d), lambda w, b, g: (w, b, g, 0, 0)),
            pl.BlockSpec((1, 1, 1, d, nc), lambda w, b, g: (w, b, g, 0, 0)),
        ),
        scratch_shapes=[pltpu.VMEM((nc + 8, CMP_HIDDEN), F32)],
        compiler_params=_cparams(("parallel", "parallel", "parallel"), VMEM_LIMIT),
        name="compress_blocks",
    )(c_all, pe, w1, w2, w2T)


GATE_ROWS = 16


def _qproj_body(x_ref, g_ref, wT_ref, qT_ref, gT_ref):
    xn = _rms(x_ref[...], g_ref[...]).astype(BF16)
    tm = xn.shape[0]
    pT = lax.dot_general(wT_ref[...], xn, (((1,), (1,)), ((), ())),
                         preferred_element_type=F32)
    hd = NSA_HEADS * NSA_D
    q = pT[0:hd] * (NSA_D ** -0.5 * LOG2_E)
    qT_ref[0] = q.reshape(NSA_HEADS, NSA_D, tm).astype(BF16)
    gates = jax.nn.sigmoid(pT[hd:hd + NSA_GROUPS * GATE_ROWS])
    gT_ref[0] = gates.reshape(NSA_GROUPS, GATE_ROWS, tm)


def q_project(x, g, w_in, *, batch, seq, tm):
    d = x.shape[1]
    hd = NSA_HEADS * NSA_D
    per_group = NSA_REP * 3
    wg = w_in[:, hd:].reshape(d, NSA_GROUPS, per_group)
    wg = jnp.pad(wg, ((0, 0), (0, 0), (0, GATE_ROWS - per_group))).reshape(d, NSA_GROUPS * GATE_ROWS)
    wT = jnp.concatenate([w_in[:, :hd], wg], axis=1).T.astype(BF16)
    nt = seq // tm
    return pl.pallas_call(
        _qproj_body,
        out_shape=(
            jax.ShapeDtypeStruct((batch, NSA_HEADS, NSA_D, seq), BF16),
            jax.ShapeDtypeStruct((batch, NSA_GROUPS, GATE_ROWS, seq), F32),
        ),
        grid=(batch, nt),
        in_specs=[
            pl.BlockSpec((tm, d), lambda b, i: (b * nt + i, 0)),
            pl.BlockSpec((1, d), lambda b, i: (0, 0)),
            pl.BlockSpec(wT.shape, lambda b, i: (0, 0)),
        ],
        out_specs=(
            pl.BlockSpec((1, NSA_HEADS, NSA_D, tm), lambda b, i: (b, 0, 0, i)),
            pl.BlockSpec((1, NSA_GROUPS, GATE_ROWS, tm), lambda b, i: (b, 0, 0, i)),
        ),
        compiler_params=_cparams(("parallel", "parallel"), VMEM_LIMIT),
        name="q_project",
    )(x, g.reshape(1, d), wT)


CMP_CHUNK = 256
CMP_PER_QTILE = Q_TILE // CMP_STRIDE
SEL_PER_QTILE = Q_TILE // SEL_LEN
LANES = NSA_REP * Q_TILE
TILES_PER_GROUP = SEL_PER_GROUP * SEL_LEN // K_TILE
BIAS_LINE = 1024
WIN_TILES = (WINDOW + Q_TILE) // K_TILE
FAR_UNROLL = 2


def _nsa_body(qT_ref, gT_ref, kc_ref, cmpL_ref, ks_ref, vsT_ref, kw_ref, vwT_ref,
              lines_ref, oT_ref,
              s_ref, imp_ref, msel_ref, qaug_ref, acc_ref, m_ref, out_ref, bc_ref, bs_ref, bw_ref, cacc_ref):
    i = pl.program_id(2)
    ns = imp_ref.shape[0]
    qT = jnp.concatenate([qT_ref[0, r] for r in range(NSA_REP)], axis=1)
    tiny = jnp.finfo(F32).tiny

    @pl.when(i == 0)
    def _():
        def expand(kind, r, rows, stride):
            line = lines_ref[0, kind, r:r + 1, :]
            shifted = pltpu.roll(jnp.broadcast_to(line, (rows, line.shape[1])), 0, 1,
                                 stride=stride, stride_axis=0)
            return shifted[:, 0:Q_TILE]

        for r in range(NSA_REP):
            cols = slice(r * Q_TILE, (r + 1) * Q_TILE)
            bc_ref[:, cols] = expand(0, r, bc_ref.shape[0], CMP_STRIDE)
            bs_ref[:, cols] = expand(1, r, bs_ref.shape[0], 1)
            bw_ref[:, cols] = expand(2, r, bw_ref.shape[0], 1)

    def gate_row(j):
        return jnp.concatenate([gT_ref[0, 0, 3 * r + j:3 * r + j + 1, :] for r in range(NSA_REP)], axis=1)

    nchunks = i // (CMP_CHUNK // CMP_PER_QTILE) + 1
    visible = CMP_PER_QTILE * (i + 1)

    def chunk_rows(c, count=1):
        return pl.ds(pl.multiple_of(c * CMP_CHUNK, CMP_CHUNK), count * CMP_CHUNK)

    def for_chunks(body):
        def pair(pr, carry):
            body(2 * pr, 2)
            return carry

        lax.fori_loop(0, nchunks // 2, pair, 0)

        @pl.when(nchunks % 2 == 1)
        def _():
            body(nchunks - 1, 1)

    def cmp_scores(c, count):
        s_ref[chunk_rows(c, count), :] = jnp.dot(kc_ref[0, 0, 0, chunk_rows(c, count), :], qT,
                                                 preferred_element_type=F32)

    for_chunks(cmp_scores)

    @pl.when(i == 0)
    def _():
        s_ref[0:CMP_PER_QTILE, :] = s_ref[0:CMP_PER_QTILE, :] + bc_ref[CMP_PER_QTILE:2 * CMP_PER_QTILE, :]

    @pl.when(i > 0)
    def _():
        rows = pl.ds(pl.multiple_of(CMP_PER_QTILE * (i - 1), CMP_PER_QTILE), 2 * CMP_PER_QTILE)
        s_ref[rows, :] = s_ref[rows, :] + bc_ref[...]

    cacc_ref[...] = jnp.zeros_like(cacc_ref)
    m_ref[...] = jnp.full(m_ref.shape, M_INIT, F32)

    def cmp_step(c, count):
        rid = c * CMP_CHUNK + lax.broadcasted_iota(jnp.int32, (count * CMP_CHUNK, LANES), 0)
        s = jnp.where(rid < visible, s_ref[chunk_rows(c, count), :], NEG_BIG)
        m_old = m_ref[0:1, :]
        m_new = jnp.maximum(m_old, jnp.max(s, axis=0, keepdims=True))
        alpha = jnp.exp2(m_old - m_new)
        p = jnp.exp2(s - m_new).astype(BF16)
        left = jnp.concatenate([cmpL_ref[0, 0, c + j] for j in range(count)], axis=1)
        cacc_ref[...] = alpha * cacc_ref[...] + jnp.dot(left, p, preferred_element_type=F32)
        m_ref[0:1, :] = m_new

    for_chunks(cmp_step)
    sees_block = m_ref[0:1, :] > 0.5 * NEG_BIG
    inv_c = jnp.where(sees_block, 1.0 / jnp.maximum(cacc_ref[NSA_D:NSA_D + 1, :], tiny), 0.0)
    out_ref[...] = (gate_row(0) * inv_c) * cacc_ref[0:NSA_D, :]
    weights = cacc_ref[V_ROWS:V_ROWS + ns, :] * inv_c
    imp = weights[:, 0:Q_TILE]
    for r in range(1, NSA_REP):
        imp = imp + weights[:, r * Q_TILE:(r + 1) * Q_TILE]
    imp_ref[...] = imp

    qaug_ref[0:NSA_D, :] = qT
    qaug_ref[NSA_D:, :] = jnp.zeros((qaug_ref.shape[0] - NSA_D, LANES), BF16)

    def reset():
        acc_ref[...] = jnp.zeros_like(acc_ref)
        m_ref[...] = jnp.full(m_ref.shape, M_INIT, F32)

    def tile_rows(j):
        return slice(j * K_TILE, (j + 1) * K_TILE)

    def scores_to_scratch(k_ref, first_tile, count, slot):
        rows = pl.ds(pl.multiple_of(first_tile * K_TILE, K_TILE), count * K_TILE)
        s_ref[slot * K_TILE:(slot + count) * K_TILE, :] = jnp.dot(
            k_ref[0, 0, rows, :], qaug_ref[...], preferred_element_type=F32)

    def softmax_from_scratch(vT_ref, tiles, slots):
        rows = slice(slots[0] * K_TILE, (slots[-1] + 1) * K_TILE)
        m_old = m_ref[0:1, :]
        m_new = jnp.maximum(m_old, jnp.max(s_ref[rows, :], axis=0, keepdims=True))
        alpha = jnp.exp2(m_old - m_new)
        p = jnp.exp2(s_ref[rows, :] - m_new).astype(BF16)
        vT = jnp.concatenate([vT_ref[0, 0, kt] for kt in tiles], axis=1)
        acc_ref[...] = alpha * acc_ref[...] + jnp.dot(vT, p, preferred_element_type=F32)
        m_ref[0:1, :] = m_new

    def finish(j):
        l = jnp.maximum(acc_ref[NSA_D:NSA_D + 1, :], tiny)
        out_ref[...] += (gate_row(j) * (1.0 / l)) * acc_ref[0:NSA_D, :]

    win_slots = [TILES_PER_GROUP + j for j in range(WIN_TILES)]
    win_tiles = [jnp.maximum(i - (WIN_TILES - 1) + j, 0) for j in range(WIN_TILES)]
    reset()
    for j, (kt, slot) in enumerate(zip(win_tiles, win_slots)):
        scores_to_scratch(kw_ref, kt, 1, slot)
        before_start = jnp.where(i - (WIN_TILES - 1) + j < 0, NEG_BIG, 0.0).astype(F32)
        s_ref[tile_rows(slot), :] = s_ref[tile_rows(slot), :] + (bw_ref[tile_rows(j), :] + before_start)
    softmax_from_scratch(vwT_ref, win_tiles, win_slots)
    finish(2)

    jrow = lax.broadcasted_iota(jnp.int32, (ns, Q_TILE), 0)
    col = lax.broadcasted_iota(jnp.int32, (ns, Q_TILE), 1)
    cur = SEL_PER_QTILE * i + col // SEL_LEN
    valid = jrow <= cur
    forced = (jrow == 0) | (jrow == cur) | (jrow == cur - 1)
    free = float(SEL_TOPK - 3)
    candidate = valid & jnp.logical_not(forced)
    v0 = jnp.where(candidate, imp_ref[...], -jnp.inf)

    def strip_max(carry):
        v, taken, theta, above = carry
        best = jnp.max(v, axis=0, keepdims=True)
        hit = v == best
        now = taken + jnp.sum(jnp.where(hit, 1.0, 0.0), axis=0, keepdims=True)
        crossed = (taken < free) & (now >= free)
        return (jnp.where(hit, -jnp.inf, v), now,
                jnp.where(crossed, best, theta), jnp.where(crossed, taken, above))

    zero_row = jnp.zeros((1, Q_TILE), F32)
    carry = (v0, zero_row, jnp.full((1, Q_TILE), jnp.inf, F32), zero_row)
    for _ in range(SEL_TOPK - 3):
        carry = strip_max(carry)
    _, _, theta, above = carry
    v0 = jnp.where(candidate, imp_ref[...], -jnp.inf)
    tied = v0 == theta
    lower = jnp.where(lax.broadcasted_iota(jnp.int32, (ns, ns), 1) < lax.broadcasted_iota(jnp.int32, (ns, ns), 0),
                      1.0, 0.0).astype(BF16)
    rank = jnp.dot(lower, jnp.where(tied, 1.0, 0.0).astype(BF16), preferred_element_type=F32)
    chosen = forced | (v0 > theta) | (tied & (rank < free - above))
    mask_bias = jnp.where(chosen & valid, 0.0, NEG_BIG).astype(BF16)
    msel_ref[...] = jnp.concatenate([mask_bias] * NSA_REP, axis=1)

    last = i // TILES_PER_GROUP
    j_i = i % TILES_PER_GROUP

    half = TILES_PER_GROUP // 2

    def sel_tiles(grp, h):
        return [grp * TILES_PER_GROUP + h * half + j for j in range(half)]

    def sel_slots(h):
        return [h * half + j for j in range(half)]

    def set_slab(grp):
        slab = pl.ds(pl.multiple_of(grp * SEL_PER_GROUP, SEL_PER_GROUP), SEL_PER_GROUP)
        qaug_ref[NSA_D:NSA_D + SEL_PER_GROUP, :] = msel_ref[slab, :]

    def sel_scores(grp, h):
        set_slab(grp)
        scores_to_scratch(ks_ref, sel_tiles(grp, h)[0], half, h * half)

    def sel_softmax(grp, h):
        softmax_from_scratch(vsT_ref, sel_tiles(grp, h), sel_slots(h))

    def add_bias(slot, count, bias_tile):
        rows = slice(slot * K_TILE, (slot + count) * K_TILE)
        s_ref[rows, :] = s_ref[rows, :] + bs_ref[bias_tile * K_TILE:(bias_tile + count) * K_TILE, :]

    reset()
    sel_scores(0, 0)

    def far(grp):
        sel_scores(grp, 1)
        sel_softmax(grp, 0)
        sel_scores(grp + 1, 0)
        sel_softmax(grp, 1)

    def far_many(trip, carry):
        for u in range(FAR_UNROLL):
            far(FAR_UNROLL * trip + u)
        return carry

    prev_is_near = (last >= 1) & (j_i == 0)
    n_far = jnp.where(prev_is_near, last - 1, last)
    lax.fori_loop(0, n_far // FAR_UNROLL, far_many, 0)

    def far_rest(grp, carry):
        far(grp)
        return carry

    lax.fori_loop((n_far // FAR_UNROLL) * FAR_UNROLL, n_far, far_rest, 0)

    @pl.when(prev_is_near)
    def _():
        sel_scores(last - 1, 1)
        sel_softmax(last - 1, 0)
        add_bias(TILES_PER_GROUP - 1, 1, 0)
        sel_scores(last, 0)
        sel_softmax(last - 1, 1)

    @pl.when(j_i == 0)
    def _():
        add_bias(0, 1, 1)

    @pl.when(j_i == 1)
    def _():
        add_bias(0, 2, 0)

    @pl.when(j_i == 2)
    def _():
        add_bias(1, 1, 0)

    @pl.when(j_i >= half)
    def _():
        sel_scores(last, 1)
        sel_softmax(last, 0)

        @pl.when(j_i == 2)
        def _():
            add_bias(2, 1, 1)

        @pl.when(j_i == 3)
        def _():
            add_bias(2, 2, 0)

        sel_softmax(last, 1)

    @pl.when(j_i < half)
    def _():
        sel_softmax(last, 0)

    finish(1)

    for r in range(NSA_REP):
        oT_ref[0, r * NSA_D:(r + 1) * NSA_D, :] = out_ref[:, r * Q_TILE:(r + 1) * Q_TILE].astype(oT_ref.dtype)


def nsa_attention(qT, gT, cmp_nat, cmp_tr, ks, vsT, kw, vwT, lines, mmapT, *, batch, seq):
    G = NSA_GROUPS
    nc = seq // CMP_STRIDE
    ns = seq // SEL_LEN
    nq = seq // Q_TILE
    nkt = seq // K_TILE
    hd = NSA_HEADS * NSA_D
    nch = nc // CMP_CHUNK
    vcT = cmp_tr[1].reshape(batch, G, NSA_D, nch, CMP_CHUNK).transpose(0, 1, 3, 2, 4)
    ones = jnp.zeros((V_ROWS - NSA_D, CMP_CHUNK), BF16).at[0].set(1.0)
    cmpL = jnp.concatenate([
        vcT,
        jnp.broadcast_to(ones, (batch, G, nch, V_ROWS - NSA_D, CMP_CHUNK)),
        jnp.broadcast_to(mmapT, (batch, G, nch, ns, CMP_CHUNK)),
    ], axis=3)
    once = pl.Buffered(1)
    return pl.pallas_call(
        _nsa_body,
        out_shape=jax.ShapeDtypeStruct((batch, hd, seq), BF16),
        grid=(batch, G, nq),
        in_specs=[
            pl.BlockSpec((1, NSA_REP, NSA_D, Q_TILE), lambda b, g, i: (b, g, 0, i)),
            pl.BlockSpec((1, 1, GATE_ROWS, Q_TILE), lambda b, g, i: (b, g, 0, i)),
            pl.BlockSpec((1, 1, 1, nc, NSA_D), lambda b, g, i: (0, b, g, 0, 0)),
            pl.BlockSpec((1, 1, nch, V_ROWS + ns, CMP_CHUNK), lambda b, g, i: (b, g, 0, 0, 0)),
            pl.BlockSpec((1, 1, seq, K_LANES), lambda b, g, i: (b, g, 0, 0), pipeline_mode=once),
            pl.BlockSpec((1, 1, nkt, V_ROWS, K_TILE), lambda b, g, i: (b, g, 0, 0, 0), pipeline_mode=once),
            pl.BlockSpec((1, 1, seq, K_LANES), lambda b, g, i: (b, g, 0, 0), pipeline_mode=once),
            pl.BlockSpec((1, 1, nkt, V_ROWS, K_TILE), lambda b, g, i: (b, g, 0, 0, 0), pipeline_mode=once),
            pl.BlockSpec((1, 3, NSA_REP, BIAS_LINE), lambda b, g, i: (g, 0, 0, 0)),
        ],
        out_specs=pl.BlockSpec((1, NSA_REP * NSA_D, Q_TILE), lambda b, g, i: (b, g, i)),
        scratch_shapes=[
            pltpu.VMEM((max(nc, (TILES_PER_GROUP + WIN_TILES) * K_TILE), LANES), F32),
            pltpu.VMEM((ns, Q_TILE), F32),
            pltpu.VMEM((ns, LANES), BF16),
            pltpu.VMEM((K_LANES, LANES), BF16),
            pltpu.VMEM((V_ROWS, LANES), F32),
            pltpu.VMEM((8, LANES), F32),
            pltpu.VMEM((NSA_D, LANES), F32),
            pltpu.VMEM((2 * CMP_PER_QTILE, LANES), F32),
            pltpu.VMEM((2 * K_TILE, LANES), F32),
            pltpu.VMEM((3 * K_TILE, LANES), F32),
            pltpu.VMEM((V_ROWS + ns, LANES), F32),
        ],
        compiler_params=_cparams(("parallel", "parallel", "arbitrary"), VMEM_LIMIT),
        name="nsa_attention",
    )(qT, gT, cmp_nat, cmpL, ks, vsT, kw, vwT, lines)


def _t5_bucket_table(n_max):
    n = np.arange(n_max)
    max_exact = REL_BUCKETS // 2
    nf = np.maximum(n, 1).astype(np.float32)
    large = max_exact + (np.log(nf / np.float32(max_exact))
                         / np.float32(math.log(REL_MAX_DIST / max_exact))
                         * np.float32(REL_BUCKETS - max_exact)).astype(np.int32)
    large = np.minimum(large, REL_BUCKETS - 1)
    return np.where(n < max_exact, n, large).astype(np.int32)


def _bias_tables(rel_bias):
    n_max = WINDOW + Q_TILE
    buckets = _t5_bucket_table(n_max)
    assert (buckets[Q_TILE - CMP_LEN + 1:] == REL_BUCKETS - 1).all()
    tab = rel_bias.astype(F32)[buckets, :] - rel_bias.astype(F32)[REL_BUCKETS - 1][None, :]
    tab = tab.T * LOG2_E
    def line(rows, stride, offset, ok, fill):
        assert Q_TILE + stride * (rows - 1) <= BIAS_LINE
        k = np.arange(BIAS_LINE)
        dist = np.where(k < Q_TILE, k, k - BIAS_LINE) + offset
        vals = jnp.where(ok(dist)[None], tab[:, np.clip(dist, 0, n_max - 1)], fill)
        return vals.reshape(NSA_GROUPS, NSA_REP, BIAS_LINE)

    return jnp.stack([
        line(2 * CMP_PER_QTILE, CMP_STRIDE, Q_TILE - CMP_LEN + 1, lambda dd: dd >= 0, NEG_BIG),
        line(2 * K_TILE, 1, K_TILE, lambda dd: dd >= 0, NEG_BIG),
        line(3 * K_TILE, 1, 2 * K_TILE, lambda dd: (dd >= 0) & (dd < WINDOW), NEG_BIG),
    ], axis=1)


def _selection_map(seq):
    nc = seq // CMP_STRIDE
    ns = seq // SEL_LEN
    n_cmp = (seq - CMP_LEN) // CMP_STRIDE + 1
    ratio = SEL_LEN // CMP_STRIDE
    lead = CMP_LEN // CMP_STRIDE - 1
    j = np.arange(ns)[:, None]
    n = np.arange(nc)[None, :]
    m = ((n >= ratio * j - lead) & (n < ratio * j + ratio) & (n < n_cmp)).astype(np.float32)
    m = m.reshape(ns, nc // CMP_CHUNK, CMP_CHUNK).transpose(1, 0, 2)
    return jnp.asarray(m, dtype=BF16)


def kernel(x, mix_norm_pre, mix_norm_post, ffn_norm_pre, ffn_norm_post, ffn_w_in, ffn_w_out,
           ret_w_in, ret_w_out, kv_norm, kv_w, cmp_pe_k, cmp_w1_k, cmp_w2_k,
           cmp_pe_v, cmp_w1_v, cmp_w2_v, nsa_w_in, nsa_w_out, rel_bias):
    batch, seq, d = x.shape
    n_ret = ret_w_in.shape[0]
    n_nsa = nsa_w_in.shape[0]
    assert seq % (SEL_PER_GROUP * SEL_LEN) == 0 and seq // SEL_LEN >= SEL_TOPK
    h = x.reshape(batch * seq, d)
    shared = None
    for layer in range(n_ret + n_nsa):
        if layer == n_ret:
            kc_nat, vc_nat, ks, kw, vsT, vwT = kv_project(h, kv_norm, kv_w, batch=batch, seq=seq, tm=TM_KV)
            cmp_nat, cmp_tr = compress_blocks(kc_nat, vc_nat, cmp_pe_k, cmp_w1_k, cmp_w2_k,
                                              cmp_pe_v, cmp_w1_v, cmp_w2_v, batch=batch, seq=seq)
            shared = (cmp_nat, cmp_tr, ks, vsT, kw, vwT, _bias_tables(rel_bias), _selection_map(seq))
        if layer < n_ret:
            w_in = _deinterleave_qk_columns(ret_w_in[layer]).astype(BF16)
            proj = norm_matmul(h, mix_norm_pre[layer], w_in, tm=TM_PROJ, tn=TN_PROJ, out_dtype=BF16)
            mixed = retention_core(proj, *_retention_tables(seq), batch=batch, seq=seq)
            w_o, feature_major = ret_w_out[layer], False
        else:
            j = layer - n_ret
            qT, gT = q_project(h, mix_norm_pre[layer], nsa_w_in[j], batch=batch, seq=seq, tm=TM_Q)
            mixed = nsa_attention(qT, gT, *shared, batch=batch, seq=seq)
            w_o, feature_major = nsa_w_out[j], True
        h = mix_out_ffn(mixed, w_o.astype(BF16), mix_norm_post[layer], h, ffn_norm_pre[layer],
                        ffn_w_in[layer].astype(BF16), ffn_w_out[layer].astype(BF16), ffn_norm_post[layer],
                        batch=batch, seq=seq, tm=TM_FFN, chunks=FFN_CHUNKS, feature_major=feature_major)
    return h.reshape(batch, seq, d)
```

```python
import functools
import math

import numpy as np
import jax
import jax.numpy as jnp
from jax import lax
from jax.experimental import pallas as pl
from jax.experimental.pallas import tpu as pltpu

F32 = jnp.float32
BF16 = jnp.bfloat16

D_MODEL = 1024
RMS_EPS = 1e-6

RET_HEADS = 4
RET_QK = 256
RET_V = 512
RET_CHUNK = 128

FFN_HIDDEN = 2816

NSA_HEADS = 16
NSA_GROUPS = 4
NSA_REP = 4
NSA_D = 64
CMP_LEN = 32
CMP_STRIDE = 16
CMP_HIDDEN = 256
SEL_LEN = 64
SEL_TOPK = 16
WINDOW = 512
REL_BUCKETS = 32
REL_MAX_DIST = 128

Q_TILE = 256
K_TILE = 256
SEL_PER_GROUP = 16
V_ROWS = 80
LOG2_E = math.log2(math.e)
NEG_BIG = -(2.0 ** 100)
M_INIT = -(2.0 ** 120)

LANE_TILE = 128
K_LANES = LANE_TILE
VMEM_V7X = 64 * 1024 * 1024
VMEM_LIMIT = VMEM_V7X - 8 * 1024 * 1024

TM_PROJ, TN_PROJ = 1024, 2048
TM_KV = 1024
TM_Q = 512
TM_FFN, FFN_CHUNKS = 512, 2


def _cparams(sem, vmem=None, flags=None):
    return pltpu.CompilerParams(dimension_semantics=sem, vmem_limit_bytes=vmem, flags=flags)


def _rms(x, g):
    return x * lax.rsqrt(jnp.mean(x * x, axis=-1, keepdims=True) + RMS_EPS) * g


def _norm_matmul_body(x_ref, g_ref, w_ref, o_ref, xn_ref):
    @pl.when(pl.program_id(1) == 0)
    def _():
        xn_ref[...] = _rms(x_ref[...], g_ref[...]).astype(BF16)

    o_ref[...] = jnp.dot(xn_ref[...], w_ref[...], preferred_element_type=F32).astype(o_ref.dtype)


def norm_matmul(x, g, w, *, tm, tn, out_dtype=F32):
    t, d = x.shape
    n = w.shape[1]
    return pl.pallas_call(
        _norm_matmul_body,
        out_shape=jax.ShapeDtypeStruct((t, n), out_dtype),
        grid=(t // tm, n // tn),
        in_specs=[
            pl.BlockSpec((tm, d), lambda i, j: (i, 0)),
            pl.BlockSpec((1, d), lambda i, j: (0, 0)),
            pl.BlockSpec((d, tn), lambda i, j: (0, j)),
        ],
        out_specs=pl.BlockSpec((tm, tn), lambda i, j: (i, j)),
        scratch_shapes=[pltpu.VMEM((tm, d), BF16)],
        compiler_params=_cparams(("parallel", "arbitrary"), VMEM_LIMIT),
        name="norm_matmul",
    )(x, g.reshape(1, d), w)


def _mix_out_ffn_body(y_ref, wp_ref, gmix_ref, res_ref, gpre_ref, win_ref, wo_ref, gpost_ref, o_ref,
                      *, chunks, feature_major):
    if feature_major:
        z = lax.dot_general(y_ref[0], wp_ref[...], (((0,), (0,)), ((), ())), preferred_element_type=F32)
    else:
        z = jnp.dot(y_ref[...], wp_ref[...], preferred_element_type=F32)
    x = res_ref[...] + _rms(z, gmix_ref[...])
    xn = _rms(x, gpre_ref[...]).astype(BF16)
    hdim = wo_ref.shape[0]
    th = hdim // chunks
    y = None
    for c in range(chunks):
        gate = jnp.dot(xn, win_ref[:, c * th:(c + 1) * th], preferred_element_type=F32)
        up = jnp.dot(xn, win_ref[:, hdim + c * th:hdim + (c + 1) * th], preferred_element_type=F32)
        act = (gate * jax.nn.sigmoid(gate) * up).astype(BF16)
        part = jnp.dot(act, wo_ref[c * th:(c + 1) * th, :], preferred_element_type=F32)
        y = part if y is None else y + part
    o_ref[...] = x + _rms(y, gpost_ref[...])


def mix_out_ffn(y, w_proj, g_mix, res, g_pre, w_in, w_out, g_post, *, batch, seq, tm, chunks, feature_major):
    k, d = w_proj.shape
    hdim = w_out.shape[0]
    assert (hdim // chunks) % LANE_TILE == 0 and hdim % chunks == 0
    nt = seq // tm
    once = pl.Buffered(1)
    row = lambda b, i: (b * nt + i, 0)
    const = lambda b, i: (0, 0)
    if feature_major:
        y_spec = pl.BlockSpec((1, k, tm), lambda b, i: (b, 0, i))
    else:
        y_spec = pl.BlockSpec((tm, k), row)
    return pl.pallas_call(
        functools.partial(_mix_out_ffn_body, chunks=chunks, feature_major=feature_major),
        out_shape=jax.ShapeDtypeStruct((batch * seq, d), F32),
        grid=(batch, nt),
        in_specs=[
            y_spec,
            pl.BlockSpec((k, d), const, pipeline_mode=once),
            pl.BlockSpec((1, d), const),
            pl.BlockSpec((tm, d), row),
            pl.BlockSpec((1, d), const),
            pl.BlockSpec((d, 2 * hdim), const, pipeline_mode=once),
            pl.BlockSpec((hdim, d), const, pipeline_mode=once),
            pl.BlockSpec((1, d), const),
        ],
        out_specs=pl.BlockSpec((tm, d), row),
        compiler_params=_cparams(("parallel", "parallel"), VMEM_LIMIT),
        name="mix_out_ffn",
    )(y, w_proj, g_mix.reshape(1, d), res, g_pre.reshape(1, d), w_in, w_out, g_post.reshape(1, d))


def _retention_body(q_ref, k_ref, v_ref, g_ref, cos_ref, sin_ref, dmat_ref, qdec_ref, kdec_ref,
                    cdec_ref, o_ref, state_ref):
    @pl.when(pl.program_id(1) == 0)
    def _():
        state_ref[...] = jnp.zeros_like(state_ref)

    cos = cos_ref[...]
    sin = sin_ref[...]
    half = RET_QK // 2

    def rotate(x_ref, h):
        x1 = x_ref[:, h * RET_QK:h * RET_QK + half].astype(F32)
        x2 = x_ref[:, h * RET_QK + half:(h + 1) * RET_QK].astype(F32)
        return jnp.concatenate([x1 * cos - x2 * sin, x1 * sin + x2 * cos], axis=1)

    for h in range(RET_HEADS):
        qr = rotate(q_ref, h)
        kr = rotate(k_ref, h) * (RET_QK ** -0.5)
        v = v_ref[:, h * RET_V:(h + 1) * RET_V].astype(BF16)
        scores = lax.dot_general(qr.astype(BF16), kr.astype(BF16), (((1,), (1,)), ((), ())),
                                 preferred_element_type=F32) * dmat_ref[h]
        state = state_ref[h]
        o = (jnp.dot(scores.astype(BF16), v, preferred_element_type=F32)
             + jnp.dot((qr * qdec_ref[h]).astype(BF16), state.astype(BF16),
                       preferred_element_type=F32))
        kd = (kr * kdec_ref[h]).astype(BF16)
        state_ref[h] = state * cdec_ref[h, 0:1, :] + lax.dot_general(
            kd, v, (((0,), (0,)), ((), ())), preferred_element_type=F32)
        o = o * lax.rsqrt(jnp.mean(o * o, axis=-1, keepdims=True) + RMS_EPS)
        gate = g_ref[:, h * RET_V:(h + 1) * RET_V].astype(F32)
        o_ref[:, h * RET_V:(h + 1) * RET_V] = (o * (gate * jax.nn.sigmoid(gate))).astype(o_ref.dtype)


def retention_core(proj, cos, sin, dmat, qdec, kdec, cdec, *, batch, seq):
    c = RET_CHUNK
    nc = seq // c
    hq = RET_HEADS * RET_QK
    hv = RET_HEADS * RET_V
    return pl.pallas_call(
        _retention_body,
        out_shape=jax.ShapeDtypeStruct((batch * seq, hv), BF16),
        grid=(batch, nc),
        in_specs=[
            pl.BlockSpec((c, hq), lambda b, t: (b * nc + t, 0)),
            pl.BlockSpec((c, hq), lambda b, t: (b * nc + t, 1)),
            pl.BlockSpec((c, hv), lambda b, t: (b * nc + t, 1)),
            pl.BlockSpec((c, hv), lambda b, t: (b * nc + t, 2)),
            pl.BlockSpec((c, RET_QK // 2), lambda b, t: (t, 0)),
            pl.BlockSpec((c, RET_QK // 2), lambda b, t: (t, 0)),
            pl.BlockSpec((RET_HEADS, c, c), lambda b, t: (0, 0, 0)),
            pl.BlockSpec((RET_HEADS, c, RET_QK), lambda b, t: (0, 0, 0)),
            pl.BlockSpec((RET_HEADS, c, RET_QK), lambda b, t: (0, 0, 0)),
            pl.BlockSpec((RET_HEADS, 8, RET_V), lambda b, t: (0, 0, 0)),
        ],
        out_specs=pl.BlockSpec((c, hv), lambda b, t: (b * nc + t, 0)),
        scratch_shapes=[pltpu.VMEM((RET_HEADS, RET_QK, RET_V), F32)],
        compiler_params=_cparams(("parallel", "arbitrary"), VMEM_LIMIT),
        name="retention_core",
    )(proj, proj, proj, proj, cos, sin, dmat, qdec, kdec, cdec)


def _retention_tables(seq):
    h, dk, c = RET_HEADS, RET_QK, RET_CHUNK
    pos = jnp.arange(seq, dtype=F32)
    theta = 1.0 / (10000.0 ** jnp.linspace(0.0, 1.0, dk // 2, dtype=F32))
    ang = pos[:, None] * theta[None, :]
    log_gamma = jnp.log(1.0 - 2.0 ** (-5.0 - jnp.arange(h, dtype=F32)))
    idx = jnp.arange(c, dtype=F32)
    rel = idx[:, None] - idx[None, :]
    dmat = jnp.where(rel >= 0, jnp.exp(jnp.maximum(rel, 0.0) * log_gamma[:, None, None]), 0.0)
    qdec = jnp.exp((idx + 1.0)[None, :] * log_gamma[:, None])
    kdec = jnp.exp((c - 1.0 - idx)[None, :] * log_gamma[:, None])
    cdec = jnp.exp(c * log_gamma)
    qdec = jnp.broadcast_to(qdec[:, :, None], (h, c, dk))
    kdec = jnp.broadcast_to(kdec[:, :, None], (h, c, dk))
    cdec = jnp.broadcast_to(cdec[:, None, None], (h, 8, RET_V))
    return jnp.cos(ang), jnp.sin(ang), dmat, qdec, kdec, cdec


def _deinterleave_qk_columns(w):
    nqk = 2 * RET_HEADS * RET_QK
    perm = []
    for h in range(2 * RET_HEADS):
        base = h * RET_QK
        perm += [base + 2 * i for i in range(RET_QK // 2)]
        perm += [base + 2 * i + 1 for i in range(RET_QK // 2)]
    perm = np.asarray(perm + list(range(nqk, w.shape[1])), dtype=np.int32)
    return w[:, perm]


def _kv_body(x_ref, g_ref, wk_ref, wvT_ref, kc_ref, vc_ref, ks_ref, kw_ref, vsT_ref, vwT_ref):
    xn = _rms(x_ref[...], g_ref[...]).astype(BF16)
    tm = xn.shape[0]
    kall = jnp.dot(xn, wk_ref[...], preferred_element_type=F32)
    gd = NSA_GROUPS * NSA_D
    kc_ref[...] = kall[:, 0:gd]
    vc_ref[...] = kall[:, gd:2 * gd]
    row = pl.program_id(1) * tm + lax.broadcasted_iota(jnp.int32, (tm, K_LANES), 0)
    lane = lax.broadcasted_iota(jnp.int32, (tm, K_LANES), 1)
    blk = (row // SEL_LEN) % SEL_PER_GROUP
    onehot = jnp.where(lane - NSA_D == blk, 1.0, 0.0).astype(F32)
    for g in range(NSA_GROUPS):
        ks = kall[:, 2 * gd + K_LANES * g:2 * gd + K_LANES * (g + 1)]
        ks_ref[0, g] = (ks + onehot).astype(BF16)
        kw_base = 2 * gd + NSA_GROUPS * K_LANES
        kw = kall[:, kw_base + K_LANES * g:kw_base + K_LANES * (g + 1)]
        kw_ref[0, g] = kw.astype(BF16)
    vT = lax.dot_general(wvT_ref[...], xn, (((1,), (1,)), ((), ())),
                         preferred_element_type=F32)
    extra = jnp.where(lax.broadcasted_iota(jnp.int32, (V_ROWS - NSA_D, K_TILE), 0) == 0, 1.0, 0.0)
    extra = extra.astype(BF16)
    for g in range(NSA_GROUPS):
        for c in range(tm // K_TILE):
            cols = slice(c * K_TILE, (c + 1) * K_TILE)
            vsT_ref[0, g, c, 0:NSA_D, :] = vT[NSA_D * g:NSA_D * (g + 1), cols].astype(BF16)
            vsT_ref[0, g, c, NSA_D:V_ROWS, :] = extra
            vwT_ref[0, g, c, 0:NSA_D, :] = vT[gd + NSA_D * g:gd + NSA_D * (g + 1), cols].astype(BF16)
            vwT_ref[0, g, c, NSA_D:V_ROWS, :] = extra


def kv_project(x, g, kv_w, *, batch, seq, tm):
    d = x.shape[1]
    gd = NSA_GROUPS * NSA_D
    k_c, v_c, k_s, v_s, k_w, v_w = [kv_w[:, i * gd:(i + 1) * gd] for i in range(6)]

    def pad_groups(w):
        w = w.reshape(d, NSA_GROUPS, NSA_D)
        return jnp.pad(w, ((0, 0), (0, 0), (0, K_LANES - NSA_D))).reshape(d, NSA_GROUPS * K_LANES)

    wk = jnp.concatenate([k_c, v_c, pad_groups(k_s), pad_groups(k_w)], axis=1).astype(BF16)
    wvT = jnp.concatenate([v_s, v_w], axis=1).T.astype(BF16)
    nt = seq // tm
    G = NSA_GROUPS
    return pl.pallas_call(
        _kv_body,
        out_shape=(
            jax.ShapeDtypeStruct((batch * seq, gd), F32),
            jax.ShapeDtypeStruct((batch * seq, gd), F32),
            jax.ShapeDtypeStruct((batch, G, seq, K_LANES), BF16),
            jax.ShapeDtypeStruct((batch, G, seq, K_LANES), BF16),
            jax.ShapeDtypeStruct((batch, G, seq // K_TILE, V_ROWS, K_TILE), BF16),
            jax.ShapeDtypeStruct((batch, G, seq // K_TILE, V_ROWS, K_TILE), BF16),
        ),
        grid=(batch, nt),
        in_specs=[
            pl.BlockSpec((tm, d), lambda b, i: (b * nt + i, 0)),
            pl.BlockSpec((1, d), lambda b, i: (0, 0)),
            pl.BlockSpec(wk.shape, lambda b, i: (0, 0)),
            pl.BlockSpec(wvT.shape, lambda b, i: (0, 0)),
        ],
        out_specs=(
            pl.BlockSpec((tm, gd), lambda b, i: (b * nt + i, 0)),
            pl.BlockSpec((tm, gd), lambda b, i: (b * nt + i, 0)),
            pl.BlockSpec((1, G, tm, K_LANES), lambda b, i: (b, 0, i, 0)),
            pl.BlockSpec((1, G, tm, K_LANES), lambda b, i: (b, 0, i, 0)),
            pl.BlockSpec((1, G, tm // K_TILE, V_ROWS, K_TILE), lambda b, i: (b, 0, i, 0, 0)),
            pl.BlockSpec((1, G, tm // K_TILE, V_ROWS, K_TILE), lambda b, i: (b, 0, i, 0, 0)),
        ),
        compiler_params=_cparams(("parallel", "parallel"), VMEM_LIMIT),
        name="kv_project",
    )(x, g.reshape(1, d), wk, wvT)


def _compress_body(c_ref, pe_ref, w1_ref, w2_ref, w2T_ref, nat_ref, tr_ref, sh_ref):
    half = CMP_STRIDE * NSA_D
    nc = c_ref.shape[3]
    c = c_ref[0, 0, 0].astype(BF16)
    w1 = w1_ref[0]
    first = jnp.dot(c, w1[0:half], preferred_element_type=F32)
    second = jnp.dot(c, w1[half:2 * half], preferred_element_type=F32)
    pe_term = jnp.dot(pe_ref[0].astype(BF16), w1, preferred_element_type=F32)
    sh_ref[0:nc, :] = second
    sh_ref[nc:nc + 8, :] = jnp.zeros((8, CMP_HIDDEN), F32)
    pre = first + sh_ref[1:nc + 1, :] + pe_term[0:1, :]
    hid = (pre * jax.nn.sigmoid(pre)).astype(BF16)
    nat_ref[0, 0, 0] = jnp.dot(hid, w2_ref[0], preferred_element_type=F32).astype(BF16)
    tr_ref[0, 0, 0] = lax.dot_general(w2T_ref[0], hid, (((1,), (1,)), ((), ())),
                                      preferred_element_type=F32).astype(BF16)


def compress_blocks(kc_nat, vc_nat, pe_k, w1_k, w2_k, pe_v, w1_v, w2_v, *, batch, seq):
    G, d = NSA_GROUPS, NSA_D
    nc = seq // CMP_STRIDE

    def to_rows(t):
        t = t.reshape(batch, nc, CMP_STRIDE, G, d).transpose(0, 3, 1, 2, 4)
        return t.reshape(batch, G, nc, CMP_STRIDE * d)

    c_all = jnp.stack([to_rows(kc_nat), to_rows(vc_nat)])
    pe = jnp.stack([pe_k.reshape(1, -1), pe_v.reshape(1, -1)])
    pe = jnp.broadcast_to(pe, (2, 8, CMP_LEN * d))
    w1 = jnp.stack([w1_k, w1_v]).astype(BF16)
    w2 = jnp.stack([w2_k, w2_v]).astype(BF16)
    w2T = jnp.stack([w2_k.T, w2_v.T]).astype(BF16)
    return pl.pallas_call(
        _compress_body,
        out_shape=(
            jax.ShapeDtypeStruct((2, batch, G, nc, d), BF16),
            jax.ShapeDtypeStruct((2, batch, G, d, nc), BF16),
        ),
        grid=(2, batch, G),
        in_specs=[
            pl.BlockSpec((1, 1, 1, nc, CMP_STRIDE * d), lambda w, b, g: (w, b, g, 0, 0)),
            pl.BlockSpec((1, 8, CMP_LEN * d), lambda w, b, g: (w, 0, 0)),
            pl.BlockSpec((1, CMP_LEN * d, CMP_HIDDEN), lambda w, b, g: (w, 0, 0)),
            pl.BlockSpec((1, CMP_HIDDEN, d), lambda w, b, g: (w, 0, 0)),
            pl.BlockSpec((1, d, CMP_HIDDEN), lambda w, b, g: (w, 0, 0)),
        ],
        out_specs=(
            pl.BlockSpec((1, 1, 1, nc, d), lambda w, b, g: (w, b, g, 0, 0)),
            pl.BlockSpec((1, 1, 1, d, nc), lambda w, b, g: (w, b, g, 0, 0)),
        ),
        scratch_shapes=[pltpu.VMEM((nc + 8, CMP_HIDDEN), F32)],
        compiler_params=_cparams(("parallel", "parallel", "parallel"), VMEM_LIMIT),
        name="compress_blocks",
    )(c_all, pe, w1, w2, w2T)


GATE_ROWS = 16


def _qproj_body(x_ref, g_ref, wT_ref, qT_ref, gT_ref):
    xn = _rms(x_ref[...], g_ref[...]).astype(BF16)
    tm = xn.shape[0]
    pT = lax.dot_general(wT_ref[...], xn, (((1,), (1,)), ((), ())),
                         preferred_element_type=F32)
    hd = NSA_HEADS * NSA_D
    q = pT[0:hd] * (NSA_D ** -0.5 * LOG2_E)
    qT_ref[0] = q.reshape(NSA_HEADS, NSA_D, tm).astype(BF16)
    gates = jax.nn.sigmoid(pT[hd:hd + NSA_GROUPS * GATE_ROWS])
    gT_ref[0] = gates.reshape(NSA_GROUPS, GATE_ROWS, tm)


def q_project(x, g, w_in, *, batch, seq, tm):
    d = x.shape[1]
    hd = NSA_HEADS * NSA_D
    per_group = NSA_REP * 3
    wg = w_in[:, hd:].reshape(d, NSA_GROUPS, per_group)
    wg = jnp.pad(wg, ((0, 0), (0, 0), (0, GATE_ROWS - per_group))).reshape(d, NSA_GROUPS * GATE_ROWS)
    wT = jnp.concatenate([w_in[:, :hd], wg], axis=1).T.astype(BF16)
    nt = seq // tm
    return pl.pallas_call(
        _qproj_body,
        out_shape=(
            jax.ShapeDtypeStruct((batch, NSA_HEADS, NSA_D, seq), BF16),
            jax.ShapeDtypeStruct((batch, NSA_GROUPS, GATE_ROWS, seq), F32),
        ),
        grid=(batch, nt),
        in_specs=[
            pl.BlockSpec((tm, d), lambda b, i: (b * nt + i, 0)),
            pl.BlockSpec((1, d), lambda b, i: (0, 0)),
            pl.BlockSpec(wT.shape, lambda b, i: (0, 0)),
        ],
        out_specs=(
            pl.BlockSpec((1, NSA_HEADS, NSA_D, tm), lambda b, i: (b, 0, 0, i)),
            pl.BlockSpec((1, NSA_GROUPS, GATE_ROWS, tm), lambda b, i: (b, 0, 0, i)),
        ),
        compiler_params=_cparams(("parallel", "parallel"), VMEM_LIMIT),
        name="q_project",
    )(x, g.reshape(1, d), wT)


CMP_CHUNK = 256
CMP_PER_QTILE = Q_TILE // CMP_STRIDE
SEL_PER_QTILE = Q_TILE // SEL_LEN
LANES = NSA_REP * Q_TILE
TILES_PER_GROUP = SEL_PER_GROUP * SEL_LEN // K_TILE
BIAS_LINE = 1024
WIN_TILES = (WINDOW + Q_TILE) // K_TILE
FAR_UNROLL = 2
SEL_ROW_STEP = 64


def _nsa_body(qT_ref, gT_ref, kc_ref, cmpL_ref, ks_ref, vsT_ref, kw_ref, vwT_ref,
              lines_ref, oT_ref,
              s_ref, imp_ref, msel_ref, qaug_ref, acc_ref, m_ref, out_ref, bc_ref, bs_ref, bw_ref, cacc_ref):
    i = pl.program_id(2)
    ns = imp_ref.shape[0]
    qT = jnp.concatenate([qT_ref[0, r] for r in range(NSA_REP)], axis=1)
    tiny = jnp.finfo(F32).tiny

    @pl.when(i == 0)
    def _():
        def expand(kind, r, rows, stride):
            line = lines_ref[0, kind, r:r + 1, :]
            shifted = pltpu.roll(jnp.broadcast_to(line, (rows, line.shape[1])), 0, 1,
                                 stride=stride, stride_axis=0)
            return shifted[:, 0:Q_TILE]

        for r in range(NSA_REP):
            cols = slice(r * Q_TILE, (r + 1) * Q_TILE)
            bc_ref[:, cols] = expand(0, r, bc_ref.shape[0], CMP_STRIDE)
            bs_ref[:, cols] = expand(1, r, bs_ref.shape[0], 1)
            bw_ref[:, cols] = expand(2, r, bw_ref.shape[0], 1)

    def gate_row(j):
        return jnp.concatenate([gT_ref[0, 0, 3 * r + j:3 * r + j + 1, :] for r in range(NSA_REP)], axis=1)

    nchunks = i // (CMP_CHUNK // CMP_PER_QTILE) + 1
    visible = CMP_PER_QTILE * (i + 1)

    def chunk_rows(c, count=1):
        return pl.ds(pl.multiple_of(c * CMP_CHUNK, CMP_CHUNK), count * CMP_CHUNK)

    def for_chunks(body):
        def pair(pr, carry):
            body(2 * pr, 2)
            return carry

        lax.fori_loop(0, nchunks // 2, pair, 0)

        @pl.when(nchunks % 2 == 1)
        def _():
            body(nchunks - 1, 1)

    def cmp_scores(c, count):
        s_ref[chunk_rows(c, count), :] = jnp.dot(kc_ref[0, 0, 0, chunk_rows(c, count), :], qT,
                                                 preferred_element_type=F32)

    for_chunks(cmp_scores)

    @pl.when(i == 0)
    def _():
        s_ref[0:CMP_PER_QTILE, :] = s_ref[0:CMP_PER_QTILE, :] + bc_ref[CMP_PER_QTILE:2 * CMP_PER_QTILE, :]

    @pl.when(i > 0)
    def _():
        rows = pl.ds(pl.multiple_of(CMP_PER_QTILE * (i - 1), CMP_PER_QTILE), 2 * CMP_PER_QTILE)
        s_ref[rows, :] = s_ref[rows, :] + bc_ref[...]

    cacc_ref[...] = jnp.zeros_like(cacc_ref)
    m_ref[...] = jnp.full(m_ref.shape, M_INIT, F32)

    def cmp_step(c, count):
        rid = c * CMP_CHUNK + lax.broadcasted_iota(jnp.int32, (count * CMP_CHUNK, LANES), 0)
        s = jnp.where(rid < visible, s_ref[chunk_rows(c, count), :], NEG_BIG)
        m_old = m_ref[0:1, :]
        m_new = jnp.maximum(m_old, jnp.max(s, axis=0, keepdims=True))
        alpha = jnp.exp2(m_old - m_new)
        p = jnp.exp2(s - m_new).astype(BF16)
        left = jnp.concatenate([cmpL_ref[0, 0, c + j] for j in range(count)], axis=1)
        cacc_ref[...] = alpha * cacc_ref[...] + jnp.dot(left, p, preferred_element_type=F32)
        m_ref[0:1, :] = m_new

    for_chunks(cmp_step)
    sees_block = m_ref[0:1, :] > 0.5 * NEG_BIG
    inv_c = jnp.where(sees_block, 1.0 / jnp.maximum(cacc_ref[NSA_D:NSA_D + 1, :], tiny), 0.0)
    out_ref[...] = (gate_row(0) * inv_c) * cacc_ref[0:NSA_D, :]
    weights = cacc_ref[V_ROWS:V_ROWS + ns, :] * inv_c
    imp = weights[:, 0:Q_TILE]
    for r in range(1, NSA_REP):
        imp = imp + weights[:, r * Q_TILE:(r + 1) * Q_TILE]
    imp_ref[...] = imp

    qaug_ref[0:NSA_D, :] = qT
    qaug_ref[NSA_D:, :] = jnp.zeros((qaug_ref.shape[0] - NSA_D, LANES), BF16)

    def reset():
        acc_ref[...] = jnp.zeros_like(acc_ref)
        m_ref[...] = jnp.full(m_ref.shape, M_INIT, F32)

    def tile_rows(j):
        return slice(j * K_TILE, (j + 1) * K_TILE)

    def scores_to_scratch(k_ref, first_tile, count, slot):
        rows = pl.ds(pl.multiple_of(first_tile * K_TILE, K_TILE), count * K_TILE)
        s_ref[slot * K_TILE:(slot + count) * K_TILE, :] = jnp.dot(
            k_ref[0, 0, rows, :], qaug_ref[...], preferred_element_type=F32)

    def softmax_from_scratch(vT_ref, tiles, slots):
        rows = slice(slots[0] * K_TILE, (slots[-1] + 1) * K_TILE)
        m_old = m_ref[0:1, :]
        m_new = jnp.maximum(m_old, jnp.max(s_ref[rows, :], axis=0, keepdims=True))
        alpha = jnp.exp2(m_old - m_new)
        p = jnp.exp2(s_ref[rows, :] - m_new).astype(BF16)
        vT = jnp.concatenate([vT_ref[0, 0, kt] for kt in tiles], axis=1)
        acc_ref[...] = alpha * acc_ref[...] + jnp.dot(vT, p, preferred_element_type=F32)
        m_ref[0:1, :] = m_new

    def finish(j):
        l = jnp.maximum(acc_ref[NSA_D:NSA_D + 1, :], tiny)
        out_ref[...] += (gate_row(j) * (1.0 / l)) * acc_ref[0:NSA_D, :]

    def window_branch():
        win_slots = [TILES_PER_GROUP + j for j in range(WIN_TILES)]
        win_tiles = [jnp.maximum(i - (WIN_TILES - 1) + j, 0) for j in range(WIN_TILES)]
        reset()
        for j, (kt, slot) in enumerate(zip(win_tiles, win_slots)):
            scores_to_scratch(kw_ref, kt, 1, slot)
            before_start = jnp.where(i - (WIN_TILES - 1) + j < 0, NEG_BIG, 0.0).astype(F32)
            s_ref[tile_rows(slot), :] = s_ref[tile_rows(slot), :] + (bw_ref[tile_rows(j), :] + before_start)
        softmax_from_scratch(vwT_ref, win_tiles, win_slots)
        finish(2)

    def select_blocks(nrows):
        jrow = lax.broadcasted_iota(jnp.int32, (nrows, Q_TILE), 0)
        col = lax.broadcasted_iota(jnp.int32, (nrows, Q_TILE), 1)
        cur = SEL_PER_QTILE * i + col // SEL_LEN
        valid = jrow <= cur
        forced = (jrow == 0) | (jrow == cur) | (jrow == cur - 1)
        free = float(SEL_TOPK - 3)
        candidate = valid & jnp.logical_not(forced)
        v0 = jnp.where(candidate, imp_ref[0:nrows, :], -jnp.inf)

        def strip_max(carry):
            v, taken, theta, above = carry
            best = jnp.max(v, axis=0, keepdims=True)
            hit = v == best
            now = taken + jnp.sum(jnp.where(hit, 1.0, 0.0), axis=0, keepdims=True)
            crossed = (taken < free) & (now >= free)
            return (jnp.where(hit, -jnp.inf, v), now,
                    jnp.where(crossed, best, theta), jnp.where(crossed, taken, above))

        zero_row = jnp.zeros((1, Q_TILE), F32)
        carry = (v0, zero_row, jnp.full((1, Q_TILE), jnp.inf, F32), zero_row)
        for _ in range(SEL_TOPK - 3):
            carry = strip_max(carry)
        _, _, theta, above = carry
        v0 = jnp.where(candidate, imp_ref[0:nrows, :], -jnp.inf)
        tied = v0 == theta
        lower = jnp.where(lax.broadcasted_iota(jnp.int32, (nrows, nrows), 1)
                          < lax.broadcasted_iota(jnp.int32, (nrows, nrows), 0), 1.0, 0.0).astype(BF16)
        rank = jnp.dot(lower, jnp.where(tied, 1.0, 0.0).astype(BF16), preferred_element_type=F32)
        chosen = forced | (v0 > theta) | (tied & (rank < free - above))
        mask_bias = jnp.where(chosen & valid, 0.0, NEG_BIG).astype(BF16)
        msel_ref[0:nrows, :] = jnp.concatenate([mask_bias] * NSA_REP, axis=1)
        if nrows < ns:
            msel_ref[nrows:ns, :] = jnp.full((ns - nrows, LANES), NEG_BIG, BF16)

    last = i // TILES_PER_GROUP
    j_i = i % TILES_PER_GROUP

    half = TILES_PER_GROUP // 2

    def sel_tiles(grp, h):
        return [grp * TILES_PER_GROUP + h * half + j for j in range(half)]

    def sel_slots(h):
        return [h * half + j for j in range(half)]

    def set_slab(grp):
        slab = pl.ds(pl.multiple_of(grp * SEL_PER_GROUP, SEL_PER_GROUP), SEL_PER_GROUP)
        qaug_ref[NSA_D:NSA_D + SEL_PER_GROUP, :] = msel_ref[slab, :]

    def sel_scores(grp, h):
        set_slab(grp)
        scores_to_scratch(ks_ref, sel_tiles(grp, h)[0], half, h * half)

    def sel_softmax(grp, h):
        softmax_from_scratch(vsT_ref, sel_tiles(grp, h), sel_slots(h))

    def add_bias(slot, count, bias_tile):
        rows = slice(slot * K_TILE, (slot + count) * K_TILE)
        s_ref[rows, :] = s_ref[rows, :] + bs_ref[bias_tile * K_TILE:(bias_tile + count) * K_TILE, :]

    row_steps = list(range(SEL_ROW_STEP, ns, SEL_ROW_STEP)) + [ns]
    for idx, nrows in enumerate(row_steps):
        lo = 0 if idx == 0 else row_steps[idx - 1] // SEL_PER_QTILE
        hi = nrows // SEL_PER_QTILE
        in_range = (i >= lo) if idx == len(row_steps) - 1 else ((i >= lo) & (i < hi))

        @pl.when(in_range)
        def _(nrows=nrows):
            window_branch()
            select_blocks(nrows)
            reset()
            sel_scores(0, 0)

    def far(grp):
        sel_scores(grp, 1)
        sel_softmax(grp, 0)
        sel_scores(grp + 1, 0)
        sel_softmax(grp, 1)

    def far_many(trip, carry):
        for u in range(FAR_UNROLL):
            far(FAR_UNROLL * trip + u)
        return carry

    prev_is_near = (last >= 1) & (j_i == 0)
    n_far = jnp.where(prev_is_near, last - 1, last)
    lax.fori_loop(0, n_far // FAR_UNROLL, far_many, 0)

    def far_rest(grp, carry):
        far(grp)
        return carry

    lax.fori_loop((n_far // FAR_UNROLL) * FAR_UNROLL, n_far, far_rest, 0)

    @pl.when(prev_is_near)
    def _():
        sel_scores(last - 1, 1)
        sel_softmax(last - 1, 0)
        add_bias(TILES_PER_GROUP - 1, 1, 0)
        sel_scores(last, 0)
        sel_softmax(last - 1, 1)

    @pl.when(j_i == 0)
    def _():
        add_bias(0, 1, 1)

    @pl.when(j_i == 1)
    def _():
        add_bias(0, 2, 0)

    @pl.when(j_i == 2)
    def _():
        add_bias(1, 1, 0)

    @pl.when(j_i >= half)
    def _():
        sel_scores(last, 1)
        sel_softmax(last, 0)

        @pl.when(j_i == 2)
        def _():
            add_bias(2, 1, 1)

        @pl.when(j_i == 3)
        def _():
            add_bias(2, 2, 0)

        sel_softmax(last, 1)

    @pl.when(j_i < half)
    def _():
        sel_softmax(last, 0)

    finish(1)

    for r in range(NSA_REP):
        oT_ref[0, r * NSA_D:(r + 1) * NSA_D, :] = out_ref[:, r * Q_TILE:(r + 1) * Q_TILE].astype(oT_ref.dtype)


def nsa_attention(qT, gT, cmp_nat, cmp_tr, ks, vsT, kw, vwT, lines, mmapT, *, batch, seq):
    G = NSA_GROUPS
    nc = seq // CMP_STRIDE
    ns = seq // SEL_LEN
    nq = seq // Q_TILE
    nkt = seq // K_TILE
    hd = NSA_HEADS * NSA_D
    nch = nc // CMP_CHUNK
    vcT = cmp_tr[1].reshape(batch, G, NSA_D, nch, CMP_CHUNK).transpose(0, 1, 3, 2, 4)
    ones = jnp.zeros((V_ROWS - NSA_D, CMP_CHUNK), BF16).at[0].set(1.0)
    cmpL = jnp.concatenate([
        vcT,
        jnp.broadcast_to(ones, (batch, G, nch, V_ROWS - NSA_D, CMP_CHUNK)),
        jnp.broadcast_to(mmapT, (batch, G, nch, ns, CMP_CHUNK)),
    ], axis=3)
    once = pl.Buffered(1)
    return pl.pallas_call(
        _nsa_body,
        out_shape=jax.ShapeDtypeStruct((batch, hd, seq), BF16),
        grid=(batch, G, nq),
        in_specs=[
            pl.BlockSpec((1, NSA_REP, NSA_D, Q_TILE), lambda b, g, i: (b, g, 0, i)),
            pl.BlockSpec((1, 1, GATE_ROWS, Q_TILE), lambda b, g, i: (b, g, 0, i)),
            pl.BlockSpec((1, 1, 1, nc, NSA_D), lambda b, g, i: (0, b, g, 0, 0)),
            pl.BlockSpec((1, 1, nch, V_ROWS + ns, CMP_CHUNK), lambda b, g, i: (b, g, 0, 0, 0)),
            pl.BlockSpec((1, 1, seq, K_LANES), lambda b, g, i: (b, g, 0, 0), pipeline_mode=once),
            pl.BlockSpec((1, 1, nkt, V_ROWS, K_TILE), lambda b, g, i: (b, g, 0, 0, 0), pipeline_mode=once),
            pl.BlockSpec((1, 1, seq, K_LANES), lambda b, g, i: (b, g, 0, 0), pipeline_mode=once),
            pl.BlockSpec((1, 1, nkt, V_ROWS, K_TILE), lambda b, g, i: (b, g, 0, 0, 0), pipeline_mode=once),
            pl.BlockSpec((1, 3, NSA_REP, BIAS_LINE), lambda b, g, i: (g, 0, 0, 0)),
        ],
        out_specs=pl.BlockSpec((1, NSA_REP * NSA_D, Q_TILE), lambda b, g, i: (b, g, i)),
        scratch_shapes=[
            pltpu.VMEM((max(nc, (TILES_PER_GROUP + WIN_TILES) * K_TILE), LANES), F32),
            pltpu.VMEM((ns, Q_TILE), F32),
            pltpu.VMEM((ns, LANES), BF16),
            pltpu.VMEM((K_LANES, LANES), BF16),
            pltpu.VMEM((V_ROWS, LANES), F32),
            pltpu.VMEM((8, LANES), F32),
            pltpu.VMEM((NSA_D, LANES), F32),
            pltpu.VMEM((2 * CMP_PER_QTILE, LANES), F32),
            pltpu.VMEM((2 * K_TILE, LANES), F32),
            pltpu.VMEM((3 * K_TILE, LANES), F32),
            pltpu.VMEM((V_ROWS + ns, LANES), F32),
        ],
        compiler_params=_cparams(("parallel", "parallel", "arbitrary"), VMEM_LIMIT),
        name="nsa_attention",
    )(qT, gT, cmp_nat, cmpL, ks, vsT, kw, vwT, lines)


def _t5_bucket_table(n_max):
    n = np.arange(n_max)
    max_exact = REL_BUCKETS // 2
    nf = np.maximum(n, 1).astype(np.float32)
    large = max_exact + (np.log(nf / np.float32(max_exact))
                         / np.float32(math.log(REL_MAX_DIST / max_exact))
                         * np.float32(REL_BUCKETS - max_exact)).astype(np.int32)
    large = np.minimum(large, REL_BUCKETS - 1)
    return np.where(n < max_exact, n, large).astype(np.int32)


def _bias_tables(rel_bias):
    n_max = WINDOW + Q_TILE
    buckets = _t5_bucket_table(n_max)
    assert (buckets[Q_TILE - CMP_LEN + 1:] == REL_BUCKETS - 1).all()
    tab = rel_bias.astype(F32)[buckets, :] - rel_bias.astype(F32)[REL_BUCKETS - 1][None, :]
    tab = tab.T * LOG2_E
    def line(rows, stride, offset, ok, fill):
        assert Q_TILE + stride * (rows - 1) <= BIAS_LINE
        k = np.arange(BIAS_LINE)
        dist = np.where(k < Q_TILE, k, k - BIAS_LINE) + offset
        vals = jnp.where(ok(dist)[None], tab[:, np.clip(dist, 0, n_max - 1)], fill)
        return vals.reshape(NSA_GROUPS, NSA_REP, BIAS_LINE)

    return jnp.stack([
        line(2 * CMP_PER_QTILE, CMP_STRIDE, Q_TILE - CMP_LEN + 1, lambda dd: dd >= 0, NEG_BIG),
        line(2 * K_TILE, 1, K_TILE, lambda dd: dd >= 0, NEG_BIG),
        line(3 * K_TILE, 1, 2 * K_TILE, lambda dd: (dd >= 0) & (dd < WINDOW), NEG_BIG),
    ], axis=1)


def _selection_map(seq):
    nc = seq // CMP_STRIDE
    ns = seq // SEL_LEN
    n_cmp = (seq - CMP_LEN) // CMP_STRIDE + 1
    ratio = SEL_LEN // CMP_STRIDE
    lead = CMP_LEN // CMP_STRIDE - 1
    j = np.arange(ns)[:, None]
    n = np.arange(nc)[None, :]
    m = ((n >= ratio * j - lead) & (n < ratio * j + ratio) & (n < n_cmp)).astype(np.float32)
    m = m.reshape(ns, nc // CMP_CHUNK, CMP_CHUNK).transpose(1, 0, 2)
    return jnp.asarray(m, dtype=BF16)


def kernel(x, mix_norm_pre, mix_norm_post, ffn_norm_pre, ffn_norm_post, ffn_w_in, ffn_w_out,
           ret_w_in, ret_w_out, kv_norm, kv_w, cmp_pe_k, cmp_w1_k, cmp_w2_k,
           cmp_pe_v, cmp_w1_v, cmp_w2_v, nsa_w_in, nsa_w_out, rel_bias):
    batch, seq, d = x.shape
    n_ret = ret_w_in.shape[0]
    n_nsa = nsa_w_in.shape[0]
    assert seq % (SEL_PER_GROUP * SEL_LEN) == 0 and seq // SEL_LEN >= SEL_TOPK
    h = x.reshape(batch * seq, d)
    shared = None
    for layer in range(n_ret + n_nsa):
        if layer == n_ret:
            kc_nat, vc_nat, ks, kw, vsT, vwT = kv_project(h, kv_norm, kv_w, batch=batch, seq=seq, tm=TM_KV)
            cmp_nat, cmp_tr = compress_blocks(kc_nat, vc_nat, cmp_pe_k, cmp_w1_k, cmp_w2_k,
                                              cmp_pe_v, cmp_w1_v, cmp_w2_v, batch=batch, seq=seq)
            shared = (cmp_nat, cmp_tr, ks, vsT, kw, vwT, _bias_tables(rel_bias), _selection_map(seq))
        if layer < n_ret:
            w_in = _deinterleave_qk_columns(ret_w_in[layer]).astype(BF16)
            proj = norm_matmul(h, mix_norm_pre[layer], w_in, tm=TM_PROJ, tn=TN_PROJ, out_dtype=BF16)
            mixed = retention_core(proj, *_retention_tables(seq), batch=batch, seq=seq)
            w_o, feature_major = ret_w_out[layer], False
        else:
            j = layer - n_ret
            qT, gT = q_project(h, mix_norm_pre[layer], nsa_w_in[j], batch=batch, seq=seq, tm=TM_Q)
            mixed = nsa_attention(qT, gT, *shared, batch=batch, seq=seq)
            w_o, feature_major = nsa_w_out[j], True
        h = mix_out_ffn(mixed, w_o.astype(BF16), mix_norm_post[layer], h, ffn_norm_pre[layer],
                        ffn_w_in[layer].astype(BF16), ffn_w_out[layer].astype(BF16), ffn_norm_post[layer],
                        batch=batch, seq=seq, tm=TM_FFN, chunks=FFN_CHUNKS, feature_major=feature_major)
    return h.reshape(batch, seq, d)
```

```python
import functools
import math

import numpy as np
import jax
import jax.numpy as jnp
from jax import lax
from jax.experimental import pallas as pl
from jax.experimental.pallas import tpu as pltpu

F32 = jnp.float32
BF16 = jnp.bfloat16

D_MODEL = 1024
RMS_EPS = 1e-6

RET_HEADS = 4
RET_QK = 256
RET_V = 512
RET_CHUNK = 128
RET_CHUNKS_PER_STEP = 2

FFN_HIDDEN = 2816

NSA_HEADS = 16
NSA_GROUPS = 4
NSA_REP = 4
NSA_D = 64
CMP_LEN = 32
CMP_STRIDE = 16
CMP_HIDDEN = 256
SEL_LEN = 64
SEL_TOPK = 16
WINDOW = 512
REL_BUCKETS = 32
REL_MAX_DIST = 128

Q_TILE = 256
K_TILE = 256
SEL_PER_GROUP = 16
V_ROWS = 80
LOG2_E = math.log2(math.e)
NEG_BIG = -(2.0 ** 100)
M_INIT = -(2.0 ** 120)

LANE_TILE = 128
K_LANES = LANE_TILE
VMEM_V7X = 64 * 1024 * 1024
VMEM_LIMIT = VMEM_V7X - 8 * 1024 * 1024

TM_PROJ, TN_PROJ = 512, 2048
TM_KV = 1024
TM_Q = 512
TM_FFN, FFN_CHUNKS = 512, 2


def _cparams(sem, vmem=None, flags=None):
    return pltpu.CompilerParams(dimension_semantics=sem, vmem_limit_bytes=vmem, flags=flags)


def _rms(x, g):
    return x * lax.rsqrt(jnp.mean(x * x, axis=-1, keepdims=True) + RMS_EPS) * g


def _norm_matmul_body(x_ref, g_ref, w_ref, o_ref, *, tn):
    xn = _rms(x_ref[...], g_ref[...]).astype(BF16)
    for c in range(w_ref.shape[1] // tn):
        cols = slice(c * tn, (c + 1) * tn)
        o_ref[:, cols] = jnp.dot(xn, w_ref[:, cols], preferred_element_type=F32).astype(o_ref.dtype)


def norm_matmul(x, g, w, *, tm, tn, out_dtype=F32):
    t, d = x.shape
    n = w.shape[1]
    assert n % tn == 0
    return pl.pallas_call(
        functools.partial(_norm_matmul_body, tn=tn),
        out_shape=jax.ShapeDtypeStruct((t, n), out_dtype),
        grid=(t // tm,),
        in_specs=[
            pl.BlockSpec((tm, d), lambda i: (i, 0)),
            pl.BlockSpec((1, d), lambda i: (0, 0)),
            pl.BlockSpec((d, n), lambda i: (0, 0), pipeline_mode=pl.Buffered(1)),
        ],
        out_specs=pl.BlockSpec((tm, n), lambda i: (i, 0)),
        compiler_params=_cparams(("parallel",), VMEM_LIMIT),
        name="norm_matmul",
    )(x, g.reshape(1, d), w)


def _mix_out_ffn_body(y_ref, wp_ref, gmix_ref, res_ref, gpre_ref, win_ref, wo_ref, gpost_ref, o_ref,
                      *, chunks, feature_major):
    if feature_major:
        z = lax.dot_general(y_ref[0], wp_ref[...], (((0,), (0,)), ((), ())), preferred_element_type=F32)
    else:
        z = jnp.dot(y_ref[...], wp_ref[...], preferred_element_type=F32)
    x = res_ref[...] + _rms(z, gmix_ref[...])
    xn = _rms(x, gpre_ref[...]).astype(BF16)
    hdim = wo_ref.shape[0]
    th = hdim // chunks
    y = None
    for c in range(chunks):
        gate = jnp.dot(xn, win_ref[:, c * th:(c + 1) * th], preferred_element_type=F32)
        up = jnp.dot(xn, win_ref[:, hdim + c * th:hdim + (c + 1) * th], preferred_element_type=F32)
        act = (gate * jax.nn.sigmoid(gate) * up).astype(BF16)
        part = jnp.dot(act, wo_ref[c * th:(c + 1) * th, :], preferred_element_type=F32)
        y = part if y is None else y + part
    o_ref[...] = x + _rms(y, gpost_ref[...])


def mix_out_ffn(y, w_proj, g_mix, res, g_pre, w_in, w_out, g_post, *, batch, seq, tm, chunks, feature_major):
    k, d = w_proj.shape
    hdim = w_out.shape[0]
    assert (hdim // chunks) % LANE_TILE == 0 and hdim % chunks == 0
    nt = seq // tm
    once = pl.Buffered(1)
    row = lambda b, i: (b * nt + i, 0)
    const = lambda b, i: (0, 0)
    if feature_major:
        y_spec = pl.BlockSpec((1, k, tm), lambda b, i: (b, 0, i))
    else:
        y_spec = pl.BlockSpec((tm, k), row)
    return pl.pallas_call(
        functools.partial(_mix_out_ffn_body, chunks=chunks, feature_major=feature_major),
        out_shape=jax.ShapeDtypeStruct((batch * seq, d), F32),
        grid=(batch, nt),
        in_specs=[
            y_spec,
            pl.BlockSpec((k, d), const, pipeline_mode=once),
            pl.BlockSpec((1, d), const),
            pl.BlockSpec((tm, d), row),
            pl.BlockSpec((1, d), const),
            pl.BlockSpec((d, 2 * hdim), const, pipeline_mode=once),
            pl.BlockSpec((hdim, d), const, pipeline_mode=once),
            pl.BlockSpec((1, d), const),
        ],
        out_specs=pl.BlockSpec((tm, d), row),
        compiler_params=_cparams(("parallel", "parallel"), VMEM_LIMIT),
        name="mix_out_ffn",
    )(y, w_proj, g_mix.reshape(1, d), res, g_pre.reshape(1, d), w_in, w_out, g_post.reshape(1, d))


def _retention_body(q_ref, k_ref, v_ref, g_ref, cos_ref, sin_ref, dmat_ref, qdec_ref, kdec_ref,
                    cdec_ref, o_ref, state_ref):
    @pl.when(pl.program_id(1) == 0)
    def _():
        state_ref[...] = jnp.zeros_like(state_ref)

    half = RET_QK // 2

    for sub in range(RET_CHUNKS_PER_STEP):
        rows = slice(sub * RET_CHUNK, (sub + 1) * RET_CHUNK)
        cos = cos_ref[rows, :]
        sin = sin_ref[rows, :]

        def rotate(x_ref, h):
            x1 = x_ref[rows, h * RET_QK:h * RET_QK + half].astype(F32)
            x2 = x_ref[rows, h * RET_QK + half:(h + 1) * RET_QK].astype(F32)
            return jnp.concatenate([x1 * cos - x2 * sin, x1 * sin + x2 * cos], axis=1)

        for h in range(RET_HEADS):
            qr = rotate(q_ref, h)
            kr = rotate(k_ref, h) * (RET_QK ** -0.5)
            v = v_ref[rows, h * RET_V:(h + 1) * RET_V].astype(BF16)
            scores = lax.dot_general(qr.astype(BF16), kr.astype(BF16), (((1,), (1,)), ((), ())),
                                     preferred_element_type=F32) * dmat_ref[h]
            state = state_ref[h]
            o = (jnp.dot(scores.astype(BF16), v, preferred_element_type=F32)
                 + jnp.dot((qr * qdec_ref[h]).astype(BF16), state.astype(BF16),
                           preferred_element_type=F32))
            kd = (kr * kdec_ref[h]).astype(BF16)
            state_ref[h] = state * cdec_ref[h, 0:1, :] + lax.dot_general(
                kd, v, (((0,), (0,)), ((), ())), preferred_element_type=F32)
            o = o * lax.rsqrt(jnp.mean(o * o, axis=-1, keepdims=True) + RMS_EPS)
            gate = g_ref[rows, h * RET_V:(h + 1) * RET_V].astype(F32)
            o_ref[rows, h * RET_V:(h + 1) * RET_V] = (o * (gate * jax.nn.sigmoid(gate))).astype(o_ref.dtype)


def retention_core(proj, cos, sin, dmat, qdec, kdec, cdec, *, batch, seq):
    c = RET_CHUNK * RET_CHUNKS_PER_STEP
    nc = seq // c
    hq = RET_HEADS * RET_QK
    hv = RET_HEADS * RET_V
    return pl.pallas_call(
        _retention_body,
        out_shape=jax.ShapeDtypeStruct((batch * seq, hv), BF16),
        grid=(batch, nc),
        in_specs=[
            pl.BlockSpec((c, hq), lambda b, t: (b * nc + t, 0)),
            pl.BlockSpec((c, hq), lambda b, t: (b * nc + t, 1)),
            pl.BlockSpec((c, hv), lambda b, t: (b * nc + t, 1)),
            pl.BlockSpec((c, hv), lambda b, t: (b * nc + t, 2)),
            pl.BlockSpec((c, RET_QK // 2), lambda b, t: (t, 0)),
            pl.BlockSpec((c, RET_QK // 2), lambda b, t: (t, 0)),
            pl.BlockSpec((RET_HEADS, RET_CHUNK, RET_CHUNK), lambda b, t: (0, 0, 0)),
            pl.BlockSpec((RET_HEADS, RET_CHUNK, RET_QK), lambda b, t: (0, 0, 0)),
            pl.BlockSpec((RET_HEADS, RET_CHUNK, RET_QK), lambda b, t: (0, 0, 0)),
            pl.BlockSpec((RET_HEADS, 8, RET_V), lambda b, t: (0, 0, 0)),
        ],
        out_specs=pl.BlockSpec((c, hv), lambda b, t: (b * nc + t, 0)),
        scratch_shapes=[pltpu.VMEM((RET_HEADS, RET_QK, RET_V), F32)],
        compiler_params=_cparams(("parallel", "arbitrary"), VMEM_LIMIT),
        name="retention_core",
    )(proj, proj, proj, proj, cos, sin, dmat, qdec, kdec, cdec)


def _retention_tables(seq):
    h, dk, c = RET_HEADS, RET_QK, RET_CHUNK
    pos = jnp.arange(seq, dtype=F32)
    theta = 1.0 / (10000.0 ** jnp.linspace(0.0, 1.0, dk // 2, dtype=F32))
    ang = pos[:, None] * theta[None, :]
    log_gamma = jnp.log(1.0 - 2.0 ** (-5.0 - jnp.arange(h, dtype=F32)))
    idx = jnp.arange(c, dtype=F32)
    rel = idx[:, None] - idx[None, :]
    dmat = jnp.where(rel >= 0, jnp.exp(jnp.maximum(rel, 0.0) * log_gamma[:, None, None]), 0.0)
    qdec = jnp.exp((idx + 1.0)[None, :] * log_gamma[:, None])
    kdec = jnp.exp((c - 1.0 - idx)[None, :] * log_gamma[:, None])
    cdec = jnp.exp(c * log_gamma)
    qdec = jnp.broadcast_to(qdec[:, :, None], (h, c, dk))
    kdec = jnp.broadcast_to(kdec[:, :, None], (h, c, dk))
    cdec = jnp.broadcast_to(cdec[:, None, None], (h, 8, RET_V))
    return jnp.cos(ang), jnp.sin(ang), dmat, qdec, kdec, cdec


def _deinterleave_qk_columns(w):
    nqk = 2 * RET_HEADS * RET_QK
    perm = []
    for h in range(2 * RET_HEADS):
        base = h * RET_QK
        perm += [base + 2 * i for i in range(RET_QK // 2)]
        perm += [base + 2 * i + 1 for i in range(RET_QK // 2)]
    perm = np.asarray(perm + list(range(nqk, w.shape[1])), dtype=np.int32)
    return w[:, perm]


def _kv_body(x_ref, g_ref, wk_ref, wvT_ref, kc_ref, vc_ref, ks_ref, kw_ref, vsT_ref, vwT_ref):
    xn = _rms(x_ref[...], g_ref[...]).astype(BF16)
    tm = xn.shape[0]
    kall = jnp.dot(xn, wk_ref[...], preferred_element_type=F32)
    gd = NSA_GROUPS * NSA_D
    kc_ref[...] = kall[:, 0:gd]
    vc_ref[...] = kall[:, gd:2 * gd]
    row = pl.program_id(1) * tm + lax.broadcasted_iota(jnp.int32, (tm, K_LANES), 0)
    lane = lax.broadcasted_iota(jnp.int32, (tm, K_LANES), 1)
    blk = (row // SEL_LEN) % SEL_PER_GROUP
    onehot = jnp.where(lane - NSA_D == blk, 1.0, 0.0).astype(F32)
    for g in range(NSA_GROUPS):
        ks = kall[:, 2 * gd + K_LANES * g:2 * gd + K_LANES * (g + 1)]
        ks_ref[0, g] = (ks + onehot).astype(BF16)
        kw_base = 2 * gd + NSA_GROUPS * K_LANES
        kw = kall[:, kw_base + K_LANES * g:kw_base + K_LANES * (g + 1)]
        kw_ref[0, g] = kw.astype(BF16)
    vT = lax.dot_general(wvT_ref[...], xn, (((1,), (1,)), ((), ())),
                         preferred_element_type=F32)
    extra = jnp.where(lax.broadcasted_iota(jnp.int32, (V_ROWS - NSA_D, K_TILE), 0) == 0, 1.0, 0.0)
    extra = extra.astype(BF16)
    for g in range(NSA_GROUPS):
        for c in range(tm // K_TILE):
            cols = slice(c * K_TILE, (c + 1) * K_TILE)
            vsT_ref[0, g, c, 0:NSA_D, :] = vT[NSA_D * g:NSA_D * (g + 1), cols].astype(BF16)
            vsT_ref[0, g, c, NSA_D:V_ROWS, :] = extra
            vwT_ref[0, g, c, 0:NSA_D, :] = vT[gd + NSA_D * g:gd + NSA_D * (g + 1), cols].astype(BF16)
            vwT_ref[0, g, c, NSA_D:V_ROWS, :] = extra


def kv_project(x, g, kv_w, *, batch, seq, tm):
    d = x.shape[1]
    gd = NSA_GROUPS * NSA_D
    k_c, v_c, k_s, v_s, k_w, v_w = [kv_w[:, i * gd:(i + 1) * gd] for i in range(6)]

    def pad_groups(w):
        w = w.reshape(d, NSA_GROUPS, NSA_D)
        return jnp.pad(w, ((0, 0), (0, 0), (0, K_LANES - NSA_D))).reshape(d, NSA_GROUPS * K_LANES)

    wk = jnp.concatenate([k_c, v_c, pad_groups(k_s), pad_groups(k_w)], axis=1).astype(BF16)
    wvT = jnp.concatenate([v_s, v_w], axis=1).T.astype(BF16)
    nt = seq // tm
    G = NSA_GROUPS
    return pl.pallas_call(
        _kv_body,
        out_shape=(
            jax.ShapeDtypeStruct((batch * seq, gd), F32),
            jax.ShapeDtypeStruct((batch * seq, gd), F32),
            jax.ShapeDtypeStruct((batch, G, seq, K_LANES), BF16),
            jax.ShapeDtypeStruct((batch, G, seq, K_LANES), BF16),
            jax.ShapeDtypeStruct((batch, G, seq // K_TILE, V_ROWS, K_TILE), BF16),
            jax.ShapeDtypeStruct((batch, G, seq // K_TILE, V_ROWS, K_TILE), BF16),
        ),
        grid=(batch, nt),
        in_specs=[
            pl.BlockSpec((tm, d), lambda b, i: (b * nt + i, 0)),
            pl.BlockSpec((1, d), lambda b, i: (0, 0)),
            pl.BlockSpec(wk.shape, lambda b, i: (0, 0)),
            pl.BlockSpec(wvT.shape, lambda b, i: (0, 0)),
        ],
        out_specs=(
            pl.BlockSpec((tm, gd), lambda b, i: (b * nt + i, 0)),
            pl.BlockSpec((tm, gd), lambda b, i: (b * nt + i, 0)),
            pl.BlockSpec((1, G, tm, K_LANES), lambda b, i: (b, 0, i, 0)),
            pl.BlockSpec((1, G, tm, K_LANES), lambda b, i: (b, 0, i, 0)),
            pl.BlockSpec((1, G, tm // K_TILE, V_ROWS, K_TILE), lambda b, i: (b, 0, i, 0, 0)),
            pl.BlockSpec((1, G, tm // K_TILE, V_ROWS, K_TILE), lambda b, i: (b, 0, i, 0, 0)),
        ),
        compiler_params=_cparams(("parallel", "parallel"), VMEM_LIMIT),
        name="kv_project",
    )(x, g.reshape(1, d), wk, wvT)


def _compress_body(c_ref, pe_ref, w1_ref, w2_ref, w2T_ref, nat_ref, tr_ref, sh_ref):
    half = CMP_STRIDE * NSA_D
    nc = c_ref.shape[3]
    c = c_ref[0, 0, 0].astype(BF16)
    w1 = w1_ref[0]
    first = jnp.dot(c, w1[0:half], preferred_element_type=F32)
    second = jnp.dot(c, w1[half:2 * half], preferred_element_type=F32)
    pe_term = jnp.dot(pe_ref[0].astype(BF16), w1, preferred_element_type=F32)
    sh_ref[0:nc, :] = second
    sh_ref[nc:nc + 8, :] = jnp.zeros((8, CMP_HIDDEN), F32)
    pre = first + sh_ref[1:nc + 1, :] + pe_term[0:1, :]
    hid = (pre * jax.nn.sigmoid(pre)).astype(BF16)
    nat_ref[0, 0, 0] = jnp.dot(hid, w2_ref[0], preferred_element_type=F32).astype(BF16)
    tr_ref[0, 0, 0] = lax.dot_general(w2T_ref[0], hid, (((1,), (1,)), ((), ())),
                                      preferred_element_type=F32).astype(BF16)


def compress_blocks(kc_nat, vc_nat, pe_k, w1_k, w2_k, pe_v, w1_v, w2_v, *, batch, seq):
    G, d = NSA_GROUPS, NSA_D
    nc = seq // CMP_STRIDE

    def to_rows(t):
        t = t.reshape(batch, nc, CMP_STRIDE, G, d).transpose(0, 3, 1, 2, 4)
        return t.reshape(batch, G, nc, CMP_STRIDE * d)

    c_all = jnp.stack([to_rows(kc_nat), to_rows(vc_nat)])
    pe = jnp.stack([pe_k.reshape(1, -1), pe_v.reshape(1, -1)])
    pe = jnp.broadcast_to(pe, (2, 8, CMP_LEN * d))
    w1 = jnp.stack([w1_k, w1_v]).astype(BF16)
    w2 = jnp.stack([w2_k, w2_v]).astype(BF16)
    w2T = jnp.stack([w2_k.T, w2_v.T]).astype(BF16)
    return pl.pallas_call(
        _compress_body,
        out_shape=(
            jax.ShapeDtypeStruct((2, batch, G, nc, d), BF16),
            jax.ShapeDtypeStruct((2, batch, G, d, nc), BF16),
        ),
        grid=(2, batch, G),
        in_specs=[
            pl.BlockSpec((1, 1, 1, nc, CMP_STRIDE * d), lambda w, b, g: (w, b, g, 0, 0)),
            pl.BlockSpec((1, 8, CMP_LEN * d), lambda w, b, g: (w, 0, 0)),
            pl.BlockSpec((1, CMP_LEN * d, CMP_HIDDEN), lambda w, b, g: (w, 0, 0)),
            pl.BlockSpec((1, CMP_HIDDEN, d), lambda w, b, g: (w, 0, 0)),
            pl.BlockSpec((1, d, CMP_HIDDEN), lambda w, b, g: (w, 0, 0)),
        ],
        out_specs=(
            pl.BlockSpec((1, 1, 1, nc, d), lambda w, b, g: (w, b, g, 0, 0)),
            pl.BlockSpec((1, 1, 1, d, nc), lambda w, b, g: (w, b, g, 0, 0)),
        ),
        scratch_shapes=[pltpu.VMEM((nc + 8, CMP_HIDDEN), F32)],
        compiler_params=_cparams(("parallel", "parallel", "parallel"), VMEM_LIMIT),
        name="compress_blocks",
    )(c_all, pe, w1, w2, w2T)


GATE_ROWS = 16


def _qproj_body(x_ref, g_ref, wT_ref, qT_ref, gT_ref):
    xn = _rms(x_ref[...], g_ref[...]).astype(BF16)
    tm = xn.shape[0]
    pT = lax.dot_general(wT_ref[...], xn, (((1,), (1,)), ((), ())),
                         preferred_element_type=F32)
    hd = NSA_HEADS * NSA_D
    q = pT[0:hd] * (NSA_D ** -0.5 * LOG2_E)
    qT_ref[0] = q.reshape(NSA_HEADS, NSA_D, tm).astype(BF16)
    gates = jax.nn.sigmoid(pT[hd:hd + NSA_GROUPS * GATE_ROWS])
    gT_ref[0] = gates.reshape(NSA_GROUPS, GATE_ROWS, tm)


def q_project(x, g, w_in, *, batch, seq, tm):
    d = x.shape[1]
    hd = NSA_HEADS * NSA_D
    per_group = NSA_REP * 3
    wg = w_in[:, hd:].reshape(d, NSA_GROUPS, per_group)
    wg = jnp.pad(wg, ((0, 0), (0, 0), (0, GATE_ROWS - per_group))).reshape(d, NSA_GROUPS * GATE_ROWS)
    wT = jnp.concatenate([w_in[:, :hd], wg], axis=1).T.astype(BF16)
    nt = seq // tm
    return pl.pallas_call(
        _qproj_body,
        out_shape=(
            jax.ShapeDtypeStruct((batch, NSA_HEADS, NSA_D, seq), BF16),
            jax.ShapeDtypeStruct((batch, NSA_GROUPS, GATE_ROWS, seq), F32),
        ),
        grid=(batch, nt),
        in_specs=[
            pl.BlockSpec((tm, d), lambda b, i: (b * nt + i, 0)),
            pl.BlockSpec((1, d), lambda b, i: (0, 0)),
            pl.BlockSpec(wT.shape, lambda b, i: (0, 0)),
        ],
        out_specs=(
            pl.BlockSpec((1, NSA_HEADS, NSA_D, tm), lambda b, i: (b, 0, 0, i)),
            pl.BlockSpec((1, NSA_GROUPS, GATE_ROWS, tm), lambda b, i: (b, 0, 0, i)),
        ),
        compiler_params=_cparams(("parallel", "parallel"), VMEM_LIMIT),
        name="q_project",
    )(x, g.reshape(1, d), wT)


CMP_CHUNK = 256
CMP_PER_QTILE = Q_TILE // CMP_STRIDE
SEL_PER_QTILE = Q_TILE // SEL_LEN
LANES = NSA_REP * Q_TILE
TILES_PER_GROUP = SEL_PER_GROUP * SEL_LEN // K_TILE
BIAS_LINE = 1024
WIN_TILES = (WINDOW + Q_TILE) // K_TILE
FAR_UNROLL = 2
SEL_ROW_STEP = 64


def _nsa_body(qT_ref, gT_ref, kc_ref, cmpL_ref, ks_ref, vsT_ref, kw_ref, vwT_ref,
              lines_ref, oT_ref,
              s_ref, imp_ref, msel_ref, qaug_ref, acc_ref, m_ref, out_ref, bc_ref, bs_ref, bw_ref, cacc_ref):
    i = pl.program_id(2)
    ns = imp_ref.shape[0]
    qT = jnp.concatenate([qT_ref[0, r] for r in range(NSA_REP)], axis=1)
    tiny = jnp.finfo(F32).tiny

    @pl.when(i == 0)
    def _():
        def expand(kind, r, rows, stride):
            line = lines_ref[0, kind, r:r + 1, :]
            shifted = pltpu.roll(jnp.broadcast_to(line, (rows, line.shape[1])), 0, 1,
                                 stride=stride, stride_axis=0)
            return shifted[:, 0:Q_TILE]

        for r in range(NSA_REP):
            cols = slice(r * Q_TILE, (r + 1) * Q_TILE)
            bc_ref[:, cols] = expand(0, r, bc_ref.shape[0], CMP_STRIDE)
            bs_ref[:, cols] = expand(1, r, bs_ref.shape[0], 1)
            bw_ref[:, cols] = expand(2, r, bw_ref.shape[0], 1)

    def gate_row(j):
        return jnp.concatenate([gT_ref[0, 0, 3 * r + j:3 * r + j + 1, :] for r in range(NSA_REP)], axis=1)

    nchunks = i // (CMP_CHUNK // CMP_PER_QTILE) + 1
    visible = CMP_PER_QTILE * (i + 1)

    def chunk_rows(c, count=1):
        return pl.ds(pl.multiple_of(c * CMP_CHUNK, CMP_CHUNK), count * CMP_CHUNK)

    def for_chunks(body):
        def pair(pr, carry):
            body(2 * pr, 2)
            return carry

        lax.fori_loop(0, nchunks // 2, pair, 0)

        @pl.when(nchunks % 2 == 1)
        def _():
            body(nchunks - 1, 1)

    def cmp_scores(c, count):
        s_ref[chunk_rows(c, count), :] = jnp.dot(kc_ref[0, 0, 0, chunk_rows(c, count), :], qT,
                                                 preferred_element_type=F32)

    for_chunks(cmp_scores)

    @pl.when(i == 0)
    def _():
        s_ref[0:CMP_PER_QTILE, :] = s_ref[0:CMP_PER_QTILE, :] + bc_ref[CMP_PER_QTILE:2 * CMP_PER_QTILE, :]

    @pl.when(i > 0)
    def _():
        rows = pl.ds(pl.multiple_of(CMP_PER_QTILE * (i - 1), CMP_PER_QTILE), 2 * CMP_PER_QTILE)
        s_ref[rows, :] = s_ref[rows, :] + bc_ref[...]

    cacc_ref[...] = jnp.zeros_like(cacc_ref)
    m_ref[...] = jnp.full(m_ref.shape, M_INIT, F32)

    def cmp_step(c, count):
        rid = c * CMP_CHUNK + lax.broadcasted_iota(jnp.int32, (count * CMP_CHUNK, LANES), 0)
        s = jnp.where(rid < visible, s_ref[chunk_rows(c, count), :], NEG_BIG)
        m_old = m_ref[0:1, :]
        m_new = jnp.maximum(m_old, jnp.max(s, axis=0, keepdims=True))
        alpha = jnp.exp2(m_old - m_new)
        p = jnp.exp2(s - m_new).astype(BF16)
        left = jnp.concatenate([cmpL_ref[0, 0, c + j] for j in range(count)], axis=1)
        cacc_ref[...] = alpha * cacc_ref[...] + jnp.dot(left, p, preferred_element_type=F32)
        m_ref[0:1, :] = m_new

    for_chunks(cmp_step)
    sees_block = m_ref[0:1, :] > 0.5 * NEG_BIG
    inv_c = jnp.where(sees_block, 1.0 / jnp.maximum(cacc_ref[NSA_D:NSA_D + 1, :], tiny), 0.0)
    out_ref[...] = (gate_row(0) * inv_c) * cacc_ref[0:NSA_D, :]
    weights = cacc_ref[V_ROWS:V_ROWS + ns, :] * inv_c
    imp = weights[:, 0:Q_TILE]
    for r in range(1, NSA_REP):
        imp = imp + weights[:, r * Q_TILE:(r + 1) * Q_TILE]
    imp_ref[...] = imp

    qaug_ref[0:NSA_D, :] = qT
    qaug_ref[NSA_D:, :] = jnp.zeros((qaug_ref.shape[0] - NSA_D, LANES), BF16)

    def reset():
        acc_ref[...] = jnp.zeros_like(acc_ref)
        m_ref[...] = jnp.full(m_ref.shape, M_INIT, F32)

    def tile_rows(j):
        return slice(j * K_TILE, (j + 1) * K_TILE)

    def scores_to_scratch(k_ref, first_tile, count, slot):
        rows = pl.ds(pl.multiple_of(first_tile * K_TILE, K_TILE), count * K_TILE)
        s_ref[slot * K_TILE:(slot + count) * K_TILE, :] = jnp.dot(
            k_ref[0, 0, rows, :], qaug_ref[...], preferred_element_type=F32)

    def softmax_from_scratch(vT_ref, tiles, slots):
        rows = slice(slots[0] * K_TILE, (slots[-1] + 1) * K_TILE)
        m_old = m_ref[0:1, :]
        m_new = jnp.maximum(m_old, jnp.max(s_ref[rows, :], axis=0, keepdims=True))
        alpha = jnp.exp2(m_old - m_new)
        p = jnp.exp2(s_ref[rows, :] - m_new).astype(BF16)
        vT = jnp.concatenate([vT_ref[0, 0, kt] for kt in tiles], axis=1)
        acc_ref[...] = alpha * acc_ref[...] + jnp.dot(vT, p, preferred_element_type=F32)
        m_ref[0:1, :] = m_new

    def finish(j):
        l = jnp.maximum(acc_ref[NSA_D:NSA_D + 1, :], tiny)
        out_ref[...] += (gate_row(j) * (1.0 / l)) * acc_ref[0:NSA_D, :]

    def window_branch():
        win_slots = [TILES_PER_GROUP + j for j in range(WIN_TILES)]
        win_tiles = [jnp.maximum(i - (WIN_TILES - 1) + j, 0) for j in range(WIN_TILES)]
        reset()
        for j, (kt, slot) in enumerate(zip(win_tiles, win_slots)):
            scores_to_scratch(kw_ref, kt, 1, slot)
            before_start = jnp.where(i - (WIN_TILES - 1) + j < 0, NEG_BIG, 0.0).astype(F32)
            s_ref[tile_rows(slot), :] = s_ref[tile_rows(slot), :] + (bw_ref[tile_rows(j), :] + before_start)
        softmax_from_scratch(vwT_ref, win_tiles, win_slots)
        finish(2)

    def select_blocks(nrows):
        jrow = lax.broadcasted_iota(jnp.int32, (nrows, Q_TILE), 0)
        col = lax.broadcasted_iota(jnp.int32, (nrows, Q_TILE), 1)
        cur = SEL_PER_QTILE * i + col // SEL_LEN
        valid = jrow <= cur
        forced = (jrow == 0) | (jrow == cur) | (jrow == cur - 1)
        free = float(SEL_TOPK - 3)
        candidate = valid & jnp.logical_not(forced)
        v0 = jnp.where(candidate, imp_ref[0:nrows, :], -jnp.inf)

        def strip_max(carry):
            v, taken, theta, above = carry
            best = jnp.max(v, axis=0, keepdims=True)
            hit = v == best
            now = taken + jnp.sum(jnp.where(hit, 1.0, 0.0), axis=0, keepdims=True)
            crossed = (taken < free) & (now >= free)
            return (jnp.where(hit, -jnp.inf, v), now,
                    jnp.where(crossed, best, theta), jnp.where(crossed, taken, above))

        zero_row = jnp.zeros((1, Q_TILE), F32)
        carry = (v0, zero_row, jnp.full((1, Q_TILE), jnp.inf, F32), zero_row)
        for _ in range(SEL_TOPK - 3):
            carry = strip_max(carry)
        _, _, theta, above = carry
        v0 = jnp.where(candidate, imp_ref[0:nrows, :], -jnp.inf)
        tied = v0 == theta
        lower = jnp.where(lax.broadcasted_iota(jnp.int32, (nrows, nrows), 1)
                          < lax.broadcasted_iota(jnp.int32, (nrows, nrows), 0), 1.0, 0.0).astype(BF16)
        rank = jnp.dot(lower, jnp.where(tied, 1.0, 0.0).astype(BF16), preferred_element_type=F32)
        chosen = forced | (v0 > theta) | (tied & (rank < free - above))
        mask_bias = jnp.where(chosen & valid, 0.0, NEG_BIG).astype(BF16)
        msel_ref[0:nrows, :] = jnp.concatenate([mask_bias] * NSA_REP, axis=1)
        if nrows < ns:
            msel_ref[nrows:ns, :] = jnp.full((ns - nrows, LANES), NEG_BIG, BF16)

    last = i // TILES_PER_GROUP
    j_i = i % TILES_PER_GROUP

    half = TILES_PER_GROUP // 2

    def sel_tiles(grp, h):
        return [grp * TILES_PER_GROUP + h * half + j for j in range(half)]

    def sel_slots(h):
        return [h * half + j for j in range(half)]

    def set_slab(grp):
        slab = pl.ds(pl.multiple_of(grp * SEL_PER_GROUP, SEL_PER_GROUP), SEL_PER_GROUP)
        qaug_ref[NSA_D:NSA_D + SEL_PER_GROUP, :] = msel_ref[slab, :]

    def sel_scores(grp, h):
        set_slab(grp)
        scores_to_scratch(ks_ref, sel_tiles(grp, h)[0], half, h * half)

    def sel_softmax(grp, h):
        softmax_from_scratch(vsT_ref, sel_tiles(grp, h), sel_slots(h))

    def add_bias(slot, count, bias_tile):
        rows = slice(slot * K_TILE, (slot + count) * K_TILE)
        s_ref[rows, :] = s_ref[rows, :] + bs_ref[bias_tile * K_TILE:(bias_tile + count) * K_TILE, :]

    row_steps = list(range(SEL_ROW_STEP, ns, SEL_ROW_STEP)) + [ns]
    for idx, nrows in enumerate(row_steps):
        lo = 0 if idx == 0 else row_steps[idx - 1] // SEL_PER_QTILE
        hi = nrows // SEL_PER_QTILE
        in_range = (i >= lo) if idx == len(row_steps) - 1 else ((i >= lo) & (i < hi))

        @pl.when(in_range)
        def _(nrows=nrows):
            window_branch()
            select_blocks(nrows)
            reset()
            sel_scores(0, 0)

    def far(grp):
        sel_scores(grp, 1)
        sel_softmax(grp, 0)
        sel_scores(grp + 1, 0)
        sel_softmax(grp, 1)

    def far_many(trip, carry):
        for u in range(FAR_UNROLL):
            far(FAR_UNROLL * trip + u)
        return carry

    prev_is_near = (last >= 1) & (j_i == 0)
    n_far = jnp.where(prev_is_near, last - 1, last)
    lax.fori_loop(0, n_far // FAR_UNROLL, far_many, 0)

    def far_rest(grp, carry):
        far(grp)
        return carry

    lax.fori_loop((n_far // FAR_UNROLL) * FAR_UNROLL, n_far, far_rest, 0)

    @pl.when(prev_is_near)
    def _():
        sel_scores(last - 1, 1)
        sel_softmax(last - 1, 0)
        add_bias(TILES_PER_GROUP - 1, 1, 0)
        sel_scores(last, 0)
        sel_softmax(last - 1, 1)

    @pl.when(j_i == 0)
    def _():
        add_bias(0, 1, 1)

    @pl.when(j_i == 1)
    def _():
        add_bias(0, 2, 0)

    @pl.when(j_i == 2)
    def _():
        add_bias(1, 1, 0)

    @pl.when(j_i >= half)
    def _():
        sel_scores(last, 1)
        sel_softmax(last, 0)

        @pl.when(j_i == 2)
        def _():
            add_bias(2, 1, 1)

        @pl.when(j_i == 3)
        def _():
            add_bias(2, 2, 0)

        sel_softmax(last, 1)

    @pl.when(j_i < half)
    def _():
        sel_softmax(last, 0)

    finish(1)

    for r in range(NSA_REP):
        oT_ref[0, r * NSA_D:(r + 1) * NSA_D, :] = out_ref[:, r * Q_TILE:(r + 1) * Q_TILE].astype(oT_ref.dtype)


def nsa_attention(qT, gT, cmp_nat, cmp_tr, ks, vsT, kw, vwT, lines, mmapT, *, batch, seq):
    G = NSA_GROUPS
    nc = seq // CMP_STRIDE
    ns = seq // SEL_LEN
    nq = seq // Q_TILE
    nkt = seq // K_TILE
    hd = NSA_HEADS * NSA_D
    nch = nc // CMP_CHUNK
    vcT = cmp_tr[1].reshape(batch, G, NSA_D, nch, CMP_CHUNK).transpose(0, 1, 3, 2, 4)
    ones = jnp.zeros((V_ROWS - NSA_D, CMP_CHUNK), BF16).at[0].set(1.0)
    cmpL = jnp.concatenate([
        vcT,
        jnp.broadcast_to(ones, (batch, G, nch, V_ROWS - NSA_D, CMP_CHUNK)),
        jnp.broadcast_to(mmapT, (batch, G, nch, ns, CMP_CHUNK)),
    ], axis=3)
    once = pl.Buffered(1)
    return pl.pallas_call(
        _nsa_body,
        out_shape=jax.ShapeDtypeStruct((batch, hd, seq), BF16),
        grid=(batch, G, nq),
        in_specs=[
            pl.BlockSpec((1, NSA_REP, NSA_D, Q_TILE), lambda b, g, i: (b, g, 0, i)),
            pl.BlockSpec((1, 1, GATE_ROWS, Q_TILE), lambda b, g, i: (b, g, 0, i)),
            pl.BlockSpec((1, 1, 1, nc, NSA_D), lambda b, g, i: (0, b, g, 0, 0)),
            pl.BlockSpec((1, 1, nch, V_ROWS + ns, CMP_CHUNK), lambda b, g, i: (b, g, 0, 0, 0)),
            pl.BlockSpec((1, 1, seq, K_LANES), lambda b, g, i: (b, g, 0, 0), pipeline_mode=once),
            pl.BlockSpec((1, 1, nkt, V_ROWS, K_TILE), lambda b, g, i: (b, g, 0, 0, 0), pipeline_mode=once),
            pl.BlockSpec((1, 1, seq, K_LANES), lambda b, g, i: (b, g, 0, 0), pipeline_mode=once),
            pl.BlockSpec((1, 1, nkt, V_ROWS, K_TILE), lambda b, g, i: (b, g, 0, 0, 0), pipeline_mode=once),
            pl.BlockSpec((1, 3, NSA_REP, BIAS_LINE), lambda b, g, i: (g, 0, 0, 0)),
        ],
        out_specs=pl.BlockSpec((1, NSA_REP * NSA_D, Q_TILE), lambda b, g, i: (b, g, i)),
        scratch_shapes=[
            pltpu.VMEM((max(nc, (TILES_PER_GROUP + WIN_TILES) * K_TILE), LANES), F32),
            pltpu.VMEM((ns, Q_TILE), F32),
            pltpu.VMEM((ns, LANES), BF16),
            pltpu.VMEM((K_LANES, LANES), BF16),
            pltpu.VMEM((V_ROWS, LANES), F32),
            pltpu.VMEM((8, LANES), F32),
            pltpu.VMEM((NSA_D, LANES), F32),
            pltpu.VMEM((2 * CMP_PER_QTILE, LANES), F32),
            pltpu.VMEM((2 * K_TILE, LANES), F32),
            pltpu.VMEM((3 * K_TILE, LANES), F32),
            pltpu.VMEM((V_ROWS + ns, LANES), F32),
        ],
        compiler_params=_cparams(("parallel", "parallel", "arbitrary"), VMEM_LIMIT),
        name="nsa_attention",
    )(qT, gT, cmp_nat, cmpL, ks, vsT, kw, vwT, lines)


def _t5_bucket_table(n_max):
    n = np.arange(n_max)
    max_exact = REL_BUCKETS // 2
    nf = np.maximum(n, 1).astype(np.float32)
    large = max_exact + (np.log(nf / np.float32(max_exact))
                         / np.float32(math.log(REL_MAX_DIST / max_exact))
                         * np.float32(REL_BUCKETS - max_exact)).astype(np.int32)
    large = np.minimum(large, REL_BUCKETS - 1)
    return np.where(n < max_exact, n, large).astype(np.int32)


def _bias_tables(rel_bias):
    n_max = WINDOW + Q_TILE
    buckets = _t5_bucket_table(n_max)
    assert (buckets[Q_TILE - CMP_LEN + 1:] == REL_BUCKETS - 1).all()
    tab = rel_bias.astype(F32)[buckets, :] - rel_bias.astype(F32)[REL_BUCKETS - 1][None, :]
    tab = tab.T * LOG2_E
    def line(rows, stride, offset, ok, fill):
        assert Q_TILE + stride * (rows - 1) <= BIAS_LINE
        k = np.arange(BIAS_LINE)
        dist = np.where(k < Q_TILE, k, k - BIAS_LINE) + offset
        vals = jnp.where(ok(dist)[None], tab[:, np.clip(dist, 0, n_max - 1)], fill)
        return vals.reshape(NSA_GROUPS, NSA_REP, BIAS_LINE)

    return jnp.stack([
        line(2 * CMP_PER_QTILE, CMP_STRIDE, Q_TILE - CMP_LEN + 1, lambda dd: dd >= 0, NEG_BIG),
        line(2 * K_TILE, 1, K_TILE, lambda dd: dd >= 0, NEG_BIG),
        line(3 * K_TILE, 1, 2 * K_TILE, lambda dd: (dd >= 0) & (dd < WINDOW), NEG_BIG),
    ], axis=1)


def _selection_map(seq):
    nc = seq // CMP_STRIDE
    ns = seq // SEL_LEN
    n_cmp = (seq - CMP_LEN) // CMP_STRIDE + 1
    ratio = SEL_LEN // CMP_STRIDE
    lead = CMP_LEN // CMP_STRIDE - 1
    j = np.arange(ns)[:, None]
    n = np.arange(nc)[None, :]
    m = ((n >= ratio * j - lead) & (n < ratio * j + ratio) & (n < n_cmp)).astype(np.float32)
    m = m.reshape(ns, nc // CMP_CHUNK, CMP_CHUNK).transpose(1, 0, 2)
    return jnp.asarray(m, dtype=BF16)


def kernel(x, mix_norm_pre, mix_norm_post, ffn_norm_pre, ffn_norm_post, ffn_w_in, ffn_w_out,
           ret_w_in, ret_w_out, kv_norm, kv_w, cmp_pe_k, cmp_w1_k, cmp_w2_k,
           cmp_pe_v, cmp_w1_v, cmp_w2_v, nsa_w_in, nsa_w_out, rel_bias):
    batch, seq, d = x.shape
    n_ret = ret_w_in.shape[0]
    n_nsa = nsa_w_in.shape[0]
    assert seq % (SEL_PER_GROUP * SEL_LEN) == 0 and seq // SEL_LEN >= SEL_TOPK
    h = x.reshape(batch * seq, d)
    shared = None
    for layer in range(n_ret + n_nsa):
        if layer == n_ret:
            kc_nat, vc_nat, ks, kw, vsT, vwT = kv_project(h, kv_norm, kv_w, batch=batch, seq=seq, tm=TM_KV)
            cmp_nat, cmp_tr = compress_blocks(kc_nat, vc_nat, cmp_pe_k, cmp_w1_k, cmp_w2_k,
                                              cmp_pe_v, cmp_w1_v, cmp_w2_v, batch=batch, seq=seq)
            shared = (cmp_nat, cmp_tr, ks, vsT, kw, vwT, _bias_tables(rel_bias), _selection_map(seq))
        if layer < n_ret:
            w_in = _deinterleave_qk_columns(ret_w_in[layer]).astype(BF16)
            proj = norm_matmul(h, mix_norm_pre[layer], w_in, tm=TM_PROJ, tn=TN_PROJ, out_dtype=BF16)
            mixed = retention_core(proj, *_retention_tables(seq), batch=batch, seq=seq)
            w_o, feature_major = ret_w_out[layer], False
        else:
            j = layer - n_ret
            qT, gT = q_project(h, mix_norm_pre[layer], nsa_w_in[j], batch=batch, seq=seq, tm=TM_Q)
            mixed = nsa_attention(qT, gT, *shared, batch=batch, seq=seq)
            w_o, feature_major = nsa_w_out[j], True
        h = mix_out_ffn(mixed, w_o.astype(BF16), mix_norm_post[layer], h, ffn_norm_pre[layer],
                        ffn_w_in[layer].astype(BF16), ffn_w_out[layer].astype(BF16), ffn_norm_post[layer],
                        batch=batch, seq=seq, tm=TM_FFN, chunks=FFN_CHUNKS, feature_major=feature_major)
    return h.reshape(batch, seq, d)
```

```python
import functools
import math

import numpy as np
import jax
import jax.numpy as jnp
from jax import lax
from jax.experimental import pallas as pl
from jax.experimental.pallas import tpu as pltpu

F32 = jnp.float32
BF16 = jnp.bfloat16

D_MODEL = 1024
RMS_EPS = 1e-6

RET_HEADS = 4
RET_QK = 256
RET_V = 512
RET_CHUNK = 128
RET_CHUNKS_PER_STEP = 4

FFN_HIDDEN = 2816

NSA_HEADS = 16
NSA_GROUPS = 4
NSA_REP = 4
NSA_D = 64
CMP_LEN = 32
CMP_STRIDE = 16
CMP_HIDDEN = 256
SEL_LEN = 64
SEL_TOPK = 16
WINDOW = 512
REL_BUCKETS = 32
REL_MAX_DIST = 128

Q_TILE = 256
K_TILE = 256
SEL_PER_GROUP = 16
V_ROWS = 80
LOG2_E = math.log2(math.e)
NEG_BIG = -(2.0 ** 100)
M_INIT = -(2.0 ** 120)

LANE_TILE = 128
K_LANES = LANE_TILE
VMEM_V7X = 64 * 1024 * 1024
VMEM_LIMIT = VMEM_V7X - 8 * 1024 * 1024

TM_PROJ, TN_PROJ = 512, 2048
TM_KV = 1024
TM_Q = 1024
TM_FFN, FFN_CHUNKS = 512, 2


def _cparams(sem, vmem=None, flags=None):
    return pltpu.CompilerParams(dimension_semantics=sem, vmem_limit_bytes=vmem, flags=flags)


def _rms(x, g):
    return x * lax.rsqrt(jnp.mean(x * x, axis=-1, keepdims=True) + RMS_EPS) * g


def _norm_matmul_body(x_ref, g_ref, w_ref, o_ref, *, tn):
    xn = _rms(x_ref[...], g_ref[...]).astype(BF16)
    for c in range(w_ref.shape[1] // tn):
        cols = slice(c * tn, (c + 1) * tn)
        o_ref[:, cols] = jnp.dot(xn, w_ref[:, cols], preferred_element_type=F32).astype(o_ref.dtype)


def norm_matmul(x, g, w, *, tm, tn, out_dtype=F32):
    t, d = x.shape
    n = w.shape[1]
    assert n % tn == 0
    return pl.pallas_call(
        functools.partial(_norm_matmul_body, tn=tn),
        out_shape=jax.ShapeDtypeStruct((t, n), out_dtype),
        grid=(t // tm,),
        in_specs=[
            pl.BlockSpec((tm, d), lambda i: (i, 0)),
            pl.BlockSpec((1, d), lambda i: (0, 0)),
            pl.BlockSpec((d, n), lambda i: (0, 0), pipeline_mode=pl.Buffered(1)),
        ],
        out_specs=pl.BlockSpec((tm, n), lambda i: (i, 0)),
        compiler_params=_cparams(("parallel",), VMEM_LIMIT),
        name="norm_matmul",
    )(x, g.reshape(1, d), w)


def _mix_out_ffn_body(y_ref, wp_ref, gmix_ref, res_ref, gpre_ref, win_ref, wo_ref, gpost_ref, o_ref,
                      *, chunks, feature_major):
    if feature_major:
        z = lax.dot_general(y_ref[0], wp_ref[...], (((0,), (0,)), ((), ())), preferred_element_type=F32)
    else:
        z = jnp.dot(y_ref[...], wp_ref[...], preferred_element_type=F32)
    x = res_ref[...] + _rms(z, gmix_ref[...])
    xn = _rms(x, gpre_ref[...]).astype(BF16)
    hdim = wo_ref.shape[0]
    th = hdim // chunks
    y = None
    for c in range(chunks):
        gate = jnp.dot(xn, win_ref[:, c * th:(c + 1) * th], preferred_element_type=F32)
        up = jnp.dot(xn, win_ref[:, hdim + c * th:hdim + (c + 1) * th], preferred_element_type=F32)
        act = (gate * jax.nn.sigmoid(gate) * up).astype(BF16)
        part = jnp.dot(act, wo_ref[c * th:(c + 1) * th, :], preferred_element_type=F32)
        y = part if y is None else y + part
    o_ref[...] = x + _rms(y, gpost_ref[...])


def mix_out_ffn(y, w_proj, g_mix, res, g_pre, w_in, w_out, g_post, *, batch, seq, tm, chunks, feature_major):
    k, d = w_proj.shape
    hdim = w_out.shape[0]
    assert (hdim // chunks) % LANE_TILE == 0 and hdim % chunks == 0
    nt = seq // tm
    once = pl.Buffered(1)
    row = lambda b, i: (b * nt + i, 0)
    const = lambda b, i: (0, 0)
    if feature_major:
        y_spec = pl.BlockSpec((1, k, tm), lambda b, i: (b, 0, i))
    else:
        y_spec = pl.BlockSpec((tm, k), row)
    return pl.pallas_call(
        functools.partial(_mix_out_ffn_body, chunks=chunks, feature_major=feature_major),
        out_shape=jax.ShapeDtypeStruct((batch * seq, d), F32),
        grid=(batch, nt),
        in_specs=[
            y_spec,
            pl.BlockSpec((k, d), const, pipeline_mode=once),
            pl.BlockSpec((1, d), const),
            pl.BlockSpec((tm, d), row),
            pl.BlockSpec((1, d), const),
            pl.BlockSpec((d, 2 * hdim), const, pipeline_mode=once),
            pl.BlockSpec((hdim, d), const, pipeline_mode=once),
            pl.BlockSpec((1, d), const),
        ],
        out_specs=pl.BlockSpec((tm, d), row),
        compiler_params=_cparams(("parallel", "parallel"), VMEM_LIMIT),
        name="mix_out_ffn",
    )(y, w_proj, g_mix.reshape(1, d), res, g_pre.reshape(1, d), w_in, w_out, g_post.reshape(1, d))


def _retention_body(q_ref, k_ref, v_ref, g_ref, cos_ref, sin_ref, dmat_ref, qdec_ref, kdec_ref,
                    cdec_ref, o_ref, state_ref):
    @pl.when(pl.program_id(1) == 0)
    def _():
        state_ref[...] = jnp.zeros_like(state_ref)

    half = RET_QK // 2

    for sub in range(RET_CHUNKS_PER_STEP):
        rows = slice(sub * RET_CHUNK, (sub + 1) * RET_CHUNK)
        cos = cos_ref[rows, :]
        sin = sin_ref[rows, :]

        def rotate(x_ref, h):
            x1 = x_ref[rows, h * RET_QK:h * RET_QK + half].astype(F32)
            x2 = x_ref[rows, h * RET_QK + half:(h + 1) * RET_QK].astype(F32)
            return jnp.concatenate([x1 * cos - x2 * sin, x1 * sin + x2 * cos], axis=1)

        for h in range(RET_HEADS):
            qr = rotate(q_ref, h)
            kr = rotate(k_ref, h) * (RET_QK ** -0.5)
            v = v_ref[rows, h * RET_V:(h + 1) * RET_V].astype(BF16)
            scores = lax.dot_general(qr.astype(BF16), kr.astype(BF16), (((1,), (1,)), ((), ())),
                                     preferred_element_type=F32) * dmat_ref[h]
            state = state_ref[h]
            o = (jnp.dot(scores.astype(BF16), v, preferred_element_type=F32)
                 + jnp.dot((qr * qdec_ref[h]).astype(BF16), state.astype(BF16),
                           preferred_element_type=F32))
            kd = (kr * kdec_ref[h]).astype(BF16)
            state_ref[h] = state * cdec_ref[h, 0:1, :] + lax.dot_general(
                kd, v, (((0,), (0,)), ((), ())), preferred_element_type=F32)
            o = o * lax.rsqrt(jnp.mean(o * o, axis=-1, keepdims=True) + RMS_EPS)
            gate = g_ref[rows, h * RET_V:(h + 1) * RET_V].astype(F32)
            o_ref[rows, h * RET_V:(h + 1) * RET_V] = (o * (gate * jax.nn.sigmoid(gate))).astype(o_ref.dtype)


def retention_core(proj, cos, sin, dmat, qdec, kdec, cdec, *, batch, seq):
    c = RET_CHUNK * RET_CHUNKS_PER_STEP
    nc = seq // c
    hq = RET_HEADS * RET_QK
    hv = RET_HEADS * RET_V
    return pl.pallas_call(
        _retention_body,
        out_shape=jax.ShapeDtypeStruct((batch * seq, hv), BF16),
        grid=(batch, nc),
        in_specs=[
            pl.BlockSpec((c, hq), lambda b, t: (b * nc + t, 0)),
            pl.BlockSpec((c, hq), lambda b, t: (b * nc + t, 1)),
            pl.BlockSpec((c, hv), lambda b, t: (b * nc + t, 1)),
            pl.BlockSpec((c, hv), lambda b, t: (b * nc + t, 2)),
            pl.BlockSpec((c, RET_QK // 2), lambda b, t: (t, 0)),
            pl.BlockSpec((c, RET_QK // 2), lambda b, t: (t, 0)),
            pl.BlockSpec((RET_HEADS, RET_CHUNK, RET_CHUNK), lambda b, t: (0, 0, 0)),
            pl.BlockSpec((RET_HEADS, RET_CHUNK, RET_QK), lambda b, t: (0, 0, 0)),
            pl.BlockSpec((RET_HEADS, RET_CHUNK, RET_QK), lambda b, t: (0, 0, 0)),
            pl.BlockSpec((RET_HEADS, 8, RET_V), lambda b, t: (0, 0, 0)),
        ],
        out_specs=pl.BlockSpec((c, hv), lambda b, t: (b * nc + t, 0)),
        scratch_shapes=[pltpu.VMEM((RET_HEADS, RET_QK, RET_V), F32)],
        compiler_params=_cparams(("parallel", "arbitrary"), VMEM_LIMIT),
        name="retention_core",
    )(proj, proj, proj, proj, cos, sin, dmat, qdec, kdec, cdec)


def _retention_tables(seq):
    h, dk, c = RET_HEADS, RET_QK, RET_CHUNK
    pos = jnp.arange(seq, dtype=F32)
    theta = 1.0 / (10000.0 ** jnp.linspace(0.0, 1.0, dk // 2, dtype=F32))
    ang = pos[:, None] * theta[None, :]
    log_gamma = jnp.log(1.0 - 2.0 ** (-5.0 - jnp.arange(h, dtype=F32)))
    idx = jnp.arange(c, dtype=F32)
    rel = idx[:, None] - idx[None, :]
    dmat = jnp.where(rel >= 0, jnp.exp(jnp.maximum(rel, 0.0) * log_gamma[:, None, None]), 0.0)
    qdec = jnp.exp((idx + 1.0)[None, :] * log_gamma[:, None])
    kdec = jnp.exp((c - 1.0 - idx)[None, :] * log_gamma[:, None])
    cdec = jnp.exp(c * log_gamma)
    qdec = jnp.broadcast_to(qdec[:, :, None], (h, c, dk))
    kdec = jnp.broadcast_to(kdec[:, :, None], (h, c, dk))
    cdec = jnp.broadcast_to(cdec[:, None, None], (h, 8, RET_V))
    return jnp.cos(ang), jnp.sin(ang), dmat, qdec, kdec, cdec


def _deinterleave_qk_columns(w):
    nqk = 2 * RET_HEADS * RET_QK
    perm = []
    for h in range(2 * RET_HEADS):
        base = h * RET_QK
        perm += [base + 2 * i for i in range(RET_QK // 2)]
        perm += [base + 2 * i + 1 for i in range(RET_QK // 2)]
    perm = np.asarray(perm + list(range(nqk, w.shape[1])), dtype=np.int32)
    return w[:, perm]


def _kv_body(x_ref, g_ref, wk_ref, wvT_ref, kc_ref, vc_ref, ks_ref, kw_ref, vsT_ref, vwT_ref):
    xn = _rms(x_ref[...], g_ref[...]).astype(BF16)
    tm = xn.shape[0]
    kall = jnp.dot(xn, wk_ref[...], preferred_element_type=F32)
    gd = NSA_GROUPS * NSA_D
    kc_ref[...] = kall[:, 0:gd]
    vc_ref[...] = kall[:, gd:2 * gd]
    row = pl.program_id(1) * tm + lax.broadcasted_iota(jnp.int32, (tm, K_LANES), 0)
    lane = lax.broadcasted_iota(jnp.int32, (tm, K_LANES), 1)
    blk = (row // SEL_LEN) % SEL_PER_GROUP
    onehot = jnp.where(lane - NSA_D == blk, 1.0, 0.0).astype(F32)
    for g in range(NSA_GROUPS):
        ks = kall[:, 2 * gd + K_LANES * g:2 * gd + K_LANES * (g + 1)]
        ks_ref[0, g] = (ks + onehot).astype(BF16)
        kw_base = 2 * gd + NSA_GROUPS * K_LANES
        kw = kall[:, kw_base + K_LANES * g:kw_base + K_LANES * (g + 1)]
        kw_ref[0, g] = kw.astype(BF16)
    vT = lax.dot_general(wvT_ref[...], xn, (((1,), (1,)), ((), ())),
                         preferred_element_type=F32)
    extra = jnp.where(lax.broadcasted_iota(jnp.int32, (V_ROWS - NSA_D, K_TILE), 0) == 0, 1.0, 0.0)
    extra = extra.astype(BF16)
    for g in range(NSA_GROUPS):
        for c in range(tm // K_TILE):
            cols = slice(c * K_TILE, (c + 1) * K_TILE)
            vsT_ref[0, g, c, 0:NSA_D, :] = vT[NSA_D * g:NSA_D * (g + 1), cols].astype(BF16)
            vsT_ref[0, g, c, NSA_D:V_ROWS, :] = extra
            vwT_ref[0, g, c, 0:NSA_D, :] = vT[gd + NSA_D * g:gd + NSA_D * (g + 1), cols].astype(BF16)
            vwT_ref[0, g, c, NSA_D:V_ROWS, :] = extra


def kv_project(x, g, kv_w, *, batch, seq, tm):
    d = x.shape[1]
    gd = NSA_GROUPS * NSA_D
    k_c, v_c, k_s, v_s, k_w, v_w = [kv_w[:, i * gd:(i + 1) * gd] for i in range(6)]

    def pad_groups(w):
        w = w.reshape(d, NSA_GROUPS, NSA_D)
        return jnp.pad(w, ((0, 0), (0, 0), (0, K_LANES - NSA_D))).reshape(d, NSA_GROUPS * K_LANES)

    wk = jnp.concatenate([k_c, v_c, pad_groups(k_s), pad_groups(k_w)], axis=1).astype(BF16)
    wvT = jnp.concatenate([v_s, v_w], axis=1).T.astype(BF16)
    nt = seq // tm
    G = NSA_GROUPS
    return pl.pallas_call(
        _kv_body,
        out_shape=(
            jax.ShapeDtypeStruct((batch * seq, gd), F32),
            jax.ShapeDtypeStruct((batch * seq, gd), F32),
            jax.ShapeDtypeStruct((batch, G, seq, K_LANES), BF16),
            jax.ShapeDtypeStruct((batch, G, seq, K_LANES), BF16),
            jax.ShapeDtypeStruct((batch, G, seq // K_TILE, V_ROWS, K_TILE), BF16),
            jax.ShapeDtypeStruct((batch, G, seq // K_TILE, V_ROWS, K_TILE), BF16),
        ),
        grid=(batch, nt),
        in_specs=[
            pl.BlockSpec((tm, d), lambda b, i: (b * nt + i, 0)),
            pl.BlockSpec((1, d), lambda b, i: (0, 0)),
            pl.BlockSpec(wk.shape, lambda b, i: (0, 0)),
            pl.BlockSpec(wvT.shape, lambda b, i: (0, 0)),
        ],
        out_specs=(
            pl.BlockSpec((tm, gd), lambda b, i: (b * nt + i, 0)),
            pl.BlockSpec((tm, gd), lambda b, i: (b * nt + i, 0)),
            pl.BlockSpec((1, G, tm, K_LANES), lambda b, i: (b, 0, i, 0)),
            pl.BlockSpec((1, G, tm, K_LANES), lambda b, i: (b, 0, i, 0)),
            pl.BlockSpec((1, G, tm // K_TILE, V_ROWS, K_TILE), lambda b, i: (b, 0, i, 0, 0)),
            pl.BlockSpec((1, G, tm // K_TILE, V_ROWS, K_TILE), lambda b, i: (b, 0, i, 0, 0)),
        ),
        compiler_params=_cparams(("parallel", "parallel"), VMEM_LIMIT),
        name="kv_project",
    )(x, g.reshape(1, d), wk, wvT)


def _compress_body(c_ref, pe_ref, w1_ref, w2_ref, w2T_ref, nat_ref, tr_ref, sh_ref):
    half = CMP_STRIDE * NSA_D
    nc = c_ref.shape[3]
    c = c_ref[0, 0, 0].astype(BF16)
    w1 = w1_ref[0]
    first = jnp.dot(c, w1[0:half], preferred_element_type=F32)
    second = jnp.dot(c, w1[half:2 * half], preferred_element_type=F32)
    pe_term = jnp.dot(pe_ref[0].astype(BF16), w1, preferred_element_type=F32)
    sh_ref[0:nc, :] = second
    sh_ref[nc:nc + 8, :] = jnp.zeros((8, CMP_HIDDEN), F32)
    pre = first + sh_ref[1:nc + 1, :] + pe_term[0:1, :]
    hid = (pre * jax.nn.sigmoid(pre)).astype(BF16)
    nat_ref[0, 0, 0] = jnp.dot(hid, w2_ref[0], preferred_element_type=F32).astype(BF16)
    tr_ref[0, 0, 0] = lax.dot_general(w2T_ref[0], hid, (((1,), (1,)), ((), ())),
                                      preferred_element_type=F32).astype(BF16)


def compress_blocks(kc_nat, vc_nat, pe_k, w1_k, w2_k, pe_v, w1_v, w2_v, *, batch, seq):
    G, d = NSA_GROUPS, NSA_D
    nc = seq // CMP_STRIDE

    def to_rows(t):
        t = t.reshape(batch, nc, CMP_STRIDE, G, d).transpose(0, 3, 1, 2, 4)
        return t.reshape(batch, G, nc, CMP_STRIDE * d)

    c_all = jnp.stack([to_rows(kc_nat), to_rows(vc_nat)])
    pe = jnp.stack([pe_k.reshape(1, -1), pe_v.reshape(1, -1)])
    pe = jnp.broadcast_to(pe, (2, 8, CMP_LEN * d))
    w1 = jnp.stack([w1_k, w1_v]).astype(BF16)
    w2 = jnp.stack([w2_k, w2_v]).astype(BF16)
    w2T = jnp.stack([w2_k.T, w2_v.T]).astype(BF16)
    return pl.pallas_call(
        _compress_body,
        out_shape=(
            jax.ShapeDtypeStruct((2, batch, G, nc, d), BF16),
            jax.ShapeDtypeStruct((2, batch, G, d, nc), BF16),
        ),
        grid=(2, batch, G),
        in_specs=[
            pl.BlockSpec((1, 1, 1, nc, CMP_STRIDE * d), lambda w, b, g: (w, b, g, 0, 0)),
            pl.BlockSpec((1, 8, CMP_LEN * d), lambda w, b, g: (w, 0, 0)),
            pl.BlockSpec((1, CMP_LEN * d, CMP_HIDDEN), lambda w, b, g: (w, 0, 0)),
            pl.BlockSpec((1, CMP_HIDDEN, d), lambda w, b, g: (w, 0, 0)),
            pl.BlockSpec((1, d, CMP_HIDDEN), lambda w, b, g: (w, 0, 0)),
        ],
        out_specs=(
            pl.BlockSpec((1, 1, 1, nc, d), lambda w, b, g: (w, b, g, 0, 0)),
            pl.BlockSpec((1, 1, 1, d, nc), lambda w, b, g: (w, b, g, 0, 0)),
        ),
        scratch_shapes=[pltpu.VMEM((nc + 8, CMP_HIDDEN), F32)],
        compiler_params=_cparams(("parallel", "parallel", "parallel"), VMEM_LIMIT),
        name="compress_blocks",
    )(c_all, pe, w1, w2, w2T)


GATE_ROWS = 16


def _qproj_body(x_ref, g_ref, wT_ref, qT_ref, gT_ref):
    xn = _rms(x_ref[...], g_ref[...]).astype(BF16)
    tm = xn.shape[0]
    pT = lax.dot_general(wT_ref[...], xn, (((1,), (1,)), ((), ())),
                         preferred_element_type=F32)
    hd = NSA_HEADS * NSA_D
    q = pT[0:hd] * (NSA_D ** -0.5 * LOG2_E)
    qT_ref[0] = q.reshape(NSA_HEADS, NSA_D, tm).astype(BF16)
    gates = jax.nn.sigmoid(pT[hd:hd + NSA_GROUPS * GATE_ROWS])
    gT_ref[0] = gates.reshape(NSA_GROUPS, GATE_ROWS, tm)


def q_project(x, g, w_in, *, batch, seq, tm):
    d = x.shape[1]
    hd = NSA_HEADS * NSA_D
    per_group = NSA_REP * 3
    wg = w_in[:, hd:].reshape(d, NSA_GROUPS, per_group)
    wg = jnp.pad(wg, ((0, 0), (0, 0), (0, GATE_ROWS - per_group))).reshape(d, NSA_GROUPS * GATE_ROWS)
    wT = jnp.concatenate([w_in[:, :hd], wg], axis=1).T.astype(BF16)
    nt = seq // tm
    return pl.pallas_call(
        _qproj_body,
        out_shape=(
            jax.ShapeDtypeStruct((batch, NSA_HEADS, NSA_D, seq), BF16),
            jax.ShapeDtypeStruct((batch, NSA_GROUPS, GATE_ROWS, seq), F32),
        ),
        grid=(batch, nt),
        in_specs=[
            pl.BlockSpec((tm, d), lambda b, i: (b * nt + i, 0)),
            pl.BlockSpec((1, d), lambda b, i: (0, 0)),
            pl.BlockSpec(wT.shape, lambda b, i: (0, 0)),
        ],
        out_specs=(
            pl.BlockSpec((1, NSA_HEADS, NSA_D, tm), lambda b, i: (b, 0, 0, i)),
            pl.BlockSpec((1, NSA_GROUPS, GATE_ROWS, tm), lambda b, i: (b, 0, 0, i)),
        ),
        compiler_params=_cparams(("parallel", "parallel"), VMEM_LIMIT),
        name="q_project",
    )(x, g.reshape(1, d), wT)


CMP_CHUNK = 256
CMP_PER_QTILE = Q_TILE // CMP_STRIDE
SEL_PER_QTILE = Q_TILE // SEL_LEN
LANES = NSA_REP * Q_TILE
TILES_PER_GROUP = SEL_PER_GROUP * SEL_LEN // K_TILE
BIAS_LINE = 1024
WIN_TILES = (WINDOW + Q_TILE) // K_TILE
FAR_UNROLL = 2
SEL_ROW_STEP = 64


def _nsa_body(qT_ref, gT_ref, kc_ref, cmpL_ref, ks_ref, vsT_ref, kw_ref, vwT_ref,
              lines_ref, oT_ref,
              s_ref, imp_ref, msel_ref, qaug_ref, acc_ref, m_ref, out_ref, bc_ref, bs_ref, bw_ref, cacc_ref):
    i = pl.program_id(2)
    ns = imp_ref.shape[0]
    qT = jnp.concatenate([qT_ref[0, r] for r in range(NSA_REP)], axis=1)
    tiny = jnp.finfo(F32).tiny

    @pl.when(i == 0)
    def _():
        def expand(kind, r, rows, stride):
            line = lines_ref[0, kind, r:r + 1, :]
            shifted = pltpu.roll(jnp.broadcast_to(line, (rows, line.shape[1])), 0, 1,
                                 stride=stride, stride_axis=0)
            return shifted[:, 0:Q_TILE]

        for r in range(NSA_REP):
            cols = slice(r * Q_TILE, (r + 1) * Q_TILE)
            bc_ref[:, cols] = expand(0, r, bc_ref.shape[0], CMP_STRIDE)
            bs_ref[:, cols] = expand(1, r, bs_ref.shape[0], 1)
            bw_ref[:, cols] = expand(2, r, bw_ref.shape[0], 1)

    def gate_row(j):
        return jnp.concatenate([gT_ref[0, 0, 3 * r + j:3 * r + j + 1, :] for r in range(NSA_REP)], axis=1)

    nchunks = i // (CMP_CHUNK // CMP_PER_QTILE) + 1
    visible = CMP_PER_QTILE * (i + 1)

    def chunk_rows(c, count=1):
        return pl.ds(pl.multiple_of(c * CMP_CHUNK, CMP_CHUNK), count * CMP_CHUNK)

    def for_chunks(body):
        def pair(pr, carry):
            body(2 * pr, 2)
            return carry

        lax.fori_loop(0, nchunks // 2, pair, 0)

        @pl.when(nchunks % 2 == 1)
        def _():
            body(nchunks - 1, 1)

    def cmp_scores(c, count):
        s_ref[chunk_rows(c, count), :] = jnp.dot(kc_ref[0, 0, 0, chunk_rows(c, count), :], qT,
                                                 preferred_element_type=F32)

    for_chunks(cmp_scores)

    @pl.when(i == 0)
    def _():
        s_ref[0:CMP_PER_QTILE, :] = s_ref[0:CMP_PER_QTILE, :] + bc_ref[CMP_PER_QTILE:2 * CMP_PER_QTILE, :]

    @pl.when(i > 0)
    def _():
        rows = pl.ds(pl.multiple_of(CMP_PER_QTILE * (i - 1), CMP_PER_QTILE), 2 * CMP_PER_QTILE)
        s_ref[rows, :] = s_ref[rows, :] + bc_ref[...]

    cacc_ref[...] = jnp.zeros_like(cacc_ref)
    m_ref[...] = jnp.full(m_ref.shape, M_INIT, F32)

    def cmp_step(c, count):
        rid = c * CMP_CHUNK + lax.broadcasted_iota(jnp.int32, (count * CMP_CHUNK, LANES), 0)
        s = jnp.where(rid < visible, s_ref[chunk_rows(c, count), :], NEG_BIG)
        m_old = m_ref[0:1, :]
        m_new = jnp.maximum(m_old, jnp.max(s, axis=0, keepdims=True))
        alpha = jnp.exp2(m_old - m_new)
        p = jnp.exp2(s - m_new).astype(BF16)
        left = jnp.concatenate([cmpL_ref[0, 0, c + j] for j in range(count)], axis=1)
        cacc_ref[...] = alpha * cacc_ref[...] + jnp.dot(left, p, preferred_element_type=F32)
        m_ref[0:1, :] = m_new

    for_chunks(cmp_step)
    sees_block = m_ref[0:1, :] > 0.5 * NEG_BIG
    inv_c = jnp.where(sees_block, 1.0 / jnp.maximum(cacc_ref[NSA_D:NSA_D + 1, :], tiny), 0.0)
    out_ref[...] = (gate_row(0) * inv_c) * cacc_ref[0:NSA_D, :]
    weights = cacc_ref[V_ROWS:V_ROWS + ns, :] * inv_c
    imp = weights[:, 0:Q_TILE]
    for r in range(1, NSA_REP):
        imp = imp + weights[:, r * Q_TILE:(r + 1) * Q_TILE]
    imp_ref[...] = imp

    qaug_ref[0:NSA_D, :] = qT
    qaug_ref[NSA_D:, :] = jnp.zeros((qaug_ref.shape[0] - NSA_D, LANES), BF16)

    def reset():
        acc_ref[...] = jnp.zeros_like(acc_ref)
        m_ref[...] = jnp.full(m_ref.shape, M_INIT, F32)

    def tile_rows(j):
        return slice(j * K_TILE, (j + 1) * K_TILE)

    def scores_to_scratch(k_ref, first_tile, count, slot):
        rows = pl.ds(pl.multiple_of(first_tile * K_TILE, K_TILE), count * K_TILE)
        s_ref[slot * K_TILE:(slot + count) * K_TILE, :] = jnp.dot(
            k_ref[0, 0, rows, :], qaug_ref[...], preferred_element_type=F32)

    def softmax_from_scratch(vT_ref, tiles, slots):
        rows = slice(slots[0] * K_TILE, (slots[-1] + 1) * K_TILE)
        m_old = m_ref[0:1, :]
        m_new = jnp.maximum(m_old, jnp.max(s_ref[rows, :], axis=0, keepdims=True))
        alpha = jnp.exp2(m_old - m_new)
        p = jnp.exp2(s_ref[rows, :] - m_new).astype(BF16)
        vT = jnp.concatenate([vT_ref[0, 0, kt] for kt in tiles], axis=1)
        acc_ref[...] = alpha * acc_ref[...] + jnp.dot(vT, p, preferred_element_type=F32)
        m_ref[0:1, :] = m_new

    def finish(j):
        l = jnp.maximum(acc_ref[NSA_D:NSA_D + 1, :], tiny)
        out_ref[...] += (gate_row(j) * (1.0 / l)) * acc_ref[0:NSA_D, :]

    def window_branch():
        win_slots = [TILES_PER_GROUP + j for j in range(WIN_TILES)]
        win_tiles = [jnp.maximum(i - (WIN_TILES - 1) + j, 0) for j in range(WIN_TILES)]
        reset()
        for j, (kt, slot) in enumerate(zip(win_tiles, win_slots)):
            scores_to_scratch(kw_ref, kt, 1, slot)
            before_start = jnp.where(i - (WIN_TILES - 1) + j < 0, NEG_BIG, 0.0).astype(F32)
            s_ref[tile_rows(slot), :] = s_ref[tile_rows(slot), :] + (bw_ref[tile_rows(j), :] + before_start)
        softmax_from_scratch(vwT_ref, win_tiles, win_slots)
        finish(2)

    def select_blocks(nrows):
        jrow = lax.broadcasted_iota(jnp.int32, (nrows, Q_TILE), 0)
        col = lax.broadcasted_iota(jnp.int32, (nrows, Q_TILE), 1)
        cur = SEL_PER_QTILE * i + col // SEL_LEN
        valid = jrow <= cur
        forced = (jrow == 0) | (jrow == cur) | (jrow == cur - 1)
        free = float(SEL_TOPK - 3)
        candidate = valid & jnp.logical_not(forced)
        v0 = jnp.where(candidate, imp_ref[0:nrows, :], -jnp.inf)

        def strip_max(carry):
            v, taken, theta, above = carry
            best = jnp.max(v, axis=0, keepdims=True)
            hit = v == best
            now = taken + jnp.sum(jnp.where(hit, 1.0, 0.0), axis=0, keepdims=True)
            crossed = (taken < free) & (now >= free)
            return (jnp.where(hit, -jnp.inf, v), now,
                    jnp.where(crossed, best, theta), jnp.where(crossed, taken, above))

        zero_row = jnp.zeros((1, Q_TILE), F32)
        carry = (v0, zero_row, jnp.full((1, Q_TILE), jnp.inf, F32), zero_row)
        for _ in range(SEL_TOPK - 3):
            carry = strip_max(carry)
        _, _, theta, above = carry
        v0 = jnp.where(candidate, imp_ref[0:nrows, :], -jnp.inf)
        tied = v0 == theta
        lower = jnp.where(lax.broadcasted_iota(jnp.int32, (nrows, nrows), 1)
                          < lax.broadcasted_iota(jnp.int32, (nrows, nrows), 0), 1.0, 0.0).astype(BF16)
        rank = jnp.dot(lower, jnp.where(tied, 1.0, 0.0).astype(BF16), preferred_element_type=F32)
        chosen = forced | (v0 > theta) | (tied & (rank < free - above))
        mask_bias = jnp.where(chosen & valid, 0.0, NEG_BIG).astype(BF16)
        msel_ref[0:nrows, :] = jnp.concatenate([mask_bias] * NSA_REP, axis=1)
        if nrows < ns:
            msel_ref[nrows:ns, :] = jnp.full((ns - nrows, LANES), NEG_BIG, BF16)

    last = i // TILES_PER_GROUP
    j_i = i % TILES_PER_GROUP

    half = TILES_PER_GROUP // 2

    def sel_tiles(grp, h):
        return [grp * TILES_PER_GROUP + h * half + j for j in range(half)]

    def sel_slots(h):
        return [h * half + j for j in range(half)]

    def set_slab(grp):
        slab = pl.ds(pl.multiple_of(grp * SEL_PER_GROUP, SEL_PER_GROUP), SEL_PER_GROUP)
        qaug_ref[NSA_D:NSA_D + SEL_PER_GROUP, :] = msel_ref[slab, :]

    def sel_scores(grp, h):
        set_slab(grp)
        scores_to_scratch(ks_ref, sel_tiles(grp, h)[0], half, h * half)

    def sel_softmax(grp, h):
        softmax_from_scratch(vsT_ref, sel_tiles(grp, h), sel_slots(h))

    def add_bias(slot, count, bias_tile):
        rows = slice(slot * K_TILE, (slot + count) * K_TILE)
        s_ref[rows, :] = s_ref[rows, :] + bs_ref[bias_tile * K_TILE:(bias_tile + count) * K_TILE, :]

    row_steps = list(range(SEL_ROW_STEP, ns, SEL_ROW_STEP)) + [ns]
    for idx, nrows in enumerate(row_steps):
        lo = 0 if idx == 0 else row_steps[idx - 1] // SEL_PER_QTILE
        hi = nrows // SEL_PER_QTILE
        in_range = (i >= lo) if idx == len(row_steps) - 1 else ((i >= lo) & (i < hi))

        @pl.when(in_range)
        def _(nrows=nrows):
            window_branch()
            select_blocks(nrows)
            reset()
            sel_scores(0, 0)

    def far(grp):
        sel_scores(grp, 1)
        sel_softmax(grp, 0)
        sel_scores(grp + 1, 0)
        sel_softmax(grp, 1)

    def far_many(trip, carry):
        for u in range(FAR_UNROLL):
            far(FAR_UNROLL * trip + u)
        return carry

    prev_is_near = (last >= 1) & (j_i == 0)
    n_far = jnp.where(prev_is_near, last - 1, last)
    lax.fori_loop(0, n_far // FAR_UNROLL, far_many, 0)

    def far_rest(grp, carry):
        far(grp)
        return carry

    lax.fori_loop((n_far // FAR_UNROLL) * FAR_UNROLL, n_far, far_rest, 0)

    @pl.when(prev_is_near)
    def _():
        sel_scores(last - 1, 1)
        sel_softmax(last - 1, 0)
        add_bias(TILES_PER_GROUP - 1, 1, 0)
        sel_scores(last, 0)
        sel_softmax(last - 1, 1)

    @pl.when(j_i == 0)
    def _():
        add_bias(0, 1, 1)

    @pl.when(j_i == 1)
    def _():
        add_bias(0, 2, 0)

    @pl.when(j_i == 2)
    def _():
        add_bias(1, 1, 0)

    @pl.when(j_i >= half)
    def _():
        sel_scores(last, 1)
        sel_softmax(last, 0)

        @pl.when(j_i == 2)
        def _():
            add_bias(2, 1, 1)

        @pl.when(j_i == 3)
        def _():
            add_bias(2, 2, 0)

        sel_softmax(last, 1)

    @pl.when(j_i < half)
    def _():
        sel_softmax(last, 0)

    finish(1)

    for r in range(NSA_REP):
        oT_ref[0, r * NSA_D:(r + 1) * NSA_D, :] = out_ref[:, r * Q_TILE:(r + 1) * Q_TILE].astype(oT_ref.dtype)


def nsa_attention(qT, gT, cmp_nat, cmp_tr, ks, vsT, kw, vwT, lines, mmapT, *, batch, seq):
    G = NSA_GROUPS
    nc = seq // CMP_STRIDE
    ns = seq // SEL_LEN
    nq = seq // Q_TILE
    nkt = seq // K_TILE
    hd = NSA_HEADS * NSA_D
    nch = nc // CMP_CHUNK
    vcT = cmp_tr[1].reshape(batch, G, NSA_D, nch, CMP_CHUNK).transpose(0, 1, 3, 2, 4)
    ones = jnp.zeros((V_ROWS - NSA_D, CMP_CHUNK), BF16).at[0].set(1.0)
    cmpL = jnp.concatenate([
        vcT,
        jnp.broadcast_to(ones, (batch, G, nch, V_ROWS - NSA_D, CMP_CHUNK)),
        jnp.broadcast_to(mmapT, (batch, G, nch, ns, CMP_CHUNK)),
    ], axis=3)
    once = pl.Buffered(1)
    return pl.pallas_call(
        _nsa_body,
        out_shape=jax.ShapeDtypeStruct((batch, hd, seq), BF16),
        grid=(batch, G, nq),
        in_specs=[
            pl.BlockSpec((1, NSA_REP, NSA_D, Q_TILE), lambda b, g, i: (b, g, 0, i)),
            pl.BlockSpec((1, 1, GATE_ROWS, Q_TILE), lambda b, g, i: (b, g, 0, i)),
            pl.BlockSpec((1, 1, 1, nc, NSA_D), lambda b, g, i: (0, b, g, 0, 0)),
            pl.BlockSpec((1, 1, nch, V_ROWS + ns, CMP_CHUNK), lambda b, g, i: (b, g, 0, 0, 0)),
            pl.BlockSpec((1, 1, seq, K_LANES), lambda b, g, i: (b, g, 0, 0), pipeline_mode=once),
            pl.BlockSpec((1, 1, nkt, V_ROWS, K_TILE), lambda b, g, i: (b, g, 0, 0, 0), pipeline_mode=once),
            pl.BlockSpec((1, 1, seq, K_LANES), lambda b, g, i: (b, g, 0, 0), pipeline_mode=once),
            pl.BlockSpec((1, 1, nkt, V_ROWS, K_TILE), lambda b, g, i: (b, g, 0, 0, 0), pipeline_mode=once),
            pl.BlockSpec((1, 3, NSA_REP, BIAS_LINE), lambda b, g, i: (g, 0, 0, 0)),
        ],
        out_specs=pl.BlockSpec((1, NSA_REP * NSA_D, Q_TILE), lambda b, g, i: (b, g, i)),
        scratch_shapes=[
            pltpu.VMEM((max(nc, (TILES_PER_GROUP + WIN_TILES) * K_TILE), LANES), F32),
            pltpu.VMEM((ns, Q_TILE), F32),
            pltpu.VMEM((ns, LANES), BF16),
            pltpu.VMEM((K_LANES, LANES), BF16),
            pltpu.VMEM((V_ROWS, LANES), F32),
            pltpu.VMEM((8, LANES), F32),
            pltpu.VMEM((NSA_D, LANES), F32),
            pltpu.VMEM((2 * CMP_PER_QTILE, LANES), F32),
            pltpu.VMEM((2 * K_TILE, LANES), F32),
            pltpu.VMEM((3 * K_TILE, LANES), F32),
            pltpu.VMEM((V_ROWS + ns, LANES), F32),
        ],
        compiler_params=_cparams(("parallel", "parallel", "arbitrary"), VMEM_LIMIT),
        name="nsa_attention",
    )(qT, gT, cmp_nat, cmpL, ks, vsT, kw, vwT, lines)


def _t5_bucket_table(n_max):
    n = np.arange(n_max)
    max_exact = REL_BUCKETS // 2
    nf = np.maximum(n, 1).astype(np.float32)
    large = max_exact + (np.log(nf / np.float32(max_exact))
                         / np.float32(math.log(REL_MAX_DIST / max_exact))
                         * np.float32(REL_BUCKETS - max_exact)).astype(np.int32)
    large = np.minimum(large, REL_BUCKETS - 1)
    return np.where(n < max_exact, n, large).astype(np.int32)


def _bias_tables(rel_bias):
    n_max = WINDOW + Q_TILE
    buckets = _t5_bucket_table(n_max)
    assert (buckets[Q_TILE - CMP_LEN + 1:] == REL_BUCKETS - 1).all()
    tab = rel_bias.astype(F32)[buckets, :] - rel_bias.astype(F32)[REL_BUCKETS - 1][None, :]
    tab = tab.T * LOG2_E
    def line(rows, stride, offset, ok, fill):
        assert Q_TILE + stride * (rows - 1) <= BIAS_LINE
        k = np.arange(BIAS_LINE)
        dist = np.where(k < Q_TILE, k, k - BIAS_LINE) + offset
        vals = jnp.where(ok(dist)[None], tab[:, np.clip(dist, 0, n_max - 1)], fill)
        return vals.reshape(NSA_GROUPS, NSA_REP, BIAS_LINE)

    return jnp.stack([
        line(2 * CMP_PER_QTILE, CMP_STRIDE, Q_TILE - CMP_LEN + 1, lambda dd: dd >= 0, NEG_BIG),
        line(2 * K_TILE, 1, K_TILE, lambda dd: dd >= 0, NEG_BIG),
        line(3 * K_TILE, 1, 2 * K_TILE, lambda dd: (dd >= 0) & (dd < WINDOW), NEG_BIG),
    ], axis=1)


def _selection_map(seq):
    nc = seq // CMP_STRIDE
    ns = seq // SEL_LEN
    n_cmp = (seq - CMP_LEN) // CMP_STRIDE + 1
    ratio = SEL_LEN // CMP_STRIDE
    lead = CMP_LEN // CMP_STRIDE - 1
    j = np.arange(ns)[:, None]
    n = np.arange(nc)[None, :]
    m = ((n >= ratio * j - lead) & (n < ratio * j + ratio) & (n < n_cmp)).astype(np.float32)
    m = m.reshape(ns, nc // CMP_CHUNK, CMP_CHUNK).transpose(1, 0, 2)
    return jnp.asarray(m, dtype=BF16)


def kernel(x, mix_norm_pre, mix_norm_post, ffn_norm_pre, ffn_norm_post, ffn_w_in, ffn_w_out,
           ret_w_in, ret_w_out, kv_norm, kv_w, cmp_pe_k, cmp_w1_k, cmp_w2_k,
           cmp_pe_v, cmp_w1_v, cmp_w2_v, nsa_w_in, nsa_w_out, rel_bias):
    batch, seq, d = x.shape
    n_ret = ret_w_in.shape[0]
    n_nsa = nsa_w_in.shape[0]
    assert seq % (SEL_PER_GROUP * SEL_LEN) == 0 and seq // SEL_LEN >= SEL_TOPK
    h = x.reshape(batch * seq, d)
    shared = None
    for layer in range(n_ret + n_nsa):
        if layer == n_ret:
            kc_nat, vc_nat, ks, kw, vsT, vwT = kv_project(h, kv_norm, kv_w, batch=batch, seq=seq, tm=TM_KV)
            cmp_nat, cmp_tr = compress_blocks(kc_nat, vc_nat, cmp_pe_k, cmp_w1_k, cmp_w2_k,
                                              cmp_pe_v, cmp_w1_v, cmp_w2_v, batch=batch, seq=seq)
            shared = (cmp_nat, cmp_tr, ks, vsT, kw, vwT, _bias_tables(rel_bias), _selection_map(seq))
        if layer < n_ret:
            w_in = _deinterleave_qk_columns(ret_w_in[layer]).astype(BF16)
            proj = norm_matmul(h, mix_norm_pre[layer], w_in, tm=TM_PROJ, tn=TN_PROJ, out_dtype=BF16)
            mixed = retention_core(proj, *_retention_tables(seq), batch=batch, seq=seq)
            w_o, feature_major = ret_w_out[layer], False
        else:
            j = layer - n_ret
            qT, gT = q_project(h, mix_norm_pre[layer], nsa_w_in[j], batch=batch, seq=seq, tm=TM_Q)
            mixed = nsa_attention(qT, gT, *shared, batch=batch, seq=seq)
            w_o, feature_major = nsa_w_out[j], True
        h = mix_out_ffn(mixed, w_o.astype(BF16), mix_norm_post[layer], h, ffn_norm_pre[layer],
                        ffn_w_in[layer].astype(BF16), ffn_w_out[layer].astype(BF16), ffn_norm_post[layer],
                        batch=batch, seq=seq, tm=TM_FFN, chunks=FFN_CHUNKS, feature_major=feature_major)
    return h.reshape(batch, seq, d)
```

```python
import functools
import math

import numpy as np
import jax
import jax.numpy as jnp
from jax import lax
from jax.experimental import pallas as pl
from jax.experimental.pallas import tpu as pltpu

F32 = jnp.float32
BF16 = jnp.bfloat16

D_MODEL = 1024
RMS_EPS = 1e-6

RET_HEADS = 4
RET_QK = 256
RET_V = 512
RET_CHUNK = 128
RET_CHUNKS_PER_STEP = 4

FFN_HIDDEN = 2816

NSA_HEADS = 16
NSA_GROUPS = 4
NSA_REP = 4
NSA_D = 64
CMP_LEN = 32
CMP_STRIDE = 16
CMP_HIDDEN = 256
SEL_LEN = 64
SEL_TOPK = 16
WINDOW = 512
REL_BUCKETS = 32
REL_MAX_DIST = 128

Q_TILE = 256
K_TILE = 256
SEL_PER_GROUP = 16
V_ROWS = 80
LOG2_E = math.log2(math.e)
NEG_BIG = -(2.0 ** 100)
M_INIT = -(2.0 ** 120)

LANE_TILE = 128
K_LANES = LANE_TILE
VMEM_V7X = 64 * 1024 * 1024
VMEM_LIMIT = VMEM_V7X - 8 * 1024 * 1024

TM_PROJ, TN_PROJ = 512, 2048
TM_KV = 1024
TM_Q = 1024
TM_FFN, FFN_CHUNKS = 512, 2


def _cparams(sem, vmem=None, flags=None):
    return pltpu.CompilerParams(dimension_semantics=sem, vmem_limit_bytes=vmem, flags=flags)


def _rms(x, g):
    return x * lax.rsqrt(jnp.mean(x * x, axis=-1, keepdims=True) + RMS_EPS) * g


def _norm_matmul_body(x_ref, g_ref, w_ref, o_ref, *, tn):
    xn = _rms(x_ref[...], g_ref[...]).astype(BF16)
    for c in range(w_ref.shape[1] // tn):
        cols = slice(c * tn, (c + 1) * tn)
        o_ref[:, cols] = jnp.dot(xn, w_ref[:, cols], preferred_element_type=F32).astype(o_ref.dtype)


def norm_matmul(x, g, w, *, tm, tn, out_dtype=F32):
    t, d = x.shape
    n = w.shape[1]
    assert n % tn == 0
    return pl.pallas_call(
        functools.partial(_norm_matmul_body, tn=tn),
        out_shape=jax.ShapeDtypeStruct((t, n), out_dtype),
        grid=(t // tm,),
        in_specs=[
            pl.BlockSpec((tm, d), lambda i: (i, 0)),
            pl.BlockSpec((1, d), lambda i: (0, 0)),
            pl.BlockSpec((d, n), lambda i: (0, 0), pipeline_mode=pl.Buffered(1)),
        ],
        out_specs=pl.BlockSpec((tm, n), lambda i: (i, 0)),
        compiler_params=_cparams(("parallel",), VMEM_LIMIT),
        name="norm_matmul",
    )(x, g.reshape(1, d), w)


def _mix_out_ffn_body(y_ref, wp_ref, gmix_ref, res_ref, gpre_ref, win_ref, wo_ref, gpost_ref, o_ref,
                      *, chunks, feature_major):
    if feature_major:
        z = lax.dot_general(y_ref[0], wp_ref[...], (((0,), (0,)), ((), ())), preferred_element_type=F32)
    else:
        z = jnp.dot(y_ref[...], wp_ref[...], preferred_element_type=F32)
    x = res_ref[...] + _rms(z, gmix_ref[...])
    xn = _rms(x, gpre_ref[...]).astype(BF16)
    hdim = wo_ref.shape[0]
    th = hdim // chunks
    y = None
    for c in range(chunks):
        gate = jnp.dot(xn, win_ref[:, c * th:(c + 1) * th], preferred_element_type=F32)
        up = jnp.dot(xn, win_ref[:, hdim + c * th:hdim + (c + 1) * th], preferred_element_type=F32)
        act = (gate * jax.nn.sigmoid(gate) * up).astype(BF16)
        part = jnp.dot(act, wo_ref[c * th:(c + 1) * th, :], preferred_element_type=F32)
        y = part if y is None else y + part
    o_ref[...] = x + _rms(y, gpost_ref[...])


def mix_out_ffn(y, w_proj, g_mix, res, g_pre, w_in, w_out, g_post, *, batch, seq, tm, chunks, feature_major):
    k, d = w_proj.shape
    hdim = w_out.shape[0]
    assert (hdim // chunks) % LANE_TILE == 0 and hdim % chunks == 0
    nt = seq // tm
    once = pl.Buffered(1)
    row = lambda b, i: (b * nt + i, 0)
    const = lambda b, i: (0, 0)
    if feature_major:
        y_spec = pl.BlockSpec((1, k, tm), lambda b, i: (b, 0, i))
    else:
        y_spec = pl.BlockSpec((tm, k), row)
    return pl.pallas_call(
        functools.partial(_mix_out_ffn_body, chunks=chunks, feature_major=feature_major),
        out_shape=jax.ShapeDtypeStruct((batch * seq, d), F32),
        grid=(batch, nt),
        in_specs=[
            y_spec,
            pl.BlockSpec((k, d), const, pipeline_mode=once),
            pl.BlockSpec((1, d), const),
            pl.BlockSpec((tm, d), row),
            pl.BlockSpec((1, d), const),
            pl.BlockSpec((d, 2 * hdim), const, pipeline_mode=once),
            pl.BlockSpec((hdim, d), const, pipeline_mode=once),
            pl.BlockSpec((1, d), const),
        ],
        out_specs=pl.BlockSpec((tm, d), row),
        compiler_params=_cparams(("parallel", "parallel"), VMEM_LIMIT),
        name="mix_out_ffn",
    )(y, w_proj, g_mix.reshape(1, d), res, g_pre.reshape(1, d), w_in, w_out, g_post.reshape(1, d))


def _retention_body(q_ref, k_ref, v_ref, g_ref, cos_ref, sin_ref, dmat_ref, qdec_ref, kdec_ref,
                    cdec_ref, o_ref, state_ref):
    @pl.when(pl.program_id(1) == 0)
    def _():
        state_ref[...] = jnp.zeros_like(state_ref)

    half = RET_QK // 2

    for sub in range(RET_CHUNKS_PER_STEP):
        rows = slice(sub * RET_CHUNK, (sub + 1) * RET_CHUNK)
        cos = cos_ref[rows, :]
        sin = sin_ref[rows, :]

        def rotate(x_ref, h):
            x1 = x_ref[rows, h * RET_QK:h * RET_QK + half].astype(F32)
            x2 = x_ref[rows, h * RET_QK + half:(h + 1) * RET_QK].astype(F32)
            return jnp.concatenate([x1 * cos - x2 * sin, x1 * sin + x2 * cos], axis=1)

        for h in range(RET_HEADS):
            qr = rotate(q_ref, h)
            kr = rotate(k_ref, h) * (RET_QK ** -0.5)
            v = v_ref[rows, h * RET_V:(h + 1) * RET_V].astype(BF16)
            scores = lax.dot_general(qr.astype(BF16), kr.astype(BF16), (((1,), (1,)), ((), ())),
                                     preferred_element_type=F32) * dmat_ref[h]
            state = state_ref[h]
            o = (jnp.dot(scores.astype(BF16), v, preferred_element_type=F32)
                 + jnp.dot((qr * qdec_ref[h]).astype(BF16), state.astype(BF16),
                           preferred_element_type=F32))
            kd = (kr * kdec_ref[h]).astype(BF16)
            state_ref[h] = state * cdec_ref[h, 0:1, :] + lax.dot_general(
                kd, v, (((0,), (0,)), ((), ())), preferred_element_type=F32)
            o = o * lax.rsqrt(jnp.mean(o * o, axis=-1, keepdims=True) + RMS_EPS)
            gate = g_ref[rows, h * RET_V:(h + 1) * RET_V].astype(F32)
            o_ref[rows, h * RET_V:(h + 1) * RET_V] = (o * (gate * jax.nn.sigmoid(gate))).astype(o_ref.dtype)


def retention_core(proj, cos, sin, dmat, qdec, kdec, cdec, *, batch, seq):
    c = RET_CHUNK * RET_CHUNKS_PER_STEP
    nc = seq // c
    hq = RET_HEADS * RET_QK
    hv = RET_HEADS * RET_V
    return pl.pallas_call(
        _retention_body,
        out_shape=jax.ShapeDtypeStruct((batch * seq, hv), BF16),
        grid=(batch, nc),
        in_specs=[
            pl.BlockSpec((c, hq), lambda b, t: (b * nc + t, 0)),
            pl.BlockSpec((c, hq), lambda b, t: (b * nc + t, 1)),
            pl.BlockSpec((c, hv), lambda b, t: (b * nc + t, 1)),
            pl.BlockSpec((c, hv), lambda b, t: (b * nc + t, 2)),
            pl.BlockSpec((c, RET_QK // 2), lambda b, t: (t, 0)),
            pl.BlockSpec((c, RET_QK // 2), lambda b, t: (t, 0)),
            pl.BlockSpec((RET_HEADS, RET_CHUNK, RET_CHUNK), lambda b, t: (0, 0, 0)),
            pl.BlockSpec((RET_HEADS, RET_CHUNK, RET_QK), lambda b, t: (0, 0, 0)),
            pl.BlockSpec((RET_HEADS, RET_CHUNK, RET_QK), lambda b, t: (0, 0, 0)),
            pl.BlockSpec((RET_HEADS, 8, RET_V), lambda b, t: (0, 0, 0)),
        ],
        out_specs=pl.BlockSpec((c, hv), lambda b, t: (b * nc + t, 0)),
        scratch_shapes=[pltpu.VMEM((RET_HEADS, RET_QK, RET_V), F32)],
        compiler_params=_cparams(("parallel", "arbitrary"), VMEM_LIMIT),
        name="retention_core",
    )(proj, proj, proj, proj, cos, sin, dmat, qdec, kdec, cdec)


def _retention_tables(seq):
    h, dk, c = RET_HEADS, RET_QK, RET_CHUNK
    pos = jnp.arange(seq, dtype=F32)
    theta = 1.0 / (10000.0 ** jnp.linspace(0.0, 1.0, dk // 2, dtype=F32))
    ang = pos[:, None] * theta[None, :]
    log_gamma = jnp.log(1.0 - 2.0 ** (-5.0 - jnp.arange(h, dtype=F32)))
    idx = jnp.arange(c, dtype=F32)
    rel = idx[:, None] - idx[None, :]
    dmat = jnp.where(rel >= 0, jnp.exp(jnp.maximum(rel, 0.0) * log_gamma[:, None, None]), 0.0)
    qdec = jnp.exp((idx + 1.0)[None, :] * log_gamma[:, None])
    kdec = jnp.exp((c - 1.0 - idx)[None, :] * log_gamma[:, None])
    cdec = jnp.exp(c * log_gamma)
    qdec = jnp.broadcast_to(qdec[:, :, None], (h, c, dk))
    kdec = jnp.broadcast_to(kdec[:, :, None], (h, c, dk))
    cdec = jnp.broadcast_to(cdec[:, None, None], (h, 8, RET_V))
    return jnp.cos(ang), jnp.sin(ang), dmat, qdec, kdec, cdec


def _deinterleave_qk_columns(w):
    nqk = 2 * RET_HEADS * RET_QK
    perm = []
    for h in range(2 * RET_HEADS):
        base = h * RET_QK
        perm += [base + 2 * i for i in range(RET_QK // 2)]
        perm += [base + 2 * i + 1 for i in range(RET_QK // 2)]
    perm = np.asarray(perm + list(range(nqk, w.shape[1])), dtype=np.int32)
    return w[:, perm]


def _kv_body(x_ref, g_ref, wk_ref, wvT_ref, kc_ref, vc_ref, ks_ref, kw_ref, vsT_ref, vwT_ref):
    xn = _rms(x_ref[...], g_ref[...]).astype(BF16)
    tm = xn.shape[0]
    kall = jnp.dot(xn, wk_ref[...], preferred_element_type=F32)
    gd = NSA_GROUPS * NSA_D
    kc_ref[...] = kall[:, 0:gd]
    vc_ref[...] = kall[:, gd:2 * gd]
    row = pl.program_id(1) * tm + lax.broadcasted_iota(jnp.int32, (tm, K_LANES), 0)
    lane = lax.broadcasted_iota(jnp.int32, (tm, K_LANES), 1)
    blk = (row // SEL_LEN) % SEL_PER_GROUP
    onehot = jnp.where(lane - NSA_D == blk, 1.0, 0.0).astype(F32)
    for g in range(NSA_GROUPS):
        ks = kall[:, 2 * gd + K_LANES * g:2 * gd + K_LANES * (g + 1)]
        ks_ref[0, g] = (ks + onehot).astype(BF16)
        kw_base = 2 * gd + NSA_GROUPS * K_LANES
        kw = kall[:, kw_base + K_LANES * g:kw_base + K_LANES * (g + 1)]
        kw_ref[0, g] = kw.astype(BF16)
    vT = lax.dot_general(wvT_ref[...], xn, (((1,), (1,)), ((), ())),
                         preferred_element_type=F32)
    extra = jnp.where(lax.broadcasted_iota(jnp.int32, (V_ROWS - NSA_D, K_TILE), 0) == 0, 1.0, 0.0)
    extra = extra.astype(BF16)
    for g in range(NSA_GROUPS):
        for c in range(tm // K_TILE):
            cols = slice(c * K_TILE, (c + 1) * K_TILE)
            vsT_ref[0, g, c, 0:NSA_D, :] = vT[NSA_D * g:NSA_D * (g + 1), cols].astype(BF16)
            vsT_ref[0, g, c, NSA_D:V_ROWS, :] = extra
            vwT_ref[0, g, c, 0:NSA_D, :] = vT[gd + NSA_D * g:gd + NSA_D * (g + 1), cols].astype(BF16)
            vwT_ref[0, g, c, NSA_D:V_ROWS, :] = extra


def kv_project(x, g, kv_w, *, batch, seq, tm):
    d = x.shape[1]
    gd = NSA_GROUPS * NSA_D
    k_c, v_c, k_s, v_s, k_w, v_w = [kv_w[:, i * gd:(i + 1) * gd] for i in range(6)]

    def pad_groups(w):
        w = w.reshape(d, NSA_GROUPS, NSA_D)
        return jnp.pad(w, ((0, 0), (0, 0), (0, K_LANES - NSA_D))).reshape(d, NSA_GROUPS * K_LANES)

    wk = jnp.concatenate([k_c, v_c, pad_groups(k_s), pad_groups(k_w)], axis=1).astype(BF16)
    wvT = jnp.concatenate([v_s, v_w], axis=1).T.astype(BF16)
    nt = seq // tm
    G = NSA_GROUPS
    return pl.pallas_call(
        _kv_body,
        out_shape=(
            jax.ShapeDtypeStruct((batch * seq, gd), F32),
            jax.ShapeDtypeStruct((batch * seq, gd), F32),
            jax.ShapeDtypeStruct((batch, G, seq, K_LANES), BF16),
            jax.ShapeDtypeStruct((batch, G, seq, K_LANES), BF16),
            jax.ShapeDtypeStruct((batch, G, seq // K_TILE, V_ROWS, K_TILE), BF16),
            jax.ShapeDtypeStruct((batch, G, seq // K_TILE, V_ROWS, K_TILE), BF16),
        ),
        grid=(batch, nt),
        in_specs=[
            pl.BlockSpec((tm, d), lambda b, i: (b * nt + i, 0)),
            pl.BlockSpec((1, d), lambda b, i: (0, 0)),
            pl.BlockSpec(wk.shape, lambda b, i: (0, 0)),
            pl.BlockSpec(wvT.shape, lambda b, i: (0, 0)),
        ],
        out_specs=(
            pl.BlockSpec((tm, gd), lambda b, i: (b * nt + i, 0)),
            pl.BlockSpec((tm, gd), lambda b, i: (b * nt + i, 0)),
            pl.BlockSpec((1, G, tm, K_LANES), lambda b, i: (b, 0, i, 0)),
            pl.BlockSpec((1, G, tm, K_LANES), lambda b, i: (b, 0, i, 0)),
            pl.BlockSpec((1, G, tm // K_TILE, V_ROWS, K_TILE), lambda b, i: (b, 0, i, 0, 0)),
            pl.BlockSpec((1, G, tm // K_TILE, V_ROWS, K_TILE), lambda b, i: (b, 0, i, 0, 0)),
        ),
        compiler_params=_cparams(("parallel", "parallel"), VMEM_LIMIT),
        name="kv_project",
    )(x, g.reshape(1, d), wk, wvT)


def _compress_body(c_ref, pe_ref, w1_ref, w2_ref, w2T_ref, nat_ref, tr_ref, sh_ref):
    half = CMP_STRIDE * NSA_D
    nc = c_ref.shape[3]
    c = c_ref[0, 0, 0].astype(BF16)
    w1 = w1_ref[0]
    first = jnp.dot(c, w1[0:half], preferred_element_type=F32)
    second = jnp.dot(c, w1[half:2 * half], preferred_element_type=F32)
    pe_term = jnp.dot(pe_ref[0].astype(BF16), w1, preferred_element_type=F32)
    sh_ref[0:nc, :] = second
    sh_ref[nc:nc + 8, :] = jnp.zeros((8, CMP_HIDDEN), F32)
    pre = first + sh_ref[1:nc + 1, :] + pe_term[0:1, :]
    hid = (pre * jax.nn.sigmoid(pre)).astype(BF16)
    nat_ref[0, 0, 0] = jnp.dot(hid, w2_ref[0], preferred_element_type=F32).astype(BF16)
    tr_ref[0, 0, 0] = lax.dot_general(w2T_ref[0], hid, (((1,), (1,)), ((), ())),
                                      preferred_element_type=F32).astype(BF16)


def compress_blocks(kc_nat, vc_nat, pe_k, w1_k, w2_k, pe_v, w1_v, w2_v, *, batch, seq):
    G, d = NSA_GROUPS, NSA_D
    nc = seq // CMP_STRIDE

    def to_rows(t):
        t = t.reshape(batch, nc, CMP_STRIDE, G, d).transpose(0, 3, 1, 2, 4)
        return t.reshape(batch, G, nc, CMP_STRIDE * d)

    c_all = jnp.stack([to_rows(kc_nat), to_rows(vc_nat)])
    pe = jnp.stack([pe_k.reshape(1, -1), pe_v.reshape(1, -1)])
    pe = jnp.broadcast_to(pe, (2, 8, CMP_LEN * d))
    w1 = jnp.stack([w1_k, w1_v]).astype(BF16)
    w2 = jnp.stack([w2_k, w2_v]).astype(BF16)
    w2T = jnp.stack([w2_k.T, w2_v.T]).astype(BF16)
    return pl.pallas_call(
        _compress_body,
        out_shape=(
            jax.ShapeDtypeStruct((2, batch, G, nc, d), BF16),
            jax.ShapeDtypeStruct((2, batch, G, d, nc), BF16),
        ),
        grid=(2, batch, G),
        in_specs=[
            pl.BlockSpec((1, 1, 1, nc, CMP_STRIDE * d), lambda w, b, g: (w, b, g, 0, 0)),
            pl.BlockSpec((1, 8, CMP_LEN * d), lambda w, b, g: (w, 0, 0)),
            pl.BlockSpec((1, CMP_LEN * d, CMP_HIDDEN), lambda w, b, g: (w, 0, 0)),
            pl.BlockSpec((1, CMP_HIDDEN, d), lambda w, b, g: (w, 0, 0)),
            pl.BlockSpec((1, d, CMP_HIDDEN), lambda w, b, g: (w, 0, 0)),
        ],
        out_specs=(
            pl.BlockSpec((1, 1, 1, nc, d), lambda w, b, g: (w, b, g, 0, 0)),
            pl.BlockSpec((1, 1, 1, d, nc), lambda w, b, g: (w, b, g, 0, 0)),
        ),
        scratch_shapes=[pltpu.VMEM((nc + 8, CMP_HIDDEN), F32)],
        compiler_params=_cparams(("parallel", "parallel", "parallel"), VMEM_LIMIT),
        name="compress_blocks",
    )(c_all, pe, w1, w2, w2T)


GATE_ROWS = 16


def _qproj_body(x_ref, g_ref, wT_ref, qT_ref, gT_ref):
    xn = _rms(x_ref[...], g_ref[...]).astype(BF16)
    tm = xn.shape[0]
    pT = lax.dot_general(wT_ref[...], xn, (((1,), (1,)), ((), ())),
                         preferred_element_type=F32)
    hd = NSA_HEADS * NSA_D
    q = pT[0:hd] * (NSA_D ** -0.5 * LOG2_E)
    qT_ref[0] = q.reshape(NSA_HEADS, NSA_D, tm).astype(BF16)
    gates = jax.nn.sigmoid(pT[hd:hd + NSA_GROUPS * GATE_ROWS])
    gT_ref[0] = gates.reshape(NSA_GROUPS, GATE_ROWS, tm)


def q_project(x, g, w_in, *, batch, seq, tm):
    d = x.shape[1]
    hd = NSA_HEADS * NSA_D
    per_group = NSA_REP * 3
    wg = w_in[:, hd:].reshape(d, NSA_GROUPS, per_group)
    wg = jnp.pad(wg, ((0, 0), (0, 0), (0, GATE_ROWS - per_group))).reshape(d, NSA_GROUPS * GATE_ROWS)
    wT = jnp.concatenate([w_in[:, :hd], wg], axis=1).T.astype(BF16)
    nt = seq // tm
    return pl.pallas_call(
        _qproj_body,
        out_shape=(
            jax.ShapeDtypeStruct((batch, NSA_HEADS, NSA_D, seq), BF16),
            jax.ShapeDtypeStruct((batch, NSA_GROUPS, GATE_ROWS, seq), F32),
        ),
        grid=(batch, nt),
        in_specs=[
            pl.BlockSpec((tm, d), lambda b, i: (b * nt + i, 0)),
            pl.BlockSpec((1, d), lambda b, i: (0, 0)),
            pl.BlockSpec(wT.shape, lambda b, i: (0, 0)),
        ],
        out_specs=(
            pl.BlockSpec((1, NSA_HEADS, NSA_D, tm), lambda b, i: (b, 0, 0, i)),
            pl.BlockSpec((1, NSA_GROUPS, GATE_ROWS, tm), lambda b, i: (b, 0, 0, i)),
        ),
        compiler_params=_cparams(("parallel", "parallel"), VMEM_LIMIT),
        name="q_project",
    )(x, g.reshape(1, d), wT)


CMP_CHUNK = 256
CMP_PER_QTILE = Q_TILE // CMP_STRIDE
SEL_PER_QTILE = Q_TILE // SEL_LEN
LANES = NSA_REP * Q_TILE
TILES_PER_GROUP = SEL_PER_GROUP * SEL_LEN // K_TILE
BIAS_LINE = 1024
WIN_TILES = (WINDOW + Q_TILE) // K_TILE
FAR_UNROLL = 4
SEL_ROW_STEP = 64


def _nsa_body(qT_ref, gT_ref, kc_ref, cmpL_ref, ks_ref, vsT_ref, kw_ref, vwT_ref,
              lines_ref, oT_ref,
              s_ref, imp_ref, msel_ref, qaug_ref, acc_ref, m_ref, out_ref, bc_ref, bs_ref, bw_ref, cacc_ref):
    i = pl.program_id(2)
    ns = imp_ref.shape[0]
    qT = jnp.concatenate([qT_ref[0, r] for r in range(NSA_REP)], axis=1)
    tiny = jnp.finfo(F32).tiny

    @pl.when(i == 0)
    def _():
        def expand(kind, r, rows, stride):
            line = lines_ref[0, kind, r:r + 1, :]
            shifted = pltpu.roll(jnp.broadcast_to(line, (rows, line.shape[1])), 0, 1,
                                 stride=stride, stride_axis=0)
            return shifted[:, 0:Q_TILE]

        for r in range(NSA_REP):
            cols = slice(r * Q_TILE, (r + 1) * Q_TILE)
            bc_ref[:, cols] = expand(0, r, bc_ref.shape[0], CMP_STRIDE)
            bs_ref[:, cols] = expand(1, r, bs_ref.shape[0], 1)
            bw_ref[:, cols] = expand(2, r, bw_ref.shape[0], 1)

    def gate_row(j):
        return jnp.concatenate([gT_ref[0, 0, 3 * r + j:3 * r + j + 1, :] for r in range(NSA_REP)], axis=1)

    nchunks = i // (CMP_CHUNK // CMP_PER_QTILE) + 1
    visible = CMP_PER_QTILE * (i + 1)

    def chunk_rows(c, count=1):
        return pl.ds(pl.multiple_of(c * CMP_CHUNK, CMP_CHUNK), count * CMP_CHUNK)

    def for_chunks(body):
        def pair(pr, carry):
            body(2 * pr, 2)
            return carry

        lax.fori_loop(0, nchunks // 2, pair, 0)

        @pl.when(nchunks % 2 == 1)
        def _():
            body(nchunks - 1, 1)

    def cmp_scores(c, count):
        s_ref[chunk_rows(c, count), :] = jnp.dot(kc_ref[0, 0, 0, chunk_rows(c, count), :], qT,
                                                 preferred_element_type=F32)

    for_chunks(cmp_scores)

    @pl.when(i == 0)
    def _():
        s_ref[0:CMP_PER_QTILE, :] = s_ref[0:CMP_PER_QTILE, :] + bc_ref[CMP_PER_QTILE:2 * CMP_PER_QTILE, :]

    @pl.when(i > 0)
    def _():
        rows = pl.ds(pl.multiple_of(CMP_PER_QTILE * (i - 1), CMP_PER_QTILE), 2 * CMP_PER_QTILE)
        s_ref[rows, :] = s_ref[rows, :] + bc_ref[...]

    cacc_ref[...] = jnp.zeros_like(cacc_ref)
    m_ref[...] = jnp.full(m_ref.shape, M_INIT, F32)

    def cmp_step(c, count):
        rid = c * CMP_CHUNK + lax.broadcasted_iota(jnp.int32, (count * CMP_CHUNK, LANES), 0)
        s = jnp.where(rid < visible, s_ref[chunk_rows(c, count), :], NEG_BIG)
        m_old = m_ref[0:1, :]
        m_new = jnp.maximum(m_old, jnp.max(s, axis=0, keepdims=True))
        alpha = jnp.exp2(m_old - m_new)
        p = jnp.exp2(s - m_new).astype(BF16)
        left = jnp.concatenate([cmpL_ref[0, 0, c + j] for j in range(count)], axis=1)
        cacc_ref[...] = alpha * cacc_ref[...] + jnp.dot(left, p, preferred_element_type=F32)
        m_ref[0:1, :] = m_new

    for_chunks(cmp_step)
    sees_block = m_ref[0:1, :] > 0.5 * NEG_BIG
    inv_c = jnp.where(sees_block, 1.0 / jnp.maximum(cacc_ref[NSA_D:NSA_D + 1, :], tiny), 0.0)
    out_ref[...] = (gate_row(0) * inv_c) * cacc_ref[0:NSA_D, :]
    weights = cacc_ref[V_ROWS:V_ROWS + ns, :] * inv_c
    imp = weights[:, 0:Q_TILE]
    for r in range(1, NSA_REP):
        imp = imp + weights[:, r * Q_TILE:(r + 1) * Q_TILE]
    imp_ref[...] = imp

    qaug_ref[0:NSA_D, :] = qT
    qaug_ref[NSA_D:, :] = jnp.zeros((qaug_ref.shape[0] - NSA_D, LANES), BF16)

    def reset():
        acc_ref[...] = jnp.zeros_like(acc_ref)
        m_ref[...] = jnp.full(m_ref.shape, M_INIT, F32)

    def tile_rows(j):
        return slice(j * K_TILE, (j + 1) * K_TILE)

    def scores_to_scratch(k_ref, first_tile, count, slot):
        rows = pl.ds(pl.multiple_of(first_tile * K_TILE, K_TILE), count * K_TILE)
        s_ref[slot * K_TILE:(slot + count) * K_TILE, :] = jnp.dot(
            k_ref[0, 0, rows, :], qaug_ref[...], preferred_element_type=F32)

    def softmax_from_scratch(vT_ref, tiles, slots):
        rows = slice(slots[0] * K_TILE, (slots[-1] + 1) * K_TILE)
        m_old = m_ref[0:1, :]
        m_new = jnp.maximum(m_old, jnp.max(s_ref[rows, :], axis=0, keepdims=True))
        alpha = jnp.exp2(m_old - m_new)
        p = jnp.exp2(s_ref[rows, :] - m_new).astype(BF16)
        vT = jnp.concatenate([vT_ref[0, 0, kt] for kt in tiles], axis=1)
        acc_ref[...] = alpha * acc_ref[...] + jnp.dot(vT, p, preferred_element_type=F32)
        m_ref[0:1, :] = m_new

    def finish(j):
        l = jnp.maximum(acc_ref[NSA_D:NSA_D + 1, :], tiny)
        out_ref[...] += (gate_row(j) * (1.0 / l)) * acc_ref[0:NSA_D, :]

    def window_branch():
        win_slots = [TILES_PER_GROUP + j for j in range(WIN_TILES)]
        win_tiles = [jnp.maximum(i - (WIN_TILES - 1) + j, 0) for j in range(WIN_TILES)]
        reset()
        for j, (kt, slot) in enumerate(zip(win_tiles, win_slots)):
            scores_to_scratch(kw_ref, kt, 1, slot)
            before_start = jnp.where(i - (WIN_TILES - 1) + j < 0, NEG_BIG, 0.0).astype(F32)
            s_ref[tile_rows(slot), :] = s_ref[tile_rows(slot), :] + (bw_ref[tile_rows(j), :] + before_start)
        softmax_from_scratch(vwT_ref, win_tiles, win_slots)
        finish(2)

    def select_blocks(nrows):
        jrow = lax.broadcasted_iota(jnp.int32, (nrows, Q_TILE), 0)
        col = lax.broadcasted_iota(jnp.int32, (nrows, Q_TILE), 1)
        cur = SEL_PER_QTILE * i + col // SEL_LEN
        valid = jrow <= cur
        forced = (jrow == 0) | (jrow == cur) | (jrow == cur - 1)
        free = float(SEL_TOPK - 3)
        candidate = valid & jnp.logical_not(forced)
        v0 = jnp.where(candidate, imp_ref[0:nrows, :], -jnp.inf)

        def strip_max(carry):
            v, taken, theta, above = carry
            best = jnp.max(v, axis=0, keepdims=True)
            hit = v == best
            now = taken + jnp.sum(jnp.where(hit, 1.0, 0.0), axis=0, keepdims=True)
            crossed = (taken < free) & (now >= free)
            return (jnp.where(hit, -jnp.inf, v), now,
                    jnp.where(crossed, best, theta), jnp.where(crossed, taken, above))

        zero_row = jnp.zeros((1, Q_TILE), F32)
        carry = (v0, zero_row, jnp.full((1, Q_TILE), jnp.inf, F32), zero_row)
        for _ in range(SEL_TOPK - 3):
            carry = strip_max(carry)
        _, _, theta, above = carry
        v0 = jnp.where(candidate, imp_ref[0:nrows, :], -jnp.inf)
        tied = v0 == theta
        lower = jnp.where(lax.broadcasted_iota(jnp.int32, (nrows, nrows), 1)
                          < lax.broadcasted_iota(jnp.int32, (nrows, nrows), 0), 1.0, 0.0).astype(BF16)
        rank = jnp.dot(lower, jnp.where(tied, 1.0, 0.0).astype(BF16), preferred_element_type=F32)
        chosen = forced | (v0 > theta) | (tied & (rank < free - above))
        mask_bias = jnp.where(chosen & valid, 0.0, NEG_BIG).astype(BF16)
        msel_ref[0:nrows, :] = jnp.concatenate([mask_bias] * NSA_REP, axis=1)
        if nrows < ns:
            msel_ref[nrows:ns, :] = jnp.full((ns - nrows, LANES), NEG_BIG, BF16)

    last = i // TILES_PER_GROUP
    j_i = i % TILES_PER_GROUP

    half = TILES_PER_GROUP // 2

    def sel_tiles(grp, h):
        return [grp * TILES_PER_GROUP + h * half + j for j in range(half)]

    def sel_slots(h):
        return [h * half + j for j in range(half)]

    def set_slab(grp):
        slab = pl.ds(pl.multiple_of(grp * SEL_PER_GROUP, SEL_PER_GROUP), SEL_PER_GROUP)
        qaug_ref[NSA_D:NSA_D + SEL_PER_GROUP, :] = msel_ref[slab, :]

    def sel_scores(grp, h):
        set_slab(grp)
        scores_to_scratch(ks_ref, sel_tiles(grp, h)[0], half, h * half)

    def sel_softmax(grp, h):
        softmax_from_scratch(vsT_ref, sel_tiles(grp, h), sel_slots(h))

    def add_bias(slot, count, bias_tile):
        rows = slice(slot * K_TILE, (slot + count) * K_TILE)
        s_ref[rows, :] = s_ref[rows, :] + bs_ref[bias_tile * K_TILE:(bias_tile + count) * K_TILE, :]

    row_steps = list(range(SEL_ROW_STEP, ns, SEL_ROW_STEP)) + [ns]
    for idx, nrows in enumerate(row_steps):
        lo = 0 if idx == 0 else row_steps[idx - 1] // SEL_PER_QTILE
        hi = nrows // SEL_PER_QTILE
        in_range = (i >= lo) if idx == len(row_steps) - 1 else ((i >= lo) & (i < hi))

        @pl.when(in_range)
        def _(nrows=nrows):
            window_branch()
            select_blocks(nrows)
            reset()
            sel_scores(0, 0)

    def far(grp):
        sel_scores(grp, 1)
        sel_softmax(grp, 0)
        sel_scores(grp + 1, 0)
        sel_softmax(grp, 1)

    def far_many(trip, carry):
        for u in range(FAR_UNROLL):
            far(FAR_UNROLL * trip + u)
        return carry

    prev_is_near = (last >= 1) & (j_i == 0)
    n_far = jnp.where(prev_is_near, last - 1, last)
    lax.fori_loop(0, n_far // FAR_UNROLL, far_many, 0)

    done = (n_far // FAR_UNROLL) * FAR_UNROLL
    piece = FAR_UNROLL // 2
    while piece >= 1:
        @pl.when((n_far - done) % (2 * piece) >= piece)
        def _(piece=piece, start=done + ((n_far - done) // (2 * piece)) * (2 * piece)):
            for u in range(piece):
                far(start + u)

        piece //= 2

    @pl.when(prev_is_near)
    def _():
        sel_scores(last - 1, 1)
        sel_softmax(last - 1, 0)
        add_bias(TILES_PER_GROUP - 1, 1, 0)
        sel_scores(last, 0)
        sel_softmax(last - 1, 1)

    @pl.when(j_i == 0)
    def _():
        add_bias(0, 1, 1)

    @pl.when(j_i == 1)
    def _():
        add_bias(0, 2, 0)

    @pl.when(j_i == 2)
    def _():
        add_bias(1, 1, 0)

    @pl.when(j_i >= half)
    def _():
        sel_scores(last, 1)
        sel_softmax(last, 0)

        @pl.when(j_i == 2)
        def _():
            add_bias(2, 1, 1)

        @pl.when(j_i == 3)
        def _():
            add_bias(2, 2, 0)

        sel_softmax(last, 1)

    @pl.when(j_i < half)
    def _():
        sel_softmax(last, 0)

    finish(1)

    for r in range(NSA_REP):
        oT_ref[0, r * NSA_D:(r + 1) * NSA_D, :] = out_ref[:, r * Q_TILE:(r + 1) * Q_TILE].astype(oT_ref.dtype)


def nsa_attention(qT, gT, cmp_nat, cmp_tr, ks, vsT, kw, vwT, lines, mmapT, *, batch, seq):
    G = NSA_GROUPS
    nc = seq // CMP_STRIDE
    ns = seq // SEL_LEN
    nq = seq // Q_TILE
    nkt = seq // K_TILE
    hd = NSA_HEADS * NSA_D
    nch = nc // CMP_CHUNK
    vcT = cmp_tr[1].reshape(batch, G, NSA_D, nch, CMP_CHUNK).transpose(0, 1, 3, 2, 4)
    ones = jnp.zeros((V_ROWS - NSA_D, CMP_CHUNK), BF16).at[0].set(1.0)
    cmpL = jnp.concatenate([
        vcT,
        jnp.broadcast_to(ones, (batch, G, nch, V_ROWS - NSA_D, CMP_CHUNK)),
        jnp.broadcast_to(mmapT, (batch, G, nch, ns, CMP_CHUNK)),
    ], axis=3)
    once = pl.Buffered(1)
    return pl.pallas_call(
        _nsa_body,
        out_shape=jax.ShapeDtypeStruct((batch, hd, seq), BF16),
        grid=(batch, G, nq),
        in_specs=[
            pl.BlockSpec((1, NSA_REP, NSA_D, Q_TILE), lambda b, g, i: (b, g, 0, i)),
            pl.BlockSpec((1, 1, GATE_ROWS, Q_TILE), lambda b, g, i: (b, g, 0, i)),
            pl.BlockSpec((1, 1, 1, nc, NSA_D), lambda b, g, i: (0, b, g, 0, 0)),
            pl.BlockSpec((1, 1, nch, V_ROWS + ns, CMP_CHUNK), lambda b, g, i: (b, g, 0, 0, 0)),
            pl.BlockSpec((1, 1, seq, K_LANES), lambda b, g, i: (b, g, 0, 0), pipeline_mode=once),
            pl.BlockSpec((1, 1, nkt, V_ROWS, K_TILE), lambda b, g, i: (b, g, 0, 0, 0), pipeline_mode=once),
            pl.BlockSpec((1, 1, seq, K_LANES), lambda b, g, i: (b, g, 0, 0), pipeline_mode=once),
            pl.BlockSpec((1, 1, nkt, V_ROWS, K_TILE), lambda b, g, i: (b, g, 0, 0, 0), pipeline_mode=once),
            pl.BlockSpec((1, 3, NSA_REP, BIAS_LINE), lambda b, g, i: (g, 0, 0, 0)),
        ],
        out_specs=pl.BlockSpec((1, NSA_REP * NSA_D, Q_TILE), lambda b, g, i: (b, g, i)),
        scratch_shapes=[
            pltpu.VMEM((max(nc, (TILES_PER_GROUP + WIN_TILES) * K_TILE), LANES), F32),
            pltpu.VMEM((ns, Q_TILE), F32),
            pltpu.VMEM((ns, LANES), BF16),
            pltpu.VMEM((K_LANES, LANES), BF16),
            pltpu.VMEM((V_ROWS, LANES), F32),
            pltpu.VMEM((8, LANES), F32),
            pltpu.VMEM((NSA_D, LANES), F32),
            pltpu.VMEM((2 * CMP_PER_QTILE, LANES), F32),
            pltpu.VMEM((2 * K_TILE, LANES), F32),
            pltpu.VMEM((3 * K_TILE, LANES), F32),
            pltpu.VMEM((V_ROWS + ns, LANES), F32),
        ],
        compiler_params=_cparams(("parallel", "parallel", "arbitrary"), VMEM_LIMIT),
        name="nsa_attention",
    )(qT, gT, cmp_nat, cmpL, ks, vsT, kw, vwT, lines)


def _t5_bucket_table(n_max):
    n = np.arange(n_max)
    max_exact = REL_BUCKETS // 2
    nf = np.maximum(n, 1).astype(np.float32)
    large = max_exact + (np.log(nf / np.float32(max_exact))
                         / np.float32(math.log(REL_MAX_DIST / max_exact))
                         * np.float32(REL_BUCKETS - max_exact)).astype(np.int32)
    large = np.minimum(large, REL_BUCKETS - 1)
    return np.where(n < max_exact, n, large).astype(np.int32)


def _bias_tables(rel_bias):
    n_max = WINDOW + Q_TILE
    buckets = _t5_bucket_table(n_max)
    assert (buckets[Q_TILE - CMP_LEN + 1:] == REL_BUCKETS - 1).all()
    tab = rel_bias.astype(F32)[buckets, :] - rel_bias.astype(F32)[REL_BUCKETS - 1][None, :]
    tab = tab.T * LOG2_E
    def line(rows, stride, offset, ok, fill):
        assert Q_TILE + stride * (rows - 1) <= BIAS_LINE
        k = np.arange(BIAS_LINE)
        dist = np.where(k < Q_TILE, k, k - BIAS_LINE) + offset
        vals = jnp.where(ok(dist)[None], tab[:, np.clip(dist, 0, n_max - 1)], fill)
        return vals.reshape(NSA_GROUPS, NSA_REP, BIAS_LINE)

    return jnp.stack([
        line(2 * CMP_PER_QTILE, CMP_STRIDE, Q_TILE - CMP_LEN + 1, lambda dd: dd >= 0, NEG_BIG),
        line(2 * K_TILE, 1, K_TILE, lambda dd: dd >= 0, NEG_BIG),
        line(3 * K_TILE, 1, 2 * K_TILE, lambda dd: (dd >= 0) & (dd < WINDOW), NEG_BIG),
    ], axis=1)


def _selection_map(seq):
    nc = seq // CMP_STRIDE
    ns = seq // SEL_LEN
    n_cmp = (seq - CMP_LEN) // CMP_STRIDE + 1
    ratio = SEL_LEN // CMP_STRIDE
    lead = CMP_LEN // CMP_STRIDE - 1
    j = np.arange(ns)[:, None]
    n = np.arange(nc)[None, :]
    m = ((n >= ratio * j - lead) & (n < ratio * j + ratio) & (n < n_cmp)).astype(np.float32)
    m = m.reshape(ns, nc // CMP_CHUNK, CMP_CHUNK).transpose(1, 0, 2)
    return jnp.asarray(m, dtype=BF16)


def kernel(x, mix_norm_pre, mix_norm_post, ffn_norm_pre, ffn_norm_post, ffn_w_in, ffn_w_out,
           ret_w_in, ret_w_out, kv_norm, kv_w, cmp_pe_k, cmp_w1_k, cmp_w2_k,
           cmp_pe_v, cmp_w1_v, cmp_w2_v, nsa_w_in, nsa_w_out, rel_bias):
    batch, seq, d = x.shape
    n_ret = ret_w_in.shape[0]
    n_nsa = nsa_w_in.shape[0]
    assert seq % (SEL_PER_GROUP * SEL_LEN) == 0 and seq // SEL_LEN >= SEL_TOPK
    h = x.reshape(batch * seq, d)
    shared = None
    for layer in range(n_ret + n_nsa):
        if layer == n_ret:
            kc_nat, vc_nat, ks, kw, vsT, vwT = kv_project(h, kv_norm, kv_w, batch=batch, seq=seq, tm=TM_KV)
            cmp_nat, cmp_tr = compress_blocks(kc_nat, vc_nat, cmp_pe_k, cmp_w1_k, cmp_w2_k,
                                              cmp_pe_v, cmp_w1_v, cmp_w2_v, batch=batch, seq=seq)
            shared = (cmp_nat, cmp_tr, ks, vsT, kw, vwT, _bias_tables(rel_bias), _selection_map(seq))
        if layer < n_ret:
            w_in = _deinterleave_qk_columns(ret_w_in[layer]).astype(BF16)
            proj = norm_matmul(h, mix_norm_pre[layer], w_in, tm=TM_PROJ, tn=TN_PROJ, out_dtype=BF16)
            mixed = retention_core(proj, *_retention_tables(seq), batch=batch, seq=seq)
            w_o, feature_major = ret_w_out[layer], False
        else:
            j = layer - n_ret
            qT, gT = q_project(h, mix_norm_pre[layer], nsa_w_in[j], batch=batch, seq=seq, tm=TM_Q)
            mixed = nsa_attention(qT, gT, *shared, batch=batch, seq=seq)
            w_o, feature_major = nsa_w_out[j], True
        h = mix_out_ffn(mixed, w_o.astype(BF16), mix_norm_post[layer], h, ffn_norm_pre[layer],
                        ffn_w_in[layer].astype(BF16), ffn_w_out[layer].astype(BF16), ffn_norm_post[layer],
                        batch=batch, seq=seq, tm=TM_FFN, chunks=FFN_CHUNKS, feature_major=feature_major)
    return h.reshape(batch, seq, d)
```

```python
import functools
import math

import numpy as np
import jax
import jax.numpy as jnp
from jax import lax
from jax.experimental import pallas as pl
from jax.experimental.pallas import tpu as pltpu

F32 = jnp.float32
BF16 = jnp.bfloat16

D_MODEL = 1024
RMS_EPS = 1e-6

RET_HEADS = 4
RET_QK = 256
RET_V = 512
RET_CHUNK = 128
RET_CHUNKS_PER_STEP = 4

FFN_HIDDEN = 2816

NSA_HEADS = 16
NSA_GROUPS = 4
NSA_REP = 4
NSA_D = 64
CMP_LEN = 32
CMP_STRIDE = 16
CMP_HIDDEN = 256
SEL_LEN = 64
SEL_TOPK = 16
WINDOW = 512
REL_BUCKETS = 32
REL_MAX_DIST = 128

Q_TILE = 256
K_TILE = 256
SEL_PER_GROUP = 16
V_ROWS = 80
LOG2_E = math.log2(math.e)
NEG_BIG = -(2.0 ** 100)
M_INIT = -(2.0 ** 120)

LANE_TILE = 128
K_LANES = LANE_TILE
VMEM_V7X = 64 * 1024 * 1024
VMEM_LIMIT = VMEM_V7X - 8 * 1024 * 1024

TM_PROJ, TN_PROJ = 512, 2048
TM_KV = 1024
TM_Q = 1024
TM_FFN, FFN_CHUNKS = 512, 1


def _cparams(sem, vmem=None, flags=None):
    return pltpu.CompilerParams(dimension_semantics=sem, vmem_limit_bytes=vmem, flags=flags)


def _rms(x, g):
    return x * lax.rsqrt(jnp.mean(x * x, axis=-1, keepdims=True) + RMS_EPS) * g


def _norm_matmul_body(x_ref, g_ref, w_ref, o_ref, *, tn):
    xn = _rms(x_ref[...], g_ref[...]).astype(BF16)
    for c in range(w_ref.shape[1] // tn):
        cols = slice(c * tn, (c + 1) * tn)
        o_ref[:, cols] = jnp.dot(xn, w_ref[:, cols], preferred_element_type=F32).astype(o_ref.dtype)


def norm_matmul(x, g, w, *, tm, tn, out_dtype=F32):
    t, d = x.shape
    n = w.shape[1]
    assert n % tn == 0
    return pl.pallas_call(
        functools.partial(_norm_matmul_body, tn=tn),
        out_shape=jax.ShapeDtypeStruct((t, n), out_dtype),
        grid=(t // tm,),
        in_specs=[
            pl.BlockSpec((tm, d), lambda i: (i, 0)),
            pl.BlockSpec((1, d), lambda i: (0, 0)),
            pl.BlockSpec((d, n), lambda i: (0, 0), pipeline_mode=pl.Buffered(1)),
        ],
        out_specs=pl.BlockSpec((tm, n), lambda i: (i, 0)),
        compiler_params=_cparams(("parallel",), VMEM_LIMIT),
        name="norm_matmul",
    )(x, g.reshape(1, d), w)


def _mix_out_ffn_body(y_ref, wp_ref, gmix_ref, res_ref, gpre_ref, win_ref, wo_ref, gpost_ref, o_ref,
                      *, chunks, feature_major):
    if feature_major:
        z = lax.dot_general(y_ref[0], wp_ref[...], (((0,), (0,)), ((), ())), preferred_element_type=F32)
    else:
        z = jnp.dot(y_ref[...], wp_ref[...], preferred_element_type=F32)
    x = res_ref[...] + _rms(z, gmix_ref[...])
    xn = _rms(x, gpre_ref[...]).astype(BF16)
    hdim = wo_ref.shape[0]
    th = hdim // chunks
    y = None
    for c in range(chunks):
        gate = jnp.dot(xn, win_ref[:, c * th:(c + 1) * th], preferred_element_type=F32)
        up = jnp.dot(xn, win_ref[:, hdim + c * th:hdim + (c + 1) * th], preferred_element_type=F32)
        act = (gate * jax.nn.sigmoid(gate) * up).astype(BF16)
        part = jnp.dot(act, wo_ref[c * th:(c + 1) * th, :], preferred_element_type=F32)
        y = part if y is None else y + part
    o_ref[...] = x + _rms(y, gpost_ref[...])


def mix_out_ffn(y, w_proj, g_mix, res, g_pre, w_in, w_out, g_post, *, batch, seq, tm, chunks, feature_major):
    k, d = w_proj.shape
    hdim = w_out.shape[0]
    assert (hdim // chunks) % LANE_TILE == 0 and hdim % chunks == 0
    nt = seq // tm
    once = pl.Buffered(1)
    row = lambda b, i: (b * nt + i, 0)
    const = lambda b, i: (0, 0)
    if feature_major:
        y_spec = pl.BlockSpec((1, k, tm), lambda b, i: (b, 0, i))
    else:
        y_spec = pl.BlockSpec((tm, k), row)
    return pl.pallas_call(
        functools.partial(_mix_out_ffn_body, chunks=chunks, feature_major=feature_major),
        out_shape=jax.ShapeDtypeStruct((batch * seq, d), F32),
        grid=(batch, nt),
        in_specs=[
            y_spec,
            pl.BlockSpec((k, d), const, pipeline_mode=once),
            pl.BlockSpec((1, d), const),
            pl.BlockSpec((tm, d), row),
            pl.BlockSpec((1, d), const),
            pl.BlockSpec((d, 2 * hdim), const, pipeline_mode=once),
            pl.BlockSpec((hdim, d), const, pipeline_mode=once),
            pl.BlockSpec((1, d), const),
        ],
        out_specs=pl.BlockSpec((tm, d), row),
        compiler_params=_cparams(("parallel", "parallel"), VMEM_LIMIT),
        name="mix_out_ffn",
    )(y, w_proj, g_mix.reshape(1, d), res, g_pre.reshape(1, d), w_in, w_out, g_post.reshape(1, d))


def _retention_body(q_ref, k_ref, v_ref, g_ref, cos_ref, sin_ref, dmat_ref, qdec_ref, kdec_ref,
                    cdec_ref, o_ref, state_ref):
    @pl.when(pl.program_id(1) == 0)
    def _():
        state_ref[...] = jnp.zeros_like(state_ref)

    half = RET_QK // 2

    for sub in range(RET_CHUNKS_PER_STEP):
        rows = slice(sub * RET_CHUNK, (sub + 1) * RET_CHUNK)
        cos = cos_ref[rows, :]
        sin = sin_ref[rows, :]

        def rotate(x_ref, h):
            x1 = x_ref[rows, h * RET_QK:h * RET_QK + half].astype(F32)
            x2 = x_ref[rows, h * RET_QK + half:(h + 1) * RET_QK].astype(F32)
            return jnp.concatenate([x1 * cos - x2 * sin, x1 * sin + x2 * cos], axis=1)

        for h in range(RET_HEADS):
            qr = rotate(q_ref, h)
            kr = rotate(k_ref, h) * (RET_QK ** -0.5)
            v = v_ref[rows, h * RET_V:(h + 1) * RET_V].astype(BF16)
            scores = lax.dot_general(qr.astype(BF16), kr.astype(BF16), (((1,), (1,)), ((), ())),
                                     preferred_element_type=F32) * dmat_ref[h]
            state = state_ref[h]
            o = (jnp.dot(scores.astype(BF16), v, preferred_element_type=F32)
                 + jnp.dot((qr * qdec_ref[h]).astype(BF16), state.astype(BF16),
                           preferred_element_type=F32))
            kd = (kr * kdec_ref[h]).astype(BF16)
            state_ref[h] = state * cdec_ref[h, 0:1, :] + lax.dot_general(
                kd, v, (((0,), (0,)), ((), ())), preferred_element_type=F32)
            o = o * lax.rsqrt(jnp.mean(o * o, axis=-1, keepdims=True) + RMS_EPS)
            gate = g_ref[rows, h * RET_V:(h + 1) * RET_V].astype(F32)
            o_ref[rows, h * RET_V:(h + 1) * RET_V] = (o * (gate * jax.nn.sigmoid(gate))).astype(o_ref.dtype)


def retention_core(proj, cos, sin, dmat, qdec, kdec, cdec, *, batch, seq):
    c = RET_CHUNK * RET_CHUNKS_PER_STEP
    nc = seq // c
    hq = RET_HEADS * RET_QK
    hv = RET_HEADS * RET_V
    return pl.pallas_call(
        _retention_body,
        out_shape=jax.ShapeDtypeStruct((batch * seq, hv), BF16),
        grid=(batch, nc),
        in_specs=[
            pl.BlockSpec((c, hq), lambda b, t: (b * nc + t, 0)),
            pl.BlockSpec((c, hq), lambda b, t: (b * nc + t, 1)),
            pl.BlockSpec((c, hv), lambda b, t: (b * nc + t, 1)),
            pl.BlockSpec((c, hv), lambda b, t: (b * nc + t, 2)),
            pl.BlockSpec((c, RET_QK // 2), lambda b, t: (t, 0)),
            pl.BlockSpec((c, RET_QK // 2), lambda b, t: (t, 0)),
            pl.BlockSpec((RET_HEADS, RET_CHUNK, RET_CHUNK), lambda b, t: (0, 0, 0)),
            pl.BlockSpec((RET_HEADS, RET_CHUNK, RET_QK), lambda b, t: (0, 0, 0)),
            pl.BlockSpec((RET_HEADS, RET_CHUNK, RET_QK), lambda b, t: (0, 0, 0)),
            pl.BlockSpec((RET_HEADS, 8, RET_V), lambda b, t: (0, 0, 0)),
        ],
        out_specs=pl.BlockSpec((c, hv), lambda b, t: (b * nc + t, 0)),
        scratch_shapes=[pltpu.VMEM((RET_HEADS, RET_QK, RET_V), F32)],
        compiler_params=_cparams(("parallel", "arbitrary"), VMEM_LIMIT),
        name="retention_core",
    )(proj, proj, proj, proj, cos, sin, dmat, qdec, kdec, cdec)


def _retention_tables(seq):
    h, dk, c = RET_HEADS, RET_QK, RET_CHUNK
    pos = jnp.arange(seq, dtype=F32)
    theta = 1.0 / (10000.0 ** jnp.linspace(0.0, 1.0, dk // 2, dtype=F32))
    ang = pos[:, None] * theta[None, :]
    log_gamma = jnp.log(1.0 - 2.0 ** (-5.0 - jnp.arange(h, dtype=F32)))
    idx = jnp.arange(c, dtype=F32)
    rel = idx[:, None] - idx[None, :]
    dmat = jnp.where(rel >= 0, jnp.exp(jnp.maximum(rel, 0.0) * log_gamma[:, None, None]), 0.0)
    qdec = jnp.exp((idx + 1.0)[None, :] * log_gamma[:, None])
    kdec = jnp.exp((c - 1.0 - idx)[None, :] * log_gamma[:, None])
    cdec = jnp.exp(c * log_gamma)
    qdec = jnp.broadcast_to(qdec[:, :, None], (h, c, dk))
    kdec = jnp.broadcast_to(kdec[:, :, None], (h, c, dk))
    cdec = jnp.broadcast_to(cdec[:, None, None], (h, 8, RET_V))
    return jnp.cos(ang), jnp.sin(ang), dmat, qdec, kdec, cdec


def _deinterleave_qk_columns(w):
    nqk = 2 * RET_HEADS * RET_QK
    perm = []
    for h in range(2 * RET_HEADS):
        base = h * RET_QK
        perm += [base + 2 * i for i in range(RET_QK // 2)]
        perm += [base + 2 * i + 1 for i in range(RET_QK // 2)]
    perm = np.asarray(perm + list(range(nqk, w.shape[1])), dtype=np.int32)
    return w[:, perm]


def _kv_body(x_ref, g_ref, wk_ref, wvT_ref, kc_ref, vc_ref, ks_ref, kw_ref, vsT_ref, vwT_ref):
    xn = _rms(x_ref[...], g_ref[...]).astype(BF16)
    tm = xn.shape[0]
    kall = jnp.dot(xn, wk_ref[...], preferred_element_type=F32)
    gd = NSA_GROUPS * NSA_D
    kc_ref[...] = kall[:, 0:gd]
    vc_ref[...] = kall[:, gd:2 * gd]
    row = pl.program_id(1) * tm + lax.broadcasted_iota(jnp.int32, (tm, K_LANES), 0)
    lane = lax.broadcasted_iota(jnp.int32, (tm, K_LANES), 1)
    blk = (row // SEL_LEN) % SEL_PER_GROUP
    onehot = jnp.where(lane - NSA_D == blk, 1.0, 0.0).astype(F32)
    for g in range(NSA_GROUPS):
        ks = kall[:, 2 * gd + K_LANES * g:2 * gd + K_LANES * (g + 1)]
        ks_ref[0, g] = (ks + onehot).astype(BF16)
        kw_base = 2 * gd + NSA_GROUPS * K_LANES
        kw = kall[:, kw_base + K_LANES * g:kw_base + K_LANES * (g + 1)]
        kw_ref[0, g] = kw.astype(BF16)
    vT = lax.dot_general(wvT_ref[...], xn, (((1,), (1,)), ((), ())),
                         preferred_element_type=F32)
    extra = jnp.where(lax.broadcasted_iota(jnp.int32, (V_ROWS - NSA_D, K_TILE), 0) == 0, 1.0, 0.0)
    extra = extra.astype(BF16)
    for g in range(NSA_GROUPS):
        for c in range(tm // K_TILE):
            cols = slice(c * K_TILE, (c + 1) * K_TILE)
            vsT_ref[0, g, c, 0:NSA_D, :] = vT[NSA_D * g:NSA_D * (g + 1), cols].astype(BF16)
            vsT_ref[0, g, c, NSA_D:V_ROWS, :] = extra
            vwT_ref[0, g, c, 0:NSA_D, :] = vT[gd + NSA_D * g:gd + NSA_D * (g + 1), cols].astype(BF16)
            vwT_ref[0, g, c, NSA_D:V_ROWS, :] = extra


def kv_project(x, g, kv_w, *, batch, seq, tm):
    d = x.shape[1]
    gd = NSA_GROUPS * NSA_D
    k_c, v_c, k_s, v_s, k_w, v_w = [kv_w[:, i * gd:(i + 1) * gd] for i in range(6)]

    def pad_groups(w):
        w = w.reshape(d, NSA_GROUPS, NSA_D)
        return jnp.pad(w, ((0, 0), (0, 0), (0, K_LANES - NSA_D))).reshape(d, NSA_GROUPS * K_LANES)

    wk = jnp.concatenate([k_c, v_c, pad_groups(k_s), pad_groups(k_w)], axis=1).astype(BF16)
    wvT = jnp.concatenate([v_s, v_w], axis=1).T.astype(BF16)
    nt = seq // tm
    G = NSA_GROUPS
    return pl.pallas_call(
        _kv_body,
        out_shape=(
            jax.ShapeDtypeStruct((batch * seq, gd), F32),
            jax.ShapeDtypeStruct((batch * seq, gd), F32),
            jax.ShapeDtypeStruct((batch, G, seq, K_LANES), BF16),
            jax.ShapeDtypeStruct((batch, G, seq, K_LANES), BF16),
            jax.ShapeDtypeStruct((batch, G, seq // K_TILE, V_ROWS, K_TILE), BF16),
            jax.ShapeDtypeStruct((batch, G, seq // K_TILE, V_ROWS, K_TILE), BF16),
        ),
        grid=(batch, nt),
        in_specs=[
            pl.BlockSpec((tm, d), lambda b, i: (b * nt + i, 0)),
            pl.BlockSpec((1, d), lambda b, i: (0, 0)),
            pl.BlockSpec(wk.shape, lambda b, i: (0, 0)),
            pl.BlockSpec(wvT.shape, lambda b, i: (0, 0)),
        ],
        out_specs=(
            pl.BlockSpec((tm, gd), lambda b, i: (b * nt + i, 0)),
            pl.BlockSpec((tm, gd), lambda b, i: (b * nt + i, 0)),
            pl.BlockSpec((1, G, tm, K_LANES), lambda b, i: (b, 0, i, 0)),
            pl.BlockSpec((1, G, tm, K_LANES), lambda b, i: (b, 0, i, 0)),
            pl.BlockSpec((1, G, tm // K_TILE, V_ROWS, K_TILE), lambda b, i: (b, 0, i, 0, 0)),
            pl.BlockSpec((1, G, tm // K_TILE, V_ROWS, K_TILE), lambda b, i: (b, 0, i, 0, 0)),
        ),
        compiler_params=_cparams(("parallel", "parallel"), VMEM_LIMIT),
        name="kv_project",
    )(x, g.reshape(1, d), wk, wvT)


def _compress_body(c_ref, pe_ref, w1_ref, w2_ref, w2T_ref, nat_ref, tr_ref, sh_ref):
    half = CMP_STRIDE * NSA_D
    nc = c_ref.shape[3]
    c = c_ref[0, 0, 0].astype(BF16)
    w1 = w1_ref[0]
    first = jnp.dot(c, w1[0:half], preferred_element_type=F32)
    second = jnp.dot(c, w1[half:2 * half], preferred_element_type=F32)
    pe_term = jnp.dot(pe_ref[0].astype(BF16), w1, preferred_element_type=F32)
    sh_ref[0:nc, :] = second
    sh_ref[nc:nc + 8, :] = jnp.zeros((8, CMP_HIDDEN), F32)
    pre = first + sh_ref[1:nc + 1, :] + pe_term[0:1, :]
    hid = (pre * jax.nn.sigmoid(pre)).astype(BF16)
    nat_ref[0, 0, 0] = jnp.dot(hid, w2_ref[0], preferred_element_type=F32).astype(BF16)
    tr_ref[0, 0, 0] = lax.dot_general(w2T_ref[0], hid, (((1,), (1,)), ((), ())),
                                      preferred_element_type=F32).astype(BF16)


def compress_blocks(kc_nat, vc_nat, pe_k, w1_k, w2_k, pe_v, w1_v, w2_v, *, batch, seq):
    G, d = NSA_GROUPS, NSA_D
    nc = seq // CMP_STRIDE

    def to_rows(t):
        t = t.reshape(batch, nc, CMP_STRIDE, G, d).transpose(0, 3, 1, 2, 4)
        return t.reshape(batch, G, nc, CMP_STRIDE * d)

    c_all = jnp.stack([to_rows(kc_nat), to_rows(vc_nat)])
    pe = jnp.stack([pe_k.reshape(1, -1), pe_v.reshape(1, -1)])
    pe = jnp.broadcast_to(pe, (2, 8, CMP_LEN * d))
    w1 = jnp.stack([w1_k, w1_v]).astype(BF16)
    w2 = jnp.stack([w2_k, w2_v]).astype(BF16)
    w2T = jnp.stack([w2_k.T, w2_v.T]).astype(BF16)
    return pl.pallas_call(
        _compress_body,
        out_shape=(
            jax.ShapeDtypeStruct((2, batch, G, nc, d), BF16),
            jax.ShapeDtypeStruct((2, batch, G, d, nc), BF16),
        ),
        grid=(2, batch, G),
        in_specs=[
            pl.BlockSpec((1, 1, 1, nc, CMP_STRIDE * d), lambda w, b, g: (w, b, g, 0, 0)),
            pl.BlockSpec((1, 8, CMP_LEN * d), lambda w, b, g: (w, 0, 0)),
            pl.BlockSpec((1, CMP_LEN * d, CMP_HIDDEN), lambda w, b, g: (w, 0, 0)),
            pl.BlockSpec((1, CMP_HIDDEN, d), lambda w, b, g: (w, 0, 0)),
            pl.BlockSpec((1, d, CMP_HIDDEN), lambda w, b, g: (w, 0, 0)),
        ],
        out_specs=(
            pl.BlockSpec((1, 1, 1, nc, d), lambda w, b, g: (w, b, g, 0, 0)),
            pl.BlockSpec((1, 1, 1, d, nc), lambda w, b, g: (w, b, g, 0, 0)),
        ),
        scratch_shapes=[pltpu.VMEM((nc + 8, CMP_HIDDEN), F32)],
        compiler_params=_cparams(("parallel", "parallel", "parallel"), VMEM_LIMIT),
        name="compress_blocks",
    )(c_all, pe, w1, w2, w2T)


GATE_ROWS = 16


def _qproj_body(x_ref, g_ref, wT_ref, qT_ref, gT_ref):
    xn = _rms(x_ref[...], g_ref[...]).astype(BF16)
    tm = xn.shape[0]
    pT = lax.dot_general(wT_ref[...], xn, (((1,), (1,)), ((), ())),
                         preferred_element_type=F32)
    hd = NSA_HEADS * NSA_D
    q = pT[0:hd] * (NSA_D ** -0.5 * LOG2_E)
    qT_ref[0] = q.reshape(NSA_HEADS, NSA_D, tm).astype(BF16)
    gates = jax.nn.sigmoid(pT[hd:hd + NSA_GROUPS * GATE_ROWS])
    gT_ref[0] = gates.reshape(NSA_GROUPS, GATE_ROWS, tm)


def q_project(x, g, w_in, *, batch, seq, tm):
    d = x.shape[1]
    hd = NSA_HEADS * NSA_D
    per_group = NSA_REP * 3
    wg = w_in[:, hd:].reshape(d, NSA_GROUPS, per_group)
    wg = jnp.pad(wg, ((0, 0), (0, 0), (0, GATE_ROWS - per_group))).reshape(d, NSA_GROUPS * GATE_ROWS)
    wT = jnp.concatenate([w_in[:, :hd], wg], axis=1).T.astype(BF16)
    nt = seq // tm
    return pl.pallas_call(
        _qproj_body,
        out_shape=(
            jax.ShapeDtypeStruct((batch, NSA_HEADS, NSA_D, seq), BF16),
            jax.ShapeDtypeStruct((batch, NSA_GROUPS, GATE_ROWS, seq), F32),
        ),
        grid=(batch, nt),
        in_specs=[
            pl.BlockSpec((tm, d), lambda b, i: (b * nt + i, 0)),
            pl.BlockSpec((1, d), lambda b, i: (0, 0)),
            pl.BlockSpec(wT.shape, lambda b, i: (0, 0)),
        ],
        out_specs=(
            pl.BlockSpec((1, NSA_HEADS, NSA_D, tm), lambda b, i: (b, 0, 0, i)),
            pl.BlockSpec((1, NSA_GROUPS, GATE_ROWS, tm), lambda b, i: (b, 0, 0, i)),
        ),
        compiler_params=_cparams(("parallel", "parallel"), VMEM_LIMIT),
        name="q_project",
    )(x, g.reshape(1, d), wT)


CMP_CHUNK = 256
CMP_PER_QTILE = Q_TILE // CMP_STRIDE
SEL_PER_QTILE = Q_TILE // SEL_LEN
LANES = NSA_REP * Q_TILE
TILES_PER_GROUP = SEL_PER_GROUP * SEL_LEN // K_TILE
BIAS_LINE = 1024
WIN_TILES = (WINDOW + Q_TILE) // K_TILE
FAR_UNROLL = 4
SEL_ROW_STEP = 64


def _nsa_body(qT_ref, gT_ref, kc_ref, cmpL_ref, ks_ref, vsT_ref, kw_ref, vwT_ref,
              lines_ref, oT_ref,
              s_ref, imp_ref, msel_ref, qaug_ref, acc_ref, m_ref, out_ref, bc_ref, bs_ref, bw_ref, cacc_ref):
    i = pl.program_id(2)
    ns = imp_ref.shape[0]
    qT = jnp.concatenate([qT_ref[0, r] for r in range(NSA_REP)], axis=1)
    tiny = jnp.finfo(F32).tiny

    @pl.when(i == 0)
    def _():
        def expand(kind, r, rows, stride):
            line = lines_ref[0, kind, r:r + 1, :]
            shifted = pltpu.roll(jnp.broadcast_to(line, (rows, line.shape[1])), 0, 1,
                                 stride=stride, stride_axis=0)
            return shifted[:, 0:Q_TILE]

        for r in range(NSA_REP):
            cols = slice(r * Q_TILE, (r + 1) * Q_TILE)
            bc_ref[:, cols] = expand(0, r, bc_ref.shape[0], CMP_STRIDE)
            bs_ref[:, cols] = expand(1, r, bs_ref.shape[0], 1)
            bw_ref[:, cols] = expand(2, r, bw_ref.shape[0], 1)

    def gate_row(j):
        return jnp.concatenate([gT_ref[0, 0, 3 * r + j:3 * r + j + 1, :] for r in range(NSA_REP)], axis=1)

    nchunks = i // (CMP_CHUNK // CMP_PER_QTILE) + 1
    visible = CMP_PER_QTILE * (i + 1)

    def chunk_rows(c, count=1):
        return pl.ds(pl.multiple_of(c * CMP_CHUNK, CMP_CHUNK), count * CMP_CHUNK)

    def for_chunks(body):
        def pair(pr, carry):
            body(2 * pr, 2)
            return carry

        lax.fori_loop(0, nchunks // 2, pair, 0)

        @pl.when(nchunks % 2 == 1)
        def _():
            body(nchunks - 1, 1)

    def cmp_scores(c, count):
        s_ref[chunk_rows(c, count), :] = jnp.dot(kc_ref[0, 0, 0, chunk_rows(c, count), :], qT,
                                                 preferred_element_type=F32)

    for_chunks(cmp_scores)

    @pl.when(i == 0)
    def _():
        s_ref[0:CMP_PER_QTILE, :] = s_ref[0:CMP_PER_QTILE, :] + bc_ref[CMP_PER_QTILE:2 * CMP_PER_QTILE, :]

    @pl.when(i > 0)
    def _():
        rows = pl.ds(pl.multiple_of(CMP_PER_QTILE * (i - 1), CMP_PER_QTILE), 2 * CMP_PER_QTILE)
        s_ref[rows, :] = s_ref[rows, :] + bc_ref[...]

    cacc_ref[...] = jnp.zeros_like(cacc_ref)
    m_ref[...] = jnp.full(m_ref.shape, M_INIT, F32)

    def cmp_step(c, count):
        rid = c * CMP_CHUNK + lax.broadcasted_iota(jnp.int32, (count * CMP_CHUNK, LANES), 0)
        s = jnp.where(rid < visible, s_ref[chunk_rows(c, count), :], NEG_BIG)
        m_old = m_ref[0:1, :]
        m_new = jnp.maximum(m_old, jnp.max(s, axis=0, keepdims=True))
        alpha = jnp.exp2(m_old - m_new)
        p = jnp.exp2(s - m_new).astype(BF16)
        left = jnp.concatenate([cmpL_ref[0, 0, c + j] for j in range(count)], axis=1)
        cacc_ref[...] = alpha * cacc_ref[...] + jnp.dot(left, p, preferred_element_type=F32)
        m_ref[0:1, :] = m_new

    for_chunks(cmp_step)
    sees_block = m_ref[0:1, :] > 0.5 * NEG_BIG
    inv_c = jnp.where(sees_block, 1.0 / jnp.maximum(cacc_ref[NSA_D:NSA_D + 1, :], tiny), 0.0)
    out_ref[...] = (gate_row(0) * inv_c) * cacc_ref[0:NSA_D, :]
    weights = cacc_ref[V_ROWS:V_ROWS + ns, :] * inv_c
    imp = weights[:, 0:Q_TILE]
    for r in range(1, NSA_REP):
        imp = imp + weights[:, r * Q_TILE:(r + 1) * Q_TILE]
    imp_ref[...] = imp

    qaug_ref[0:NSA_D, :] = qT
    qaug_ref[NSA_D:, :] = jnp.zeros((qaug_ref.shape[0] - NSA_D, LANES), BF16)

    def reset():
        acc_ref[...] = jnp.zeros_like(acc_ref)
        m_ref[...] = jnp.full(m_ref.shape, M_INIT, F32)

    def tile_rows(j):
        return slice(j * K_TILE, (j + 1) * K_TILE)

    def scores_to_scratch(k_ref, first_tile, count, slot):
        rows = pl.ds(pl.multiple_of(first_tile * K_TILE, K_TILE), count * K_TILE)
        s_ref[slot * K_TILE:(slot + count) * K_TILE, :] = jnp.dot(
            k_ref[0, 0, rows, :], qaug_ref[...], preferred_element_type=F32)

    def softmax_from_scratch(vT_ref, tiles, slots):
        rows = slice(slots[0] * K_TILE, (slots[-1] + 1) * K_TILE)
        m_old = m_ref[0:1, :]
        m_new = jnp.maximum(m_old, jnp.max(s_ref[rows, :], axis=0, keepdims=True))
        alpha = jnp.exp2(m_old - m_new)
        p = jnp.exp2(s_ref[rows, :] - m_new).astype(BF16)
        vT = jnp.concatenate([vT_ref[0, 0, kt] for kt in tiles], axis=1)
        acc_ref[...] = alpha * acc_ref[...] + jnp.dot(vT, p, preferred_element_type=F32)
        m_ref[0:1, :] = m_new

    def finish(j):
        l = jnp.maximum(acc_ref[NSA_D:NSA_D + 1, :], tiny)
        out_ref[...] += (gate_row(j) * (1.0 / l)) * acc_ref[0:NSA_D, :]

    def window_branch():
        win_slots = [TILES_PER_GROUP + j for j in range(WIN_TILES)]
        win_tiles = [jnp.maximum(i - (WIN_TILES - 1) + j, 0) for j in range(WIN_TILES)]
        reset()
        for j, (kt, slot) in enumerate(zip(win_tiles, win_slots)):
            scores_to_scratch(kw_ref, kt, 1, slot)
            before_start = jnp.where(i - (WIN_TILES - 1) + j < 0, NEG_BIG, 0.0).astype(F32)
            s_ref[tile_rows(slot), :] = s_ref[tile_rows(slot), :] + (bw_ref[tile_rows(j), :] + before_start)
        softmax_from_scratch(vwT_ref, win_tiles, win_slots)
        finish(2)

    def select_blocks(nrows):
        jrow = lax.broadcasted_iota(jnp.int32, (nrows, Q_TILE), 0)
        col = lax.broadcasted_iota(jnp.int32, (nrows, Q_TILE), 1)
        cur = SEL_PER_QTILE * i + col // SEL_LEN
        valid = jrow <= cur
        forced = (jrow == 0) | (jrow == cur) | (jrow == cur - 1)
        free = float(SEL_TOPK - 3)
        candidate = valid & jnp.logical_not(forced)
        v0 = jnp.where(candidate, imp_ref[0:nrows, :], -jnp.inf)

        def strip_max(carry):
            v, taken, theta, above = carry
            best = jnp.max(v, axis=0, keepdims=True)
            hit = v == best
            now = taken + jnp.sum(jnp.where(hit, 1.0, 0.0), axis=0, keepdims=True)
            crossed = (taken < free) & (now >= free)
            return (jnp.where(hit, -jnp.inf, v), now,
                    jnp.where(crossed, best, theta), jnp.where(crossed, taken, above))

        zero_row = jnp.zeros((1, Q_TILE), F32)
        carry = (v0, zero_row, jnp.full((1, Q_TILE), jnp.inf, F32), zero_row)
        for _ in range(SEL_TOPK - 3):
            carry = strip_max(carry)
        _, _, theta, above = carry
        v0 = jnp.where(candidate, imp_ref[0:nrows, :], -jnp.inf)
        tied = v0 == theta
        lower = jnp.where(lax.broadcasted_iota(jnp.int32, (nrows, nrows), 1)
                          < lax.broadcasted_iota(jnp.int32, (nrows, nrows), 0), 1.0, 0.0).astype(BF16)
        rank = jnp.dot(lower, jnp.where(tied, 1.0, 0.0).astype(BF16), preferred_element_type=F32)
        chosen = forced | (v0 > theta) | (tied & (rank < free - above))
        mask_bias = jnp.where(chosen & valid, 0.0, NEG_BIG).astype(BF16)
        msel_ref[0:nrows, :] = jnp.concatenate([mask_bias] * NSA_REP, axis=1)
        if nrows < ns:
            msel_ref[nrows:ns, :] = jnp.full((ns - nrows, LANES), NEG_BIG, BF16)

    last = i // TILES_PER_GROUP
    j_i = i % TILES_PER_GROUP

    half = TILES_PER_GROUP // 2

    def sel_tiles(grp, h):
        return [grp * TILES_PER_GROUP + h * half + j for j in range(half)]

    def sel_slots(h):
        return [h * half + j for j in range(half)]

    def set_slab(grp):
        slab = pl.ds(pl.multiple_of(grp * SEL_PER_GROUP, SEL_PER_GROUP), SEL_PER_GROUP)
        qaug_ref[NSA_D:NSA_D + SEL_PER_GROUP, :] = msel_ref[slab, :]

    def sel_scores(grp, h):
        set_slab(grp)
        scores_to_scratch(ks_ref, sel_tiles(grp, h)[0], half, h * half)

    def sel_softmax(grp, h):
        softmax_from_scratch(vsT_ref, sel_tiles(grp, h), sel_slots(h))

    def add_bias(slot, count, bias_tile):
        rows = slice(slot * K_TILE, (slot + count) * K_TILE)
        s_ref[rows, :] = s_ref[rows, :] + bs_ref[bias_tile * K_TILE:(bias_tile + count) * K_TILE, :]

    row_steps = list(range(SEL_ROW_STEP, ns, SEL_ROW_STEP)) + [ns]
    for idx, nrows in enumerate(row_steps):
        lo = 0 if idx == 0 else row_steps[idx - 1] // SEL_PER_QTILE
        hi = nrows // SEL_PER_QTILE
        in_range = (i >= lo) if idx == len(row_steps) - 1 else ((i >= lo) & (i < hi))

        @pl.when(in_range)
        def _(nrows=nrows):
            window_branch()
            select_blocks(nrows)
            reset()
            sel_scores(0, 0)

    def far(grp):
        sel_scores(grp, 1)
        sel_softmax(grp, 0)
        sel_scores(grp + 1, 0)
        sel_softmax(grp, 1)

    def far_many(trip, carry):
        for u in range(FAR_UNROLL):
            far(FAR_UNROLL * trip + u)
        return carry

    prev_is_near = (last >= 1) & (j_i == 0)
    n_far = jnp.where(prev_is_near, last - 1, last)
    lax.fori_loop(0, n_far // FAR_UNROLL, far_many, 0)

    done = (n_far // FAR_UNROLL) * FAR_UNROLL
    piece = FAR_UNROLL // 2
    while piece >= 1:
        @pl.when((n_far - done) % (2 * piece) >= piece)
        def _(piece=piece, start=done + ((n_far - done) // (2 * piece)) * (2 * piece)):
            for u in range(piece):
                far(start + u)

        piece //= 2

    @pl.when(prev_is_near)
    def _():
        sel_scores(last - 1, 1)
        sel_softmax(last - 1, 0)
        add_bias(TILES_PER_GROUP - 1, 1, 0)
        sel_scores(last, 0)
        sel_softmax(last - 1, 1)

    @pl.when(j_i == 0)
    def _():
        add_bias(0, 1, 1)

    @pl.when(j_i == 1)
    def _():
        add_bias(0, 2, 0)

    @pl.when(j_i == 2)
    def _():
        add_bias(1, 1, 0)

    @pl.when(j_i >= half)
    def _():
        sel_scores(last, 1)
        sel_softmax(last, 0)

        @pl.when(j_i == 2)
        def _():
            add_bias(2, 1, 1)

        @pl.when(j_i == 3)
        def _():
            add_bias(2, 2, 0)

        sel_softmax(last, 1)

    @pl.when(j_i < half)
    def _():
        sel_softmax(last, 0)

    finish(1)

    for r in range(NSA_REP):
        oT_ref[0, r * NSA_D:(r + 1) * NSA_D, :] = out_ref[:, r * Q_TILE:(r + 1) * Q_TILE].astype(oT_ref.dtype)


def nsa_attention(qT, gT, cmp_nat, cmp_tr, ks, vsT, kw, vwT, lines, mmapT, *, batch, seq):
    G = NSA_GROUPS
    nc = seq // CMP_STRIDE
    ns = seq // SEL_LEN
    nq = seq // Q_TILE
    nkt = seq // K_TILE
    hd = NSA_HEADS * NSA_D
    nch = nc // CMP_CHUNK
    vcT = cmp_tr[1].reshape(batch, G, NSA_D, nch, CMP_CHUNK).transpose(0, 1, 3, 2, 4)
    ones = jnp.zeros((V_ROWS - NSA_D, CMP_CHUNK), BF16).at[0].set(1.0)
    cmpL = jnp.concatenate([
        vcT,
        jnp.broadcast_to(ones, (batch, G, nch, V_ROWS - NSA_D, CMP_CHUNK)),
        jnp.broadcast_to(mmapT, (batch, G, nch, ns, CMP_CHUNK)),
    ], axis=3)
    once = pl.Buffered(1)
    return pl.pallas_call(
        _nsa_body,
        out_shape=jax.ShapeDtypeStruct((batch, hd, seq), BF16),
        grid=(batch, G, nq),
        in_specs=[
            pl.BlockSpec((1, NSA_REP, NSA_D, Q_TILE), lambda b, g, i: (b, g, 0, i)),
            pl.BlockSpec((1, 1, GATE_ROWS, Q_TILE), lambda b, g, i: (b, g, 0, i)),
            pl.BlockSpec((1, 1, 1, nc, NSA_D), lambda b, g, i: (0, b, g, 0, 0)),
            pl.BlockSpec((1, 1, nch, V_ROWS + ns, CMP_CHUNK), lambda b, g, i: (b, g, 0, 0, 0)),
            pl.BlockSpec((1, 1, seq, K_LANES), lambda b, g, i: (b, g, 0, 0), pipeline_mode=once),
            pl.BlockSpec((1, 1, nkt, V_ROWS, K_TILE), lambda b, g, i: (b, g, 0, 0, 0), pipeline_mode=once),
            pl.BlockSpec((1, 1, seq, K_LANES), lambda b, g, i: (b, g, 0, 0), pipeline_mode=once),
            pl.BlockSpec((1, 1, nkt, V_ROWS, K_TILE), lambda b, g, i: (b, g, 0, 0, 0), pipeline_mode=once),
            pl.BlockSpec((1, 3, NSA_REP, BIAS_LINE), lambda b, g, i: (g, 0, 0, 0)),
        ],
        out_specs=pl.BlockSpec((1, NSA_REP * NSA_D, Q_TILE), lambda b, g, i: (b, g, i)),
        scratch_shapes=[
            pltpu.VMEM((max(nc, (TILES_PER_GROUP + WIN_TILES) * K_TILE), LANES), F32),
            pltpu.VMEM((ns, Q_TILE), F32),
            pltpu.VMEM((ns, LANES), BF16),
            pltpu.VMEM((K_LANES, LANES), BF16),
            pltpu.VMEM((V_ROWS, LANES), F32),
            pltpu.VMEM((8, LANES), F32),
            pltpu.VMEM((NSA_D, LANES), F32),
            pltpu.VMEM((2 * CMP_PER_QTILE, LANES), F32),
            pltpu.VMEM((2 * K_TILE, LANES), F32),
            pltpu.VMEM((3 * K_TILE, LANES), F32),
            pltpu.VMEM((V_ROWS + ns, LANES), F32),
        ],
        compiler_params=_cparams(("parallel", "parallel", "arbitrary"), VMEM_LIMIT),
        name="nsa_attention",
    )(qT, gT, cmp_nat, cmpL, ks, vsT, kw, vwT, lines)


def _t5_bucket_table(n_max):
    n = np.arange(n_max)
    max_exact = REL_BUCKETS // 2
    nf = np.maximum(n, 1).astype(np.float32)
    large = max_exact + (np.log(nf / np.float32(max_exact))
                         / np.float32(math.log(REL_MAX_DIST / max_exact))
                         * np.float32(REL_BUCKETS - max_exact)).astype(np.int32)
    large = np.minimum(large, REL_BUCKETS - 1)
    return np.where(n < max_exact, n, large).astype(np.int32)


def _bias_tables(rel_bias):
    n_max = WINDOW + Q_TILE
    buckets = _t5_bucket_table(n_max)
    assert (buckets[Q_TILE - CMP_LEN + 1:] == REL_BUCKETS - 1).all()
    tab = rel_bias.astype(F32)[buckets, :] - rel_bias.astype(F32)[REL_BUCKETS - 1][None, :]
    tab = tab.T * LOG2_E
    def line(rows, stride, offset, ok, fill):
        assert Q_TILE + stride * (rows - 1) <= BIAS_LINE
        k = np.arange(BIAS_LINE)
        dist = np.where(k < Q_TILE, k, k - BIAS_LINE) + offset
        vals = jnp.where(ok(dist)[None], tab[:, np.clip(dist, 0, n_max - 1)], fill)
        return vals.reshape(NSA_GROUPS, NSA_REP, BIAS_LINE)

    return jnp.stack([
        line(2 * CMP_PER_QTILE, CMP_STRIDE, Q_TILE - CMP_LEN + 1, lambda dd: dd >= 0, NEG_BIG),
        line(2 * K_TILE, 1, K_TILE, lambda dd: dd >= 0, NEG_BIG),
        line(3 * K_TILE, 1, 2 * K_TILE, lambda dd: (dd >= 0) & (dd < WINDOW), NEG_BIG),
    ], axis=1)


def _selection_map(seq):
    nc = seq // CMP_STRIDE
    ns = seq // SEL_LEN
    n_cmp = (seq - CMP_LEN) // CMP_STRIDE + 1
    ratio = SEL_LEN // CMP_STRIDE
    lead = CMP_LEN // CMP_STRIDE - 1
    j = np.arange(ns)[:, None]
    n = np.arange(nc)[None, :]
    m = ((n >= ratio * j - lead) & (n < ratio * j + ratio) & (n < n_cmp)).astype(np.float32)
    m = m.reshape(ns, nc // CMP_CHUNK, CMP_CHUNK).transpose(1, 0, 2)
    return jnp.asarray(m, dtype=BF16)


def kernel(x, mix_norm_pre, mix_norm_post, ffn_norm_pre, ffn_norm_post, ffn_w_in, ffn_w_out,
           ret_w_in, ret_w_out, kv_norm, kv_w, cmp_pe_k, cmp_w1_k, cmp_w2_k,
           cmp_pe_v, cmp_w1_v, cmp_w2_v, nsa_w_in, nsa_w_out, rel_bias):
    batch, seq, d = x.shape
    n_ret = ret_w_in.shape[0]
    n_nsa = nsa_w_in.shape[0]
    assert seq % (SEL_PER_GROUP * SEL_LEN) == 0 and seq // SEL_LEN >= SEL_TOPK
    h = x.reshape(batch * seq, d)
    shared = None
    for layer in range(n_ret + n_nsa):
        if layer == n_ret:
            kc_nat, vc_nat, ks, kw, vsT, vwT = kv_project(h, kv_norm, kv_w, batch=batch, seq=seq, tm=TM_KV)
            cmp_nat, cmp_tr = compress_blocks(kc_nat, vc_nat, cmp_pe_k, cmp_w1_k, cmp_w2_k,
                                              cmp_pe_v, cmp_w1_v, cmp_w2_v, batch=batch, seq=seq)
            shared = (cmp_nat, cmp_tr, ks, vsT, kw, vwT, _bias_tables(rel_bias), _selection_map(seq))
        if layer < n_ret:
            w_in = _deinterleave_qk_columns(ret_w_in[layer]).astype(BF16)
            proj = norm_matmul(h, mix_norm_pre[layer], w_in, tm=TM_PROJ, tn=TN_PROJ, out_dtype=BF16)
            mixed = retention_core(proj, *_retention_tables(seq), batch=batch, seq=seq)
            w_o, feature_major = ret_w_out[layer], False
        else:
            j = layer - n_ret
            qT, gT = q_project(h, mix_norm_pre[layer], nsa_w_in[j], batch=batch, seq=seq, tm=TM_Q)
            mixed = nsa_attention(qT, gT, *shared, batch=batch, seq=seq)
            w_o, feature_major = nsa_w_out[j], True
        h = mix_out_ffn(mixed, w_o.astype(BF16), mix_norm_post[layer], h, ffn_norm_pre[layer],
                        ffn_w_in[layer].astype(BF16), ffn_w_out[layer].astype(BF16), ffn_norm_post[layer],
                        batch=batch, seq=seq, tm=TM_FFN, chunks=FFN_CHUNKS, feature_major=feature_major)
    return h.reshape(batch, seq, d)
```
